```python
import jax, jax.numpy as jnp
from jax import lax
import numpy as np

D_MODEL = 1024
BATCH = 32
SEQ = 2048
DEPTH = 2

HEAD_DIM = 64
W_FOX = (D_MODEL * 3 // 8) // HEAD_DIM * HEAD_DIM
N_HEADS_FOX = W_FOX // HEAD_DIM
CONV_CH = D_MODEL // 4
W_DIL = D_MODEL - W_FOX - CONV_CH
N_HEADS_DIL = W_DIL // HEAD_DIM
DILATION_PAIRS = ((128, 1), (512, 4), (2048, 16))
CONV_K = 31
FFN_CONV_K = 3
D_FF = ((8 * D_MODEL // 3 + 127) // 128) * 128
Q_BLOCK = 128
EPS = 1e-6
OFF_QA = 0
OFF_KA = OFF_QA + W_FOX
OFF_VA = OFF_KA + W_FOX
OFF_FA = OFF_VA + W_FOX
OFF_QB = OFF_FA + N_HEADS_FOX
OFF_KB = OFF_QB + W_DIL
OFF_VB = OFF_KB + W_DIL
OFF_GV = OFF_VB + W_DIL
OFF_GG = OFF_GV + CONV_CH
N_IN = OFF_GG + CONV_CH

kernel_name = 'hybrid_fox_dilated_conformer_convffn'


def rmsnorm(x, g):
    xf = x.astype(jnp.float32)
    y = xf * lax.rsqrt(jnp.mean(xf * xf, axis=-1, keepdims=True) + EPS)
    return (y * g.astype(jnp.float32)).astype(x.dtype)


def layernorm(x, g, b):
    xf = x.astype(jnp.float32)
    mu = jnp.mean(xf, axis=-1, keepdims=True)
    xc = xf - mu
    y = xc * lax.rsqrt(jnp.mean(xc * xc, axis=-1, keepdims=True) + EPS)
    return (y * g.astype(jnp.float32) + b.astype(jnp.float32)).astype(x.dtype)


def causal_dwconv(x, w, b):
    k, c = w.shape
    y = lax.conv_general_dilated(x, w[:, None, :].astype(x.dtype), window_strides=(1,),
                                 padding=[(k - 1, 0)], dimension_numbers=('NWC', 'WIO', 'NWC'),
                                 feature_group_count=c)
    return y + b.astype(x.dtype)


def forgetting_attention(q, k, v, log_f):
    B, S, H, Dh = q.shape
    nb = S // Q_BLOCK
    c = jnp.cumsum(log_f, axis=1)
    c_k = jnp.transpose(c, (0, 2, 1))[:, :, None, :]
    qb = q.reshape(B, nb, Q_BLOCK, H, Dh).swapaxes(0, 1)
    cb = c.reshape(B, nb, Q_BLOCK, H).swapaxes(0, 1)
    kpos = jnp.arange(S)
    scale = Dh ** -0.5

    def block(args):
        qi, ci, i = args
        s = jnp.einsum('bqhd,bkhd->bhqk', qi, k, preferred_element_type=jnp.float32) * scale
        s = s + jnp.transpose(ci, (0, 2, 1))[..., None] - c_k
        qpos = i * Q_BLOCK + jnp.arange(Q_BLOCK)
        s = jnp.where(kpos[None, :] <= qpos[:, None], s, -jnp.inf)
        p = jax.nn.softmax(s, axis=-1)
        return jnp.einsum('bhqk,bkhd->bqhd', p.astype(v.dtype), v)

    o = lax.map(block, (qb, cb, jnp.arange(nb)))
    return o.swapaxes(0, 1).reshape(B, S, H, Dh)


def dilated_branch(q, k, v, window, dilation):
    B, S, H, Dh = q.shape
    blk = window // dilation
    span = blk * dilation
    sp = -(-S // span) * span
    nb = sp // span
    pad = ((0, 0), (0, sp - S), (0, 0), (0, 0))

    def strided(t):
        return jnp.pad(t, pad).reshape(B, nb, blk, dilation, H, Dh)

    qs, ks, vs = strided(q), strided(k), strided(v)
    def with_prev(t):
        prev = jnp.concatenate([jnp.zeros_like(t[:, :1]), t[:, :-1]], axis=1)
        return jnp.concatenate([prev, t], axis=2)
    kc, vc = with_prev(ks), with_prev(vs)
    s = jnp.einsum('bnqrhd,bnkrhd->bnrhqk', qs, kc, preferred_element_type=jnp.float32) * (Dh ** -0.5)
    qi = blk + jnp.arange(blk)
    ki = jnp.arange(2 * blk)
    delta = qi[:, None] - ki[None, :]
    band = (delta >= 0) & (delta <= blk)
    first = (jnp.arange(nb) == 0)[:, None, None] & (ki < blk)[None, None, :]
    valid = band[None] & ~first
    s = jnp.where(valid[None, :, None, None], s, -jnp.inf)
    m = jnp.max(s, axis=-1, keepdims=True)
    e = jnp.exp(s - m)
    l = jnp.sum(e, axis=-1)
    num = jnp.einsum('bnrhqk,bnkrhd->bnqrhd', e.astype(v.dtype), vc,
                     preferred_element_type=jnp.float32)
    num = num.reshape(B, sp, H, Dh)[:, :S]
    def to_seq(t):
        return jnp.transpose(t, (0, 1, 4, 2, 3)).reshape(B, sp, H)[:, :S]
    return num, to_seq(m[..., 0]), to_seq(l)


def dilated_mixture(q, k, v):
    parts = [dilated_branch(q, k, v, w, d) for (w, d) in DILATION_PAIRS]
    m_all = parts[0][1]
    for _, m, _ in parts[1:]:
        m_all = jnp.maximum(m_all, m)
    num = 0.0
    den = 0.0
    for n_p, m_p, l_p in parts:
        a = jnp.exp(m_p - m_all)
        num = num + a[..., None] * n_p
        den = den + a * l_p
    return (num / den[..., None]).astype(q.dtype)


def _fwd_setup_inputs(seed: int = 0) -> dict:
    key = jax.random.key(seed)
    ks = jax.random.split(key, 20)
    f32 = jnp.float32
    def nrm(k, shape, scale):
        return jax.random.normal(k, shape, f32) * scale
    return {
        'x': nrm(ks[0], (BATCH, SEQ, D_MODEL), 1.0),
        'ln1_g': 1.0 + nrm(ks[1], (DEPTH, D_MODEL), 0.02),
        'w_in': nrm(ks[2], (DEPTH, D_MODEL, N_IN), D_MODEL ** -0.5),
        'b_forget': jax.random.uniform(ks[3], (DEPTH, N_HEADS_FOX), f32, 1.0, 4.0),
        'g_out_fox': 1.0 + nrm(ks[4], (DEPTH, W_FOX), 0.02),
        'g_out_dil': 1.0 + nrm(ks[5], (DEPTH, W_DIL), 0.02),
        'conv_w': nrm(ks[6], (DEPTH, CONV_K, CONV_CH), CONV_K ** -0.5),
        'conv_b': nrm(ks[7], (DEPTH, CONV_CH), 0.02),
        'cnorm_g': 1.0 + nrm(ks[8], (DEPTH, CONV_CH), 0.02),
        'cnorm_b': nrm(ks[9], (DEPTH, CONV_CH), 0.02),
        'w_o': nrm(ks[10], (DEPTH, D_MODEL, D_MODEL), D_MODEL ** -0.5),
        'ln2_g': 1.0 + nrm(ks[11], (DEPTH, D_MODEL), 0.02),
        'w_up': nrm(ks[12], (DEPTH, D_MODEL, 2 * D_FF), D_MODEL ** -0.5),
        'ffn_conv_w': nrm(ks[13], (DEPTH, FFN_CONV_K, 2 * D_FF), FFN_CONV_K ** -0.5),
        'ffn_conv_b': nrm(ks[14], (DEPTH, 2 * D_FF), 0.02),
        'w_down': nrm(ks[15], (DEPTH, D_FF, D_MODEL), D_FF ** -0.5),
        'g_final': 1.0 + nrm(ks[16], (D_MODEL,), 0.02),
    }


def _fwd_reference(x, ln1_g, w_in, b_forget, g_out_fox, g_out_dil, conv_w, conv_b, cnorm_g, cnorm_b,
              w_o, ln2_g, w_up, ffn_conv_w, ffn_conv_b, w_down, g_final):
    B, S, _ = x.shape
    for l in range(DEPTH):
        h = rmsnorm(x, ln1_g[l])
        p = jnp.einsum('bsd,dn->bsn', h, w_in[l])
        qa = p[..., OFF_QA:OFF_KA].reshape(B, S, N_HEADS_FOX, HEAD_DIM)
        ka = p[..., OFF_KA:OFF_VA].reshape(B, S, N_HEADS_FOX, HEAD_DIM)
        va = p[..., OFF_VA:OFF_FA].reshape(B, S, N_HEADS_FOX, HEAD_DIM)
        log_f = jax.nn.log_sigmoid(p[..., OFF_FA:OFF_QB].astype(jnp.float32)
                                   + b_forget[l].astype(jnp.float32))
        ya = forgetting_attention(qa, ka, va, log_f).reshape(B, S, W_FOX)
        ya = rmsnorm(ya, g_out_fox[l])
        qb = p[..., OFF_QB:OFF_KB].reshape(B, S, N_HEADS_DIL, HEAD_DIM)
        kb = p[..., OFF_KB:OFF_VB].reshape(B, S, N_HEADS_DIL, HEAD_DIM)
        vb = p[..., OFF_VB:OFF_GV].reshape(B, S, N_HEADS_DIL, HEAD_DIM)
        yb = dilated_mixture(qb, kb, vb).reshape(B, S, W_DIL)
        yb = rmsnorm(yb, g_out_dil[l])
        yc = p[..., OFF_GV:OFF_GG] * jax.nn.sigmoid(p[..., OFF_GG:N_IN])
        yc = causal_dwconv(yc, conv_w[l], conv_b[l])
        yc = jax.nn.silu(layernorm(yc, cnorm_g[l], cnorm_b[l]))
        y = jnp.concatenate([ya, yb, yc], axis=-1)
        x = x + jnp.einsum('bsd,de->bse', y, w_o[l])
        h2 = rmsnorm(x, ln2_g[l])
        u = jnp.einsum('bsd,df->bsf', h2, w_up[l])
        u = causal_dwconv(u, ffn_conv_w[l], ffn_conv_b[l])
        hidden = jax.nn.silu(u[..., :D_FF]) * u[..., D_FF:]
        x = x + jnp.einsum('bsf,fd->bsd', hidden, w_down[l])
    return rmsnorm(x, g_final)


import jax as _jax
import jax.numpy as _jnp

TWIN_FORMAT = 'train_step'
FWD_PARAMS = ['x', 'ln1_g', 'w_in', 'b_forget', 'g_out_fox', 'g_out_dil', 'conv_w', 'conv_b', 'cnorm_g', 'cnorm_b', 'w_o', 'ln2_g', 'w_up', 'ffn_conv_w', 'ffn_conv_b', 'w_down', 'g_final']
TWIN_WEIGHTS = ['ln1_g', 'w_in', 'b_forget', 'g_out_fox', 'g_out_dil', 'conv_w', 'conv_b', 'cnorm_g', 'cnorm_b', 'w_o', 'ln2_g', 'w_up', 'ffn_conv_w', 'ffn_conv_b', 'w_down', 'g_final']
TWIN_DIFF_INPUT = 'x'
TWIN_INPUTS = ['x', 'ln1_g', 'w_in', 'b_forget', 'g_out_fox', 'g_out_dil', 'conv_w', 'conv_b', 'cnorm_g', 'cnorm_b', 'w_o', 'ln2_g', 'w_up', 'ffn_conv_w', 'ffn_conv_b', 'w_down', 'g_final', 'loss_target', 'm_ln1_g', 'm_w_in', 'm_b_forget', 'm_g_out_fox', 'm_g_out_dil', 'm_conv_w', 'm_conv_b', 'm_cnorm_g', 'm_cnorm_b', 'm_w_o', 'm_ln2_g', 'm_w_up', 'm_ffn_conv_w', 'm_ffn_conv_b', 'm_w_down', 'm_g_final', 'v_ln1_g', 'v_w_in', 'v_b_forget', 'v_g_out_fox', 'v_g_out_dil', 'v_conv_w', 'v_conv_b', 'v_cnorm_g', 'v_cnorm_b', 'v_w_o', 'v_ln2_g', 'v_w_up', 'v_ffn_conv_w', 'v_ffn_conv_b', 'v_w_down', 'v_g_final']
TWIN_OUTPUTS = ['loss', 'grad_x', 'grad_ln1_g', 'grad_w_in', 'grad_b_forget', 'grad_g_out_fox', 'grad_g_out_dil', 'grad_conv_w', 'grad_conv_b', 'grad_cnorm_g', 'grad_cnorm_b', 'grad_w_o', 'grad_ln2_g', 'grad_w_up', 'grad_ffn_conv_w', 'grad_ffn_conv_b', 'grad_w_down', 'grad_g_final', 'delta_ln1_g', 'delta_w_in', 'delta_b_forget', 'delta_g_out_fox', 'delta_g_out_dil', 'delta_conv_w', 'delta_conv_b', 'delta_cnorm_g', 'delta_cnorm_b', 'delta_w_o', 'delta_ln2_g', 'delta_w_up', 'delta_ffn_conv_w', 'delta_ffn_conv_b', 'delta_w_down', 'delta_g_final', 'new_m_ln1_g', 'new_m_w_in', 'new_m_b_forget', 'new_m_g_out_fox', 'new_m_g_out_dil', 'new_m_conv_w', 'new_m_conv_b', 'new_m_cnorm_g', 'new_m_cnorm_b', 'new_m_w_o', 'new_m_ln2_g', 'new_m_w_up', 'new_m_ffn_conv_w', 'new_m_ffn_conv_b', 'new_m_w_down', 'new_m_g_final', 'new_v_ln1_g', 'new_v_w_in', 'new_v_b_forget', 'new_v_g_out_fox', 'new_v_g_out_dil', 'new_v_conv_w', 'new_v_conv_b', 'new_v_cnorm_g', 'new_v_cnorm_b', 'new_v_w_o', 'new_v_ln2_g', 'new_v_w_up', 'new_v_ffn_conv_w', 'new_v_ffn_conv_b', 'new_v_w_down', 'new_v_g_final']
TWIN_LEAF_KINDS = {'loss': 'loss', 'grad_x': 'grad_x', 'grad_ln1_g': 'grad_w', 'grad_w_in': 'grad_w', 'grad_b_forget': 'grad_w', 'grad_g_out_fox': 'grad_w', 'grad_g_out_dil': 'grad_w', 'grad_conv_w': 'grad_w', 'grad_conv_b': 'grad_w', 'grad_cnorm_g': 'grad_w', 'grad_cnorm_b': 'grad_w', 'grad_w_o': 'grad_w', 'grad_ln2_g': 'grad_w', 'grad_w_up': 'grad_w', 'grad_ffn_conv_w': 'grad_w', 'grad_ffn_conv_b': 'grad_w', 'grad_w_down': 'grad_w', 'grad_g_final': 'grad_w', 'delta_ln1_g': 'delta_w', 'delta_w_in': 'delta_w', 'delta_b_forget': 'delta_w', 'delta_g_out_fox': 'delta_w', 'delta_g_out_dil': 'delta_w', 'delta_conv_w': 'delta_w', 'delta_conv_b': 'delta_w', 'delta_cnorm_g': 'delta_w', 'delta_cnorm_b': 'delta_w', 'delta_w_o': 'delta_w', 'delta_ln2_g': 'delta_w', 'delta_w_up': 'delta_w', 'delta_ffn_conv_w': 'delta_w', 'delta_ffn_conv_b': 'delta_w', 'delta_w_down': 'delta_w', 'delta_g_final': 'delta_w', 'new_m_ln1_g': 'new_m', 'new_m_w_in': 'new_m', 'new_m_b_forget': 'new_m', 'new_m_g_out_fox': 'new_m', 'new_m_g_out_dil': 'new_m', 'new_m_conv_w': 'new_m', 'new_m_conv_b': 'new_m', 'new_m_cnorm_g': 'new_m', 'new_m_cnorm_b': 'new_m', 'new_m_w_o': 'new_m', 'new_m_ln2_g': 'new_m', 'new_m_w_up': 'new_m', 'new_m_ffn_conv_w': 'new_m', 'new_m_ffn_conv_b': 'new_m', 'new_m_w_down': 'new_m', 'new_m_g_final': 'new_m', 'new_v_ln1_g': 'new_v', 'new_v_w_in': 'new_v', 'new_v_b_forget': 'new_v', 'new_v_g_out_fox': 'new_v', 'new_v_g_out_dil': 'new_v', 'new_v_conv_w': 'new_v', 'new_v_conv_b': 'new_v', 'new_v_cnorm_g': 'new_v', 'new_v_cnorm_b': 'new_v', 'new_v_w_o': 'new_v', 'new_v_ln2_g': 'new_v', 'new_v_w_up': 'new_v', 'new_v_ffn_conv_w': 'new_v', 'new_v_ffn_conv_b': 'new_v', 'new_v_w_down': 'new_v', 'new_v_g_final': 'new_v'}


def _forward(args):
    return _fwd_reference(*[args[k] for k in FWD_PARAMS])


def _output_shape():
    out = _jax.eval_shape(lambda: _forward(_fwd_setup_inputs(0)))
    return out.shape, out.dtype

N_MICROBATCH = 1
ADAM_LR = 0.001
ADAM_B1 = 0.9
ADAM_B2 = 0.999
ADAM_EPS = 1e-08
ADAM_WD = 0.01
ADAM_STEP = 10
PER_EXAMPLE_BATCH_AXIS = {'x': 0, 'loss_target': 0}
SHARED_INPUTS = []
_WEIGHT_DTYPES = {'ln1_g': _jnp.float32, 'w_in': _jnp.float32, 'b_forget': _jnp.float32, 'g_out_fox': _jnp.float32, 'g_out_dil': _jnp.float32, 'conv_w': _jnp.float32, 'conv_b': _jnp.float32, 'cnorm_g': _jnp.float32, 'cnorm_b': _jnp.float32, 'w_o': _jnp.float32, 'ln2_g': _jnp.float32, 'w_up': _jnp.float32, 'ffn_conv_w': _jnp.float32, 'ffn_conv_b': _jnp.float32, 'w_down': _jnp.float32, 'g_final': _jnp.float32}
MOMENT_SCALE = {'ln1_g': 2.557470e-01, 'w_in': 1.629231e-01, 'b_forget': 9.456687e-01, 'g_out_fox': 1.973380e-01, 'g_out_dil': 2.175357e-01, 'conv_w': 1.210774e-01, 'conv_b': 4.164469e-01, 'cnorm_g': 2.467787e-01, 'cnorm_b': 2.828953e-01, 'w_o': 1.912541e-01, 'ln2_g': 1.441256e-01, 'w_up': 6.226816e-02, 'ffn_conv_w': 6.124473e-02, 'ffn_conv_b': 7.204247e-02, 'w_down': 1.021729e-01, 'g_final': 6.376225e+01}


def _to_microbatches(a, axis):
    t = _jnp.moveaxis(a, axis, 0)
    t = t.reshape((N_MICROBATCH, t.shape[0] // N_MICROBATCH) + t.shape[1:])
    return _jnp.moveaxis(t, 1, axis + 1)


def setup_inputs(seed: int = 0) -> dict:
    inp = _fwd_setup_inputs(seed)
    key = _jax.random.fold_in(_jax.random.key(seed), 7919)
    shape, _ = _output_shape()
    out = dict(inp)
    out["loss_target"] = _jax.random.normal(_jax.random.fold_in(key, 0), shape, _jnp.float32)
    for i, name in enumerate(TWIN_WEIGHTS):
        w = inp[name].astype(_jnp.float32)
        if MOMENT_SCALE is None:
            s = _jnp.sqrt(_jnp.mean(_jnp.square(w)) + 1e-30)
        else:
            s = MOMENT_SCALE[name]
        km, kv = _jax.random.split(_jax.random.fold_in(key, i + 1))
        out[name] = w
        out["m_" + name] = s * _jax.random.normal(km, w.shape, _jnp.float32)
        out["v_" + name] = (s * s) * _jax.random.uniform(kv, w.shape, _jnp.float32, 0.5, 1.5)
    if N_MICROBATCH > 1:
        for name, axis in PER_EXAMPLE_BATCH_AXIS.items():
            out[name] = _to_microbatches(out[name], axis)
    return {'x': out['x'], 'ln1_g': out['ln1_g'], 'w_in': out['w_in'], 'b_forget': out['b_forget'], 'g_out_fox': out['g_out_fox'], 'g_out_dil': out['g_out_dil'], 'conv_w': out['conv_w'], 'conv_b': out['conv_b'], 'cnorm_g': out['cnorm_g'], 'cnorm_b': out['cnorm_b'], 'w_o': out['w_o'], 'ln2_g': out['ln2_g'], 'w_up': out['w_up'], 'ffn_conv_w': out['ffn_conv_w'], 'ffn_conv_b': out['ffn_conv_b'], 'w_down': out['w_down'], 'g_final': out['g_final'], 'loss_target': out['loss_target'], 'm_ln1_g': out['m_ln1_g'], 'm_w_in': out['m_w_in'], 'm_b_forget': out['m_b_forget'], 'm_g_out_fox': out['m_g_out_fox'], 'm_g_out_dil': out['m_g_out_dil'], 'm_conv_w': out['m_conv_w'], 'm_conv_b': out['m_conv_b'], 'm_cnorm_g': out['m_cnorm_g'], 'm_cnorm_b': out['m_cnorm_b'], 'm_w_o': out['m_w_o'], 'm_ln2_g': out['m_ln2_g'], 'm_w_up': out['m_w_up'], 'm_ffn_conv_w': out['m_ffn_conv_w'], 'm_ffn_conv_b': out['m_ffn_conv_b'], 'm_w_down': out['m_w_down'], 'm_g_final': out['m_g_final'], 'v_ln1_g': out['v_ln1_g'], 'v_w_in': out['v_w_in'], 'v_b_forget': out['v_b_forget'], 'v_g_out_fox': out['v_g_out_fox'], 'v_g_out_dil': out['v_g_out_dil'], 'v_conv_w': out['v_conv_w'], 'v_conv_b': out['v_conv_b'], 'v_cnorm_g': out['v_cnorm_g'], 'v_cnorm_b': out['v_cnorm_b'], 'v_w_o': out['v_w_o'], 'v_ln2_g': out['v_ln2_g'], 'v_w_up': out['v_w_up'], 'v_ffn_conv_w': out['v_ffn_conv_w'], 'v_ffn_conv_b': out['v_ffn_conv_b'], 'v_w_down': out['v_w_down'], 'v_g_final': out['v_g_final']}


def _loss(weights, diff, rest, loss_target):
    with _jax.named_scope("forward"):
        args = {**rest, TWIN_DIFF_INPUT: diff, **{k: w.astype(_WEIGHT_DTYPES[k]) for k, w in weights.items()}}
        y = _forward(args)
    with _jax.named_scope("loss_head"):
        err = _jnp.square(y.astype(_jnp.float32) - loss_target)
        return 0.5 * _jnp.sum(_jnp.mean(err, axis=-1)) if err.ndim else 0.5 * err


def _adamw(w, g, m, v):
    m = ADAM_B1 * m + (1.0 - ADAM_B1) * g
    v = ADAM_B2 * v + (1.0 - ADAM_B2) * _jnp.square(g)
    m_hat = m / (1.0 - ADAM_B1 ** ADAM_STEP)
    v_hat = v / (1.0 - ADAM_B2 ** ADAM_STEP)
    delta = -ADAM_LR * (m_hat / (_jnp.sqrt(v_hat) + ADAM_EPS) + ADAM_WD * w)
    return delta, m, v


def reference(x, ln1_g, w_in, b_forget, g_out_fox, g_out_dil, conv_w, conv_b, cnorm_g, cnorm_b, w_o, ln2_g, w_up, ffn_conv_w, ffn_conv_b, w_down, g_final, loss_target, m_ln1_g, m_w_in, m_b_forget, m_g_out_fox, m_g_out_dil, m_conv_w, m_conv_b, m_cnorm_g, m_cnorm_b, m_w_o, m_ln2_g, m_w_up, m_ffn_conv_w, m_ffn_conv_b, m_w_down, m_g_final, v_ln1_g, v_w_in, v_b_forget, v_g_out_fox, v_g_out_dil, v_conv_w, v_conv_b, v_cnorm_g, v_cnorm_b, v_w_o, v_ln2_g, v_w_up, v_ffn_conv_w, v_ffn_conv_b, v_w_down, v_g_final):
    given = dict(x=x, ln1_g=ln1_g, w_in=w_in, b_forget=b_forget, g_out_fox=g_out_fox, g_out_dil=g_out_dil, conv_w=conv_w, conv_b=conv_b, cnorm_g=cnorm_g, cnorm_b=cnorm_b, w_o=w_o, ln2_g=ln2_g, w_up=w_up, ffn_conv_w=ffn_conv_w, ffn_conv_b=ffn_conv_b, w_down=w_down, g_final=g_final, loss_target=loss_target, m_ln1_g=m_ln1_g, m_w_in=m_w_in, m_b_forget=m_b_forget, m_g_out_fox=m_g_out_fox, m_g_out_dil=m_g_out_dil, m_conv_w=m_conv_w, m_conv_b=m_conv_b, m_cnorm_g=m_cnorm_g, m_cnorm_b=m_cnorm_b, m_w_o=m_w_o, m_ln2_g=m_ln2_g, m_w_up=m_w_up, m_ffn_conv_w=m_ffn_conv_w, m_ffn_conv_b=m_ffn_conv_b, m_w_down=m_w_down, m_g_final=m_g_final, v_ln1_g=v_ln1_g, v_w_in=v_w_in, v_b_forget=v_b_forget, v_g_out_fox=v_g_out_fox, v_g_out_dil=v_g_out_dil, v_conv_w=v_conv_w, v_conv_b=v_conv_b, v_cnorm_g=v_cnorm_g, v_cnorm_b=v_cnorm_b, v_w_o=v_w_o, v_ln2_g=v_ln2_g, v_w_up=v_w_up, v_ffn_conv_w=v_ffn_conv_w, v_ffn_conv_b=v_ffn_conv_b, v_w_down=v_w_down, v_g_final=v_g_final)
    weights = {n: given[n] for n in TWIN_WEIGHTS}
    shared = {n: given[n] for n in SHARED_INPUTS}
    per_example = {n: given[n] for n in ['x']}
    grad_fn = _jax.value_and_grad(_loss, argnums=(0, 1))

    def one_microbatch(ex, loss_target):
        ex = dict(ex)
        diff = ex.pop(TWIN_DIFF_INPUT)
        return grad_fn(weights, diff, {**shared, **ex}, loss_target)

    if N_MICROBATCH == 1:
        loss, (grad_w, grad_x) = one_microbatch(per_example, given["loss_target"])
    else:
        def body(carry, xs):
            loss_sum, grad_sum = carry
            l_k, (gw_k, gx_k) = one_microbatch(xs[0], xs[1])
            with _jax.named_scope("update"):
                return (loss_sum + l_k, _jax.tree.map(_jnp.add, grad_sum, gw_k)), gx_k

        init = (_jnp.zeros((), _jnp.float32), _jax.tree.map(_jnp.zeros_like, weights))
        (loss, grad_w), grad_x = _jax.lax.scan(body, init, (per_example, given["loss_target"]))
    with _jax.named_scope("update"):
        delta_w, new_m, new_v = {}, {}, {}
        for n in TWIN_WEIGHTS:
            delta_w[n], new_m[n], new_v[n] = _adamw(weights[n], grad_w[n], given["m_" + n], given["v_" + n])
    return (loss, grad_x, *[grad_w[n] for n in TWIN_WEIGHTS], *[delta_w[n] for n in TWIN_WEIGHTS],
            *[new_m[n] for n in TWIN_WEIGHTS], *[new_v[n] for n in TWIN_WEIGHTS])
```

```python
import functools

import jax
import jax.numpy as jnp
from jax import lax
from jax.experimental import pallas as pl
from jax.experimental.pallas import tpu as pltpu

F32 = jnp.float32
BF16 = jnp.bfloat16

D = 1024
S = 2048
DEPTH = 2
HD = 64
WA = 384
NHP = 3
CC = 256
CK = 31
FK = 3
DFF = 2816
NIN = 2822
NP = 3072
SCALE = 0.125
EPS = 1e-6
NEG = -1e30
NCHIP = 4
NDEV = 8
LANE = 128

CB_QA, CB_KA, CB_VA, CB_QB, CB_KB, CB_VB = 0, 3, 6, 9, 12, 15
CB_GV, CB_GG = 9, 10
CB_FA = 22

ADAM_LR, ADAM_B1, ADAM_B2, ADAM_EPS, ADAM_WD, ADAM_STEP = 0.001, 0.9, 0.999, 1e-08, 0.01, 10

VMEM_LIMIT = 56 * 1024 * 1024


def _cparams(sem=None):
    return pltpu.CompilerParams(dimension_semantics=sem, vmem_limit_bytes=VMEM_LIMIT)


def _split3(x):
    hi = x.astype(BF16)
    r1 = x - hi.astype(F32)
    mid = r1.astype(BF16)
    lo = (r1 - mid.astype(F32)).astype(BF16)
    return hi, mid, lo


def _sigmoid(z):
    return 1.0 / (1.0 + jnp.exp(-z))


def _matmul(a, b, *, ta=False, tb=False, out_dtype=F32, add=None, tm, tn, tk, name):
    M = a.shape[1] if ta else a.shape[0]
    K = a.shape[0] if ta else a.shape[1]
    N = b.shape[0] if tb else b.shape[1]
    assert (b.shape[1] if tb else b.shape[0]) == K
    assert M % tm == 0 and N % tn == 0 and K % tk == 0, (M, N, K, tm, tn, tk)
    nk = K // tk
    dn = (((0 if ta else 1,), (1 if tb else 0,)), ((), ()))

    def body(*refs):
        if add is not None:
            a_ref, b_ref, add_ref, o_ref, acc = refs
        else:
            a_ref, b_ref, o_ref, acc = refs
        k = pl.program_id(2)
        prod = lax.dot_general(a_ref[...].astype(BF16), b_ref[...].astype(BF16), dn, preferred_element_type=F32)

        def finish(r):
            if add is not None:
                r = r + add_ref[...]
            o_ref[...] = r.astype(o_ref.dtype)

        if nk == 1:
            finish(prod)
        else:
            @pl.when(k == 0)
            def _():
                acc[...] = prod

            @pl.when(k > 0)
            def _():
                acc[...] += prod

            @pl.when(k == nk - 1)
            def _():
                finish(acc[...])

    a_spec = pl.BlockSpec((tk, tm), lambda i, j, k: (k, i)) if ta else pl.BlockSpec((tm, tk), lambda i, j, k: (i, k))
    b_spec = pl.BlockSpec((tn, tk), lambda i, j, k: (j, k)) if tb else pl.BlockSpec((tk, tn), lambda i, j, k: (k, j))
    o_spec = pl.BlockSpec((tm, tn), lambda i, j, k: (i, j))
    in_specs = [a_spec, b_spec]
    args = [a, b]
    if add is not None:
        in_specs.append(o_spec)
        args.append(add)
    return pl.pallas_call(
        body, name=name, grid=(M // tm, N // tn, nk),
        in_specs=in_specs, out_specs=o_spec,
        out_shape=jax.ShapeDtypeStruct((M, N), out_dtype),
        scratch_shapes=[pltpu.VMEM((tm, tn) if nk > 1 else (8, 128), F32)],
        compiler_params=_cparams(("parallel", "parallel", "arbitrary")),
    )(*args)


def _rms_fwd(x, g, name):
    T = x.shape[0]
    tr = 512

    def body(x_ref, g_ref, h_ref):
        xv = x_ref[...]
        r = lax.rsqrt(jnp.mean(xv * xv, axis=1, keepdims=True) + EPS)
        h_ref[...] = (xv * r * g_ref[...]).astype(BF16)

    return pl.pallas_call(
        body, name=name, grid=(T // tr,),
        in_specs=[pl.BlockSpec((tr, D), lambda i: (i, 0)), pl.BlockSpec((1, D), lambda i: (0, 0))],
        out_specs=pl.BlockSpec((tr, D), lambda i: (i, 0)),
        out_shape=jax.ShapeDtypeStruct((T, D), BF16),
        compiler_params=_cparams(("parallel",)),
    )(x, g.reshape(1, D))


def _rms_bwd(x, g, dh, dres, name):
    T = x.shape[0]
    tr = 512

    def body(x_ref, g_ref, dh_ref, dres_ref, dx_ref, dxb_ref, dg_ref):
        i = pl.program_id(0)
        xv = x_ref[...]
        dhv = dh_ref[...].astype(F32)
        r = lax.rsqrt(jnp.mean(xv * xv, axis=1, keepdims=True) + EPS)
        a = dhv * g_ref[...]
        dx = dres_ref[...] + r * a - xv * (r * r * r * jnp.mean(xv * a, axis=1, keepdims=True))
        dx_ref[...] = dx
        dxb_ref[...] = dx.astype(BF16)
        part = jnp.sum(dhv * xv * r, axis=0, keepdims=True)

        @pl.when(i == 0)
        def _():
            dg_ref[...] = part

        @pl.when(i > 0)
        def _():
            dg_ref[...] += part

    row = pl.BlockSpec((tr, D), lambda i: (i, 0))
    vec = pl.BlockSpec((1, D), lambda i: (0, 0))
    return pl.pallas_call(
        body, name=name, grid=(T // tr,),
        in_specs=[row, vec, row, row], out_specs=[row, row, vec],
        out_shape=[jax.ShapeDtypeStruct((T, D), F32), jax.ShapeDtypeStruct((T, D), BF16),
                   jax.ShapeDtypeStruct((1, D), F32)],
        compiler_params=_cparams(("arbitrary",)),
    )(x, g.reshape(1, D), dh, dres)


def _loss_head(x, g, target, name):
    T = x.shape[0]
    tr = 512

    def body(x_ref, g_ref, t_ref, loss_ref, dx_ref, dxb_ref, dg_ref):
        i = pl.program_id(0)
        xv = x_ref[...]
        gv = g_ref[...]
        r = lax.rsqrt(jnp.mean(xv * xv, axis=1, keepdims=True) + EPS)
        n = xv * r
        err = n * gv - t_ref[...]
        lpart = 0.5 * jnp.sum(jnp.mean(err * err, axis=1, keepdims=True), axis=0, keepdims=True)
        dy = err * (1.0 / D)
        a = dy * gv
        dx = r * a - xv * (r * r * r * jnp.mean(xv * a, axis=1, keepdims=True))
        dx_ref[...] = dx
        dxb_ref[...] = dx.astype(BF16)
        part = jnp.sum(dy * n, axis=0, keepdims=True)
        lfull = jnp.broadcast_to(lpart, (8, LANE))

        @pl.when(i == 0)
        def _():
            dg_ref[...] = part
            loss_ref[...] = lfull

        @pl.when(i > 0)
        def _():
            dg_ref[...] += part
            loss_ref[...] += lfull

    row = pl.BlockSpec((tr, D), lambda i: (i, 0))
    vec = pl.BlockSpec((1, D), lambda i: (0, 0))
    lsp = pl.BlockSpec((8, LANE), lambda i: (0, 0))
    return pl.pallas_call(
        body, name=name, grid=(T // tr,),
        in_specs=[row, vec, row], out_specs=[lsp, row, row, vec],
        out_shape=[jax.ShapeDtypeStruct((8, LANE), F32), jax.ShapeDtypeStruct((T, D), F32),
                   jax.ShapeDtypeStruct((T, D), BF16), jax.ShapeDtypeStruct((1, D), F32)],
        compiler_params=_cparams(("arbitrary",)),
    )(x, g.reshape(1, D), target)


CUM_BLK = 256


def _tri(n, upper):
    r = lax.broadcasted_iota(jnp.int32, (n, n), 0)
    c = lax.broadcasted_iota(jnp.int32, (n, n), 1)
    return jnp.where((c >= r) if upper else (c <= r), 1.0, 0.0).astype(BF16)


def _tri_apply(tri, x):
    hi, mid, lo = _split3(x)
    out = jnp.dot(tri, hi, preferred_element_type=F32)
    out = out + jnp.dot(tri, mid, preferred_element_type=F32)
    return out + jnp.dot(tri, lo, preferred_element_type=F32)


def _forget_fwd(P, bf_pad, nb, name):
    nblk = S // CUM_BLK

    def body(fa_ref, b_ref, c_ref, ct_ref):
        tri = _tri(CUM_BLK, upper=False)
        carry = jnp.zeros((1, LANE), F32)
        for i in range(nblk):
            z = fa_ref[pl.ds(i * CUM_BLK, CUM_BLK), :] + b_ref[...]
            lf = jnp.minimum(z, 0.0) - jnp.log(1.0 + jnp.exp(-jnp.abs(z)))
            cb = _tri_apply(tri, lf) + carry
            c_ref[pl.ds(i * CUM_BLK, CUM_BLK), :] = cb
            ct_ref[:, pl.ds(i * CUM_BLK, CUM_BLK)] = cb.T[0:8, :]
            carry = cb[CUM_BLK - 1:CUM_BLK, :]

    return pl.pallas_call(
        body, name=name, grid=(nb,),
        in_specs=[pl.BlockSpec((S, LANE), lambda b: (b, CB_FA)), pl.BlockSpec((1, LANE), lambda b: (0, 0))],
        out_specs=[pl.BlockSpec((S, LANE), lambda b: (b, 0)), pl.BlockSpec((None, 8, S), lambda b: (b, 0, 0))],
        out_shape=[jax.ShapeDtypeStruct((nb * S, LANE), F32), jax.ShapeDtypeStruct((nb, 8, S), F32)],
        compiler_params=_cparams(("parallel",)),
    )(P, bf_pad)


def _forget_bwd(P, bf_pad, dcb, nb, name):
    nblk = S // CUM_BLK

    def body(fa_ref, b_ref, dc_ref, dfa_ref, db_ref):
        b = pl.program_id(0)
        tri = _tri(CUM_BLK, upper=True)
        lane = lax.broadcasted_iota(jnp.int32, (CUM_BLK, LANE), 1)
        carry = jnp.zeros((1, LANE), F32)
        dbacc = jnp.zeros((1, LANE), F32)
        for i in reversed(range(nblk)):
            rows = pl.ds(i * CUM_BLK, CUM_BLK)
            dc = jnp.zeros((CUM_BLK, LANE), F32)
            dcv = dc_ref[rows, :]
            for h in range(2 * NHP):
                dc = jnp.where(lane == h, -dcv[:, HD * h:HD * h + 1], dc)
            dl = _tri_apply(tri, dc) + carry
            carry = dl[0:1, :]
            z = fa_ref[rows, :] + b_ref[...]
            dz = jnp.where(lane < 2 * NHP, dl * (1.0 - _sigmoid(z)), 0.0)
            dfa_ref[rows, :] = dz.astype(BF16)
            dbacc = dbacc + jnp.sum(dz, axis=0, keepdims=True)

        dbfull = jnp.broadcast_to(dbacc, (8, LANE))

        @pl.when(b == 0)
        def _():
            db_ref[...] = dbfull

        @pl.when(b > 0)
        def _():
            db_ref[...] += dbfull

    return pl.pallas_call(
        body, name=name, grid=(nb,),
        in_specs=[pl.BlockSpec((S, LANE), lambda b: (b, CB_FA)), pl.BlockSpec((1, LANE), lambda b: (0, 0)),
                  pl.BlockSpec((S, WA), lambda b: (b, 0))],
        out_specs=[pl.BlockSpec((S, LANE), lambda b: (b, 0)), pl.BlockSpec((8, LANE), lambda b: (0, 0))],
        out_shape=[jax.ShapeDtypeStruct((nb * S, LANE), BF16), jax.ShapeDtypeStruct((8, LANE), F32)],
        compiler_params=_cparams(("arbitrary",)),
    )(P, bf_pad, dcb)


FT = 256


def _head_masks(shape):
    lane = lax.broadcasted_iota(jnp.int32, shape, 1)
    return lane < HD, lane >= HD


def _fox_fwd(P, c, cT, nb, name):
    nq = S // FT

    def body(q_ref, k_ref, v_ref, c_ref, ct_ref, o_ref, lse_ref, qs, kb, vb):
        hp = pl.program_id(1)
        qs[...] = (q_ref[...] * SCALE).astype(BF16)
        kb[...] = k_ref[...].astype(BF16)
        vb[...] = v_ref[...].astype(BF16)
        lane = lax.broadcasted_iota(jnp.int32, (FT, LANE), 1)
        masks = _head_masks((FT, LANE))
        row = lax.broadcasted_iota(jnp.int32, (FT, FT), 0)
        col = lax.broadcasted_iota(jnp.int32, (FT, FT), 1)

        def qloop(qi, _):
            q0 = pl.multiple_of(qi * FT, FT)
            qt = qs[pl.ds(q0, FT), :]
            ct = c_ref[pl.ds(q0, FT), :]
            res = []
            for e in range(2):
                h = 2 * hp + e
                qm = jnp.where(masks[e], qt, jnp.zeros_like(qt))
                cq = jnp.sum(jnp.where(lane == h, ct, 0.0), axis=1, keepdims=True)

                def kloop(kj, carry, qm=qm, cq=cq, h=h):
                    m, l, acc = carry
                    k0 = pl.multiple_of(kj * FT, FT)
                    kt = kb[pl.ds(k0, FT), :]
                    vt = vb[pl.ds(k0, FT), :]
                    ck = ct_ref[pl.ds(h, 1), pl.ds(k0, FT)]
                    s = lax.dot_general(qm, kt, (((1,), (1,)), ((), ())), preferred_element_type=F32)
                    s = s + cq - ck
                    s = jnp.where(k0 + col <= q0 + row, s, NEG)
                    mn = jnp.maximum(m, jnp.max(s, axis=1, keepdims=True))
                    p = jnp.exp(s - mn)
                    a = jnp.exp(m - mn)
                    l = a * l + jnp.sum(p, axis=1, keepdims=True)
                    acc = a * acc + jnp.dot(p.astype(BF16), vt, preferred_element_type=F32)
                    return mn, l, acc

                init = (jnp.full((FT, 1), NEG, F32), jnp.zeros((FT, 1), F32), jnp.zeros((FT, LANE), F32))
                res.append(lax.fori_loop(0, qi + 1, kloop, init))
            (m0, l0, a0), (m1, l1, a1) = res
            o_ref[pl.ds(q0, FT), :] = jnp.where(masks[0], a0 / l0, a1 / l1)
            lse_ref[pl.ds(q0, FT), :] = jnp.where(masks[0], m0 + jnp.log(l0), m1 + jnp.log(l1))
            return 0

        lax.fori_loop(0, nq, qloop, 0)

    def colblk(off):
        return pl.BlockSpec((S, LANE), lambda b, hp: (b, off + hp))

    return pl.pallas_call(
        body, name=name, grid=(nb, NHP),
        in_specs=[colblk(CB_QA), colblk(CB_KA), colblk(CB_VA),
                  pl.BlockSpec((S, LANE), lambda b, hp: (b, 0)), pl.BlockSpec((None, 8, S), lambda b, hp: (b, 0, 0))],
        out_specs=[colblk(0), colblk(0)],
        out_shape=[jax.ShapeDtypeStruct((nb * S, WA), F32), jax.ShapeDtypeStruct((nb * S, WA), F32)],
        scratch_shapes=[pltpu.VMEM((S, LANE), BF16)] * 3,
        compiler_params=_cparams(("parallel", "parallel")),
    )(P, P, P, c, cT)


def _fox_bwd(P, c, cT, o, lse, do, nb, name):
    nq = S // FT

    def body(q_ref, k_ref, v_ref, c_ref, ct_ref, o_ref, lse_ref, do_ref,
             dq_ref, dk_ref, dv_ref, dc_ref, qs, kb, vb, dob, dsum, dqa, dra):
        hp = pl.program_id(1)
        qs[...] = (q_ref[...] * SCALE).astype(BF16)
        kb[...] = k_ref[...].astype(BF16)
        vb[...] = v_ref[...].astype(BF16)
        dob[...] = do_ref[...].astype(BF16)
        masks = _head_masks((S, LANE))
        prod = do_ref[...] * o_ref[...]
        d0 = jnp.sum(jnp.where(masks[0], prod, 0.0), axis=1, keepdims=True)
        d1 = jnp.sum(jnp.where(masks[1], prod, 0.0), axis=1, keepdims=True)
        dsum[...] = jnp.where(masks[0], d0, d1)
        dqa[...] = jnp.zeros_like(dqa)
        dra[...] = jnp.zeros_like(dra)
        lane = lax.broadcasted_iota(jnp.int32, (FT, LANE), 1)
        tmask = _head_masks((FT, LANE))
        row = lax.broadcasted_iota(jnp.int32, (FT, FT), 0)
        col = lax.broadcasted_iota(jnp.int32, (FT, FT), 1)
        tn = (((0,), (0,)), ((), ()))

        def kloop(kj, _):
            k0 = pl.multiple_of(kj * FT, FT)
            kt = kb[pl.ds(k0, FT), :]
            vt = vb[pl.ds(k0, FT), :]

            def qloop(qi, carry):
                dka, dva, dca = carry
                q0 = pl.multiple_of(qi * FT, FT)
                qt = qs[pl.ds(q0, FT), :]
                dot = dob[pl.ds(q0, FT), :]
                ct = c_ref[pl.ds(q0, FT), :]
                lt = lse_ref[pl.ds(q0, FT), :]
                dt = dsum[pl.ds(q0, FT), :]
                dqt = jnp.zeros((FT, LANE), F32)
                rsum = []
                for e in range(2):
                    h = 2 * hp + e
                    qm = jnp.where(tmask[e], qt, jnp.zeros_like(qt))
                    dom = jnp.where(tmask[e], dot, jnp.zeros_like(dot))
                    km = jnp.where(tmask[e], kt, jnp.zeros_like(kt))
                    ones = jnp.where(tmask[e], 1.0, 0.0).astype(BF16)
                    cq = jnp.sum(jnp.where(lane == h, ct, 0.0), axis=1, keepdims=True)
                    ck = ct_ref[pl.ds(h, 1), pl.ds(k0, FT)]
                    s = lax.dot_general(qm, kt, (((1,), (1,)), ((), ())), preferred_element_type=F32)
                    s = s + cq - ck
                    s = jnp.where(k0 + col <= q0 + row, s, NEG)
                    p = jnp.exp(s - lt[:, HD * e:HD * e + 1])
                    dp = lax.dot_general(dom, vt, (((1,), (1,)), ((), ())), preferred_element_type=F32)
                    ds = p * (dp - dt[:, HD * e:HD * e + 1])
                    rsum.append(jnp.sum(ds, axis=1, keepdims=True))
                    dsb = ds.astype(BF16)
                    dsl = (ds - dsb.astype(F32)).astype(BF16)
                    dva = dva + lax.dot_general(p.astype(BF16), dom, tn, preferred_element_type=F32)
                    dka = dka + lax.dot_general(dsb, qm, tn, preferred_element_type=F32)
                    dca = dca + lax.dot_general(dsb, ones, tn, preferred_element_type=F32)
                    dca = dca + lax.dot_general(dsl, ones, tn, preferred_element_type=F32)
                    dqt = dqt + jnp.dot(dsb, km, preferred_element_type=F32)
                dqa[pl.ds(q0, FT), :] += dqt
                dra[pl.ds(q0, FT), :] += jnp.where(tmask[0], rsum[0], rsum[1])
                return dka, dva, dca

            z = jnp.zeros((FT, LANE), F32)
            dka, dva, dca = lax.fori_loop(kj, nq, qloop, (z, z, z))
            dk_ref[pl.ds(k0, FT), :] = dka.astype(BF16)
            dv_ref[pl.ds(k0, FT), :] = dva.astype(BF16)
            dc_ref[pl.ds(k0, FT), :] = dca
            return 0

        lax.fori_loop(0, nq, kloop, 0)
        dq_ref[...] = (dqa[...] * SCALE).astype(BF16)
        dc_ref[...] = dc_ref[...] - dra[...]

    def colblk(off):
        return pl.BlockSpec((S, LANE), lambda b, hp: (b, off + hp))

    return pl.pallas_call(
        body, name=name, grid=(nb, NHP),
        in_specs=[colblk(CB_QA), colblk(CB_KA), colblk(CB_VA),
                  pl.BlockSpec((S, LANE), lambda b, hp: (b, 0)), pl.BlockSpec((None, 8, S), lambda b, hp: (b, 0, 0)),
                  colblk(0), colblk(0), colblk(0)],
        out_specs=[colblk(0)] * 4,
        out_shape=[jax.ShapeDtypeStruct((nb * S, WA), BF16)] * 3 + [jax.ShapeDtypeStruct((nb * S, WA), F32)],
        scratch_shapes=[pltpu.VMEM((S, LANE), BF16)] * 4 + [pltpu.VMEM((S, LANE), F32)] * 3,
        compiler_params=_cparams(("parallel", "parallel")),
    )(P, P, P, c, cT, o, lse, do)


DILS = (1, 4, 16)
DB = 128


def _regroup_load(ref, d, scale=None):
    if d == 1:
        v = ref[...]
    else:
        L = S // d
        v = jnp.concatenate([ref[pl.ds(r, L, stride=d), :] for r in range(d)], axis=0)
    return v if scale is None else v * scale


def _regroup_store(ref, d, val_ref, accumulate):
    L = S // d
    for r in range(d):
        src = val_ref[pl.ds(r * L, L), :]
        dst = (slice(None), slice(None)) if d == 1 else (pl.ds(r, L, stride=d), slice(None))
        if accumulate:
            ref[dst] = ref[dst] + src
        else:
            ref[dst] = src


def _dil_valid(bk, d):
    qi = lax.broadcasted_iota(jnp.int32, (DB, 2 * DB), 0)
    ki = lax.broadcasted_iota(jnp.int32, (DB, 2 * DB), 1)
    nper = (S // d) // DB
    has_prev = (bk % nper) > 0
    return (ki >= qi) & (ki <= qi + DB) & ((ki >= DB) | has_prev)


def _dil_fwd(P, nb, name):
    nblk = S // DB

    def body(q_ref, k_ref, v_ref, o_ref, lse_ref, qd, kd, vd, rnum, rm, rl, num_n, m_n, l_n):
        masks = _head_masks((DB, LANE))
        for bi, d in enumerate(DILS):
            qd[...] = _regroup_load(q_ref, d, SCALE).astype(BF16)
            kd[pl.ds(0, DB), :] = jnp.zeros((DB, LANE), BF16)
            vd[pl.ds(0, DB), :] = jnp.zeros((DB, LANE), BF16)
            kd[pl.ds(DB, S), :] = _regroup_load(k_ref, d).astype(BF16)
            vd[pl.ds(DB, S), :] = _regroup_load(v_ref, d).astype(BF16)

            def blk(bk, _, d=d):
                r0 = pl.multiple_of(bk * DB, DB)
                qt = qd[pl.ds(r0, DB), :]
                kk = kd[pl.ds(r0, 2 * DB), :]
                vv = vd[pl.ds(r0, 2 * DB), :]
                valid = _dil_valid(bk, d)
                res = []
                for e in range(2):
                    qm = jnp.where(masks[e], qt, jnp.zeros_like(qt))
                    s = lax.dot_general(qm, kk, (((1,), (1,)), ((), ())), preferred_element_type=F32)
                    s = jnp.where(valid, s, NEG)
                    m = jnp.max(s, axis=1, keepdims=True)
                    p = jnp.exp(s - m)
                    l = jnp.sum(p, axis=1, keepdims=True)
                    num = jnp.dot(p.astype(BF16), vv, preferred_element_type=F32)
                    res.append((m, l, num))
                (m0, l0, n0), (m1, l1, n1) = res
                rnum[pl.ds(r0, DB), :] = jnp.where(masks[0], n0, n1)
                rm[pl.ds(r0, DB), :] = jnp.where(masks[0], m0, m1)
                rl[pl.ds(r0, DB), :] = jnp.where(masks[0], l0, l1)
                return 0

            lax.fori_loop(0, nblk, blk, 0)
            _regroup_store(num_n.at[bi], d, rnum, False)
            _regroup_store(m_n.at[bi], d, rm, False)
            _regroup_store(l_n.at[bi], d, rl, False)

        m_all = jnp.maximum(jnp.maximum(m_n[0], m_n[1]), m_n[2])
        num = jnp.zeros((S, LANE), F32)
        den = jnp.zeros((S, LANE), F32)
        for bi in range(3):
            a = jnp.exp(m_n[bi] - m_all)
            num = num + a * num_n[bi]
            den = den + a * l_n[bi]
        o_ref[...] = num / den
        lse_ref[...] = m_all + jnp.log(den)

    def colblk(off):
        return pl.BlockSpec((S, LANE), lambda b, hp: (b, off + hp))

    return pl.pallas_call(
        body, name=name, grid=(nb, NHP),
        in_specs=[colblk(CB_QB), colblk(CB_KB), colblk(CB_VB)],
        out_specs=[colblk(0), colblk(0)],
        out_shape=[jax.ShapeDtypeStruct((nb * S, WA), F32), jax.ShapeDtypeStruct((nb * S, WA), F32)],
        scratch_shapes=[pltpu.VMEM((S, LANE), BF16), pltpu.VMEM((S + DB, LANE), BF16), pltpu.VMEM((S + DB, LANE), BF16),
                        pltpu.VMEM((S, LANE), F32), pltpu.VMEM((S, LANE), F32), pltpu.VMEM((S, LANE), F32),
                        pltpu.VMEM((3, S, LANE), F32), pltpu.VMEM((3, S, LANE), F32), pltpu.VMEM((3, S, LANE), F32)],
        compiler_params=_cparams(("parallel", "parallel")),
    )(P, P, P)


def _dil_bwd(P, o, lse, do, nb, name):
    nblk = S // DB

    def body(q_ref, k_ref, v_ref, o_ref, lse_ref, do_ref, dq_ref, dk_ref, dv_ref,
             qd, kd, vd, dod, lsed, dsd, dsum, dq_r, dk_r, dv_r, dq_n, dk_n, dv_n):
        masks = _head_masks((DB, LANE))
        fmask = _head_masks((S, LANE))
        prod = do_ref[...] * o_ref[...]
        d0 = jnp.sum(jnp.where(fmask[0], prod, 0.0), axis=1, keepdims=True)
        d1 = jnp.sum(jnp.where(fmask[1], prod, 0.0), axis=1, keepdims=True)
        dsum[...] = jnp.where(fmask[0], d0, d1)
        tn = (((0,), (0,)), ((), ()))
        for bi, d in enumerate(DILS):
            qd[...] = _regroup_load(q_ref, d, SCALE).astype(BF16)
            kd[pl.ds(0, DB), :] = jnp.zeros((DB, LANE), BF16)
            vd[pl.ds(0, DB), :] = jnp.zeros((DB, LANE), BF16)
            kd[pl.ds(DB, S), :] = _regroup_load(k_ref, d).astype(BF16)
            vd[pl.ds(DB, S), :] = _regroup_load(v_ref, d).astype(BF16)
            dod[...] = _regroup_load(do_ref, d).astype(BF16)
            lsed[...] = _regroup_load(lse_ref, d)
            dsd[...] = _regroup_load(dsum, d)
            dq_r[...] = jnp.zeros_like(dq_r)
            dk_r[...] = jnp.zeros_like(dk_r)
            dv_r[...] = jnp.zeros_like(dv_r)

            def blk(bk, _, d=d):
                r0 = pl.multiple_of(bk * DB, DB)
                qt = qd[pl.ds(r0, DB), :]
                dot = dod[pl.ds(r0, DB), :]
                lt = lsed[pl.ds(r0, DB), :]
                dt = dsd[pl.ds(r0, DB), :]
                kk = kd[pl.ds(r0, 2 * DB), :]
                vv = vd[pl.ds(r0, 2 * DB), :]
                kmask = _head_masks((2 * DB, LANE))
                valid = _dil_valid(bk, d)
                dqt = jnp.zeros((DB, LANE), F32)
                dkt = jnp.zeros((2 * DB, LANE), F32)
                dvt = jnp.zeros((2 * DB, LANE), F32)
                for e in range(2):
                    qm = jnp.where(masks[e], qt, jnp.zeros_like(qt))
                    dom = jnp.where(masks[e], dot, jnp.zeros_like(dot))
                    km = jnp.where(kmask[e], kk, jnp.zeros_like(kk))
                    s = lax.dot_general(qm, kk, (((1,), (1,)), ((), ())), preferred_element_type=F32)
                    s = jnp.where(valid, s, NEG)
                    p = jnp.exp(s - lt[:, HD * e:HD * e + 1])
                    dp = lax.dot_general(dom, vv, (((1,), (1,)), ((), ())), preferred_element_type=F32)
                    ds = (p * (dp - dt[:, HD * e:HD * e + 1])).astype(BF16)
                    dvt = dvt + lax.dot_general(p.astype(BF16), dom, tn, preferred_element_type=F32)
                    dkt = dkt + lax.dot_general(ds, qm, tn, preferred_element_type=F32)
                    dqt = dqt + jnp.dot(ds, km, preferred_element_type=F32)
                dq_r[pl.ds(r0, DB), :] = dqt
                dk_r[pl.ds(r0, 2 * DB), :] += dkt
                dv_r[pl.ds(r0, 2 * DB), :] += dvt
                return 0

            lax.fori_loop(0, nblk, blk, 0)
            _regroup_store(dq_n, d, dq_r, bi > 0)
            _regroup_store(dk_n, d, dk_r.at[pl.ds(DB, S)], bi > 0)
            _regroup_store(dv_n, d, dv_r.at[pl.ds(DB, S)], bi > 0)

        dq_ref[...] = (dq_n[...] * SCALE).astype(BF16)
        dk_ref[...] = dk_n[...].astype(BF16)
        dv_ref[...] = dv_n[...].astype(BF16)

    def colblk(off):
        return pl.BlockSpec((S, LANE), lambda b, hp: (b, off + hp))

    big = pltpu.VMEM((S, LANE), F32)
    bigp = pltpu.VMEM((S + DB, LANE), F32)
    return pl.pallas_call(
        body, name=name, grid=(nb, NHP),
        in_specs=[colblk(CB_QB), colblk(CB_KB), colblk(CB_VB), colblk(0), colblk(0), colblk(0)],
        out_specs=[colblk(0)] * 3,
        out_shape=[jax.ShapeDtypeStruct((nb * S, WA), BF16)] * 3,
        scratch_shapes=[pltpu.VMEM((S, LANE), BF16), pltpu.VMEM((S + DB, LANE), BF16), pltpu.VMEM((S + DB, LANE), BF16),
                        pltpu.VMEM((S, LANE), BF16), big, big, big, big, bigp, bigp, big, big, big],
        compiler_params=_cparams(("parallel", "parallel")),
    )(P, P, P, o, lse, do)


RC = 256
CPAD = 32


def _conv_branch_chunk(gpad, r0, cw_ref, cb_ref, cng_ref, cnb_ref):
    acc = jnp.zeros((RC, CC), F32) + cb_ref[...]
    for k in range(CK):
        acc = acc + cw_ref[k:k + 1, :] * gpad[pl.ds(r0 + CPAD - (CK - 1) + k, RC), :]
    mu = jnp.mean(acc, axis=1, keepdims=True)
    xc = acc - mu
    rstd = lax.rsqrt(jnp.mean(xc * xc, axis=1, keepdims=True) + EPS)
    n = xc * rstd
    return n, rstd, n * cng_ref[...] + cnb_ref[...]


NORM_ROWS = 512
YC_BLK = 2 * WA // CC


def _attn_norm_fwd(of, od, gof, god, name):
    T = of.shape[0]

    def body(of_ref, od_ref, gof_ref, god_ref, y_ref):
        for i, (src, g_ref) in enumerate(((of_ref, gof_ref), (od_ref, god_ref))):
            v = src[...]
            r = lax.rsqrt(jnp.mean(v * v, axis=1, keepdims=True) + EPS)
            y_ref[:, i * WA:(i + 1) * WA] = (v * r * g_ref[...]).astype(BF16)

    row = lambda w: pl.BlockSpec((NORM_ROWS, w), lambda i: (i, 0))
    par = pl.BlockSpec((1, WA), lambda i: (0, 0))
    return pl.pallas_call(
        body, name=name, grid=(T // NORM_ROWS,),
        in_specs=[row(WA), row(WA), par, par], out_specs=row(2 * WA),
        out_shape=jax.ShapeDtypeStruct((T, D), BF16),
        compiler_params=_cparams(("parallel",)),
    )(of, od, gof, god)


def _attn_norm_bwd(of, od, dy, gof, god, name):
    T = of.shape[0]

    def body(of_ref, od_ref, dy_ref, gof_ref, god_ref, dof_ref, dod_ref, dgo_ref):
        @pl.when(pl.program_id(0) == 0)
        def _():
            dgo_ref[...] = jnp.zeros_like(dgo_ref)

        for i, (src, g_ref, dst) in enumerate(((of_ref, gof_ref, dof_ref), (od_ref, god_ref, dod_ref))):
            v = src[...]
            dyv = dy_ref[:, i * WA:(i + 1) * WA].astype(F32)
            r = lax.rsqrt(jnp.mean(v * v, axis=1, keepdims=True) + EPS)
            a = dyv * g_ref[...]
            dst[...] = r * a - v * (r * r * r * jnp.mean(v * a, axis=1, keepdims=True))
            dgo_ref[i:i + 1, :] += jnp.sum(dyv * v * r, axis=0, keepdims=True)

    row = lambda w: pl.BlockSpec((NORM_ROWS, w), lambda i: (i, 0))
    par = pl.BlockSpec((1, WA), lambda i: (0, 0))
    return pl.pallas_call(
        body, name=name, grid=(T // NORM_ROWS,),
        in_specs=[row(WA), row(WA), row(2 * WA), par, par],
        out_specs=[row(WA), row(WA), pl.BlockSpec((8, WA), lambda i: (0, 0))],
        out_shape=[jax.ShapeDtypeStruct((T, WA), F32), jax.ShapeDtypeStruct((T, WA), F32),
                   jax.ShapeDtypeStruct((8, WA), F32)],
        compiler_params=_cparams(("arbitrary",)),
    )(of, od, dy, gof, god)


def _conv_specs():
    gblk = lambda off: pl.BlockSpec((S, CC), lambda b: (b, off))
    par = lambda r: pl.BlockSpec((r, CC), lambda b: (0, 0))
    return gblk, par


def _conv_fwd(P, y, cw, cb, cng, cnb, nb, name):
    def body(gv_ref, gg_ref, cw_ref, cb_ref, cng_ref, cnb_ref, y_in, y_ref, gpad):
        del y_in
        gpad[pl.ds(0, CPAD), :] = jnp.zeros((CPAD, CC), F32)
        gpad[pl.ds(CPAD, S), :] = gv_ref[...] * _sigmoid(gg_ref[...])
        for ci in range(S // RC):
            r0 = ci * RC
            _, _, z = _conv_branch_chunk(gpad, r0, cw_ref, cb_ref, cng_ref, cnb_ref)
            y_ref[pl.ds(r0, RC), :] = (z * _sigmoid(z)).astype(BF16)

    gblk, par = _conv_specs()
    return pl.pallas_call(
        body, name=name, grid=(nb,),
        in_specs=[gblk(CB_GV), gblk(CB_GG), par(CPAD), par(1), par(1), par(1), pl.BlockSpec(memory_space=pl.ANY)],
        out_specs=gblk(YC_BLK), out_shape=jax.ShapeDtypeStruct((nb * S, D), BF16),
        input_output_aliases={6: 0},
        scratch_shapes=[pltpu.VMEM((S + CPAD, CC), F32)],
        compiler_params=_cparams(("parallel",)),
    )(P, P, cw, cb, cng, cnb, y)


def _conv_bwd(P, dy, cw, cb, cng, cnb, nb, name):
    def body(gv_ref, gg_ref, dy_ref, cw_ref, cb_ref, cng_ref, cnb_ref, dg_ref, dcw_ref, dsm_ref, gpad, dpad):
        @pl.when(pl.program_id(0) == 0)
        def _():
            dcw_ref[...] = jnp.zeros_like(dcw_ref)
            dsm_ref[...] = jnp.zeros_like(dsm_ref)

        gpad[pl.ds(0, CPAD), :] = jnp.zeros((CPAD, CC), F32)
        gpad[pl.ds(CPAD, S), :] = gv_ref[...] * _sigmoid(gg_ref[...])
        dpad[pl.ds(S, CPAD), :] = jnp.zeros((CPAD, CC), F32)
        zero = jnp.zeros((1, CC), F32)
        dcb, dcng, dcnb = zero, zero, zero
        for ci in range(S // RC):
            r0 = ci * RC
            n, rstd, z = _conv_branch_chunk(gpad, r0, cw_ref, cb_ref, cng_ref, cnb_ref)
            sz = _sigmoid(z)
            dz = dy_ref[pl.ds(r0, RC), :].astype(F32) * (sz * (1.0 + z * (1.0 - sz)))
            dcng = dcng + jnp.sum(dz * n, axis=0, keepdims=True)
            dcnb = dcnb + jnp.sum(dz, axis=0, keepdims=True)
            dn = dz * cng_ref[...]
            dc0 = rstd * (dn - jnp.mean(dn, axis=1, keepdims=True) - n * jnp.mean(dn * n, axis=1, keepdims=True))
            dcb = dcb + jnp.sum(dc0, axis=0, keepdims=True)
            dpad[pl.ds(r0, RC), :] = dc0
        dsm_ref[0:1, :] += dcb
        dsm_ref[1:2, :] += dcng
        dsm_ref[2:3, :] += dcnb

        dws = [zero] * CK
        for ci in range(S // RC):
            r0 = ci * RC
            dct = dpad[pl.ds(r0, RC), :]
            dgl = jnp.zeros((RC, CC), F32)
            for k in range(CK):
                dws[k] = dws[k] + jnp.sum(dct * gpad[pl.ds(r0 + CPAD - (CK - 1) + k, RC), :], axis=0, keepdims=True)
                dgl = dgl + cw_ref[k:k + 1, :] * dpad[pl.ds(r0 + (CK - 1) - k, RC), :]
            sg = _sigmoid(gg_ref[pl.ds(r0, RC), :])
            dg_ref[pl.ds(r0, RC), 0:CC] = (dgl * sg).astype(BF16)
            dg_ref[pl.ds(r0, RC), CC:2 * CC] = (dgl * gv_ref[pl.ds(r0, RC), :] * sg * (1.0 - sg)).astype(BF16)
        for k in range(CK):
            dcw_ref[k:k + 1, :] += dws[k]

    gblk, par = _conv_specs()
    return pl.pallas_call(
        body, name=name, grid=(nb,),
        in_specs=[gblk(CB_GV), gblk(CB_GG), gblk(YC_BLK), par(CPAD), par(1), par(1), par(1)],
        out_specs=[pl.BlockSpec((S, 2 * CC), lambda b: (b, 0)), par(CPAD), par(8)],
        out_shape=[jax.ShapeDtypeStruct((nb * S, 2 * CC), BF16), jax.ShapeDtypeStruct((CPAD, CC), F32),
                   jax.ShapeDtypeStruct((8, CC), F32)],
        scratch_shapes=[pltpu.VMEM((S + CPAD, CC), F32), pltpu.VMEM((S + CPAD, CC), F32)],
        compiler_params=_cparams(("arbitrary",)),
    )(P, P, dy, cw, cb, cng, cnb)


FC = 512
FPAD = 8
NFB = 2 * DFF // FC


def _ffn_u2_chunk(upad, r0, fw_ref, fb_ref):
    acc = jnp.zeros((RC, FC), F32) + fb_ref[...]
    for k in range(FK):
        acc = acc + fw_ref[k:k + 1, :] * upad[pl.ds(r0 + FPAD - (FK - 1) + k, RC), :]
    return acc


def _ffn_fwd(U, fw, fb, nb, name):
    def body(u_ref, fw_ref, fb_ref, h_ref, upad):
        upad[pl.ds(0, FPAD), :] = jnp.zeros((FPAD, FC), F32)
        upad[pl.ds(FPAD, S), :] = u_ref[...].astype(F32)
        for ci in range(S // RC):
            r0 = ci * RC
            u2 = _ffn_u2_chunk(upad, r0, fw_ref, fb_ref)
            a2, b2 = u2[:, :FC // 2], u2[:, FC // 2:]
            h_ref[pl.ds(r0, RC), :] = (a2 * _sigmoid(a2) * b2).astype(BF16)

    return pl.pallas_call(
        body, name=name, grid=(nb, NFB),
        in_specs=[pl.BlockSpec((S, FC), lambda b, j: (b, j)), pl.BlockSpec((8, FC), lambda b, j: (0, j)),
                  pl.BlockSpec((1, FC), lambda b, j: (0, j))],
        out_specs=pl.BlockSpec((S, FC // 2), lambda b, j: (b, j)),
        out_shape=jax.ShapeDtypeStruct((nb * S, DFF), BF16),
        scratch_shapes=[pltpu.VMEM((S + FPAD, FC), F32)],
        compiler_params=_cparams(("parallel", "parallel")),
    )(U, fw, fb)


def _ffn_bwd(U, dhid, fw, fb, nb, name):
    def body(u_ref, dh_ref, fw_ref, fb_ref, du_ref, dfw_ref, upad, dpad):
        @pl.when(pl.program_id(1) == 0)
        def _():
            dfw_ref[...] = jnp.zeros_like(dfw_ref)

        upad[pl.ds(0, FPAD), :] = jnp.zeros((FPAD, FC), F32)
        upad[pl.ds(FPAD, S), :] = u_ref[...].astype(F32)
        dpad[pl.ds(S, FPAD), :] = jnp.zeros((FPAD, FC), F32)
        zero = jnp.zeros((1, FC), F32)
        dbias = zero
        for ci in range(S // RC):
            r0 = ci * RC
            u2 = _ffn_u2_chunk(upad, r0, fw_ref, fb_ref)
            a2, b2 = u2[:, :FC // 2], u2[:, FC // 2:]
            sa = _sigmoid(a2)
            dh = dh_ref[pl.ds(r0, RC), :].astype(F32)
            du2 = jnp.concatenate([dh * b2 * (sa * (1.0 + a2 * (1.0 - sa))), dh * a2 * sa], axis=1)
            dpad[pl.ds(r0, RC), :] = du2
            dbias = dbias + jnp.sum(du2, axis=0, keepdims=True)
        dws = [zero] * FK
        for ci in range(S // RC):
            r0 = ci * RC
            dct = dpad[pl.ds(r0, RC), :]
            du = jnp.zeros((RC, FC), F32)
            for k in range(FK):
                dws[k] = dws[k] + jnp.sum(dct * upad[pl.ds(r0 + FPAD - (FK - 1) + k, RC), :], axis=0, keepdims=True)
                du = du + fw_ref[k:k + 1, :] * dpad[pl.ds(r0 + (FK - 1) - k, RC), :]
            du_ref[pl.ds(r0, RC), :] = du.astype(BF16)
        for k in range(FK):
            dfw_ref[k:k + 1, :] += dws[k]
        dfw_ref[FK:FK + 1, :] += dbias

    return pl.pallas_call(
        body, name=name, grid=(NFB, nb),
        in_specs=[pl.BlockSpec((S, FC), lambda j, b: (b, j)), pl.BlockSpec((S, FC // 2), lambda j, b: (b, j)),
                  pl.BlockSpec((8, FC), lambda j, b: (0, j)), pl.BlockSpec((1, FC), lambda j, b: (0, j))],
        out_specs=[pl.BlockSpec((S, FC), lambda j, b: (b, j)), pl.BlockSpec((8, FC), lambda j, b: (0, j))],
        out_shape=[jax.ShapeDtypeStruct((nb * S, 2 * DFF), BF16), jax.ShapeDtypeStruct((8, 2 * DFF), F32)],
        scratch_shapes=[pltpu.VMEM((S + FPAD, FC), F32), pltpu.VMEM((S + FPAD, FC), F32)],
        compiler_params=_cparams(("parallel", "arbitrary")),
    )(U, dhid, fw, fb)


def _adamw_body(w_ref, g_ref, m_ref, v_ref, d_ref, nm_ref, nv_ref):
    g = g_ref[...]
    m = ADAM_B1 * m_ref[...] + (1.0 - ADAM_B1) * g
    v = ADAM_B2 * v_ref[...] + (1.0 - ADAM_B2) * (g * g)
    m_hat = m / (1.0 - ADAM_B1 ** ADAM_STEP)
    v_hat = v / (1.0 - ADAM_B2 ** ADAM_STEP)
    d_ref[...] = -ADAM_LR * (m_hat / (jnp.sqrt(v_hat) + ADAM_EPS) + ADAM_WD * w_ref[...])
    nm_ref[...] = m
    nv_ref[...] = v


def _adamw(w, g, m, v, name):
    shape = w.shape
    R = 1
    for s in shape[:-1]:
        R *= s
    C = shape[-1]
    args = [a.reshape(R, C) for a in (w, g, m, v)]
    tr = R
    for cand in (512, 352, 256, 128, 64, 32, 16, 8):
        if R % cand == 0 and cand * C * 4 * 14 <= 24 * 1024 * 1024:
            tr = cand
            break
    blk = pl.BlockSpec((tr, C), lambda i: (i, 0))
    outs = pl.pallas_call(
        functools.partial(_adamw_body), name=name, grid=(R // tr,),
        in_specs=[blk] * 4, out_specs=[blk] * 3,
        out_shape=[jax.ShapeDtypeStruct((R, C), F32)] * 3,
        compiler_params=_cparams(("parallel",)),
    )(*args)
    return [o.reshape(shape) for o in outs]


def _add_half(g, r1, c_idx, name):
    _, R, C = g.shape
    H = R // 2
    th = 128 if H % 128 == 0 else 88
    assert H % th == 0
    nh = H // th

    def body(c_ref, g_ref, r_ref, o_ref):
        o_ref[...] = g_ref[...] + r_ref[...]

    grid_spec = pltpu.PrefetchScalarGridSpec(
        num_scalar_prefetch=1, grid=(NCHIP, nh),
        in_specs=[pl.BlockSpec((None, th, C), lambda p, i, c: (p, c[0] * nh + i, 0)),
                  pl.BlockSpec((None, th, C), lambda p, i, c: (p, i, 0))],
        out_specs=pl.BlockSpec((None, th, C), lambda p, i, c: (p, i, 0)))
    return pl.pallas_call(
        body, name=name, grid_spec=grid_spec, out_shape=jax.ShapeDtypeStruct((NCHIP, H, C), F32),
        compiler_params=_cparams(("parallel", "parallel")),
    )(c_idx, g, r1)


def _sum_slots(r2, name):
    _, H, C = r2.shape
    th = 128 if H % 128 == 0 else 88
    assert H % th == 0

    def body(r_ref, o_ref):
        o_ref[...] = ((r_ref[0] + r_ref[1]) + r_ref[2]) + r_ref[3]

    return pl.pallas_call(
        body, name=name, grid=(H // th,),
        in_specs=[pl.BlockSpec((NCHIP, th, C), lambda i: (0, i, 0))],
        out_specs=pl.BlockSpec((th, C), lambda i: (i, 0)),
        out_shape=jax.ShapeDtypeStruct((H, C), F32),
        compiler_params=_cparams(("parallel",)),
    )(r2)


MESH = pl.DeviceIdType.MESH
HBM = pl.BlockSpec(memory_space=pltpu.HBM)


def _place():
    x, y, c = lax.axis_index("x"), lax.axis_index("y"), lax.axis_index("c")
    chips = [(1 - x, y), (x, 1 - y), (1 - x, 1 - y)]
    return x, y, c, chips


def _rcopy(src, dst, ssem, rsem, dev):
    return pltpu.make_async_remote_copy(src_ref=src, dst_ref=dst, send_sem=ssem, recv_sem=rsem,
                                        device_id=dev, device_id_type=MESH)


def _allgather(shards, split):
    n = len(shards)

    def body(*refs):
        ins, outs = refs[:n], refs[n:2 * n]
        ssem, rsem, fssem, frsem, lsem = refs[2 * n:]
        x, y, c, chips = _place()
        me = 2 * x + y
        sib = (x, y, 1 - c)

        def window(t, chip, half):
            if not split[t]:
                return outs[t].at[:, chip]
            H = shards[t].shape[1] // 2
            return outs[t].at[:, chip, pl.ds(half * H, H)]

        pending = []
        for t in range(n):
            cp = pltpu.make_async_copy(ins[t], outs[t].at[:, me], lsem.at[t])
            cp.start()
            pending.append(cp)
        sends = []
        for t in range(n):
            H = shards[t].shape[1] // 2
            src = ins[t].at[:, pl.ds(c * H, H)] if split[t] else ins[t]
            for j, (cx, cy) in enumerate(chips):
                cp = _rcopy(src, window(t, me, c), ssem.at[3 * t + j], rsem.at[3 * t + j], (cx, cy, c))
                cp.start()
                sends.append(cp)
        for t in range(n):
            for j, (cx, cy) in enumerate(chips):
                win = window(t, 2 * cx + cy, c)
                _rcopy(win, win, ssem.at[3 * t + j], rsem.at[3 * t + j], (cx, cy, c)).wait_recv()
                if split[t]:
                    cp = _rcopy(win, win, fssem.at[3 * t + j], frsem.at[3 * t + j], sib)
                    cp.start()
                    sends.append(cp)
        for t in range(n):
            if split[t]:
                for j, (cx, cy) in enumerate(chips):
                    win = window(t, 2 * cx + cy, 1 - c)
                    _rcopy(win, win, fssem.at[3 * t + j], frsem.at[3 * t + j], sib).wait_recv()
        for cp in sends:
            cp.wait_send()
        for cp in pending:
            cp.wait()

    out_shape = [jax.ShapeDtypeStruct((s.shape[0], NCHIP) + s.shape[1:], s.dtype) for s in shards]
    return pl.pallas_call(
        body, name="allgather_weights", in_specs=[HBM] * n, out_specs=[HBM] * n, out_shape=out_shape,
        scratch_shapes=[pltpu.SemaphoreType.DMA((3 * n,))] * 4 + [pltpu.SemaphoreType.DMA((n,))],
    )(*shards)


def _rs_pair_exchange(gs):
    n = len(gs)

    def body(*refs):
        ins, outs = refs[:n], refs[n:2 * n]
        ssem, rsem = refs[2 * n:]
        x, y, c, _ = _place()
        cps = []
        for t in range(n):
            H = gs[t].shape[1] // 2
            cp = _rcopy(ins[t].at[:, pl.ds((1 - c) * H, H)], outs[t], ssem.at[t], rsem.at[t], (x, y, 1 - c))
            cp.start()
            cps.append(cp)
        for cp in cps:
            cp.wait_recv()
        for cp in cps:
            cp.wait_send()

    out_shape = [jax.ShapeDtypeStruct((NCHIP, g.shape[1] // 2, g.shape[2]), F32) for g in gs]
    return pl.pallas_call(
        body, name="rs_pair_exchange", in_specs=[HBM] * n, out_specs=[HBM] * n, out_shape=out_shape,
        scratch_shapes=[pltpu.SemaphoreType.DMA((n,))] * 2,
    )(*gs)


def _rs_chip_scatter(hs):
    n = len(hs)

    def body(*refs):
        ins, outs = refs[:n], refs[n:2 * n]
        ssem, rsem, lsem = refs[2 * n:]
        x, y, c, chips = _place()
        me = 2 * x + y
        pending, sends = [], []
        for t in range(n):
            cp = pltpu.make_async_copy(ins[t].at[me], outs[t].at[me], lsem.at[t])
            cp.start()
            pending.append(cp)
            for j, (cx, cy) in enumerate(chips):
                cp = _rcopy(ins[t].at[2 * cx + cy], outs[t].at[me], ssem.at[3 * t + j], rsem.at[3 * t + j], (cx, cy, c))
                cp.start()
                sends.append(cp)
        for t in range(n):
            for j, (cx, cy) in enumerate(chips):
                win = outs[t].at[2 * cx + cy]
                _rcopy(win, win, ssem.at[3 * t + j], rsem.at[3 * t + j], (cx, cy, c)).wait_recv()
        for cp in sends:
            cp.wait_send()
        for cp in pending:
            cp.wait()

    out_shape = [jax.ShapeDtypeStruct(h.shape, F32) for h in hs]
    return pl.pallas_call(
        body, name="rs_chip_scatter", in_specs=[HBM] * n, out_specs=[HBM] * n, out_shape=out_shape,
        scratch_shapes=[pltpu.SemaphoreType.DMA((3 * n,))] * 2 + [pltpu.SemaphoreType.DMA((n,))],
    )(*hs)


def _rs_pair_gather(fs):
    n = len(fs)

    def body(*refs):
        ins, outs = refs[:n], refs[n:2 * n]
        ssem, rsem, lsem = refs[2 * n:]
        x, y, c, _ = _place()
        pending, sends = [], []
        for t in range(n):
            cp = pltpu.make_async_copy(ins[t], outs[t].at[c], lsem.at[t])
            cp.start()
            pending.append(cp)
            cp = _rcopy(ins[t], outs[t].at[c], ssem.at[t], rsem.at[t], (x, y, 1 - c))
            cp.start()
            sends.append(cp)
        for t in range(n):
            win = outs[t].at[1 - c]
            _rcopy(win, win, ssem.at[t], rsem.at[t], (x, y, 1 - c)).wait_recv()
        for cp in sends:
            cp.wait_send()
        for cp in pending:
            cp.wait()

    out_shape = [jax.ShapeDtypeStruct((2,) + f.shape, F32) for f in fs]
    return pl.pallas_call(
        body, name="rs_pair_gather", in_specs=[HBM] * n, out_specs=[HBM] * n, out_shape=out_shape,
        scratch_shapes=[pltpu.SemaphoreType.DMA((n,))] * 3,
    )(*fs)


def _allreduce_small(buf):
    R = buf.shape[0]

    def body(in_ref, out_ref, slots, ssem, rsem):
        x, y, c, _ = _place()
        me = 4 * x + 2 * y + c
        slots[me] = in_ref[...]
        cps = []
        for k in range(1, NDEV):
            px = 1 - x if k & 4 else x
            py = 1 - y if k & 2 else y
            pc = 1 - c if k & 1 else c
            cp = _rcopy(in_ref, slots.at[me], ssem.at[k - 1], rsem.at[k - 1], (px, py, pc))
            cp.start()
            cps.append((cp, 4 * px + 2 * py + pc))
        for k, (cp, peer) in enumerate(cps):
            _rcopy(in_ref, slots.at[peer], ssem.at[k], rsem.at[k], (x, y, c)).wait_recv()
        for cp, _ in cps:
            cp.wait_send()
        acc = slots[0]
        for p in range(1, NDEV):
            acc = acc + slots[p]
        out_ref[...] = acc

    return pl.pallas_call(
        body, name="allreduce_small", out_shape=jax.ShapeDtypeStruct((R, LANE), F32),
        in_specs=[pl.BlockSpec(memory_space=pltpu.VMEM)], out_specs=pl.BlockSpec(memory_space=pltpu.VMEM),
        scratch_shapes=[pltpu.VMEM((NDEV, R, LANE), F32), pltpu.SemaphoreType.DMA((NDEV - 1,)),
                        pltpu.SemaphoreType.DMA((NDEV - 1,))],
        compiler_params=pltpu.CompilerParams(vmem_limit_bytes=VMEM_LIMIT),
    )(buf)


def _interleave(a):
    lead = a.shape[:-1]
    return a.reshape(*lead, 2, NFB, FC // 2).swapaxes(-3, -2).reshape(*lead, 2 * DFF)


def _uninterleave(a):
    lead = a.shape[:-1]
    return a.reshape(*lead, NFB, 2, FC // 2).swapaxes(-3, -2).reshape(*lead, 2 * DFF)


N_QKV = 3 * WA
N_FG = 2 * NHP


def _pack_in_cols(w):
    pad = jnp.zeros(w.shape[:-1] + (NP - NIN,), w.dtype)
    return jnp.concatenate([w[..., :N_QKV], w[..., N_QKV + N_FG:], w[..., N_QKV:N_QKV + N_FG], pad], axis=-1)


def _unpack_in_cols(g):
    return jnp.concatenate([g[..., :N_QKV], g[..., NIN - N_FG:NIN], g[..., N_QKV:NIN - N_FG]], axis=-1)


def _train_compute(xt, tgt, W, nb):
    saved = []
    xc = xt
    for l in range(DEPTH):
        t = f"_l{l}"
        h = _rms_fwd(xc, W["ln1"][l], "rms1_fwd" + t)
        P = _matmul(h, W["in"][l], tm=1024, tn=1024, tk=D, name="proj_in" + t)
        c, cT = _forget_fwd(P, W["bf"][l], nb, "forget_fwd" + t)
        of, lsef = _fox_fwd(P, c, cT, nb, "fox_fwd" + t)
        od, lsed = _dil_fwd(P, nb, "dil_fwd" + t)
        convp = (W["cw"][l], W["cb"][l], W["cng"][l], W["cnb"][l])
        y = _attn_norm_fwd(of, od, W["gof"][l], W["god"][l], "attn_norm_fwd" + t)
        y = _conv_fwd(P, y, *convp, nb, "conv_fwd" + t)
        xm = _matmul(y, W["o"][l], add=xc, tm=1024, tn=1024, tk=D, name="proj_out" + t)
        h2 = _rms_fwd(xm, W["ln2"][l], "rms2_fwd" + t)
        U = _matmul(h2, W["up"][l], out_dtype=BF16, tm=1024, tn=FC, tk=D, name="ffn_up" + t)
        hid = _ffn_fwd(U, W["fw"][l], W["fb"][l], nb, "ffn_act_fwd" + t)
        xo = _matmul(hid, W["down"][l], add=xm, tm=1024, tn=512, tk=DFF, name="ffn_down" + t)
        saved.append((xc, h, P, c, cT, of, lsef, od, lsed, convp, y, xm, h2, U, hid))
        xc = xo

    loss8, dx, dxb, dgfin = _loss_head(xc, W["gfin"], tgt, "loss_head")

    big = [None] * DEPTH
    small = [None] * DEPTH
    for l in reversed(range(DEPTH)):
        t = f"_l{l}"
        xin, h, P, c, cT, of, lsef, od, lsed, convp, y, xm, h2, U, hid = saved[l]
        dhid = _matmul(dxb, W["down"][l], tb=True, out_dtype=BF16, tm=1024, tn=DFF // 2, tk=D, name="ffn_down_dx" + t)
        dWd = _matmul(hid, dxb, ta=True, tm=DFF // 2, tn=D, tk=512, name="ffn_down_dw" + t)
        dU, dfw = _ffn_bwd(U, dhid, W["fw"][l], W["fb"][l], nb, "ffn_act_bwd" + t)
        dh2 = _matmul(dU, W["up"][l], tb=True, tm=1024, tn=D, tk=DFF, name="ffn_up_dx" + t)
        dWup = _matmul(h2, dU, ta=True, tm=D, tn=DFF // 2, tk=512, name="ffn_up_dw" + t)
        dxm, dxmb, dln2 = _rms_bwd(xm, W["ln2"][l], dh2, dx, "rms2_bwd" + t)
        dy = _matmul(dxmb, W["o"][l], tb=True, out_dtype=BF16, tm=1024, tn=D, tk=D, name="proj_out_dx" + t)
        dWo = _matmul(y, dxmb, ta=True, tm=D, tn=D, tk=512, name="proj_out_dw" + t)
        dof, dod, dgo = _attn_norm_bwd(of, od, dy, W["gof"][l], W["god"][l], "attn_norm_bwd" + t)
        dgvgg, dcw, dsm = _conv_bwd(P, dy, *convp, nb, "conv_bwd" + t)
        dqa, dka, dva, dcb = _fox_bwd(P, c, cT, of, lsef, dof, nb, "fox_bwd" + t)
        dfa, dbf = _forget_bwd(P, W["bf"][l], dcb, nb, "forget_bwd" + t)
        dqb, dkb, dvb = _dil_bwd(P, od, lsed, dod, nb, "dil_bwd" + t)
        dP = jnp.concatenate([dqa, dka, dva, dqb, dkb, dvb, dgvgg, dfa, jnp.zeros_like(dfa)], axis=1)
        dh = _matmul(dP, W["in"][l], tb=True, tm=1024, tn=D, tk=NP, name="proj_in_dx" + t)
        dWin = _matmul(h, dP, ta=True, tm=D, tn=1024, tk=512, name="proj_in_dw" + t)
        dx, dxb, dln1 = _rms_bwd(xin, W["ln1"][l], dh, dxm, "rms1_bwd" + t)
        big[l] = (dWin, dWo, dWup, dWd)
        small[l] = (dln1, dbf, dgo, dcw, dsm, dln2, dfw)
    return loss8, dx, big, small, dgfin


_SMALL_ROWS = (D // LANE, 8, 8 * WA // LANE, CPAD * CC // LANE, 8 * CC // LANE, D // LANE, 8 * 2 * DFF // LANE)


def kernel(x, ln1_g, w_in, b_forget, g_out_fox, g_out_dil, conv_w, conv_b, cnorm_g, cnorm_b, w_o, ln2_g, w_up, ffn_conv_w, ffn_conv_b, w_down, g_final, loss_target, m_ln1_g, m_w_in, m_b_forget, m_g_out_fox, m_g_out_dil, m_conv_w, m_conv_b, m_cnorm_g, m_cnorm_b, m_w_o, m_ln2_g, m_w_up, m_ffn_conv_w, m_ffn_conv_b, m_w_down, m_g_final, v_ln1_g, v_w_in, v_b_forget, v_g_out_fox, v_g_out_dil, v_conv_w, v_conv_b, v_cnorm_g, v_cnorm_b, v_w_o, v_ln2_g, v_w_up, v_ffn_conv_w, v_ffn_conv_b, v_w_down, v_g_final):
    names = ("ln1_g", "w_in", "b_forget", "g_out_fox", "g_out_dil", "conv_w", "conv_b", "cnorm_g", "cnorm_b",
             "w_o", "ln2_g", "w_up", "ffn_conv_w", "ffn_conv_b", "w_down", "g_final")
    w = dict(zip(names, (ln1_g, w_in, b_forget, g_out_fox, g_out_dil, conv_w, conv_b, cnorm_g, cnorm_b,
                         w_o, ln2_g, w_up, ffn_conv_w, ffn_conv_b, w_down, g_final)))
    m = dict(zip(names, (m_ln1_g, m_w_in, m_b_forget, m_g_out_fox, m_g_out_dil, m_conv_w, m_conv_b, m_cnorm_g,
                         m_cnorm_b, m_w_o, m_ln2_g, m_w_up, m_ffn_conv_w, m_ffn_conv_b, m_w_down, m_g_final)))
    v = dict(zip(names, (v_ln1_g, v_w_in, v_b_forget, v_g_out_fox, v_g_out_dil, v_conv_w, v_conv_b, v_cnorm_g,
                         v_cnorm_b, v_w_o, v_ln2_g, v_w_up, v_ffn_conv_w, v_ffn_conv_b, v_w_down, v_g_final)))
    nb = x.shape[0]
    T = nb * S
    xi, yi, ci = lax.axis_index("x"), lax.axis_index("y"), lax.axis_index("c")
    chip = 2 * xi + yi
    cw_cols = CC // NCHIP
    up_cols = 2 * DFF // NCHIP

    shards = [_pack_in_cols(w_in).astype(BF16), w_o.astype(BF16), w_up.astype(BF16), w_down.astype(BF16),
              jnp.pad(ffn_conv_w, ((0, 0), (0, 8 - FK), (0, 0))),
              jnp.pad(conv_w, ((0, 0), (0, CPAD - CK), (0, LANE - cw_cols)))]
    g_in, g_o, g_up, g_dn, g_fw, g_cw = _allgather(shards, (True, True, True, True, False, False))
    up_full = _interleave(g_up.transpose(0, 2, 1, 3).reshape(DEPTH, D, 2 * DFF))
    fw_full = _interleave(g_fw.transpose(0, 2, 1, 3).reshape(DEPTH, 8, 2 * DFF))
    cw_full = g_cw[..., :cw_cols].transpose(0, 2, 1, 3).reshape(DEPTH, CPAD, CC)
    fb_full = _interleave(ffn_conv_b)
    W = {
        "in": [g_in[l].reshape(D, NP) for l in range(DEPTH)],
        "o": [g_o[l].reshape(D, D) for l in range(DEPTH)],
        "up": [up_full[l] for l in range(DEPTH)],
        "down": [g_dn[l].reshape(DFF, D) for l in range(DEPTH)],
        "ln1": [ln1_g[l] for l in range(DEPTH)],
        "ln2": [ln2_g[l] for l in range(DEPTH)],
        "bf": [jnp.pad(b_forget[l], (0, LANE - N_FG)).reshape(1, LANE) for l in range(DEPTH)],
        "gof": [g_out_fox[l].reshape(1, WA) for l in range(DEPTH)],
        "god": [g_out_dil[l].reshape(1, WA) for l in range(DEPTH)],
        "cw": [cw_full[l] for l in range(DEPTH)],
        "cb": [conv_b[l].reshape(1, CC) for l in range(DEPTH)],
        "cng": [cnorm_g[l].reshape(1, CC) for l in range(DEPTH)],
        "cnb": [cnorm_b[l].reshape(1, CC) for l in range(DEPTH)],
        "fw": [fw_full[l] for l in range(DEPTH)],
        "fb": [fb_full[l].reshape(1, 2 * DFF) for l in range(DEPTH)],
        "gfin": g_final,
    }

    loss8, dx, big, small, dgfin = _train_compute(x.reshape(T, D), loss_target.reshape(T, D), W, nb)

    gs = []
    for l in range(DEPTH):
        dWin, dWo, dWup, dWd = big[l]
        gs += [dWin.reshape(NCHIP, D // NCHIP, NP), dWo.reshape(NCHIP, D // NCHIP, D),
               _uninterleave(dWup).reshape(D, NCHIP, up_cols).transpose(1, 0, 2),
               dWd.reshape(NCHIP, DFF // NCHIP, D)]
    r1 = _rs_pair_exchange(gs)
    c_idx = ci.reshape(1).astype(jnp.int32)
    hs = [_add_half(g, r, c_idx, f"rs_add_pair_{i}") for i, (g, r) in enumerate(zip(gs, r1))]
    r2 = _rs_chip_scatter(hs)
    fs = [_sum_slots(r, f"rs_add_chips_{i}") for i, r in enumerate(r2)]
    red = _rs_pair_gather(fs)
    red = [r.reshape(r.shape[0] * r.shape[1], r.shape[2]) for r in red]
    grads = {
        "w_in": jnp.stack([_unpack_in_cols(red[4 * l]) for l in range(DEPTH)]),
        "w_o": jnp.stack([red[4 * l + 1] for l in range(DEPTH)]),
        "w_up": jnp.stack([red[4 * l + 2] for l in range(DEPTH)]),
        "w_down": jnp.stack([red[4 * l + 3] for l in range(DEPTH)]),
    }

    parts = []
    for l in range(DEPTH):
        parts += [p.reshape(-1, LANE) for p in small[l]]
    parts += [dgfin.reshape(-1, LANE), loss8]
    tot = _allreduce_small(jnp.concatenate(parts, axis=0))
    off = 0
    per_layer = []
    for l in range(DEPTH):
        vals = []
        for rows in _SMALL_ROWS:
            vals.append(tot[off:off + rows])
            off += rows
        per_layer.append(vals)
    gfin_sum = tot[off:off + D // LANE].reshape(D)
    loss = tot[off + D // LANE, 0]

    def layer_stack(fn):
        return jnp.stack([fn(*per_layer[l]) for l in range(DEPTH)])

    fw_sum = layer_stack(lambda a, b, c_, d, e, f, g: _uninterleave(g.reshape(8, 2 * DFF)))
    cw_sum = layer_stack(lambda a, b, c_, d, e, f, g: d.reshape(CPAD, CC)[:CK])
    sm_sum = layer_stack(lambda a, b, c_, d, e, f, g: e.reshape(8, CC))
    go_sum = layer_stack(lambda a, b, c_, d, e, f, g: c_.reshape(8, WA))
    grads.update({
        "ln1_g": layer_stack(lambda a, b, c_, d, e, f, g: a.reshape(D)),
        "b_forget": layer_stack(lambda a, b, c_, d, e, f, g: b[0, :N_FG]),
        "g_out_fox": go_sum[:, 0],
        "g_out_dil": go_sum[:, 1],
        "conv_w": lax.dynamic_slice_in_dim(cw_sum, chip * cw_cols, cw_cols, axis=2),
        "conv_b": sm_sum[:, 0],
        "cnorm_g": sm_sum[:, 1],
        "cnorm_b": sm_sum[:, 2],
        "ln2_g": layer_stack(lambda a, b, c_, d, e, f, g: f.reshape(D)),
        "ffn_conv_w": lax.dynamic_slice_in_dim(fw_sum[:, :FK], chip * up_cols, up_cols, axis=2),
        "ffn_conv_b": fw_sum[:, FK],
        "g_final": gfin_sum,
    })

    delta, new_m, new_v = {}, {}, {}
    for n in names:
        delta[n], new_m[n], new_v[n] = _adamw(w[n], grads[n], m[n], v[n], "adamw_" + n)
    return (loss, dx.reshape(nb, S, D), *[grads[n] for n in names], *[delta[n] for n in names],
            *[new_m[n] for n in names], *[new_v[n] for n in names])
```

```python
import functools

import jax
import jax.numpy as jnp
from jax import lax
from jax.experimental import pallas as pl
from jax.experimental.pallas import tpu as pltpu

F32 = jnp.float32
BF16 = jnp.bfloat16

D = 1024
S = 2048
DEPTH = 2
HD = 64
WA = 384
NHP = 3
CC = 256
CK = 31
FK = 3
DFF = 2816
NIN = 2822
NP = 3072
SCALE = 0.125
EPS = 1e-6
NEG = -1e30
NCHIP = 4
NDEV = 8
LANE = 128

CB_QA, CB_KA, CB_VA, CB_QB, CB_KB, CB_VB = 0, 3, 6, 9, 12, 15
CB_GV, CB_GG = 9, 10
CB_FA = 22

ADAM_LR, ADAM_B1, ADAM_B2, ADAM_EPS, ADAM_WD, ADAM_STEP = 0.001, 0.9, 0.999, 1e-08, 0.01, 10

VMEM_LIMIT = 56 * 1024 * 1024


def _cparams(sem=None):
    return pltpu.CompilerParams(dimension_semantics=sem, vmem_limit_bytes=VMEM_LIMIT)


def _split3(x):
    hi = x.astype(BF16)
    r1 = x - hi.astype(F32)
    mid = r1.astype(BF16)
    lo = (r1 - mid.astype(F32)).astype(BF16)
    return hi, mid, lo


def _sigmoid(z):
    return 0.5 * jnp.tanh(0.5 * z) + 0.5


def _matmul(a, b, *, ta=False, tb=False, out_dtype=F32, add=None, tm, tn, tk, name):
    M = a.shape[1] if ta else a.shape[0]
    K = a.shape[0] if ta else a.shape[1]
    N = b.shape[0] if tb else b.shape[1]
    assert (b.shape[1] if tb else b.shape[0]) == K
    assert M % tm == 0 and N % tn == 0 and K % tk == 0, (M, N, K, tm, tn, tk)
    nk = K // tk
    dn = (((0 if ta else 1,), (1 if tb else 0,)), ((), ()))

    def body(*refs):
        if add is not None:
            a_ref, b_ref, add_ref, o_ref, acc = refs
        else:
            a_ref, b_ref, o_ref, acc = refs
        k = pl.program_id(2)
        prod = lax.dot_general(a_ref[...].astype(BF16), b_ref[...].astype(BF16), dn, preferred_element_type=F32)

        def finish(r):
            if add is not None:
                r = r + add_ref[...]
            o_ref[...] = r.astype(o_ref.dtype)

        if nk == 1:
            finish(prod)
        else:
            @pl.when(k == 0)
            def _():
                acc[...] = prod

            @pl.when(k > 0)
            def _():
                acc[...] += prod

            @pl.when(k == nk - 1)
            def _():
                finish(acc[...])

    a_spec = pl.BlockSpec((tk, tm), lambda i, j, k: (k, i)) if ta else pl.BlockSpec((tm, tk), lambda i, j, k: (i, k))
    b_spec = pl.BlockSpec((tn, tk), lambda i, j, k: (j, k)) if tb else pl.BlockSpec((tk, tn), lambda i, j, k: (k, j))
    o_spec = pl.BlockSpec((tm, tn), lambda i, j, k: (i, j))
    in_specs = [a_spec, b_spec]
    args = [a, b]
    if add is not None:
        in_specs.append(o_spec)
        args.append(add)
    return pl.pallas_call(
        body, name=name, grid=(M // tm, N // tn, nk),
        in_specs=in_specs, out_specs=o_spec,
        out_shape=jax.ShapeDtypeStruct((M, N), out_dtype),
        scratch_shapes=[pltpu.VMEM((tm, tn) if nk > 1 else (8, 128), F32)],
        compiler_params=_cparams(("parallel", "parallel", "arbitrary")),
    )(*args)


def _rms_fwd(x, g, name):
    T = x.shape[0]
    tr = 512

    def body(x_ref, g_ref, h_ref):
        xv = x_ref[...]
        r = lax.rsqrt(jnp.mean(xv * xv, axis=1, keepdims=True) + EPS)
        h_ref[...] = (xv * r * g_ref[...]).astype(BF16)

    return pl.pallas_call(
        body, name=name, grid=(T // tr,),
        in_specs=[pl.BlockSpec((tr, D), lambda i: (i, 0)), pl.BlockSpec((1, D), lambda i: (0, 0))],
        out_specs=pl.BlockSpec((tr, D), lambda i: (i, 0)),
        out_shape=jax.ShapeDtypeStruct((T, D), BF16),
        compiler_params=_cparams(("parallel",)),
    )(x, g.reshape(1, D))


def _rms_bwd(x, g, dh, dres, name):
    T = x.shape[0]
    tr = 512

    def body(x_ref, g_ref, dh_ref, dres_ref, dx_ref, dxb_ref, dg_ref):
        i = pl.program_id(0)
        xv = x_ref[...]
        dhv = dh_ref[...].astype(F32)
        r = lax.rsqrt(jnp.mean(xv * xv, axis=1, keepdims=True) + EPS)
        a = dhv * g_ref[...]
        dx = dres_ref[...] + r * a - xv * (r * r * r * jnp.mean(xv * a, axis=1, keepdims=True))
        dx_ref[...] = dx
        dxb_ref[...] = dx.astype(BF16)
        part = jnp.sum(dhv * xv * r, axis=0, keepdims=True)

        @pl.when(i == 0)
        def _():
            dg_ref[...] = part

        @pl.when(i > 0)
        def _():
            dg_ref[...] += part

    row = pl.BlockSpec((tr, D), lambda i: (i, 0))
    vec = pl.BlockSpec((1, D), lambda i: (0, 0))
    return pl.pallas_call(
        body, name=name, grid=(T // tr,),
        in_specs=[row, vec, row, row], out_specs=[row, row, vec],
        out_shape=[jax.ShapeDtypeStruct((T, D), F32), jax.ShapeDtypeStruct((T, D), BF16),
                   jax.ShapeDtypeStruct((1, D), F32)],
        compiler_params=_cparams(("arbitrary",)),
    )(x, g.reshape(1, D), dh, dres)


def _loss_head(x, g, target, name):
    T = x.shape[0]
    tr = 512

    def body(x_ref, g_ref, t_ref, loss_ref, dx_ref, dxb_ref, dg_ref):
        i = pl.program_id(0)
        xv = x_ref[...]
        gv = g_ref[...]
        r = lax.rsqrt(jnp.mean(xv * xv, axis=1, keepdims=True) + EPS)
        n = xv * r
        err = n * gv - t_ref[...]
        lpart = 0.5 * jnp.sum(jnp.mean(err * err, axis=1, keepdims=True), axis=0, keepdims=True)
        dy = err * (1.0 / D)
        a = dy * gv
        dx = r * a - xv * (r * r * r * jnp.mean(xv * a, axis=1, keepdims=True))
        dx_ref[...] = dx
        dxb_ref[...] = dx.astype(BF16)
        part = jnp.sum(dy * n, axis=0, keepdims=True)
        lfull = jnp.broadcast_to(lpart, (8, LANE))

        @pl.when(i == 0)
        def _():
            dg_ref[...] = part
            loss_ref[...] = lfull

        @pl.when(i > 0)
        def _():
            dg_ref[...] += part
            loss_ref[...] += lfull

    row = pl.BlockSpec((tr, D), lambda i: (i, 0))
    vec = pl.BlockSpec((1, D), lambda i: (0, 0))
    lsp = pl.BlockSpec((8, LANE), lambda i: (0, 0))
    return pl.pallas_call(
        body, name=name, grid=(T // tr,),
        in_specs=[row, vec, row], out_specs=[lsp, row, row, vec],
        out_shape=[jax.ShapeDtypeStruct((8, LANE), F32), jax.ShapeDtypeStruct((T, D), F32),
                   jax.ShapeDtypeStruct((T, D), BF16), jax.ShapeDtypeStruct((1, D), F32)],
        compiler_params=_cparams(("arbitrary",)),
    )(x, g.reshape(1, D), target)


CUM_BLK = 256


def _tri(n, upper):
    r = lax.broadcasted_iota(jnp.int32, (n, n), 0)
    c = lax.broadcasted_iota(jnp.int32, (n, n), 1)
    return jnp.where((c >= r) if upper else (c <= r), 1.0, 0.0).astype(BF16)


def _tri_apply(tri, x):
    hi, mid, lo = _split3(x)
    out = jnp.dot(tri, hi, preferred_element_type=F32)
    out = out + jnp.dot(tri, mid, preferred_element_type=F32)
    return out + jnp.dot(tri, lo, preferred_element_type=F32)


def _forget_fwd(P, bf_pad, nb, name):
    nblk = S // CUM_BLK

    def body(fa_ref, b_ref, c_ref):
        tri = _tri(CUM_BLK, upper=False)
        carry = jnp.zeros((1, LANE), F32)
        for i in range(nblk):
            z = fa_ref[pl.ds(i * CUM_BLK, CUM_BLK), :] + b_ref[...]
            lf = jnp.minimum(z, 0.0) - jnp.log(1.0 + jnp.exp(-jnp.abs(z)))
            cb = _tri_apply(tri, lf) + carry
            c_ref[pl.ds(i * CUM_BLK, CUM_BLK), :] = cb
            carry = cb[CUM_BLK - 1:CUM_BLK, :]

    return pl.pallas_call(
        body, name=name, grid=(nb,),
        in_specs=[pl.BlockSpec((S, LANE), lambda b: (b, CB_FA)), pl.BlockSpec((1, LANE), lambda b: (0, 0))],
        out_specs=pl.BlockSpec((S, LANE), lambda b: (b, 0)),
        out_shape=jax.ShapeDtypeStruct((nb * S, LANE), F32),
        compiler_params=_cparams(("parallel",)),
    )(P, bf_pad)


def _forget_bwd(P, bf_pad, dcb, nb, name):
    nblk = S // CUM_BLK

    def body(fa_ref, b_ref, dc_ref, dfa_ref, db_ref):
        b = pl.program_id(0)
        tri = _tri(CUM_BLK, upper=True)
        lane = lax.broadcasted_iota(jnp.int32, (CUM_BLK, LANE), 1)
        carry = jnp.zeros((1, LANE), F32)
        dbacc = jnp.zeros((1, LANE), F32)
        for i in reversed(range(nblk)):
            rows = pl.ds(i * CUM_BLK, CUM_BLK)
            dc = jnp.zeros((CUM_BLK, LANE), F32)
            dcv = dc_ref[rows, :]
            for h in range(2 * NHP):
                dc = jnp.where(lane == h, -dcv[:, HD * h:HD * h + 1], dc)
            dl = _tri_apply(tri, dc) + carry
            carry = dl[0:1, :]
            z = fa_ref[rows, :] + b_ref[...]
            dz = jnp.where(lane < 2 * NHP, dl * (1.0 - _sigmoid(z)), 0.0)
            dfa_ref[rows, :] = dz.astype(BF16)
            dbacc = dbacc + jnp.sum(dz, axis=0, keepdims=True)

        dbfull = jnp.broadcast_to(dbacc, (8, LANE))

        @pl.when(b == 0)
        def _():
            db_ref[...] = dbfull

        @pl.when(b > 0)
        def _():
            db_ref[...] += dbfull

    return pl.pallas_call(
        body, name=name, grid=(nb,),
        in_specs=[pl.BlockSpec((S, LANE), lambda b: (b, CB_FA)), pl.BlockSpec((1, LANE), lambda b: (0, 0)),
                  pl.BlockSpec((S, WA), lambda b: (b, 0))],
        out_specs=[pl.BlockSpec((S, LANE), lambda b: (b, 0)), pl.BlockSpec((8, LANE), lambda b: (0, 0))],
        out_shape=[jax.ShapeDtypeStruct((nb * S, LANE), BF16), jax.ShapeDtypeStruct((8, LANE), F32)],
        compiler_params=_cparams(("arbitrary",)),
    )(P, bf_pad, dcb)


FQ = 128
NT_DIMS = (((1,), (1,)), ((), ()))


def _head_masks(shape):
    lane = lax.broadcasted_iota(jnp.int32, shape, 1)
    return lane < HD, lane >= HD


def _fox_operands(q_ref, k_ref, c_ref, hp, qa, ka):
    lane = lax.broadcasted_iota(jnp.int32, (S, LANE), 1)
    masks = _head_masks((S, LANE))
    qv = (q_ref[...] * SCALE).astype(BF16)
    kv = k_ref[...].astype(BF16)
    cv = c_ref[...]
    one = jnp.ones((S, 1), BF16)
    zero = jnp.zeros((S, LANE), BF16)
    for e in range(2):
        base = HD * (1 - e)
        ccol = jnp.sum(jnp.where(lane == 2 * hp + e, cv, 0.0), axis=1, keepdims=True)
        hi, mid, lo = _split3(ccol)
        qe = jnp.where(masks[e], qv, zero)
        ke = jnp.where(masks[e], kv, zero)
        for i, (a, b) in enumerate(((hi, one), (mid, one), (lo, one), (one, -hi), (one, -mid), (one, -lo))):
            qe = jnp.where(lane == base + i, a, qe)
            ke = jnp.where(lane == base + i, b, ke)
        qa[e][...] = qe
        ka[e][...] = ke


def _fox_fwd(P, c, nb, name):
    def body(q_ref, k_ref, v_ref, c_ref, o_ref, lse_ref, qa0, qa1, ka0, ka1, va0, va1):
        hp = pl.program_id(1)
        qa, ka, va = (qa0, qa1), (ka0, ka1), (va0, va1)
        _fox_operands(q_ref, k_ref, c_ref, hp, qa, ka)
        lane = lax.broadcasted_iota(jnp.int32, (S, LANE), 1)
        masks = _head_masks((S, LANE))
        vv = v_ref[...].astype(BF16)
        for e in range(2):
            va[e][...] = jnp.where(masks[e], vv, jnp.where(lane == HD * (1 - e), 1.0, 0.0).astype(BF16))
        tmask = _head_masks((FQ, LANE))
        row = lax.broadcasted_iota(jnp.int32, (FQ, FQ), 0)
        col = lax.broadcasted_iota(jnp.int32, (FQ, FQ), 1)
        for i in range(S // FQ):
            r0 = i * FQ
            res = []
            for e in range(2):
                qt = qa[e][pl.ds(r0, FQ), :]
                sd = lax.dot_general(qt, ka[e][pl.ds(r0, FQ), :], NT_DIMS, preferred_element_type=F32)
                sd = jnp.where(col <= row, sd, NEG)
                m = jnp.max(sd, axis=1, keepdims=True)
                if i > 0:
                    so = lax.dot_general(qt, ka[e][pl.ds(0, r0), :], NT_DIMS, preferred_element_type=F32)
                    m = jnp.maximum(m, jnp.max(so, axis=1, keepdims=True))
                acc = jnp.dot(jnp.exp(sd - m).astype(BF16), va[e][pl.ds(r0, FQ), :], preferred_element_type=F32)
                if i > 0:
                    acc = acc + jnp.dot(jnp.exp(so - m).astype(BF16), va[e][pl.ds(0, r0), :],
                                        preferred_element_type=F32)
                l = acc[:, HD * (1 - e):HD * (1 - e) + 1]
                res.append((acc / l, m + jnp.log(l)))
            o_ref[pl.ds(r0, FQ), :] = jnp.where(tmask[0], res[0][0], res[1][0])
            lse_ref[pl.ds(r0, FQ), :] = jnp.where(tmask[0], res[0][1], res[1][1])

    def colblk(off):
        return pl.BlockSpec((S, LANE), lambda b, hp: (b, off + hp))

    return pl.pallas_call(
        body, name=name, grid=(nb, NHP),
        in_specs=[colblk(CB_QA), colblk(CB_KA), colblk(CB_VA), pl.BlockSpec((S, LANE), lambda b, hp: (b, 0))],
        out_specs=[colblk(0), colblk(0)],
        out_shape=[jax.ShapeDtypeStruct((nb * S, WA), F32), jax.ShapeDtypeStruct((nb * S, WA), F32)],
        scratch_shapes=[pltpu.VMEM((S, LANE), BF16)] * 6,
        compiler_params=_cparams(("parallel", "parallel")),
    )(P, P, P, c)


def _fox_bwd(P, c, o, lse, do, nb, name):
    def body(q_ref, k_ref, v_ref, c_ref, o_ref, lse_ref, do_ref, dq_ref, dk_ref, dv_ref, dc_ref,
             qa0, qa1, ka0, ka1, vm0, vm1, dm0, dm1, kt0, kt1, rows, dqt, rsum):
        hp = pl.program_id(1)
        qa, ka, vm, dm, kt = (qa0, qa1), (ka0, ka1), (vm0, vm1), (dm0, dm1), (kt0, kt1)
        _fox_operands(q_ref, k_ref, c_ref, hp, qa, ka)
        masks = _head_masks((S, LANE))
        vv = v_ref[...].astype(BF16)
        dov = do_ref[...]
        dob = dov.astype(BF16)
        zero = jnp.zeros((S, LANE), BF16)
        ktf = k_ref[...].T
        prodt = (dov * o_ref[...]).T
        lset = lse_ref[...].T
        hrow = lax.broadcasted_iota(jnp.int32, (LANE, S), 0)
        for e in range(2):
            vm[e][...] = jnp.where(masks[e], vv, zero)
            dm[e][...] = jnp.where(masks[e], dob, zero)
            inhead = (hrow >= HD * e) & (hrow < HD * (e + 1))
            kt[e][...] = jnp.where(inhead, ktf, 0.0).astype(BF16)
            rows[e:e + 1, :] = lset[HD * e:HD * e + 1, :]
            rows[2 + e:3 + e, :] = jnp.sum(prodt[HD * e:HD * (e + 1), :], axis=0, keepdims=True)
        dqt[...] = jnp.zeros_like(dqt)
        rsum[...] = jnp.zeros_like(rsum)
        tmask = _head_masks((FQ, LANE))
        row = lax.broadcasted_iota(jnp.int32, (FQ, FQ), 0)
        col = lax.broadcasted_iota(jnp.int32, (FQ, FQ), 1)
        for j in range(S // FQ):
            k0 = j * FQ
            rest = S - k0 - FQ
            spans = [(k0, FQ)] + ([(k0 + FQ, rest)] if rest > 0 else [])
            res = []
            for e in range(2):
                kte = ka[e][pl.ds(k0, FQ), :]
                vte = vm[e][pl.ds(k0, FQ), :]
                ktt = kt[e][:, pl.ds(k0, FQ)]
                dke = jnp.zeros((FQ, LANE), F32)
                dve = jnp.zeros((FQ, LANE), F32)
                cse = jnp.zeros((FQ, 1), F32)
                for si, (q0, n) in enumerate(spans):
                    qs = qa[e][pl.ds(q0, n), :]
                    dos = dm[e][pl.ds(q0, n), :]
                    st = lax.dot_general(kte, qs, NT_DIMS, preferred_element_type=F32)
                    if si == 0:
                        st = jnp.where(col >= row, st, NEG)
                    pt = jnp.exp(st - rows[e:e + 1, pl.ds(q0, n)])
                    dpt = lax.dot_general(vte, dos, NT_DIMS, preferred_element_type=F32)
                    dst = pt * (dpt - rows[2 + e:3 + e, pl.ds(q0, n)])
                    dsb = dst.astype(BF16)
                    dve = dve + jnp.dot(pt.astype(BF16), dos, preferred_element_type=F32)
                    dke = dke + jnp.dot(dsb, qs, preferred_element_type=F32)
                    dqt[:, pl.ds(q0, n)] += jnp.dot(ktt, dsb, preferred_element_type=F32)
                    cse = cse + jnp.sum(dst, axis=1, keepdims=True)
                    rsum[HD * e:HD * e + 1, pl.ds(q0, n)] += jnp.sum(dst, axis=0, keepdims=True)
                res.append((dke, dve, cse))
            dk_ref[pl.ds(k0, FQ), :] = jnp.where(tmask[0], res[0][0], res[1][0]).astype(BF16)
            dv_ref[pl.ds(k0, FQ), :] = jnp.where(tmask[0], res[0][1], res[1][1]).astype(BF16)
            dc_ref[pl.ds(k0, FQ), :] = jnp.where(tmask[0], res[0][2], res[1][2])
        dq_ref[...] = (dqt[...].T * SCALE).astype(BF16)
        dc_ref[...] = dc_ref[...] - rsum[...].T

    def colblk(off):
        return pl.BlockSpec((S, LANE), lambda b, hp: (b, off + hp))

    return pl.pallas_call(
        body, name=name, grid=(nb, NHP),
        in_specs=[colblk(CB_QA), colblk(CB_KA), colblk(CB_VA), pl.BlockSpec((S, LANE), lambda b, hp: (b, 0)),
                  colblk(0), colblk(0), colblk(0)],
        out_specs=[colblk(0)] * 4,
        out_shape=[jax.ShapeDtypeStruct((nb * S, WA), BF16)] * 3 + [jax.ShapeDtypeStruct((nb * S, WA), F32)],
        scratch_shapes=[pltpu.VMEM((S, LANE), BF16)] * 8 + [pltpu.VMEM((LANE, S), BF16)] * 2
        + [pltpu.VMEM((8, S), F32), pltpu.VMEM((LANE, S), F32), pltpu.VMEM((LANE, S), F32)],
        compiler_params=_cparams(("parallel", "parallel")),
    )(P, P, P, c, o, lse, do)


DILS = (1, 4, 16)
DB = 128


def _regroup_load(ref, d, scale=None):
    if d == 1:
        v = ref[...]
    else:
        L = S // d
        v = jnp.concatenate([ref[pl.ds(r, L, stride=d), :] for r in range(d)], axis=0)
    return v if scale is None else v * scale


def _regroup_store(ref, d, val_ref, accumulate):
    L = S // d
    for r in range(d):
        src = val_ref[pl.ds(r * L, L), :]
        dst = (slice(None), slice(None)) if d == 1 else (pl.ds(r, L, stride=d), slice(None))
        if accumulate:
            ref[dst] = ref[dst] + src
        else:
            ref[dst] = src


def _dil_bands():
    qi = lax.broadcasted_iota(jnp.int32, (DB, 2 * DB), 0)
    ki = lax.broadcasted_iota(jnp.int32, (DB, 2 * DB), 1)
    band = (ki >= qi) & (ki <= qi + DB)
    return band, band & (ki >= DB)


def _dil_valid(bands, bk, d):
    band, own = bands
    has_prev = (bk % ((S // d) // DB)) > 0
    return own | (band & has_prev)


def _dil_fwd(P, nb, name):
    nblk = S // DB

    def body(q_ref, k_ref, v_ref, o_ref, lse_ref, qd, kd, vd, rnum, rm, rl, num_n, m_n, l_n):
        masks = _head_masks((DB, LANE))
        bands = _dil_bands()
        for bi, d in enumerate(DILS):
            qd[...] = _regroup_load(q_ref, d, SCALE).astype(BF16)
            kd[pl.ds(0, DB), :] = jnp.zeros((DB, LANE), BF16)
            vd[pl.ds(0, DB), :] = jnp.zeros((DB, LANE), BF16)
            kd[pl.ds(DB, S), :] = _regroup_load(k_ref, d).astype(BF16)
            vd[pl.ds(DB, S), :] = _regroup_load(v_ref, d).astype(BF16)

            def blk(bk, _, d=d):
                r0 = pl.multiple_of(bk * DB, DB)
                qt = qd[pl.ds(r0, DB), :]
                kk = kd[pl.ds(r0, 2 * DB), :]
                vv = vd[pl.ds(r0, 2 * DB), :]
                valid = _dil_valid(bands, bk, d)
                res = []
                for e in range(2):
                    qm = jnp.where(masks[e], qt, jnp.zeros_like(qt))
                    s = lax.dot_general(qm, kk, (((1,), (1,)), ((), ())), preferred_element_type=F32)
                    s = jnp.where(valid, s, NEG)
                    m = jnp.max(s, axis=1, keepdims=True)
                    p = jnp.exp(s - m)
                    l = jnp.sum(p, axis=1, keepdims=True)
                    num = jnp.dot(p.astype(BF16), vv, preferred_element_type=F32)
                    res.append((m, l, num))
                (m0, l0, n0), (m1, l1, n1) = res
                rnum[pl.ds(r0, DB), :] = jnp.where(masks[0], n0, n1)
                rm[pl.ds(r0, DB), :] = jnp.where(masks[0], m0, m1)
                rl[pl.ds(r0, DB), :] = jnp.where(masks[0], l0, l1)
                return 0

            lax.fori_loop(0, nblk, blk, 0, unroll=4)
            _regroup_store(num_n.at[bi], d, rnum, False)
            _regroup_store(m_n.at[bi], d, rm, False)
            _regroup_store(l_n.at[bi], d, rl, False)

        m_all = jnp.maximum(jnp.maximum(m_n[0], m_n[1]), m_n[2])
        num = jnp.zeros((S, LANE), F32)
        den = jnp.zeros((S, LANE), F32)
        for bi in range(3):
            a = jnp.exp(m_n[bi] - m_all)
            num = num + a * num_n[bi]
            den = den + a * l_n[bi]
        o_ref[...] = num / den
        lse_ref[...] = m_all + jnp.log(den)

    def colblk(off):
        return pl.BlockSpec((S, LANE), lambda b, hp: (b, off + hp))

    return pl.pallas_call(
        body, name=name, grid=(nb, NHP),
        in_specs=[colblk(CB_QB), colblk(CB_KB), colblk(CB_VB)],
        out_specs=[colblk(0), colblk(0)],
        out_shape=[jax.ShapeDtypeStruct((nb * S, WA), F32), jax.ShapeDtypeStruct((nb * S, WA), F32)],
        scratch_shapes=[pltpu.VMEM((S, LANE), BF16), pltpu.VMEM((S + DB, LANE), BF16), pltpu.VMEM((S + DB, LANE), BF16),
                        pltpu.VMEM((S, LANE), F32), pltpu.VMEM((S, LANE), F32), pltpu.VMEM((S, LANE), F32),
                        pltpu.VMEM((3, S, LANE), F32), pltpu.VMEM((3, S, LANE), F32), pltpu.VMEM((3, S, LANE), F32)],
        compiler_params=_cparams(("parallel", "parallel")),
    )(P, P, P)


def _dil_bwd(P, o, lse, do, nb, name):
    nblk = S // DB

    def body(q_ref, k_ref, v_ref, o_ref, lse_ref, do_ref, dq_ref, dk_ref, dv_ref,
             qd, kd, vd, dod, lsed, dsd, dsum, dq_r, dk_r, dv_r, dq_n, dk_n, dv_n):
        masks = _head_masks((DB, LANE))
        fmask = _head_masks((S, LANE))
        prod = do_ref[...] * o_ref[...]
        d0 = jnp.sum(jnp.where(fmask[0], prod, 0.0), axis=1, keepdims=True)
        d1 = jnp.sum(jnp.where(fmask[1], prod, 0.0), axis=1, keepdims=True)
        dsum[...] = jnp.where(fmask[0], d0, d1)
        tn = (((0,), (0,)), ((), ()))
        bands = _dil_bands()
        for bi, d in enumerate(DILS):
            qd[...] = _regroup_load(q_ref, d, SCALE).astype(BF16)
            kd[pl.ds(0, DB), :] = jnp.zeros((DB, LANE), BF16)
            vd[pl.ds(0, DB), :] = jnp.zeros((DB, LANE), BF16)
            kd[pl.ds(DB, S), :] = _regroup_load(k_ref, d).astype(BF16)
            vd[pl.ds(DB, S), :] = _regroup_load(v_ref, d).astype(BF16)
            dod[...] = _regroup_load(do_ref, d).astype(BF16)
            lsed[...] = _regroup_load(lse_ref, d)
            dsd[...] = _regroup_load(dsum, d)
            dq_r[...] = jnp.zeros_like(dq_r)
            dk_r[...] = jnp.zeros_like(dk_r)
            dv_r[...] = jnp.zeros_like(dv_r)

            def blk(bk, _, d=d):
                r0 = pl.multiple_of(bk * DB, DB)
                qt = qd[pl.ds(r0, DB), :]
                dot = dod[pl.ds(r0, DB), :]
                lt = lsed[pl.ds(r0, DB), :]
                dt = dsd[pl.ds(r0, DB), :]
                kk = kd[pl.ds(r0, 2 * DB), :]
                vv = vd[pl.ds(r0, 2 * DB), :]
                kmask = _head_masks((2 * DB, LANE))
                valid = _dil_valid(bands, bk, d)
                dqt = jnp.zeros((DB, LANE), F32)
                dkt = jnp.zeros((2 * DB, LANE), F32)
                dvt = jnp.zeros((2 * DB, LANE), F32)
                for e in range(2):
                    qm = jnp.where(masks[e], qt, jnp.zeros_like(qt))
                    dom = jnp.where(masks[e], dot, jnp.zeros_like(dot))
                    km = jnp.where(kmask[e], kk, jnp.zeros_like(kk))
                    s = lax.dot_general(qm, kk, (((1,), (1,)), ((), ())), preferred_element_type=F32)
                    s = jnp.where(valid, s, NEG)
                    p = jnp.exp(s - lt[:, HD * e:HD * e + 1])
                    dp = lax.dot_general(dom, vv, (((1,), (1,)), ((), ())), preferred_element_type=F32)
                    ds = (p * (dp - dt[:, HD * e:HD * e + 1])).astype(BF16)
                    dvt = dvt + lax.dot_general(p.astype(BF16), dom, tn, preferred_element_type=F32)
                    dkt = dkt + lax.dot_general(ds, qm, tn, preferred_element_type=F32)
                    dqt = dqt + jnp.dot(ds, km, preferred_element_type=F32)
                dq_r[pl.ds(r0, DB), :] = dqt
                dk_r[pl.ds(r0, 2 * DB), :] += dkt
                dv_r[pl.ds(r0, 2 * DB), :] += dvt
                return 0

            lax.fori_loop(0, nblk, blk, 0, unroll=2)
            _regroup_store(dq_n, d, dq_r, bi > 0)
            _regroup_store(dk_n, d, dk_r.at[pl.ds(DB, S)], bi > 0)
            _regroup_store(dv_n, d, dv_r.at[pl.ds(DB, S)], bi > 0)

        dq_ref[...] = (dq_n[...] * SCALE).astype(BF16)
        dk_ref[...] = dk_n[...].astype(BF16)
        dv_ref[...] = dv_n[...].astype(BF16)

    def colblk(off):
        return pl.BlockSpec((S, LANE), lambda b, hp: (b, off + hp))

    big = pltpu.VMEM((S, LANE), F32)
    bigp = pltpu.VMEM((S + DB, LANE), F32)
    return pl.pallas_call(
        body, name=name, grid=(nb, NHP),
        in_specs=[colblk(CB_QB), colblk(CB_KB), colblk(CB_VB), colblk(0), colblk(0), colblk(0)],
        out_specs=[colblk(0)] * 3,
        out_shape=[jax.ShapeDtypeStruct((nb * S, WA), BF16)] * 3,
        scratch_shapes=[pltpu.VMEM((S, LANE), BF16), pltpu.VMEM((S + DB, LANE), BF16), pltpu.VMEM((S + DB, LANE), BF16),
                        pltpu.VMEM((S, LANE), BF16), big, big, big, big, bigp, bigp, big, big, big],
        compiler_params=_cparams(("parallel", "parallel")),
    )(P, P, P, o, lse, do)


RC = 256
CPAD = 32


def _conv_branch_chunk(gpad, r0, cw_ref, cb_ref, cng_ref, cnb_ref):
    acc = jnp.zeros((RC, CC), F32) + cb_ref[...]
    for k in range(CK):
        acc = acc + cw_ref[k:k + 1, :] * gpad[pl.ds(r0 + CPAD - (CK - 1) + k, RC), :]
    mu = jnp.mean(acc, axis=1, keepdims=True)
    xc = acc - mu
    rstd = lax.rsqrt(jnp.mean(xc * xc, axis=1, keepdims=True) + EPS)
    n = xc * rstd
    return n, rstd, n * cng_ref[...] + cnb_ref[...]


NORM_ROWS = 512
YC_BLK = 2 * WA // CC


def _attn_norm_fwd(of, od, gof, god, name):
    T = of.shape[0]

    def body(of_ref, od_ref, gof_ref, god_ref, y_ref):
        for i, (src, g_ref) in enumerate(((of_ref, gof_ref), (od_ref, god_ref))):
            v = src[...]
            r = lax.rsqrt(jnp.mean(v * v, axis=1, keepdims=True) + EPS)
            y_ref[:, i * WA:(i + 1) * WA] = (v * r * g_ref[...]).astype(BF16)

    row = lambda w: pl.BlockSpec((NORM_ROWS, w), lambda i: (i, 0))
    par = pl.BlockSpec((1, WA), lambda i: (0, 0))
    return pl.pallas_call(
        body, name=name, grid=(T // NORM_ROWS,),
        in_specs=[row(WA), row(WA), par, par], out_specs=row(2 * WA),
        out_shape=jax.ShapeDtypeStruct((T, D), BF16),
        compiler_params=_cparams(("parallel",)),
    )(of, od, gof, god)


def _attn_norm_bwd(of, od, dy, gof, god, name):
    T = of.shape[0]

    def body(of_ref, od_ref, dy_ref, gof_ref, god_ref, dof_ref, dod_ref, dgo_ref):
        @pl.when(pl.program_id(0) == 0)
        def _():
            dgo_ref[...] = jnp.zeros_like(dgo_ref)

        for i, (src, g_ref, dst) in enumerate(((of_ref, gof_ref, dof_ref), (od_ref, god_ref, dod_ref))):
            v = src[...]
            dyv = dy_ref[:, i * WA:(i + 1) * WA].astype(F32)
            r = lax.rsqrt(jnp.mean(v * v, axis=1, keepdims=True) + EPS)
            a = dyv * g_ref[...]
            dst[...] = r * a - v * (r * r * r * jnp.mean(v * a, axis=1, keepdims=True))
            dgo_ref[i:i + 1, :] += jnp.sum(dyv * v * r, axis=0, keepdims=True)

    row = lambda w: pl.BlockSpec((NORM_ROWS, w), lambda i: (i, 0))
    par = pl.BlockSpec((1, WA), lambda i: (0, 0))
    return pl.pallas_call(
        body, name=name, grid=(T // NORM_ROWS,),
        in_specs=[row(WA), row(WA), row(2 * WA), par, par],
        out_specs=[row(WA), row(WA), pl.BlockSpec((8, WA), lambda i: (0, 0))],
        out_shape=[jax.ShapeDtypeStruct((T, WA), F32), jax.ShapeDtypeStruct((T, WA), F32),
                   jax.ShapeDtypeStruct((8, WA), F32)],
        compiler_params=_cparams(("arbitrary",)),
    )(of, od, dy, gof, god)


def _conv_specs():
    gblk = lambda off: pl.BlockSpec((S, CC), lambda b: (b, off))
    par = lambda r: pl.BlockSpec((r, CC), lambda b: (0, 0))
    return gblk, par


def _conv_fwd(P, y, cw, cb, cng, cnb, nb, name):
    def body(gv_ref, gg_ref, cw_ref, cb_ref, cng_ref, cnb_ref, y_in, y_ref, gpad):
        del y_in
        gpad[pl.ds(0, CPAD), :] = jnp.zeros((CPAD, CC), F32)
        gpad[pl.ds(CPAD, S), :] = gv_ref[...] * _sigmoid(gg_ref[...])
        for ci in range(S // RC):
            r0 = ci * RC
            _, _, z = _conv_branch_chunk(gpad, r0, cw_ref, cb_ref, cng_ref, cnb_ref)
            y_ref[pl.ds(r0, RC), :] = (z * _sigmoid(z)).astype(BF16)

    gblk, par = _conv_specs()
    return pl.pallas_call(
        body, name=name, grid=(nb,),
        in_specs=[gblk(CB_GV), gblk(CB_GG), par(CPAD), par(1), par(1), par(1), pl.BlockSpec(memory_space=pl.ANY)],
        out_specs=gblk(YC_BLK), out_shape=jax.ShapeDtypeStruct((nb * S, D), BF16),
        input_output_aliases={6: 0},
        scratch_shapes=[pltpu.VMEM((S + CPAD, CC), F32)],
        compiler_params=_cparams(("parallel",)),
    )(P, P, cw, cb, cng, cnb, y)


def _conv_bwd(P, dy, cw, cb, cng, cnb, nb, name):
    def body(gv_ref, gg_ref, dy_ref, cw_ref, cb_ref, cng_ref, cnb_ref, dg_ref, dcw_ref, dsm_ref, gpad, dpad):
        @pl.when(pl.program_id(0) == 0)
        def _():
            dcw_ref[...] = jnp.zeros_like(dcw_ref)
            dsm_ref[...] = jnp.zeros_like(dsm_ref)

        gpad[pl.ds(0, CPAD), :] = jnp.zeros((CPAD, CC), F32)
        gpad[pl.ds(CPAD, S), :] = gv_ref[...] * _sigmoid(gg_ref[...])
        dpad[pl.ds(S, CPAD), :] = jnp.zeros((CPAD, CC), F32)
        zero = jnp.zeros((1, CC), F32)
        dcb, dcng, dcnb = zero, zero, zero
        for ci in range(S // RC):
            r0 = ci * RC
            n, rstd, z = _conv_branch_chunk(gpad, r0, cw_ref, cb_ref, cng_ref, cnb_ref)
            sz = _sigmoid(z)
            dz = dy_ref[pl.ds(r0, RC), :].astype(F32) * (sz * (1.0 + z * (1.0 - sz)))
            dcng = dcng + jnp.sum(dz * n, axis=0, keepdims=True)
            dcnb = dcnb + jnp.sum(dz, axis=0, keepdims=True)
            dn = dz * cng_ref[...]
            dc0 = rstd * (dn - jnp.mean(dn, axis=1, keepdims=True) - n * jnp.mean(dn * n, axis=1, keepdims=True))
            dcb = dcb + jnp.sum(dc0, axis=0, keepdims=True)
            dpad[pl.ds(r0, RC), :] = dc0
        dsm_ref[0:1, :] += dcb
        dsm_ref[1:2, :] += dcng
        dsm_ref[2:3, :] += dcnb

        dws = [zero] * CK
        for ci in range(S // RC):
            r0 = ci * RC
            dct = dpad[pl.ds(r0, RC), :]
            dgl = jnp.zeros((RC, CC), F32)
            for k in range(CK):
                dws[k] = dws[k] + jnp.sum(dct * gpad[pl.ds(r0 + CPAD - (CK - 1) + k, RC), :], axis=0, keepdims=True)
                dgl = dgl + cw_ref[k:k + 1, :] * dpad[pl.ds(r0 + (CK - 1) - k, RC), :]
            sg = _sigmoid(gg_ref[pl.ds(r0, RC), :])
            dg_ref[pl.ds(r0, RC), 0:CC] = (dgl * sg).astype(BF16)
            dg_ref[pl.ds(r0, RC), CC:2 * CC] = (dgl * gv_ref[pl.ds(r0, RC), :] * sg * (1.0 - sg)).astype(BF16)
        for k in range(CK):
            dcw_ref[k:k + 1, :] += dws[k]

    gblk, par = _conv_specs()
    return pl.pallas_call(
        body, name=name, grid=(nb,),
        in_specs=[gblk(CB_GV), gblk(CB_GG), gblk(YC_BLK), par(CPAD), par(1), par(1), par(1)],
        out_specs=[pl.BlockSpec((S, 2 * CC), lambda b: (b, 0)), par(CPAD), par(8)],
        out_shape=[jax.ShapeDtypeStruct((nb * S, 2 * CC), BF16), jax.ShapeDtypeStruct((CPAD, CC), F32),
                   jax.ShapeDtypeStruct((8, CC), F32)],
        scratch_shapes=[pltpu.VMEM((S + CPAD, CC), F32), pltpu.VMEM((S + CPAD, CC), F32)],
        compiler_params=_cparams(("arbitrary",)),
    )(P, P, dy, cw, cb, cng, cnb)


FC = 512
FPAD = 8
NFB = 2 * DFF // FC


def _ffn_u2_chunk(upad, r0, fw_ref, fb_ref):
    acc = jnp.zeros((RC, FC), F32) + fb_ref[...]
    for k in range(FK):
        acc = acc + fw_ref[k:k + 1, :] * upad[pl.ds(r0 + FPAD - (FK - 1) + k, RC), :]
    return acc


def _ffn_fwd(U, fw, fb, nb, name):
    def body(u_ref, fw_ref, fb_ref, h_ref, upad):
        upad[pl.ds(0, FPAD), :] = jnp.zeros((FPAD, FC), F32)
        upad[pl.ds(FPAD, S), :] = u_ref[...].astype(F32)
        for ci in range(S // RC):
            r0 = ci * RC
            u2 = _ffn_u2_chunk(upad, r0, fw_ref, fb_ref)
            a2, b2 = u2[:, :FC // 2], u2[:, FC // 2:]
            h_ref[pl.ds(r0, RC), :] = (a2 * _sigmoid(a2) * b2).astype(BF16)

    return pl.pallas_call(
        body, name=name, grid=(nb, NFB),
        in_specs=[pl.BlockSpec((S, FC), lambda b, j: (b, j)), pl.BlockSpec((8, FC), lambda b, j: (0, j)),
                  pl.BlockSpec((1, FC), lambda b, j: (0, j))],
        out_specs=pl.BlockSpec((S, FC // 2), lambda b, j: (b, j)),
        out_shape=jax.ShapeDtypeStruct((nb * S, DFF), BF16),
        scratch_shapes=[pltpu.VMEM((S + FPAD, FC), F32)],
        compiler_params=_cparams(("parallel", "parallel")),
    )(U, fw, fb)


def _ffn_bwd(U, dhid, fw, fb, nb, name):
    def body(u_ref, dh_ref, fw_ref, fb_ref, du_ref, dfw_ref, upad, dpad):
        @pl.when(pl.program_id(1) == 0)
        def _():
            dfw_ref[...] = jnp.zeros_like(dfw_ref)

        upad[pl.ds(0, FPAD), :] = jnp.zeros((FPAD, FC), F32)
        upad[pl.ds(FPAD, S), :] = u_ref[...].astype(F32)
        dpad[pl.ds(S, FPAD), :] = jnp.zeros((FPAD, FC), F32)
        zero = jnp.zeros((1, FC), F32)
        dbias = zero
        for ci in range(S // RC):
            r0 = ci * RC
            u2 = _ffn_u2_chunk(upad, r0, fw_ref, fb_ref)
            a2, b2 = u2[:, :FC // 2], u2[:, FC // 2:]
            sa = _sigmoid(a2)
            dh = dh_ref[pl.ds(r0, RC), :].astype(F32)
            du2 = jnp.concatenate([dh * b2 * (sa * (1.0 + a2 * (1.0 - sa))), dh * a2 * sa], axis=1)
            dpad[pl.ds(r0, RC), :] = du2
            dbias = dbias + jnp.sum(du2, axis=0, keepdims=True)
        dws = [zero] * FK
        for ci in range(S // RC):
            r0 = ci * RC
            dct = dpad[pl.ds(r0, RC), :]
            du = jnp.zeros((RC, FC), F32)
            for k in range(FK):
                dws[k] = dws[k] + jnp.sum(dct * upad[pl.ds(r0 + FPAD - (FK - 1) + k, RC), :], axis=0, keepdims=True)
                du = du + fw_ref[k:k + 1, :] * dpad[pl.ds(r0 + (FK - 1) - k, RC), :]
            du_ref[pl.ds(r0, RC), :] = du.astype(BF16)
        for k in range(FK):
            dfw_ref[k:k + 1, :] += dws[k]
        dfw_ref[FK:FK + 1, :] += dbias

    return pl.pallas_call(
        body, name=name, grid=(NFB, nb),
        in_specs=[pl.BlockSpec((S, FC), lambda j, b: (b, j)), pl.BlockSpec((S, FC // 2), lambda j, b: (b, j)),
                  pl.BlockSpec((8, FC), lambda j, b: (0, j)), pl.BlockSpec((1, FC), lambda j, b: (0, j))],
        out_specs=[pl.BlockSpec((S, FC), lambda j, b: (b, j)), pl.BlockSpec((8, FC), lambda j, b: (0, j))],
        out_shape=[jax.ShapeDtypeStruct((nb * S, 2 * DFF), BF16), jax.ShapeDtypeStruct((8, 2 * DFF), F32)],
        scratch_shapes=[pltpu.VMEM((S + FPAD, FC), F32), pltpu.VMEM((S + FPAD, FC), F32)],
        compiler_params=_cparams(("parallel", "arbitrary")),
    )(U, dhid, fw, fb)


def _adamw_body(w_ref, g_ref, m_ref, v_ref, d_ref, nm_ref, nv_ref):
    g = g_ref[...]
    m = ADAM_B1 * m_ref[...] + (1.0 - ADAM_B1) * g
    v = ADAM_B2 * v_ref[...] + (1.0 - ADAM_B2) * (g * g)
    m_hat = m / (1.0 - ADAM_B1 ** ADAM_STEP)
    v_hat = v / (1.0 - ADAM_B2 ** ADAM_STEP)
    d_ref[...] = -ADAM_LR * (m_hat / (jnp.sqrt(v_hat) + ADAM_EPS) + ADAM_WD * w_ref[...])
    nm_ref[...] = m
    nv_ref[...] = v


def _adamw(w, g, m, v, name):
    shape = w.shape
    R = 1
    for s in shape[:-1]:
        R *= s
    C = shape[-1]
    args = [a.reshape(R, C) for a in (w, g, m, v)]
    tr = R
    for cand in (512, 352, 256, 128, 64, 32, 16, 8):
        if R % cand == 0 and cand * C * 4 * 14 <= 24 * 1024 * 1024:
            tr = cand
            break
    blk = pl.BlockSpec((tr, C), lambda i: (i, 0))
    outs = pl.pallas_call(
        functools.partial(_adamw_body), name=name, grid=(R // tr,),
        in_specs=[blk] * 4, out_specs=[blk] * 3,
        out_shape=[jax.ShapeDtypeStruct((R, C), F32)] * 3,
        compiler_params=_cparams(("parallel",)),
    )(*args)
    return [o.reshape(shape) for o in outs]


def _rs_row_tile(H):
    th = 128 if H % 128 == 0 else 176
    assert H % th == 0
    return th


def _add_half(g, r1, place, name):
    _, R, C = g.shape
    H = R // 2
    th = _rs_row_tile(H)
    nh = H // th

    def body(s_ref, g_ref, r_ref, o_ref):
        o_ref[...] = (g_ref[...] + r_ref[...]).astype(BF16)

    grid_spec = pltpu.PrefetchScalarGridSpec(
        num_scalar_prefetch=1, grid=(NCHIP, nh),
        in_specs=[pl.BlockSpec((None, th, C), lambda p, i, s: (p, s[1] * nh + i, 0)),
                  pl.BlockSpec((None, th, C), lambda p, i, s: (p, i, 0))],
        out_specs=pl.BlockSpec((None, th, C), lambda p, i, s: (p, i, 0)))
    return pl.pallas_call(
        body, name=name, grid_spec=grid_spec, out_shape=jax.ShapeDtypeStruct((NCHIP, H, C), BF16),
        compiler_params=_cparams(("parallel", "parallel")),
    )(place, g, r1)


def _sum_slots(g, r1, r2, place, name):
    _, R, C = g.shape
    H = R // 2
    th = _rs_row_tile(H)
    nh = H // th

    def body(s_ref, g_ref, r1_ref, r2_ref, o_ref):
        acc = g_ref[...] + r1_ref[...]
        for j in range(NCHIP - 1):
            acc = acc + r2_ref[j].astype(F32)
        o_ref[...] = acc

    grid_spec = pltpu.PrefetchScalarGridSpec(
        num_scalar_prefetch=1, grid=(nh,),
        in_specs=[pl.BlockSpec((None, th, C), lambda i, s: (s[0], s[1] * nh + i, 0)),
                  pl.BlockSpec((None, th, C), lambda i, s: (s[0], i, 0)),
                  pl.BlockSpec((NCHIP - 1, th, C), lambda i, s: (0, i, 0))],
        out_specs=pl.BlockSpec((None, th, C), lambda i, s: (s[1], i, 0)))
    return pl.pallas_call(
        body, name=name, grid_spec=grid_spec, out_shape=jax.ShapeDtypeStruct((2, H, C), F32),
        compiler_params=_cparams(("parallel",)),
    )(place, g, r1, r2)


MESH = pl.DeviceIdType.MESH
HBM = pl.BlockSpec(memory_space=pltpu.HBM)


def _place():
    x, y, c = lax.axis_index("x"), lax.axis_index("y"), lax.axis_index("c")
    chips = [(1 - x, y), (x, 1 - y), (1 - x, 1 - y)]
    return x, y, c, chips


def _rcopy(src, dst, ssem, rsem, dev):
    return pltpu.make_async_remote_copy(src_ref=src, dst_ref=dst, send_sem=ssem, recv_sem=rsem,
                                        device_id=dev, device_id_type=MESH)


def _allgather(shards, split):
    n = len(shards)

    def body(*refs):
        ins, outs = refs[:n], refs[n:2 * n]
        ssem, rsem, fssem, frsem = refs[2 * n:]
        x, y, c, chips = _place()
        me = 2 * x + y
        sib = (x, y, 1 - c)

        def window(t, chip, half):
            if not split[t]:
                return outs[t].at[:, chip]
            H = shards[t].shape[1] // 2
            return outs[t].at[:, chip, pl.ds(half * H, H)]

        sends = []
        for t in range(n):
            H = shards[t].shape[1] // 2
            src = ins[t].at[:, pl.ds(c * H, H)] if split[t] else ins[t]
            for j, (cx, cy) in enumerate(chips):
                cp = _rcopy(src, window(t, me, c), ssem.at[3 * t + j], rsem.at[3 * t + j], (cx, cy, c))
                cp.start()
                sends.append(cp)
        for t in range(n):
            for j, (cx, cy) in enumerate(chips):
                win = window(t, 2 * cx + cy, c)
                _rcopy(win, win, ssem.at[3 * t + j], rsem.at[3 * t + j], (cx, cy, c)).wait_recv()
                if split[t]:
                    cp = _rcopy(win, win, fssem.at[3 * t + j], frsem.at[3 * t + j], sib)
                    cp.start()
                    sends.append(cp)
        for t in range(n):
            if split[t]:
                for j, (cx, cy) in enumerate(chips):
                    win = window(t, 2 * cx + cy, 1 - c)
                    _rcopy(win, win, fssem.at[3 * t + j], frsem.at[3 * t + j], sib).wait_recv()
        for cp in sends:
            cp.wait_send()

    out_shape = [jax.ShapeDtypeStruct((s.shape[0], NCHIP) + s.shape[1:], s.dtype) for s in shards]
    return pl.pallas_call(
        body, name="allgather_weights", in_specs=[HBM] * n, out_specs=[HBM] * n, out_shape=out_shape,
        scratch_shapes=[pltpu.SemaphoreType.DMA((3 * n,))] * 4,
    )(*shards)


def _rs_pair_exchange(gs):
    n = len(gs)

    def body(*refs):
        ins, outs = refs[:n], refs[n:2 * n]
        ssem, rsem = refs[2 * n:]
        x, y, c, _ = _place()
        cps = []
        for t in range(n):
            H = gs[t].shape[1] // 2
            cp = _rcopy(ins[t].at[:, pl.ds((1 - c) * H, H)], outs[t], ssem.at[t], rsem.at[t], (x, y, 1 - c))
            cp.start()
            cps.append(cp)
        for cp in cps:
            cp.wait_recv()
        for cp in cps:
            cp.wait_send()

    out_shape = [jax.ShapeDtypeStruct((NCHIP, g.shape[1] // 2, g.shape[2]), F32) for g in gs]
    return pl.pallas_call(
        body, name="rs_pair_exchange", in_specs=[HBM] * n, out_specs=[HBM] * n, out_shape=out_shape,
        scratch_shapes=[pltpu.SemaphoreType.DMA((n,))] * 2,
    )(*gs)


def _rs_chip_scatter(hs):
    n = len(hs)

    def body(*refs):
        ins, outs = refs[:n], refs[n:2 * n]
        ssem, rsem = refs[2 * n:]
        x, y, c, chips = _place()
        sends = []
        for t in range(n):
            for j, (cx, cy) in enumerate(chips):
                cp = _rcopy(ins[t].at[2 * cx + cy], outs[t].at[j], ssem.at[3 * t + j], rsem.at[3 * t + j], (cx, cy, c))
                cp.start()
                sends.append(cp)
        for cp in sends:
            cp.wait_recv()
        for cp in sends:
            cp.wait_send()

    out_shape = [jax.ShapeDtypeStruct((NCHIP - 1,) + h.shape[1:], h.dtype) for h in hs]
    return pl.pallas_call(
        body, name="rs_chip_scatter", in_specs=[HBM] * n, out_specs=[HBM] * n, out_shape=out_shape,
        scratch_shapes=[pltpu.SemaphoreType.DMA((3 * n,))] * 2,
    )(*hs)


def _rs_pair_gather(fs):
    n = len(fs)

    def body(*refs):
        bufs = refs[n:2 * n]
        ssem, rsem = refs[2 * n:]
        x, y, c, _ = _place()
        sends = []
        for t in range(n):
            cp = _rcopy(bufs[t].at[c], bufs[t].at[c], ssem.at[t], rsem.at[t], (x, y, 1 - c))
            cp.start()
            sends.append(cp)
        for t in range(n):
            win = bufs[t].at[1 - c]
            _rcopy(win, win, ssem.at[t], rsem.at[t], (x, y, 1 - c)).wait_recv()
        for cp in sends:
            cp.wait_send()

    out_shape = [jax.ShapeDtypeStruct(f.shape, F32) for f in fs]
    return pl.pallas_call(
        body, name="rs_pair_gather", in_specs=[HBM] * n, out_specs=[HBM] * n, out_shape=out_shape,
        input_output_aliases={t: t for t in range(n)},
        scratch_shapes=[pltpu.SemaphoreType.DMA((n,))] * 2,
    )(*fs)


def _allreduce_small(buf):
    R = buf.shape[0]

    def body(in_ref, out_ref, slots, ssem, rsem):
        x, y, c, _ = _place()
        me = 4 * x + 2 * y + c
        slots[me] = in_ref[...]
        cps = []
        for k in range(1, NDEV):
            px = 1 - x if k & 4 else x
            py = 1 - y if k & 2 else y
            pc = 1 - c if k & 1 else c
            cp = _rcopy(in_ref, slots.at[me], ssem.at[k - 1], rsem.at[k - 1], (px, py, pc))
            cp.start()
            cps.append((cp, 4 * px + 2 * py + pc))
        for k, (cp, peer) in enumerate(cps):
            _rcopy(in_ref, slots.at[peer], ssem.at[k], rsem.at[k], (x, y, c)).wait_recv()
        for cp, _ in cps:
            cp.wait_send()
        acc = slots[0]
        for p in range(1, NDEV):
            acc = acc + slots[p]
        out_ref[...] = acc

    return pl.pallas_call(
        body, name="allreduce_small", out_shape=jax.ShapeDtypeStruct((R, LANE), F32),
        in_specs=[pl.BlockSpec(memory_space=pltpu.VMEM)], out_specs=pl.BlockSpec(memory_space=pltpu.VMEM),
        scratch_shapes=[pltpu.VMEM((NDEV, R, LANE), F32), pltpu.SemaphoreType.DMA((NDEV - 1,)),
                        pltpu.SemaphoreType.DMA((NDEV - 1,))],
        compiler_params=pltpu.CompilerParams(vmem_limit_bytes=VMEM_LIMIT),
    )(buf)


def _interleave(a):
    lead = a.shape[:-1]
    return a.reshape(*lead, 2, NFB, FC // 2).swapaxes(-3, -2).reshape(*lead, 2 * DFF)


def _uninterleave(a):
    lead = a.shape[:-1]
    return a.reshape(*lead, NFB, 2, FC // 2).swapaxes(-3, -2).reshape(*lead, 2 * DFF)


N_QKV = 3 * WA
N_FG = 2 * NHP


def _pack_in_cols(w):
    pad = jnp.zeros(w.shape[:-1] + (NP - NIN,), w.dtype)
    return jnp.concatenate([w[..., :N_QKV], w[..., N_QKV + N_FG:], w[..., N_QKV:N_QKV + N_FG], pad], axis=-1)


def _unpack_in_cols(g):
    return jnp.concatenate([g[..., :N_QKV], g[..., NIN - N_FG:NIN], g[..., N_QKV:NIN - N_FG]], axis=-1)


def _train_compute(xt, tgt, W, nb):
    saved = []
    xc = xt
    for l in range(DEPTH):
        t = f"_l{l}"
        h = _rms_fwd(xc, W["ln1"][l], "rms1_fwd" + t)
        P = _matmul(h, W["in"][l], tm=1024, tn=1024, tk=D, name="proj_in" + t)
        c = _forget_fwd(P, W["bf"][l], nb, "forget_fwd" + t)
        of, lsef = _fox_fwd(P, c, nb, "fox_fwd" + t)
        od, lsed = _dil_fwd(P, nb, "dil_fwd" + t)
        convp = (W["cw"][l], W["cb"][l], W["cng"][l], W["cnb"][l])
        y = _attn_norm_fwd(of, od, W["gof"][l], W["god"][l], "attn_norm_fwd" + t)
        y = _conv_fwd(P, y, *convp, nb, "conv_fwd" + t)
        xm = _matmul(y, W["o"][l], add=xc, tm=1024, tn=1024, tk=D, name="proj_out" + t)
        h2 = _rms_fwd(xm, W["ln2"][l], "rms2_fwd" + t)
        U = _matmul(h2, W["up"][l], out_dtype=BF16, tm=1024, tn=FC, tk=D, name="ffn_up" + t)
        hid = _ffn_fwd(U, W["fw"][l], W["fb"][l], nb, "ffn_act_fwd" + t)
        xo = _matmul(hid, W["down"][l], add=xm, tm=1024, tn=512, tk=DFF, name="ffn_down" + t)
        saved.append((xc, h, P, c, of, lsef, od, lsed, convp, y, xm, h2, U, hid))
        xc = xo

    loss8, dx, dxb, dgfin = _loss_head(xc, W["gfin"], tgt, "loss_head")

    big = [None] * DEPTH
    small = [None] * DEPTH
    for l in reversed(range(DEPTH)):
        t = f"_l{l}"
        xin, h, P, c, of, lsef, od, lsed, convp, y, xm, h2, U, hid = saved[l]
        dhid = _matmul(dxb, W["down"][l], tb=True, out_dtype=BF16, tm=1024, tn=DFF // 2, tk=D, name="ffn_down_dx" + t)
        dWd = _matmul(hid, dxb, ta=True, tm=DFF // 2, tn=D, tk=512, name="ffn_down_dw" + t)
        dU, dfw = _ffn_bwd(U, dhid, W["fw"][l], W["fb"][l], nb, "ffn_act_bwd" + t)
        dh2 = _matmul(dU, W["up"][l], tb=True, tm=1024, tn=D, tk=DFF, name="ffn_up_dx" + t)
        dWup = _matmul(h2, dU, ta=True, tm=D, tn=DFF // 2, tk=512, name="ffn_up_dw" + t)
        dxm, dxmb, dln2 = _rms_bwd(xm, W["ln2"][l], dh2, dx, "rms2_bwd" + t)
        dy = _matmul(dxmb, W["o"][l], tb=True, out_dtype=BF16, tm=1024, tn=D, tk=D, name="proj_out_dx" + t)
        dWo = _matmul(y, dxmb, ta=True, tm=D, tn=D, tk=512, name="proj_out_dw" + t)
        dof, dod, dgo = _attn_norm_bwd(of, od, dy, W["gof"][l], W["god"][l], "attn_norm_bwd" + t)
        dgvgg, dcw, dsm = _conv_bwd(P, dy, *convp, nb, "conv_bwd" + t)
        dqa, dka, dva, dcb = _fox_bwd(P, c, of, lsef, dof, nb, "fox_bwd" + t)
        dfa, dbf = _forget_bwd(P, W["bf"][l], dcb, nb, "forget_bwd" + t)
        dqb, dkb, dvb = _dil_bwd(P, od, lsed, dod, nb, "dil_bwd" + t)
        dP = jnp.concatenate([dqa, dka, dva, dqb, dkb, dvb, dgvgg, dfa, jnp.zeros_like(dfa)], axis=1)
        dh = _matmul(dP, W["in"][l], tb=True, tm=1024, tn=D, tk=NP, name="proj_in_dx" + t)
        dWin = _matmul(h, dP, ta=True, tm=D, tn=1024, tk=512, name="proj_in_dw" + t)
        dx, dxb, dln1 = _rms_bwd(xin, W["ln1"][l], dh, dxm, "rms1_bwd" + t)
        big[l] = (dWin, dWo, dWup, dWd)
        small[l] = (dln1, dbf, dgo, dcw, dsm, dln2, dfw)
    return loss8, dx, big, small, dgfin


_SMALL_ROWS = (D // LANE, 8, 8 * WA // LANE, CPAD * CC // LANE, 8 * CC // LANE, D // LANE, 8 * 2 * DFF // LANE)


def kernel(x, ln1_g, w_in, b_forget, g_out_fox, g_out_dil, conv_w, conv_b, cnorm_g, cnorm_b, w_o, ln2_g, w_up, ffn_conv_w, ffn_conv_b, w_down, g_final, loss_target, m_ln1_g, m_w_in, m_b_forget, m_g_out_fox, m_g_out_dil, m_conv_w, m_conv_b, m_cnorm_g, m_cnorm_b, m_w_o, m_ln2_g, m_w_up, m_ffn_conv_w, m_ffn_conv_b, m_w_down, m_g_final, v_ln1_g, v_w_in, v_b_forget, v_g_out_fox, v_g_out_dil, v_conv_w, v_conv_b, v_cnorm_g, v_cnorm_b, v_w_o, v_ln2_g, v_w_up, v_ffn_conv_w, v_ffn_conv_b, v_w_down, v_g_final):
    names = ("ln1_g", "w_in", "b_forget", "g_out_fox", "g_out_dil", "conv_w", "conv_b", "cnorm_g", "cnorm_b",
             "w_o", "ln2_g", "w_up", "ffn_conv_w", "ffn_conv_b", "w_down", "g_final")
    w = dict(zip(names, (ln1_g, w_in, b_forget, g_out_fox, g_out_dil, conv_w, conv_b, cnorm_g, cnorm_b,
                         w_o, ln2_g, w_up, ffn_conv_w, ffn_conv_b, w_down, g_final)))
    m = dict(zip(names, (m_ln1_g, m_w_in, m_b_forget, m_g_out_fox, m_g_out_dil, m_conv_w, m_conv_b, m_cnorm_g,
                         m_cnorm_b, m_w_o, m_ln2_g, m_w_up, m_ffn_conv_w, m_ffn_conv_b, m_w_down, m_g_final)))
    v = dict(zip(names, (v_ln1_g, v_w_in, v_b_forget, v_g_out_fox, v_g_out_dil, v_conv_w, v_conv_b, v_cnorm_g,
                         v_cnorm_b, v_w_o, v_ln2_g, v_w_up, v_ffn_conv_w, v_ffn_conv_b, v_w_down, v_g_final)))
    nb = x.shape[0]
    T = nb * S
    xi, yi, ci = lax.axis_index("x"), lax.axis_index("y"), lax.axis_index("c")
    chip = 2 * xi + yi
    cw_cols = CC // NCHIP
    up_cols = 2 * DFF // NCHIP

    shards = [_pack_in_cols(w_in).astype(BF16), w_o.astype(BF16), w_up.astype(BF16), w_down.astype(BF16),
              jnp.pad(ffn_conv_w, ((0, 0), (0, 8 - FK), (0, 0))),
              jnp.pad(conv_w, ((0, 0), (0, CPAD - CK), (0, LANE - cw_cols)))]
    gathered = _allgather(shards, (True, True, True, True, False, False))
    g_in, g_o, g_up, g_dn, g_fw, g_cw = [
        lax.dynamic_update_slice(g, s[:, None], (0, chip, 0, 0)) for g, s in zip(gathered, shards)]
    up_full = _interleave(g_up.transpose(0, 2, 1, 3).reshape(DEPTH, D, 2 * DFF))
    fw_full = _interleave(g_fw.transpose(0, 2, 1, 3).reshape(DEPTH, 8, 2 * DFF))
    cw_full = g_cw[..., :cw_cols].transpose(0, 2, 1, 3).reshape(DEPTH, CPAD, CC)
    fb_full = _interleave(ffn_conv_b)
    W = {
        "in": [g_in[l].reshape(D, NP) for l in range(DEPTH)],
        "o": [g_o[l].reshape(D, D) for l in range(DEPTH)],
        "up": [up_full[l] for l in range(DEPTH)],
        "down": [g_dn[l].reshape(DFF, D) for l in range(DEPTH)],
        "ln1": [ln1_g[l] for l in range(DEPTH)],
        "ln2": [ln2_g[l] for l in range(DEPTH)],
        "bf": [jnp.pad(b_forget[l], (0, LANE - N_FG)).reshape(1, LANE) for l in range(DEPTH)],
        "gof": [g_out_fox[l].reshape(1, WA) for l in range(DEPTH)],
        "god": [g_out_dil[l].reshape(1, WA) for l in range(DEPTH)],
        "cw": [cw_full[l] for l in range(DEPTH)],
        "cb": [conv_b[l].reshape(1, CC) for l in range(DEPTH)],
        "cng": [cnorm_g[l].reshape(1, CC) for l in range(DEPTH)],
        "cnb": [cnorm_b[l].reshape(1, CC) for l in range(DEPTH)],
        "fw": [fw_full[l] for l in range(DEPTH)],
        "fb": [fb_full[l].reshape(1, 2 * DFF) for l in range(DEPTH)],
        "gfin": g_final,
    }

    loss8, dx, big, small, dgfin = _train_compute(x.reshape(T, D), loss_target.reshape(T, D), W, nb)

    gs = []
    for l in range(DEPTH):
        dWin, dWo, dWup, dWd = big[l]
        gs += [dWin.reshape(NCHIP, D // NCHIP, NP), dWo.reshape(NCHIP, D // NCHIP, D),
               _uninterleave(dWup).reshape(D, NCHIP, up_cols).transpose(1, 0, 2),
               dWd.reshape(NCHIP, DFF // NCHIP, D)]
    r1 = _rs_pair_exchange(gs)
    place = jnp.stack([chip, ci]).astype(jnp.int32)
    hs = [_add_half(g, r, place, f"rs_add_pair_{i}") for i, (g, r) in enumerate(zip(gs, r1))]
    r2 = _rs_chip_scatter(hs)
    fs = [_sum_slots(g, a, b, place, f"rs_add_chips_{i}") for i, (g, a, b) in enumerate(zip(gs, r1, r2))]
    red = _rs_pair_gather(fs)
    red = [r.reshape(r.shape[0] * r.shape[1], r.shape[2]) for r in red]
    grads = {
        "w_in": jnp.stack([_unpack_in_cols(red[4 * l]) for l in range(DEPTH)]),
        "w_o": jnp.stack([red[4 * l + 1] for l in range(DEPTH)]),
        "w_up": jnp.stack([red[4 * l + 2] for l in range(DEPTH)]),
        "w_down": jnp.stack([red[4 * l + 3] for l in range(DEPTH)]),
    }

    parts = []
    for l in range(DEPTH):
        parts += [p.reshape(-1, LANE) for p in small[l]]
    parts += [dgfin.reshape(-1, LANE), loss8]
    tot = _allreduce_small(jnp.concatenate(parts, axis=0))
    off = 0
    per_layer = []
    for l in range(DEPTH):
        vals = []
        for rows in _SMALL_ROWS:
            vals.append(tot[off:off + rows])
            off += rows
        per_layer.append(vals)
    gfin_sum = tot[off:off + D // LANE].reshape(D)
    loss = tot[off + D // LANE, 0]

    def layer_stack(fn):
        return jnp.stack([fn(*per_layer[l]) for l in range(DEPTH)])

    fw_sum = layer_stack(lambda a, b, c_, d, e, f, g: _uninterleave(g.reshape(8, 2 * DFF)))
    cw_sum = layer_stack(lambda a, b, c_, d, e, f, g: d.reshape(CPAD, CC)[:CK])
    sm_sum = layer_stack(lambda a, b, c_, d, e, f, g: e.reshape(8, CC))
    go_sum = layer_stack(lambda a, b, c_, d, e, f, g: c_.reshape(8, WA))
    grads.update({
        "ln1_g": layer_stack(lambda a, b, c_, d, e, f, g: a.reshape(D)),
        "b_forget": layer_stack(lambda a, b, c_, d, e, f, g: b[0, :N_FG]),
        "g_out_fox": go_sum[:, 0],
        "g_out_dil": go_sum[:, 1],
        "conv_w": lax.dynamic_slice_in_dim(cw_sum, chip * cw_cols, cw_cols, axis=2),
        "conv_b": sm_sum[:, 0],
        "cnorm_g": sm_sum[:, 1],
        "cnorm_b": sm_sum[:, 2],
        "ln2_g": layer_stack(lambda a, b, c_, d, e, f, g: f.reshape(D)),
        "ffn_conv_w": lax.dynamic_slice_in_dim(fw_sum[:, :FK], chip * up_cols, up_cols, axis=2),
        "ffn_conv_b": fw_sum[:, FK],
        "g_final": gfin_sum,
    })

    delta, new_m, new_v = {}, {}, {}
    for n in names:
        delta[n], new_m[n], new_v[n] = _adamw(w[n], grads[n], m[n], v[n], "adamw_" + n)
    return (loss, dx.reshape(nb, S, D), *[grads[n] for n in names], *[delta[n] for n in names],
            *[new_m[n] for n in names], *[new_v[n] for n in names])
```

```python
import functools

import jax
import jax.numpy as jnp
from jax import lax
from jax.experimental import pallas as pl
from jax.experimental.pallas import tpu as pltpu

F32 = jnp.float32
BF16 = jnp.bfloat16

D = 1024
S = 2048
DEPTH = 2
HD = 64
WA = 384
NHP = 3
CC = 256
CK = 31
FK = 3
DFF = 2816
NIN = 2822
NP = 3072
SCALE = 0.125
EPS = 1e-6
NEG = -1e30
NCHIP = 4
NDEV = 8
LANE = 128

CB_QA, CB_KA, CB_VA, CB_QB, CB_KB, CB_VB = 0, 3, 6, 9, 12, 15
CB_GV, CB_GG = 9, 10
CB_FA = 22

ADAM_LR, ADAM_B1, ADAM_B2, ADAM_EPS, ADAM_WD, ADAM_STEP = 0.001, 0.9, 0.999, 1e-08, 0.01, 10

VMEM_LIMIT = 56 * 1024 * 1024


def _cparams(sem=None):
    return pltpu.CompilerParams(dimension_semantics=sem, vmem_limit_bytes=VMEM_LIMIT)


def _split3(x):
    hi = x.astype(BF16)
    r1 = x - hi.astype(F32)
    mid = r1.astype(BF16)
    lo = (r1 - mid.astype(F32)).astype(BF16)
    return hi, mid, lo


def _sigmoid(z):
    return 0.5 * jnp.tanh(0.5 * z) + 0.5


def _matmul(a, b, *, ta=False, tb=False, out_dtype=F32, add=None, tm, tn, tk, name):
    M = a.shape[1] if ta else a.shape[0]
    K = a.shape[0] if ta else a.shape[1]
    N = b.shape[0] if tb else b.shape[1]
    assert (b.shape[1] if tb else b.shape[0]) == K
    assert M % tm == 0 and N % tn == 0 and K % tk == 0, (M, N, K, tm, tn, tk)
    nk = K // tk
    dn = (((0 if ta else 1,), (1 if tb else 0,)), ((), ()))

    def body(*refs):
        if add is not None:
            a_ref, b_ref, add_ref, o_ref, acc = refs
        else:
            a_ref, b_ref, o_ref, acc = refs
        k = pl.program_id(2)
        prod = lax.dot_general(a_ref[...].astype(BF16), b_ref[...].astype(BF16), dn, preferred_element_type=F32)

        def finish(r):
            if add is not None:
                r = r + add_ref[...]
            o_ref[...] = r.astype(o_ref.dtype)

        if nk == 1:
            finish(prod)
        else:
            @pl.when(k == 0)
            def _():
                acc[...] = prod

            @pl.when(k > 0)
            def _():
                acc[...] += prod

            @pl.when(k == nk - 1)
            def _():
                finish(acc[...])

    a_spec = pl.BlockSpec((tk, tm), lambda i, j, k: (k, i)) if ta else pl.BlockSpec((tm, tk), lambda i, j, k: (i, k))
    b_spec = pl.BlockSpec((tn, tk), lambda i, j, k: (j, k)) if tb else pl.BlockSpec((tk, tn), lambda i, j, k: (k, j))
    o_spec = pl.BlockSpec((tm, tn), lambda i, j, k: (i, j))
    in_specs = [a_spec, b_spec]
    args = [a, b]
    if add is not None:
        in_specs.append(o_spec)
        args.append(add)
    return pl.pallas_call(
        body, name=name, grid=(M // tm, N // tn, nk),
        in_specs=in_specs, out_specs=o_spec,
        out_shape=jax.ShapeDtypeStruct((M, N), out_dtype),
        scratch_shapes=[pltpu.VMEM((tm, tn) if nk > 1 else (8, 128), F32)],
        compiler_params=_cparams(("parallel", "parallel", "arbitrary")),
    )(*args)


def _rms_fwd(x, g, name):
    T = x.shape[0]
    tr = 512

    def body(x_ref, g_ref, h_ref):
        xv = x_ref[...]
        r = lax.rsqrt(jnp.mean(xv * xv, axis=1, keepdims=True) + EPS)
        h_ref[...] = (xv * r * g_ref[...]).astype(BF16)

    return pl.pallas_call(
        body, name=name, grid=(T // tr,),
        in_specs=[pl.BlockSpec((tr, D), lambda i: (i, 0)), pl.BlockSpec((1, D), lambda i: (0, 0))],
        out_specs=pl.BlockSpec((tr, D), lambda i: (i, 0)),
        out_shape=jax.ShapeDtypeStruct((T, D), BF16),
        compiler_params=_cparams(("parallel",)),
    )(x, g.reshape(1, D))


def _rms_bwd(x, g, dh, dres, name):
    T = x.shape[0]
    tr = 512

    def body(x_ref, g_ref, dh_ref, dres_ref, dx_ref, dxb_ref, dg_ref):
        i = pl.program_id(0)
        xv = x_ref[...]
        dhv = dh_ref[...].astype(F32)
        r = lax.rsqrt(jnp.mean(xv * xv, axis=1, keepdims=True) + EPS)
        a = dhv * g_ref[...]
        dx = dres_ref[...] + r * a - xv * (r * r * r * jnp.mean(xv * a, axis=1, keepdims=True))
        dx_ref[...] = dx
        dxb_ref[...] = dx.astype(BF16)
        part = jnp.sum(dhv * xv * r, axis=0, keepdims=True)

        @pl.when(i == 0)
        def _():
            dg_ref[...] = part

        @pl.when(i > 0)
        def _():
            dg_ref[...] += part

    row = pl.BlockSpec((tr, D), lambda i: (i, 0))
    vec = pl.BlockSpec((1, D), lambda i: (0, 0))
    return pl.pallas_call(
        body, name=name, grid=(T // tr,),
        in_specs=[row, vec, row, row], out_specs=[row, row, vec],
        out_shape=[jax.ShapeDtypeStruct((T, D), F32), jax.ShapeDtypeStruct((T, D), BF16),
                   jax.ShapeDtypeStruct((1, D), F32)],
        compiler_params=_cparams(("arbitrary",)),
    )(x, g.reshape(1, D), dh, dres)


def _loss_head(x, g, target, name):
    T = x.shape[0]
    tr = 512

    def body(x_ref, g_ref, t_ref, loss_ref, dx_ref, dxb_ref, dg_ref):
        i = pl.program_id(0)
        xv = x_ref[...]
        gv = g_ref[...]
        r = lax.rsqrt(jnp.mean(xv * xv, axis=1, keepdims=True) + EPS)
        n = xv * r
        err = n * gv - t_ref[...]
        lpart = 0.5 * jnp.sum(jnp.mean(err * err, axis=1, keepdims=True), axis=0, keepdims=True)
        dy = err * (1.0 / D)
        a = dy * gv
        dx = r * a - xv * (r * r * r * jnp.mean(xv * a, axis=1, keepdims=True))
        dx_ref[...] = dx
        dxb_ref[...] = dx.astype(BF16)
        part = jnp.sum(dy * n, axis=0, keepdims=True)
        lfull = jnp.broadcast_to(lpart, (8, LANE))

        @pl.when(i == 0)
        def _():
            dg_ref[...] = part
            loss_ref[...] = lfull

        @pl.when(i > 0)
        def _():
            dg_ref[...] += part
            loss_ref[...] += lfull

    row = pl.BlockSpec((tr, D), lambda i: (i, 0))
    vec = pl.BlockSpec((1, D), lambda i: (0, 0))
    lsp = pl.BlockSpec((8, LANE), lambda i: (0, 0))
    return pl.pallas_call(
        body, name=name, grid=(T // tr,),
        in_specs=[row, vec, row], out_specs=[lsp, row, row, vec],
        out_shape=[jax.ShapeDtypeStruct((8, LANE), F32), jax.ShapeDtypeStruct((T, D), F32),
                   jax.ShapeDtypeStruct((T, D), BF16), jax.ShapeDtypeStruct((1, D), F32)],
        compiler_params=_cparams(("arbitrary",)),
    )(x, g.reshape(1, D), target)


CUM_BLK = 256


def _tri(n, upper):
    r = lax.broadcasted_iota(jnp.int32, (n, n), 0)
    c = lax.broadcasted_iota(jnp.int32, (n, n), 1)
    return jnp.where((c >= r) if upper else (c <= r), 1.0, 0.0).astype(BF16)


def _tri_apply(tri, x):
    hi, mid, lo = _split3(x)
    out = jnp.dot(tri, hi, preferred_element_type=F32)
    out = out + jnp.dot(tri, mid, preferred_element_type=F32)
    return out + jnp.dot(tri, lo, preferred_element_type=F32)


def _forget_fwd(P, bf_pad, nb, name):
    nblk = S // CUM_BLK

    def body(fa_ref, b_ref, c_ref):
        tri = _tri(CUM_BLK, upper=False)
        carry = jnp.zeros((1, LANE), F32)
        for i in range(nblk):
            z = fa_ref[pl.ds(i * CUM_BLK, CUM_BLK), :] + b_ref[...]
            lf = jnp.minimum(z, 0.0) - jnp.log(1.0 + jnp.exp(-jnp.abs(z)))
            cb = _tri_apply(tri, lf) + carry
            c_ref[pl.ds(i * CUM_BLK, CUM_BLK), :] = cb
            carry = cb[CUM_BLK - 1:CUM_BLK, :]

    return pl.pallas_call(
        body, name=name, grid=(nb,),
        in_specs=[pl.BlockSpec((S, LANE), lambda b: (b, CB_FA)), pl.BlockSpec((1, LANE), lambda b: (0, 0))],
        out_specs=pl.BlockSpec((S, LANE), lambda b: (b, 0)),
        out_shape=jax.ShapeDtypeStruct((nb * S, LANE), F32),
        compiler_params=_cparams(("parallel",)),
    )(P, bf_pad)


def _forget_bwd(P, bf_pad, dcb, nb, name):
    nblk = S // CUM_BLK

    def body(fa_ref, b_ref, dc_ref, dfa_ref, db_ref):
        b = pl.program_id(0)
        tri = _tri(CUM_BLK, upper=True)
        lane = lax.broadcasted_iota(jnp.int32, (CUM_BLK, LANE), 1)
        carry = jnp.zeros((1, LANE), F32)
        dbacc = jnp.zeros((1, LANE), F32)
        for i in reversed(range(nblk)):
            rows = pl.ds(i * CUM_BLK, CUM_BLK)
            dc = jnp.zeros((CUM_BLK, LANE), F32)
            dcv = dc_ref[rows, :]
            for h in range(2 * NHP):
                dc = jnp.where(lane == h, -dcv[:, HD * h:HD * h + 1], dc)
            dl = _tri_apply(tri, dc) + carry
            carry = dl[0:1, :]
            z = fa_ref[rows, :] + b_ref[...]
            dz = jnp.where(lane < 2 * NHP, dl * (1.0 - _sigmoid(z)), 0.0)
            dfa_ref[rows, :] = dz.astype(BF16)
            dbacc = dbacc + jnp.sum(dz, axis=0, keepdims=True)

        dbfull = jnp.broadcast_to(dbacc, (8, LANE))

        @pl.when(b == 0)
        def _():
            db_ref[...] = dbfull

        @pl.when(b > 0)
        def _():
            db_ref[...] += dbfull

    return pl.pallas_call(
        body, name=name, grid=(nb,),
        in_specs=[pl.BlockSpec((S, LANE), lambda b: (b, CB_FA)), pl.BlockSpec((1, LANE), lambda b: (0, 0)),
                  pl.BlockSpec((S, WA), lambda b: (b, 0))],
        out_specs=[pl.BlockSpec((S, LANE), lambda b: (b, 0)), pl.BlockSpec((8, LANE), lambda b: (0, 0))],
        out_shape=[jax.ShapeDtypeStruct((nb * S, LANE), BF16), jax.ShapeDtypeStruct((8, LANE), F32)],
        compiler_params=_cparams(("arbitrary",)),
    )(P, bf_pad, dcb)


FQ = 128
NT_DIMS = (((1,), (1,)), ((), ()))


def _head_masks(shape):
    lane = lax.broadcasted_iota(jnp.int32, shape, 1)
    return lane < HD, lane >= HD


def _fox_operands(q_ref, k_ref, c_ref, hp, qa, ka):
    lane = lax.broadcasted_iota(jnp.int32, (S, LANE), 1)
    masks = _head_masks((S, LANE))
    qv = (q_ref[...] * SCALE).astype(BF16)
    kv = k_ref[...].astype(BF16)
    cv = c_ref[...]
    one = jnp.ones((S, 1), BF16)
    zero = jnp.zeros((S, LANE), BF16)
    for e in range(2):
        base = HD * (1 - e)
        ccol = jnp.sum(jnp.where(lane == 2 * hp + e, cv, 0.0), axis=1, keepdims=True)
        hi, mid, lo = _split3(ccol)
        qe = jnp.where(masks[e], qv, zero)
        ke = jnp.where(masks[e], kv, zero)
        for i, (a, b) in enumerate(((hi, one), (mid, one), (lo, one), (one, -hi), (one, -mid), (one, -lo))):
            qe = jnp.where(lane == base + i, a, qe)
            ke = jnp.where(lane == base + i, b, ke)
        qa[e][...] = qe
        ka[e][...] = ke


def _fox_fwd(P, c, nb, name):
    def body(q_ref, k_ref, v_ref, c_ref, o_ref, lse_ref, qa0, qa1, ka0, ka1, va0, va1):
        hp = pl.program_id(1)
        qa, ka, va = (qa0, qa1), (ka0, ka1), (va0, va1)
        _fox_operands(q_ref, k_ref, c_ref, hp, qa, ka)
        lane = lax.broadcasted_iota(jnp.int32, (S, LANE), 1)
        masks = _head_masks((S, LANE))
        vv = v_ref[...].astype(BF16)
        for e in range(2):
            va[e][...] = jnp.where(masks[e], vv, jnp.where(lane == HD * (1 - e), 1.0, 0.0).astype(BF16))
        tmask = _head_masks((FQ, LANE))
        row = lax.broadcasted_iota(jnp.int32, (FQ, FQ), 0)
        col = lax.broadcasted_iota(jnp.int32, (FQ, FQ), 1)
        for i in range(S // FQ):
            r0 = i * FQ
            res = []
            for e in range(2):
                qt = qa[e][pl.ds(r0, FQ), :]
                sd = lax.dot_general(qt, ka[e][pl.ds(r0, FQ), :], NT_DIMS, preferred_element_type=F32)
                sd = jnp.where(col <= row, sd, NEG)
                m = jnp.max(sd, axis=1, keepdims=True)
                if i > 0:
                    so = lax.dot_general(qt, ka[e][pl.ds(0, r0), :], NT_DIMS, preferred_element_type=F32)
                    m = jnp.maximum(m, jnp.max(so, axis=1, keepdims=True))
                acc = jnp.dot(jnp.exp(sd - m).astype(BF16), va[e][pl.ds(r0, FQ), :], preferred_element_type=F32)
                if i > 0:
                    acc = acc + jnp.dot(jnp.exp(so - m).astype(BF16), va[e][pl.ds(0, r0), :],
                                        preferred_element_type=F32)
                l = acc[:, HD * (1 - e):HD * (1 - e) + 1]
                res.append((acc / l, m + jnp.log(l)))
            o_ref[pl.ds(r0, FQ), :] = jnp.where(tmask[0], res[0][0], res[1][0])
            lse_ref[pl.ds(r0, FQ), :] = jnp.where(tmask[0], res[0][1], res[1][1])

    def colblk(off):
        return pl.BlockSpec((S, LANE), lambda b, hp: (b, off + hp))

    return pl.pallas_call(
        body, name=name, grid=(nb, NHP),
        in_specs=[colblk(CB_QA), colblk(CB_KA), colblk(CB_VA), pl.BlockSpec((S, LANE), lambda b, hp: (b, 0))],
        out_specs=[colblk(0), colblk(0)],
        out_shape=[jax.ShapeDtypeStruct((nb * S, WA), F32), jax.ShapeDtypeStruct((nb * S, WA), F32)],
        scratch_shapes=[pltpu.VMEM((S, LANE), BF16)] * 6,
        compiler_params=_cparams(("parallel", "parallel")),
    )(P, P, P, c)


def _fox_bwd(P, c, o, lse, do, nb, name):
    def body(q_ref, k_ref, v_ref, c_ref, o_ref, lse_ref, do_ref, dq_ref, dk_ref, dv_ref, dc_ref,
             qa0, qa1, ka0, ka1, vm0, vm1, dm0, dm1, kt0, kt1, rows, dqt, rsum):
        hp = pl.program_id(1)
        qa, ka, vm, dm, kt = (qa0, qa1), (ka0, ka1), (vm0, vm1), (dm0, dm1), (kt0, kt1)
        _fox_operands(q_ref, k_ref, c_ref, hp, qa, ka)
        masks = _head_masks((S, LANE))
        vv = v_ref[...].astype(BF16)
        dov = do_ref[...]
        dob = dov.astype(BF16)
        zero = jnp.zeros((S, LANE), BF16)
        ktf = k_ref[...].T
        prodt = (dov * o_ref[...]).T
        lset = lse_ref[...].T
        hrow = lax.broadcasted_iota(jnp.int32, (LANE, S), 0)
        for e in range(2):
            vm[e][...] = jnp.where(masks[e], vv, zero)
            dm[e][...] = jnp.where(masks[e], dob, zero)
            inhead = (hrow >= HD * e) & (hrow < HD * (e + 1))
            kt[e][...] = jnp.where(inhead, ktf, 0.0).astype(BF16)
            rows[e:e + 1, :] = lset[HD * e:HD * e + 1, :]
            rows[2 + e:3 + e, :] = jnp.sum(prodt[HD * e:HD * (e + 1), :], axis=0, keepdims=True)
        dqt[...] = jnp.zeros_like(dqt)
        rsum[...] = jnp.zeros_like(rsum)
        tmask = _head_masks((FQ, LANE))
        row = lax.broadcasted_iota(jnp.int32, (FQ, FQ), 0)
        col = lax.broadcasted_iota(jnp.int32, (FQ, FQ), 1)
        for j in range(S // FQ):
            k0 = j * FQ
            rest = S - k0 - FQ
            spans = [(k0, FQ)] + ([(k0 + FQ, rest)] if rest > 0 else [])
            res = []
            for e in range(2):
                kte = ka[e][pl.ds(k0, FQ), :]
                vte = vm[e][pl.ds(k0, FQ), :]
                ktt = kt[e][:, pl.ds(k0, FQ)]
                dke = jnp.zeros((FQ, LANE), F32)
                dve = jnp.zeros((FQ, LANE), F32)
                cse = jnp.zeros((FQ, 1), F32)
                for si, (q0, n) in enumerate(spans):
                    qs = qa[e][pl.ds(q0, n), :]
                    dos = dm[e][pl.ds(q0, n), :]
                    st = lax.dot_general(kte, qs, NT_DIMS, preferred_element_type=F32)
                    if si == 0:
                        st = jnp.where(col >= row, st, NEG)
                    pt = jnp.exp(st - rows[e:e + 1, pl.ds(q0, n)])
                    dpt = lax.dot_general(vte, dos, NT_DIMS, preferred_element_type=F32)
                    dst = pt * (dpt - rows[2 + e:3 + e, pl.ds(q0, n)])
                    dsb = dst.astype(BF16)
                    dve = dve + jnp.dot(pt.astype(BF16), dos, preferred_element_type=F32)
                    dke = dke + jnp.dot(dsb, qs, preferred_element_type=F32)
                    dqt[:, pl.ds(q0, n)] += jnp.dot(ktt, dsb, preferred_element_type=F32)
                    cse = cse + jnp.sum(dst, axis=1, keepdims=True)
                    rsum[HD * e:HD * e + 1, pl.ds(q0, n)] += jnp.sum(dst, axis=0, keepdims=True)
                res.append((dke, dve, cse))
            dk_ref[pl.ds(k0, FQ), :] = jnp.where(tmask[0], res[0][0], res[1][0]).astype(BF16)
            dv_ref[pl.ds(k0, FQ), :] = jnp.where(tmask[0], res[0][1], res[1][1]).astype(BF16)
            dc_ref[pl.ds(k0, FQ), :] = jnp.where(tmask[0], res[0][2], res[1][2])
        dq_ref[...] = (dqt[...].T * SCALE).astype(BF16)
        dc_ref[...] = dc_ref[...] - rsum[...].T

    def colblk(off):
        return pl.BlockSpec((S, LANE), lambda b, hp: (b, off + hp))

    return pl.pallas_call(
        body, name=name, grid=(nb, NHP),
        in_specs=[colblk(CB_QA), colblk(CB_KA), colblk(CB_VA), pl.BlockSpec((S, LANE), lambda b, hp: (b, 0)),
                  colblk(0), colblk(0), colblk(0)],
        out_specs=[colblk(0)] * 4,
        out_shape=[jax.ShapeDtypeStruct((nb * S, WA), BF16)] * 3 + [jax.ShapeDtypeStruct((nb * S, WA), F32)],
        scratch_shapes=[pltpu.VMEM((S, LANE), BF16)] * 8 + [pltpu.VMEM((LANE, S), BF16)] * 2
        + [pltpu.VMEM((8, S), F32), pltpu.VMEM((LANE, S), F32), pltpu.VMEM((LANE, S), F32)],
        compiler_params=_cparams(("parallel", "parallel")),
    )(P, P, P, c, o, lse, do)


DILS = (1, 4, 16)
DB = 128


def _regroup_load(ref, d, scale=None):
    if d == 1:
        v = ref[...]
    else:
        L = S // d
        v = jnp.concatenate([ref[pl.ds(r, L, stride=d), :] for r in range(d)], axis=0)
    return v if scale is None else v * scale


def _regroup_store(ref, d, val_ref, accumulate):
    L = S // d
    for r in range(d):
        src = val_ref[pl.ds(r * L, L), :]
        dst = (slice(None), slice(None)) if d == 1 else (pl.ds(r, L, stride=d), slice(None))
        if accumulate:
            ref[dst] = ref[dst] + src
        else:
            ref[dst] = src


def _dil_bands():
    qi = lax.broadcasted_iota(jnp.int32, (DB, 2 * DB), 0)
    ki = lax.broadcasted_iota(jnp.int32, (DB, 2 * DB), 1)
    band = (ki >= qi) & (ki <= qi + DB)
    return band, band & (ki >= DB)


def _dil_valid(bands, bk, d):
    band, own = bands
    has_prev = (bk % ((S // d) // DB)) > 0
    return own | (band & has_prev)


def _dil_fwd(P, nb, name):
    nblk = S // DB

    def body(q_ref, k_ref, v_ref, o_ref, lse_ref, qd, kd, vd, rnum, rm, rl, num_n, m_n, l_n):
        masks = _head_masks((DB, LANE))
        bands = _dil_bands()
        for bi, d in enumerate(DILS):
            qd[...] = _regroup_load(q_ref, d, SCALE).astype(BF16)
            kd[pl.ds(0, DB), :] = jnp.zeros((DB, LANE), BF16)
            vd[pl.ds(0, DB), :] = jnp.zeros((DB, LANE), BF16)
            kd[pl.ds(DB, S), :] = _regroup_load(k_ref, d).astype(BF16)
            vd[pl.ds(DB, S), :] = _regroup_load(v_ref, d).astype(BF16)

            def blk(bk, _, d=d):
                r0 = pl.multiple_of(bk * DB, DB)
                qt = qd[pl.ds(r0, DB), :]
                kk = kd[pl.ds(r0, 2 * DB), :]
                vv = vd[pl.ds(r0, 2 * DB), :]
                valid = _dil_valid(bands, bk, d)
                res = []
                for e in range(2):
                    qm = jnp.where(masks[e], qt, jnp.zeros_like(qt))
                    s = lax.dot_general(qm, kk, (((1,), (1,)), ((), ())), preferred_element_type=F32)
                    s = jnp.where(valid, s, NEG)
                    m = jnp.max(s, axis=1, keepdims=True)
                    p = jnp.exp(s - m)
                    l = jnp.sum(p, axis=1, keepdims=True)
                    num = jnp.dot(p.astype(BF16), vv, preferred_element_type=F32)
                    res.append((m, l, num))
                (m0, l0, n0), (m1, l1, n1) = res
                rnum[pl.ds(r0, DB), :] = jnp.where(masks[0], n0, n1)
                rm[pl.ds(r0, DB), :] = jnp.where(masks[0], m0, m1)
                rl[pl.ds(r0, DB), :] = jnp.where(masks[0], l0, l1)
                return 0

            lax.fori_loop(0, nblk, blk, 0, unroll=4)
            _regroup_store(num_n.at[bi], d, rnum, False)
            _regroup_store(m_n.at[bi], d, rm, False)
            _regroup_store(l_n.at[bi], d, rl, False)

        m_all = jnp.maximum(jnp.maximum(m_n[0], m_n[1]), m_n[2])
        num = jnp.zeros((S, LANE), F32)
        den = jnp.zeros((S, LANE), F32)
        for bi in range(3):
            a = jnp.exp(m_n[bi] - m_all)
            num = num + a * num_n[bi]
            den = den + a * l_n[bi]
        o_ref[...] = num / den
        lse_ref[...] = m_all + jnp.log(den)

    def colblk(off):
        return pl.BlockSpec((S, LANE), lambda b, hp: (b, off + hp))

    return pl.pallas_call(
        body, name=name, grid=(nb, NHP),
        in_specs=[colblk(CB_QB), colblk(CB_KB), colblk(CB_VB)],
        out_specs=[colblk(0), colblk(0)],
        out_shape=[jax.ShapeDtypeStruct((nb * S, WA), F32), jax.ShapeDtypeStruct((nb * S, WA), F32)],
        scratch_shapes=[pltpu.VMEM((S, LANE), BF16), pltpu.VMEM((S + DB, LANE), BF16), pltpu.VMEM((S + DB, LANE), BF16),
                        pltpu.VMEM((S, LANE), F32), pltpu.VMEM((S, LANE), F32), pltpu.VMEM((S, LANE), F32),
                        pltpu.VMEM((3, S, LANE), F32), pltpu.VMEM((3, S, LANE), F32), pltpu.VMEM((3, S, LANE), F32)],
        compiler_params=_cparams(("parallel", "parallel")),
    )(P, P, P)


def _dil_bwd(P, o, lse, do, nb, name):
    nblk = S // DB

    def body(q_ref, k_ref, v_ref, o_ref, lse_ref, do_ref, dq_ref, dk_ref, dv_ref,
             qd, kd, vd, dod, lsed, dsd, dsum, dq_r, dk_r, dv_r, dq_n, dk_n, dv_n):
        masks = _head_masks((DB, LANE))
        fmask = _head_masks((S, LANE))
        prod = do_ref[...] * o_ref[...]
        d0 = jnp.sum(jnp.where(fmask[0], prod, 0.0), axis=1, keepdims=True)
        d1 = jnp.sum(jnp.where(fmask[1], prod, 0.0), axis=1, keepdims=True)
        dsum[...] = jnp.where(fmask[0], d0, d1)
        tn = (((0,), (0,)), ((), ()))
        bands = _dil_bands()
        for bi, d in enumerate(DILS):
            qd[...] = _regroup_load(q_ref, d, SCALE).astype(BF16)
            kd[pl.ds(0, DB), :] = jnp.zeros((DB, LANE), BF16)
            vd[pl.ds(0, DB), :] = jnp.zeros((DB, LANE), BF16)
            kd[pl.ds(DB, S), :] = _regroup_load(k_ref, d).astype(BF16)
            vd[pl.ds(DB, S), :] = _regroup_load(v_ref, d).astype(BF16)
            dod[...] = _regroup_load(do_ref, d).astype(BF16)
            lsed[...] = _regroup_load(lse_ref, d)
            dsd[...] = _regroup_load(dsum, d)
            dq_r[...] = jnp.zeros_like(dq_r)
            dk_r[...] = jnp.zeros_like(dk_r)
            dv_r[...] = jnp.zeros_like(dv_r)

            def blk(bk, _, d=d):
                r0 = pl.multiple_of(bk * DB, DB)
                qt = qd[pl.ds(r0, DB), :]
                dot = dod[pl.ds(r0, DB), :]
                lt = lsed[pl.ds(r0, DB), :]
                dt = dsd[pl.ds(r0, DB), :]
                kk = kd[pl.ds(r0, 2 * DB), :]
                vv = vd[pl.ds(r0, 2 * DB), :]
                kmask = _head_masks((2 * DB, LANE))
                valid = _dil_valid(bands, bk, d)
                dqt = jnp.zeros((DB, LANE), F32)
                dkt = jnp.zeros((2 * DB, LANE), F32)
                dvt = jnp.zeros((2 * DB, LANE), F32)
                for e in range(2):
                    qm = jnp.where(masks[e], qt, jnp.zeros_like(qt))
                    dom = jnp.where(masks[e], dot, jnp.zeros_like(dot))
                    km = jnp.where(kmask[e], kk, jnp.zeros_like(kk))
                    s = lax.dot_general(qm, kk, (((1,), (1,)), ((), ())), preferred_element_type=F32)
                    s = jnp.where(valid, s, NEG)
                    p = jnp.exp(s - lt[:, HD * e:HD * e + 1])
                    dp = lax.dot_general(dom, vv, (((1,), (1,)), ((), ())), preferred_element_type=F32)
                    ds = (p * (dp - dt[:, HD * e:HD * e + 1])).astype(BF16)
                    dvt = dvt + lax.dot_general(p.astype(BF16), dom, tn, preferred_element_type=F32)
                    dkt = dkt + lax.dot_general(ds, qm, tn, preferred_element_type=F32)
                    dqt = dqt + jnp.dot(ds, km, preferred_element_type=F32)
                dq_r[pl.ds(r0, DB), :] = dqt
                dk_r[pl.ds(r0, 2 * DB), :] += dkt
                dv_r[pl.ds(r0, 2 * DB), :] += dvt
                return 0

            lax.fori_loop(0, nblk, blk, 0, unroll=2)
            _regroup_store(dq_n, d, dq_r, bi > 0)
            _regroup_store(dk_n, d, dk_r.at[pl.ds(DB, S)], bi > 0)
            _regroup_store(dv_n, d, dv_r.at[pl.ds(DB, S)], bi > 0)

        dq_ref[...] = (dq_n[...] * SCALE).astype(BF16)
        dk_ref[...] = dk_n[...].astype(BF16)
        dv_ref[...] = dv_n[...].astype(BF16)

    def colblk(off):
        return pl.BlockSpec((S, LANE), lambda b, hp: (b, off + hp))

    big = pltpu.VMEM((S, LANE), F32)
    bigp = pltpu.VMEM((S + DB, LANE), F32)
    return pl.pallas_call(
        body, name=name, grid=(nb, NHP),
        in_specs=[colblk(CB_QB), colblk(CB_KB), colblk(CB_VB), colblk(0), colblk(0), colblk(0)],
        out_specs=[colblk(0)] * 3,
        out_shape=[jax.ShapeDtypeStruct((nb * S, WA), BF16)] * 3,
        scratch_shapes=[pltpu.VMEM((S, LANE), BF16), pltpu.VMEM((S + DB, LANE), BF16), pltpu.VMEM((S + DB, LANE), BF16),
                        pltpu.VMEM((S, LANE), BF16), big, big, big, big, bigp, bigp, big, big, big],
        compiler_params=_cparams(("parallel", "parallel")),
    )(P, P, P, o, lse, do)


RC = 256
CPAD = 32


def _conv_chunk(gpad, r0, cw_ref, cb_ref):
    acc = jnp.zeros((RC, CC), F32) + cb_ref[...]
    for k in range(CK):
        acc = acc + cw_ref[k:k + 1, :] * gpad[pl.ds(r0 + CPAD - (CK - 1) + k, RC), :]
    return acc


def _cnorm(c0, cng_ref, cnb_ref):
    mu = jnp.mean(c0, axis=1, keepdims=True)
    xc = c0 - mu
    rstd = lax.rsqrt(jnp.mean(xc * xc, axis=1, keepdims=True) + EPS)
    n = xc * rstd
    return n, rstd, n * cng_ref[...] + cnb_ref[...]


NORM_ROWS = 512
YC_BLK = 2 * WA // CC


def _attn_norm_fwd(of, od, gof, god, name):
    T = of.shape[0]

    def body(of_ref, od_ref, gof_ref, god_ref, y_ref):
        for i, (src, g_ref) in enumerate(((of_ref, gof_ref), (od_ref, god_ref))):
            v = src[...]
            r = lax.rsqrt(jnp.mean(v * v, axis=1, keepdims=True) + EPS)
            y_ref[:, i * WA:(i + 1) * WA] = (v * r * g_ref[...]).astype(BF16)

    row = lambda w: pl.BlockSpec((NORM_ROWS, w), lambda i: (i, 0))
    par = pl.BlockSpec((1, WA), lambda i: (0, 0))
    return pl.pallas_call(
        body, name=name, grid=(T // NORM_ROWS,),
        in_specs=[row(WA), row(WA), par, par], out_specs=row(2 * WA),
        out_shape=jax.ShapeDtypeStruct((T, D), BF16),
        compiler_params=_cparams(("parallel",)),
    )(of, od, gof, god)


def _attn_norm_bwd(of, od, dy, gof, god, name):
    T = of.shape[0]

    def body(of_ref, od_ref, dy_ref, gof_ref, god_ref, dof_ref, dod_ref, dgo_ref):
        @pl.when(pl.program_id(0) == 0)
        def _():
            dgo_ref[...] = jnp.zeros_like(dgo_ref)

        for i, (src, g_ref, dst) in enumerate(((of_ref, gof_ref, dof_ref), (od_ref, god_ref, dod_ref))):
            v = src[...]
            dyv = dy_ref[:, i * WA:(i + 1) * WA].astype(F32)
            r = lax.rsqrt(jnp.mean(v * v, axis=1, keepdims=True) + EPS)
            a = dyv * g_ref[...]
            dst[...] = r * a - v * (r * r * r * jnp.mean(v * a, axis=1, keepdims=True))
            dgo_ref[i:i + 1, :] += jnp.sum(dyv * v * r, axis=0, keepdims=True)

    row = lambda w: pl.BlockSpec((NORM_ROWS, w), lambda i: (i, 0))
    par = pl.BlockSpec((1, WA), lambda i: (0, 0))
    return pl.pallas_call(
        body, name=name, grid=(T // NORM_ROWS,),
        in_specs=[row(WA), row(WA), row(2 * WA), par, par],
        out_specs=[row(WA), row(WA), pl.BlockSpec((8, WA), lambda i: (0, 0))],
        out_shape=[jax.ShapeDtypeStruct((T, WA), F32), jax.ShapeDtypeStruct((T, WA), F32),
                   jax.ShapeDtypeStruct((8, WA), F32)],
        compiler_params=_cparams(("arbitrary",)),
    )(of, od, dy, gof, god)


def _conv_specs():
    gblk = lambda off: pl.BlockSpec((S, CC), lambda b: (b, off))
    par = lambda r: pl.BlockSpec((r, CC), lambda b: (0, 0))
    return gblk, par


def _conv_fwd(P, y, cw, cb, cng, cnb, nb, name):
    def body(gv_ref, gg_ref, cw_ref, cb_ref, cng_ref, cnb_ref, y_in, y_ref, c0_ref, gpad):
        del y_in
        gpad[pl.ds(0, CPAD), :] = jnp.zeros((CPAD, CC), F32)
        gpad[pl.ds(CPAD, S), :] = gv_ref[...] * _sigmoid(gg_ref[...])
        for ci in range(S // RC):
            r0 = ci * RC
            c0 = _conv_chunk(gpad, r0, cw_ref, cb_ref)
            c0_ref[pl.ds(r0, RC), :] = c0
            _, _, z = _cnorm(c0, cng_ref, cnb_ref)
            y_ref[pl.ds(r0, RC), :] = (z * _sigmoid(z)).astype(BF16)

    gblk, par = _conv_specs()
    return pl.pallas_call(
        body, name=name, grid=(nb,),
        in_specs=[gblk(CB_GV), gblk(CB_GG), par(CPAD), par(1), par(1), par(1), pl.BlockSpec(memory_space=pl.ANY)],
        out_specs=[gblk(YC_BLK), gblk(0)],
        out_shape=[jax.ShapeDtypeStruct((nb * S, D), BF16), jax.ShapeDtypeStruct((nb * S, CC), F32)],
        input_output_aliases={6: 0},
        scratch_shapes=[pltpu.VMEM((S + CPAD, CC), F32)],
        compiler_params=_cparams(("parallel",)),
    )(P, P, cw, cb, cng, cnb, y)


def _conv_bwd(P, c0, dy, cw, cng, cnb, nb, name):
    def body(gv_ref, gg_ref, c0_ref, dy_ref, cw_ref, cng_ref, cnb_ref, dg_ref, dcw_ref, dsm_ref, dpad):
        @pl.when(pl.program_id(0) == 0)
        def _():
            dcw_ref[...] = jnp.zeros_like(dcw_ref)
            dsm_ref[...] = jnp.zeros_like(dsm_ref)

        dpad[pl.ds(S, CPAD), :] = jnp.zeros((CPAD, CC), F32)
        zero = jnp.zeros((1, CC), F32)
        dcb, dcng, dcnb = zero, zero, zero
        for ci in range(S // RC):
            r0 = ci * RC
            n, rstd, z = _cnorm(c0_ref[pl.ds(r0, RC), :], cng_ref, cnb_ref)
            sz = _sigmoid(z)
            dz = dy_ref[pl.ds(r0, RC), :].astype(F32) * (sz * (1.0 + z * (1.0 - sz)))
            dcng = dcng + jnp.sum(dz * n, axis=0, keepdims=True)
            dcnb = dcnb + jnp.sum(dz, axis=0, keepdims=True)
            dn = dz * cng_ref[...]
            dc0 = rstd * (dn - jnp.mean(dn, axis=1, keepdims=True) - n * jnp.mean(dn * n, axis=1, keepdims=True))
            dcb = dcb + jnp.sum(dc0, axis=0, keepdims=True)
            dpad[pl.ds(r0, RC), :] = dc0
        dsm_ref[0:1, :] += dcb
        dsm_ref[1:2, :] += dcng
        dsm_ref[2:3, :] += dcnb

        dws = [zero] * CK
        for ci in range(S // RC):
            r0 = ci * RC
            sg = _sigmoid(gg_ref[pl.ds(r0, RC), :])
            gvc = gv_ref[pl.ds(r0, RC), :]
            glu = gvc * sg
            dgl = jnp.zeros((RC, CC), F32)
            for k in range(CK):
                win = dpad[pl.ds(r0 + (CK - 1) - k, RC), :]
                dws[k] = dws[k] + jnp.sum(win * glu, axis=0, keepdims=True)
                dgl = dgl + cw_ref[k:k + 1, :] * win
            dg_ref[pl.ds(r0, RC), 0:CC] = (dgl * sg).astype(BF16)
            dg_ref[pl.ds(r0, RC), CC:2 * CC] = (dgl * gvc * sg * (1.0 - sg)).astype(BF16)
        for k in range(CK):
            dcw_ref[k:k + 1, :] += dws[k]

    gblk, par = _conv_specs()
    return pl.pallas_call(
        body, name=name, grid=(nb,),
        in_specs=[gblk(CB_GV), gblk(CB_GG), gblk(0), gblk(YC_BLK), par(CPAD), par(1), par(1)],
        out_specs=[pl.BlockSpec((S, 2 * CC), lambda b: (b, 0)), par(CPAD), par(8)],
        out_shape=[jax.ShapeDtypeStruct((nb * S, 2 * CC), BF16), jax.ShapeDtypeStruct((CPAD, CC), F32),
                   jax.ShapeDtypeStruct((8, CC), F32)],
        scratch_shapes=[pltpu.VMEM((S + CPAD, CC), F32)],
        compiler_params=_cparams(("arbitrary",)),
    )(P, P, c0, dy, cw, cng, cnb)


FC = 512
FPAD = 8
NFB = 2 * DFF // FC


def _ffn_u2_chunk(upad, r0, fw_ref, fb_ref):
    acc = jnp.zeros((RC, FC), F32) + fb_ref[...]
    for k in range(FK):
        acc = acc + fw_ref[k:k + 1, :] * upad[pl.ds(r0 + FPAD - (FK - 1) + k, RC), :]
    return acc


def _ffn_fwd(U, fw, fb, nb, name):
    def body(u_ref, fw_ref, fb_ref, h_ref, u2_ref, upad):
        upad[pl.ds(0, FPAD), :] = jnp.zeros((FPAD, FC), F32)
        upad[pl.ds(FPAD, S), :] = u_ref[...].astype(F32)
        for ci in range(S // RC):
            r0 = ci * RC
            u2 = _ffn_u2_chunk(upad, r0, fw_ref, fb_ref)
            u2_ref[pl.ds(r0, RC), :] = u2.astype(BF16)
            a2, b2 = u2[:, :FC // 2], u2[:, FC // 2:]
            h_ref[pl.ds(r0, RC), :] = (a2 * _sigmoid(a2) * b2).astype(BF16)

    return pl.pallas_call(
        body, name=name, grid=(nb, NFB),
        in_specs=[pl.BlockSpec((S, FC), lambda b, j: (b, j)), pl.BlockSpec((8, FC), lambda b, j: (0, j)),
                  pl.BlockSpec((1, FC), lambda b, j: (0, j))],
        out_specs=[pl.BlockSpec((S, FC // 2), lambda b, j: (b, j)), pl.BlockSpec((S, FC), lambda b, j: (b, j))],
        out_shape=[jax.ShapeDtypeStruct((nb * S, DFF), BF16), jax.ShapeDtypeStruct((nb * S, 2 * DFF), BF16)],
        scratch_shapes=[pltpu.VMEM((S + FPAD, FC), F32)],
        compiler_params=_cparams(("parallel", "parallel")),
    )(U, fw, fb)


def _ffn_bwd(U, U2, dhid, fw, nb, name):
    def body(u_ref, u2_ref, dh_ref, fw_ref, du_ref, dfw_ref, dpad):
        @pl.when(pl.program_id(1) == 0)
        def _():
            dfw_ref[...] = jnp.zeros_like(dfw_ref)

        dpad[pl.ds(S, FPAD), :] = jnp.zeros((FPAD, FC), F32)
        zero = jnp.zeros((1, FC), F32)
        dbias = zero
        for ci in range(S // RC):
            r0 = ci * RC
            u2 = u2_ref[pl.ds(r0, RC), :].astype(F32)
            a2, b2 = u2[:, :FC // 2], u2[:, FC // 2:]
            sa = _sigmoid(a2)
            dh = dh_ref[pl.ds(r0, RC), :].astype(F32)
            du2 = jnp.concatenate([dh * b2 * (sa * (1.0 + a2 * (1.0 - sa))), dh * a2 * sa], axis=1)
            dpad[pl.ds(r0, RC), :] = du2
            dbias = dbias + jnp.sum(du2, axis=0, keepdims=True)
        dws = [zero] * FK
        for ci in range(S // RC):
            r0 = ci * RC
            uc = u_ref[pl.ds(r0, RC), :].astype(F32)
            du = jnp.zeros((RC, FC), F32)
            for k in range(FK):
                win = dpad[pl.ds(r0 + (FK - 1) - k, RC), :]
                dws[k] = dws[k] + jnp.sum(win * uc, axis=0, keepdims=True)
                du = du + fw_ref[k:k + 1, :] * win
            du_ref[pl.ds(r0, RC), :] = du.astype(BF16)
        for k in range(FK):
            dfw_ref[k:k + 1, :] += dws[k]
        dfw_ref[FK:FK + 1, :] += dbias

    blk = pl.BlockSpec((S, FC), lambda j, b: (b, j))
    return pl.pallas_call(
        body, name=name, grid=(NFB, nb),
        in_specs=[blk, blk, pl.BlockSpec((S, FC // 2), lambda j, b: (b, j)), pl.BlockSpec((8, FC), lambda j, b: (0, j))],
        out_specs=[blk, pl.BlockSpec((8, FC), lambda j, b: (0, j))],
        out_shape=[jax.ShapeDtypeStruct((nb * S, 2 * DFF), BF16), jax.ShapeDtypeStruct((8, 2 * DFF), F32)],
        scratch_shapes=[pltpu.VMEM((S + FPAD, FC), F32)],
        compiler_params=_cparams(("parallel", "arbitrary")),
    )(U, U2, dhid, fw)


def _matmul_ffn(a, b, mode, *, out_dtype=F32, tm=1024, tk=512, name):
    HF = FC // 2
    if mode == "fwd":
        M, K = a.shape

        def body(a_ref, b1_ref, b2_ref, o_ref):
            av = a_ref[...]
            o_ref[:, :HF] = jnp.dot(av, b1_ref[...], preferred_element_type=F32).astype(o_ref.dtype)
            o_ref[:, HF:] = jnp.dot(av, b2_ref[...], preferred_element_type=F32).astype(o_ref.dtype)

        return pl.pallas_call(
            body, name=name, grid=(M // tm, NFB),
            in_specs=[pl.BlockSpec((tm, K), lambda i, j: (i, 0)), pl.BlockSpec((K, HF), lambda i, j: (0, j)),
                      pl.BlockSpec((K, HF), lambda i, j: (0, NFB + j))],
            out_specs=pl.BlockSpec((tm, FC), lambda i, j: (i, j)),
            out_shape=jax.ShapeDtypeStruct((M, 2 * DFF), out_dtype),
            compiler_params=_cparams(("parallel", "parallel")),
        )(a, b, b)
    if mode == "dx":
        M = a.shape[0]
        N = b.shape[0]

        def body(a_ref, b1_ref, b2_ref, o_ref, acc):
            k = pl.program_id(1)
            prod = lax.dot_general(a_ref[:, :HF], b1_ref[...], NT_DIMS, preferred_element_type=F32)
            prod = prod + lax.dot_general(a_ref[:, HF:], b2_ref[...], NT_DIMS, preferred_element_type=F32)

            @pl.when(k == 0)
            def _():
                acc[...] = prod

            @pl.when(k > 0)
            def _():
                acc[...] += prod

            @pl.when(k == NFB - 1)
            def _():
                o_ref[...] = acc[...].astype(o_ref.dtype)

        return pl.pallas_call(
            body, name=name, grid=(M // tm, NFB),
            in_specs=[pl.BlockSpec((tm, FC), lambda i, k: (i, k)), pl.BlockSpec((N, HF), lambda i, k: (0, k)),
                      pl.BlockSpec((N, HF), lambda i, k: (0, NFB + k))],
            out_specs=pl.BlockSpec((tm, N), lambda i, k: (i, 0)),
            out_shape=jax.ShapeDtypeStruct((M, N), out_dtype),
            scratch_shapes=[pltpu.VMEM((tm, N), F32)],
            compiler_params=_cparams(("parallel", "arbitrary")),
        )(a, b, b)
    assert mode == "dw"
    T, M = a.shape
    nk = T // tk

    def body(a_ref, g_ref, oa_ref, ob_ref, acc):
        k = pl.program_id(1)
        prod = lax.dot_general(a_ref[...], g_ref[...], (((0,), (0,)), ((), ())), preferred_element_type=F32)

        @pl.when(k == 0)
        def _():
            acc[...] = prod

        @pl.when(k > 0)
        def _():
            acc[...] += prod

        @pl.when(k == nk - 1)
        def _():
            oa_ref[...] = acc[:, :HF]
            ob_ref[...] = acc[:, HF:]

    half = pl.BlockSpec((M, HF), lambda j, k: (0, j))
    return pl.pallas_call(
        body, name=name, grid=(NFB, nk),
        in_specs=[pl.BlockSpec((tk, M), lambda j, k: (k, 0)), pl.BlockSpec((tk, FC), lambda j, k: (k, j))],
        out_specs=[half, half],
        out_shape=[jax.ShapeDtypeStruct((M, DFF), F32)] * 2,
        scratch_shapes=[pltpu.VMEM((M, FC), F32)],
        compiler_params=_cparams(("parallel", "arbitrary")),
    )(a, b)


def _adamw_body(w_ref, g_ref, m_ref, v_ref, d_ref, nm_ref, nv_ref):
    g = g_ref[...]
    m = ADAM_B1 * m_ref[...] + (1.0 - ADAM_B1) * g
    v = ADAM_B2 * v_ref[...] + (1.0 - ADAM_B2) * (g * g)
    m_hat = m / (1.0 - ADAM_B1 ** ADAM_STEP)
    v_hat = v / (1.0 - ADAM_B2 ** ADAM_STEP)
    d_ref[...] = -ADAM_LR * (m_hat / (jnp.sqrt(v_hat) + ADAM_EPS) + ADAM_WD * w_ref[...])
    nm_ref[...] = m
    nv_ref[...] = v


def _adamw(w, g, m, v, name):
    shape = w.shape
    R = 1
    for s in shape[:-1]:
        R *= s
    C = shape[-1]
    args = [a.reshape(R, C) for a in (w, g, m, v)]
    tr = R
    for cand in (512, 352, 256, 128, 64, 32, 16, 8):
        if R % cand == 0 and cand * C * 4 * 14 <= 24 * 1024 * 1024:
            tr = cand
            break
    blk = pl.BlockSpec((tr, C), lambda i: (i, 0))
    outs = pl.pallas_call(
        functools.partial(_adamw_body), name=name, grid=(R // tr,),
        in_specs=[blk] * 4, out_specs=[blk] * 3,
        out_shape=[jax.ShapeDtypeStruct((R, C), F32)] * 3,
        compiler_params=_cparams(("parallel",)),
    )(*args)
    return [o.reshape(shape) for o in outs]


def _rs_row_tile(H):
    th = 128 if H % 128 == 0 else 176
    assert H % th == 0
    return th


def _add_half(g, r1, place, name):
    _, R, C = g.shape
    H = R // 2
    th = _rs_row_tile(H)
    nh = H // th

    def body(s_ref, g_ref, r_ref, o_ref):
        o_ref[...] = (g_ref[...] + r_ref[...]).astype(BF16)

    grid_spec = pltpu.PrefetchScalarGridSpec(
        num_scalar_prefetch=1, grid=(NCHIP, nh),
        in_specs=[pl.BlockSpec((None, th, C), lambda p, i, s: (p, s[1] * nh + i, 0)),
                  pl.BlockSpec((None, th, C), lambda p, i, s: (p, i, 0))],
        out_specs=pl.BlockSpec((None, th, C), lambda p, i, s: (p, i, 0)))
    return pl.pallas_call(
        body, name=name, grid_spec=grid_spec, out_shape=jax.ShapeDtypeStruct((NCHIP, H, C), BF16),
        compiler_params=_cparams(("parallel", "parallel")),
    )(place, g, r1)


def _sum_slots(g, r1, r2, place, name):
    _, R, C = g.shape
    H = R // 2
    th = _rs_row_tile(H)
    nh = H // th

    def body(s_ref, g_ref, r1_ref, r2_ref, o_ref):
        acc = g_ref[...] + r1_ref[...]
        for j in range(NCHIP - 1):
            acc = acc + r2_ref[j].astype(F32)
        o_ref[...] = acc

    grid_spec = pltpu.PrefetchScalarGridSpec(
        num_scalar_prefetch=1, grid=(nh,),
        in_specs=[pl.BlockSpec((None, th, C), lambda i, s: (s[0], s[1] * nh + i, 0)),
                  pl.BlockSpec((None, th, C), lambda i, s: (s[0], i, 0)),
                  pl.BlockSpec((NCHIP - 1, th, C), lambda i, s: (0, i, 0))],
        out_specs=pl.BlockSpec((None, th, C), lambda i, s: (s[1], i, 0)))
    return pl.pallas_call(
        body, name=name, grid_spec=grid_spec, out_shape=jax.ShapeDtypeStruct((2, H, C), F32),
        compiler_params=_cparams(("parallel",)),
    )(place, g, r1, r2)


MESH = pl.DeviceIdType.MESH
HBM = pl.BlockSpec(memory_space=pltpu.HBM)


def _place():
    x, y, c = lax.axis_index("x"), lax.axis_index("y"), lax.axis_index("c")
    chips = [(1 - x, y), (x, 1 - y), (1 - x, 1 - y)]
    return x, y, c, chips


def _rcopy(src, dst, ssem, rsem, dev):
    return pltpu.make_async_remote_copy(src_ref=src, dst_ref=dst, send_sem=ssem, recv_sem=rsem,
                                        device_id=dev, device_id_type=MESH)


def _allgather(shards, split):
    n = len(shards)

    def body(*refs):
        ins, outs = refs[:n], refs[n:2 * n]
        ssem, rsem, fssem, frsem = refs[2 * n:]
        x, y, c, chips = _place()
        me = 2 * x + y
        sib = (x, y, 1 - c)

        def window(t, chip, half):
            if not split[t]:
                return outs[t].at[:, chip]
            H = shards[t].shape[1] // 2
            return outs[t].at[:, chip, pl.ds(half * H, H)]

        sends = []
        for t in range(n):
            H = shards[t].shape[1] // 2
            src = ins[t].at[:, pl.ds(c * H, H)] if split[t] else ins[t]
            for j, (cx, cy) in enumerate(chips):
                cp = _rcopy(src, window(t, me, c), ssem.at[3 * t + j], rsem.at[3 * t + j], (cx, cy, c))
                cp.start()
                sends.append(cp)
        for t in range(n):
            for j, (cx, cy) in enumerate(chips):
                win = window(t, 2 * cx + cy, c)
                _rcopy(win, win, ssem.at[3 * t + j], rsem.at[3 * t + j], (cx, cy, c)).wait_recv()
                if split[t]:
                    cp = _rcopy(win, win, fssem.at[3 * t + j], frsem.at[3 * t + j], sib)
                    cp.start()
                    sends.append(cp)
        for t in range(n):
            if split[t]:
                for j, (cx, cy) in enumerate(chips):
                    win = window(t, 2 * cx + cy, 1 - c)
                    _rcopy(win, win, fssem.at[3 * t + j], frsem.at[3 * t + j], sib).wait_recv()
        for cp in sends:
            cp.wait_send()

    out_shape = [jax.ShapeDtypeStruct((s.shape[0], NCHIP) + s.shape[1:], s.dtype) for s in shards]
    return pl.pallas_call(
        body, name="allgather_weights", in_specs=[HBM] * n, out_specs=[HBM] * n, out_shape=out_shape,
        scratch_shapes=[pltpu.SemaphoreType.DMA((3 * n,))] * 4,
    )(*shards)


def _rs_pair_exchange(gs):
    n = len(gs)

    def body(*refs):
        ins, outs = refs[:n], refs[n:2 * n]
        ssem, rsem = refs[2 * n:]
        x, y, c, _ = _place()
        cps = []
        for t in range(n):
            H = gs[t].shape[1] // 2
            cp = _rcopy(ins[t].at[:, pl.ds((1 - c) * H, H)], outs[t], ssem.at[t], rsem.at[t], (x, y, 1 - c))
            cp.start()
            cps.append(cp)
        for cp in cps:
            cp.wait_recv()
        for cp in cps:
            cp.wait_send()

    out_shape = [jax.ShapeDtypeStruct((NCHIP, g.shape[1] // 2, g.shape[2]), F32) for g in gs]
    return pl.pallas_call(
        body, name="rs_pair_exchange", in_specs=[HBM] * n, out_specs=[HBM] * n, out_shape=out_shape,
        scratch_shapes=[pltpu.SemaphoreType.DMA((n,))] * 2,
    )(*gs)


def _rs_chip_scatter(hs):
    n = len(hs)

    def body(*refs):
        ins, outs = refs[:n], refs[n:2 * n]
        ssem, rsem = refs[2 * n:]
        x, y, c, chips = _place()
        sends = []
        for t in range(n):
            for j, (cx, cy) in enumerate(chips):
                cp = _rcopy(ins[t].at[2 * cx + cy], outs[t].at[j], ssem.at[3 * t + j], rsem.at[3 * t + j], (cx, cy, c))
                cp.start()
                sends.append(cp)
        for cp in sends:
            cp.wait_recv()
        for cp in sends:
            cp.wait_send()

    out_shape = [jax.ShapeDtypeStruct((NCHIP - 1,) + h.shape[1:], h.dtype) for h in hs]
    return pl.pallas_call(
        body, name="rs_chip_scatter", in_specs=[HBM] * n, out_specs=[HBM] * n, out_shape=out_shape,
        scratch_shapes=[pltpu.SemaphoreType.DMA((3 * n,))] * 2,
    )(*hs)


def _rs_pair_gather(fs):
    n = len(fs)

    def body(*refs):
        bufs = refs[n:2 * n]
        ssem, rsem = refs[2 * n:]
        x, y, c, _ = _place()
        sends = []
        for t in range(n):
            cp = _rcopy(bufs[t].at[c], bufs[t].at[c], ssem.at[t], rsem.at[t], (x, y, 1 - c))
            cp.start()
            sends.append(cp)
        for t in range(n):
            win = bufs[t].at[1 - c]
            _rcopy(win, win, ssem.at[t], rsem.at[t], (x, y, 1 - c)).wait_recv()
        for cp in sends:
            cp.wait_send()

    out_shape = [jax.ShapeDtypeStruct(f.shape, F32) for f in fs]
    return pl.pallas_call(
        body, name="rs_pair_gather", in_specs=[HBM] * n, out_specs=[HBM] * n, out_shape=out_shape,
        input_output_aliases={t: t for t in range(n)},
        scratch_shapes=[pltpu.SemaphoreType.DMA((n,))] * 2,
    )(*fs)


def _allreduce_small(buf):
    R = buf.shape[0]

    def body(in_ref, out_ref, slots, ssem, rsem):
        x, y, c, _ = _place()
        me = 4 * x + 2 * y + c
        slots[me] = in_ref[...]
        cps = []
        for k in range(1, NDEV):
            px = 1 - x if k & 4 else x
            py = 1 - y if k & 2 else y
            pc = 1 - c if k & 1 else c
            cp = _rcopy(in_ref, slots.at[me], ssem.at[k - 1], rsem.at[k - 1], (px, py, pc))
            cp.start()
            cps.append((cp, 4 * px + 2 * py + pc))
        for k, (cp, peer) in enumerate(cps):
            _rcopy(in_ref, slots.at[peer], ssem.at[k], rsem.at[k], (x, y, c)).wait_recv()
        for cp, _ in cps:
            cp.wait_send()
        acc = slots[0]
        for p in range(1, NDEV):
            acc = acc + slots[p]
        out_ref[...] = acc

    return pl.pallas_call(
        body, name="allreduce_small", out_shape=jax.ShapeDtypeStruct((R, LANE), F32),
        in_specs=[pl.BlockSpec(memory_space=pltpu.VMEM)], out_specs=pl.BlockSpec(memory_space=pltpu.VMEM),
        scratch_shapes=[pltpu.VMEM((NDEV, R, LANE), F32), pltpu.SemaphoreType.DMA((NDEV - 1,)),
                        pltpu.SemaphoreType.DMA((NDEV - 1,))],
        compiler_params=pltpu.CompilerParams(vmem_limit_bytes=VMEM_LIMIT),
    )(buf)


def _interleave(a):
    lead = a.shape[:-1]
    return a.reshape(*lead, 2, NFB, FC // 2).swapaxes(-3, -2).reshape(*lead, 2 * DFF)


def _uninterleave(a):
    lead = a.shape[:-1]
    return a.reshape(*lead, NFB, 2, FC // 2).swapaxes(-3, -2).reshape(*lead, 2 * DFF)


N_QKV = 3 * WA
N_FG = 2 * NHP


def _pack_in_cols(w):
    pad = jnp.zeros(w.shape[:-1] + (NP - NIN,), w.dtype)
    return jnp.concatenate([w[..., :N_QKV], w[..., N_QKV + N_FG:], w[..., N_QKV:N_QKV + N_FG], pad], axis=-1)


def _unpack_in_cols(g):
    return jnp.concatenate([g[..., :N_QKV], g[..., NIN - N_FG:NIN], g[..., N_QKV:NIN - N_FG]], axis=-1)


def _train_compute(xt, tgt, W, nb):
    saved = []
    xc = xt
    for l in range(DEPTH):
        t = f"_l{l}"
        h = _rms_fwd(xc, W["ln1"][l], "rms1_fwd" + t)
        P = _matmul(h, W["in"][l], tm=1024, tn=1024, tk=D, name="proj_in" + t)
        c = _forget_fwd(P, W["bf"][l], nb, "forget_fwd" + t)
        of, lsef = _fox_fwd(P, c, nb, "fox_fwd" + t)
        od, lsed = _dil_fwd(P, nb, "dil_fwd" + t)
        convp = (W["cw"][l], W["cb"][l], W["cng"][l], W["cnb"][l])
        y = _attn_norm_fwd(of, od, W["gof"][l], W["god"][l], "attn_norm_fwd" + t)
        y, c0 = _conv_fwd(P, y, *convp, nb, "conv_fwd" + t)
        xm = _matmul(y, W["o"][l], add=xc, tm=1024, tn=1024, tk=D, name="proj_out" + t)
        h2 = _rms_fwd(xm, W["ln2"][l], "rms2_fwd" + t)
        U = _matmul_ffn(h2, W["up"][l], "fwd", out_dtype=BF16, name="ffn_up" + t)
        hid, U2 = _ffn_fwd(U, W["fw"][l], W["fb"][l], nb, "ffn_act_fwd" + t)
        xo = _matmul(hid, W["down"][l], add=xm, tm=1024, tn=512, tk=DFF, name="ffn_down" + t)
        saved.append((xc, h, P, c, of, lsef, od, lsed, convp, c0, y, xm, h2, U, U2, hid))
        xc = xo

    loss8, dx, dxb, dgfin = _loss_head(xc, W["gfin"], tgt, "loss_head")

    big = [None] * DEPTH
    small = [None] * DEPTH
    for l in reversed(range(DEPTH)):
        t = f"_l{l}"
        xin, h, P, c, of, lsef, od, lsed, convp, c0, y, xm, h2, U, U2, hid = saved[l]
        dhid = _matmul(dxb, W["down"][l], tb=True, out_dtype=BF16, tm=1024, tn=DFF // 2, tk=D, name="ffn_down_dx" + t)
        dWd = _matmul(hid, dxb, ta=True, tm=DFF // 2, tn=D, tk=512, name="ffn_down_dw" + t)
        dU, dfw = _ffn_bwd(U, U2, dhid, W["fw"][l], nb, "ffn_act_bwd" + t)
        dh2 = _matmul_ffn(dU, W["up"][l], "dx", name="ffn_up_dx" + t)
        dWup = _matmul_ffn(h2, dU, "dw", name="ffn_up_dw" + t)
        dxm, dxmb, dln2 = _rms_bwd(xm, W["ln2"][l], dh2, dx, "rms2_bwd" + t)
        dy = _matmul(dxmb, W["o"][l], tb=True, out_dtype=BF16, tm=1024, tn=D, tk=D, name="proj_out_dx" + t)
        dWo = _matmul(y, dxmb, ta=True, tm=D, tn=D, tk=512, name="proj_out_dw" + t)
        dof, dod, dgo = _attn_norm_bwd(of, od, dy, W["gof"][l], W["god"][l], "attn_norm_bwd" + t)
        dgvgg, dcw, dsm = _conv_bwd(P, c0, dy, convp[0], convp[2], convp[3], nb, "conv_bwd" + t)
        dqa, dka, dva, dcb = _fox_bwd(P, c, of, lsef, dof, nb, "fox_bwd" + t)
        dfa, dbf = _forget_bwd(P, W["bf"][l], dcb, nb, "forget_bwd" + t)
        dqb, dkb, dvb = _dil_bwd(P, od, lsed, dod, nb, "dil_bwd" + t)
        dP = jnp.concatenate([dqa, dka, dva, dqb, dkb, dvb, dgvgg, dfa, jnp.zeros_like(dfa)], axis=1)
        dh = _matmul(dP, W["in"][l], tb=True, tm=1024, tn=D, tk=NP, name="proj_in_dx" + t)
        dWin = _matmul(h, dP, ta=True, tm=D, tn=1024, tk=512, name="proj_in_dw" + t)
        dx, dxb, dln1 = _rms_bwd(xin, W["ln1"][l], dh, dxm, "rms1_bwd" + t)
        big[l] = (dWin, dWo, dWup, dWd)
        small[l] = (dln1, dbf, dgo, dcw, dsm, dln2, dfw)
    return loss8, dx, big, small, dgfin


_SMALL_ROWS = (D // LANE, 8, 8 * WA // LANE, CPAD * CC // LANE, 8 * CC // LANE, D // LANE, 8 * 2 * DFF // LANE)


def kernel(x, ln1_g, w_in, b_forget, g_out_fox, g_out_dil, conv_w, conv_b, cnorm_g, cnorm_b, w_o, ln2_g, w_up, ffn_conv_w, ffn_conv_b, w_down, g_final, loss_target, m_ln1_g, m_w_in, m_b_forget, m_g_out_fox, m_g_out_dil, m_conv_w, m_conv_b, m_cnorm_g, m_cnorm_b, m_w_o, m_ln2_g, m_w_up, m_ffn_conv_w, m_ffn_conv_b, m_w_down, m_g_final, v_ln1_g, v_w_in, v_b_forget, v_g_out_fox, v_g_out_dil, v_conv_w, v_conv_b, v_cnorm_g, v_cnorm_b, v_w_o, v_ln2_g, v_w_up, v_ffn_conv_w, v_ffn_conv_b, v_w_down, v_g_final):
    names = ("ln1_g", "w_in", "b_forget", "g_out_fox", "g_out_dil", "conv_w", "conv_b", "cnorm_g", "cnorm_b",
             "w_o", "ln2_g", "w_up", "ffn_conv_w", "ffn_conv_b", "w_down", "g_final")
    w = dict(zip(names, (ln1_g, w_in, b_forget, g_out_fox, g_out_dil, conv_w, conv_b, cnorm_g, cnorm_b,
                         w_o, ln2_g, w_up, ffn_conv_w, ffn_conv_b, w_down, g_final)))
    m = dict(zip(names, (m_ln1_g, m_w_in, m_b_forget, m_g_out_fox, m_g_out_dil, m_conv_w, m_conv_b, m_cnorm_g,
                         m_cnorm_b, m_w_o, m_ln2_g, m_w_up, m_ffn_conv_w, m_ffn_conv_b, m_w_down, m_g_final)))
    v = dict(zip(names, (v_ln1_g, v_w_in, v_b_forget, v_g_out_fox, v_g_out_dil, v_conv_w, v_conv_b, v_cnorm_g,
                         v_cnorm_b, v_w_o, v_ln2_g, v_w_up, v_ffn_conv_w, v_ffn_conv_b, v_w_down, v_g_final)))
    nb = x.shape[0]
    T = nb * S
    xi, yi, ci = lax.axis_index("x"), lax.axis_index("y"), lax.axis_index("c")
    chip = 2 * xi + yi
    cw_cols = CC // NCHIP
    up_cols = 2 * DFF // NCHIP

    shards = [_pack_in_cols(w_in).astype(BF16), w_o.astype(BF16), w_up.astype(BF16), w_down.astype(BF16),
              jnp.pad(ffn_conv_w, ((0, 0), (0, 8 - FK), (0, 0))),
              jnp.pad(conv_w, ((0, 0), (0, CPAD - CK), (0, LANE - cw_cols)))]
    gathered = _allgather(shards, (True, True, True, True, False, False))
    g_in, g_o, g_up, g_dn, g_fw, g_cw = [
        lax.dynamic_update_slice(g, s[:, None], (0, chip, 0, 0)) for g, s in zip(gathered, shards)]
    up_full = g_up.transpose(0, 2, 1, 3).reshape(DEPTH, D, 2 * DFF)
    fw_full = _interleave(g_fw.transpose(0, 2, 1, 3).reshape(DEPTH, 8, 2 * DFF))
    cw_full = g_cw[..., :cw_cols].transpose(0, 2, 1, 3).reshape(DEPTH, CPAD, CC)
    fb_full = _interleave(ffn_conv_b)
    W = {
        "in": [g_in[l].reshape(D, NP) for l in range(DEPTH)],
        "o": [g_o[l].reshape(D, D) for l in range(DEPTH)],
        "up": [up_full[l] for l in range(DEPTH)],
        "down": [g_dn[l].reshape(DFF, D) for l in range(DEPTH)],
        "ln1": [ln1_g[l] for l in range(DEPTH)],
        "ln2": [ln2_g[l] for l in range(DEPTH)],
        "bf": [jnp.pad(b_forget[l], (0, LANE - N_FG)).reshape(1, LANE) for l in range(DEPTH)],
        "gof": [g_out_fox[l].reshape(1, WA) for l in range(DEPTH)],
        "god": [g_out_dil[l].reshape(1, WA) for l in range(DEPTH)],
        "cw": [cw_full[l] for l in range(DEPTH)],
        "cb": [conv_b[l].reshape(1, CC) for l in range(DEPTH)],
        "cng": [cnorm_g[l].reshape(1, CC) for l in range(DEPTH)],
        "cnb": [cnorm_b[l].reshape(1, CC) for l in range(DEPTH)],
        "fw": [fw_full[l] for l in range(DEPTH)],
        "fb": [fb_full[l].reshape(1, 2 * DFF) for l in range(DEPTH)],
        "gfin": g_final,
    }

    loss8, dx, big, small, dgfin = _train_compute(x.reshape(T, D), loss_target.reshape(T, D), W, nb)

    gs = []
    for l in range(DEPTH):
        dWin, dWo, dWup, dWd = big[l]
        gs += [dWin.reshape(NCHIP, D // NCHIP, NP), dWo.reshape(NCHIP, D // NCHIP, D),
               jnp.stack([half[:, i * up_cols:(i + 1) * up_cols] for half in dWup for i in range(2)]),
               dWd.reshape(NCHIP, DFF // NCHIP, D)]
    r1 = _rs_pair_exchange(gs)
    place = jnp.stack([chip, ci]).astype(jnp.int32)
    hs = [_add_half(g, r, place, f"rs_add_pair_{i}") for i, (g, r) in enumerate(zip(gs, r1))]
    r2 = _rs_chip_scatter(hs)
    fs = [_sum_slots(g, a, b, place, f"rs_add_chips_{i}") for i, (g, a, b) in enumerate(zip(gs, r1, r2))]
    red = _rs_pair_gather(fs)
    red = [r.reshape(r.shape[0] * r.shape[1], r.shape[2]) for r in red]
    grads = {
        "w_in": jnp.stack([_unpack_in_cols(red[4 * l]) for l in range(DEPTH)]),
        "w_o": jnp.stack([red[4 * l + 1] for l in range(DEPTH)]),
        "w_up": jnp.stack([red[4 * l + 2] for l in range(DEPTH)]),
        "w_down": jnp.stack([red[4 * l + 3] for l in range(DEPTH)]),
    }

    parts = []
    for l in range(DEPTH):
        parts += [p.reshape(-1, LANE) for p in small[l]]
    parts += [dgfin.reshape(-1, LANE), loss8]
    tot = _allreduce_small(jnp.concatenate(parts, axis=0))
    off = 0
    per_layer = []
    for l in range(DEPTH):
        vals = []
        for rows in _SMALL_ROWS:
            vals.append(tot[off:off + rows])
            off += rows
        per_layer.append(vals)
    gfin_sum = tot[off:off + D // LANE].reshape(D)
    loss = tot[off + D // LANE, 0]

    def layer_stack(fn):
        return jnp.stack([fn(*per_layer[l]) for l in range(DEPTH)])

    fw_sum = layer_stack(lambda a, b, c_, d, e, f, g: _uninterleave(g.reshape(8, 2 * DFF)))
    cw_sum = layer_stack(lambda a, b, c_, d, e, f, g: d.reshape(CPAD, CC)[:CK])
    sm_sum = layer_stack(lambda a, b, c_, d, e, f, g: e.reshape(8, CC))
    go_sum = layer_stack(lambda a, b, c_, d, e, f, g: c_.reshape(8, WA))
    grads.update({
        "ln1_g": layer_stack(lambda a, b, c_, d, e, f, g: a.reshape(D)),
        "b_forget": layer_stack(lambda a, b, c_, d, e, f, g: b[0, :N_FG]),
        "g_out_fox": go_sum[:, 0],
        "g_out_dil": go_sum[:, 1],
        "conv_w": lax.dynamic_slice_in_dim(cw_sum, chip * cw_cols, cw_cols, axis=2),
        "conv_b": sm_sum[:, 0],
        "cnorm_g": sm_sum[:, 1],
        "cnorm_b": sm_sum[:, 2],
        "ln2_g": layer_stack(lambda a, b, c_, d, e, f, g: f.reshape(D)),
        "ffn_conv_w": lax.dynamic_slice_in_dim(fw_sum[:, :FK], chip * up_cols, up_cols, axis=2),
        "ffn_conv_b": fw_sum[:, FK],
        "g_final": gfin_sum,
    })

    delta, new_m, new_v = {}, {}, {}
    for n in names:
        delta[n], new_m[n], new_v[n] = _adamw(w[n], grads[n], m[n], v[n], "adamw_" + n)
    return (loss, dx.reshape(nb, S, D), *[grads[n] for n in names], *[delta[n] for n in names],
            *[new_m[n] for n in names], *[new_v[n] for n in names])
```

```python
import functools

import jax
import jax.numpy as jnp
from jax import lax
from jax.experimental import pallas as pl
from jax.experimental.pallas import tpu as pltpu

F32 = jnp.float32
BF16 = jnp.bfloat16

D = 1024
S = 2048
DEPTH = 2
HD = 64
WA = 384
NHP = 3
CC = 256
CK = 31
FK = 3
DFF = 2816
NIN = 2822
NP = 3072
SCALE = 0.125
EPS = 1e-6
NEG = -1e30
NCHIP = 4
NDEV = 8
LANE = 128

CB_QA, CB_KA, CB_VA, CB_QB, CB_KB, CB_VB = 0, 3, 6, 9, 12, 15
CB_GV, CB_GG = 9, 10
CB_FA = 22

ADAM_LR, ADAM_B1, ADAM_B2, ADAM_EPS, ADAM_WD, ADAM_STEP = 0.001, 0.9, 0.999, 1e-08, 0.01, 10

VMEM_LIMIT = 56 * 1024 * 1024


def _cparams(sem=None):
    return pltpu.CompilerParams(dimension_semantics=sem, vmem_limit_bytes=VMEM_LIMIT)


def _split3(x):
    hi = x.astype(BF16)
    r1 = x - hi.astype(F32)
    mid = r1.astype(BF16)
    lo = (r1 - mid.astype(F32)).astype(BF16)
    return hi, mid, lo


def _sum8(x):
    r, c = x.shape
    return jnp.sum(x.reshape(r // 8, 8, c), axis=0)


def _sigmoid(z):
    return 0.5 * jnp.tanh(0.5 * z) + 0.5


def _matmul(a, b, *, ta=False, tb=False, out_dtype=F32, add=None, tm, tn, tk, name):
    M = a.shape[1] if ta else a.shape[0]
    K = a.shape[0] if ta else a.shape[1]
    N = b.shape[0] if tb else b.shape[1]
    assert (b.shape[1] if tb else b.shape[0]) == K
    assert M % tm == 0 and N % tn == 0 and K % tk == 0, (M, N, K, tm, tn, tk)
    nk = K // tk
    dn = (((0 if ta else 1,), (1 if tb else 0,)), ((), ()))

    def body(*refs):
        if add is not None:
            a_ref, b_ref, add_ref, o_ref, acc = refs
        else:
            a_ref, b_ref, o_ref, acc = refs
        k = pl.program_id(2)
        prod = lax.dot_general(a_ref[...].astype(BF16), b_ref[...].astype(BF16), dn, preferred_element_type=F32)

        def finish(r):
            if add is not None:
                r = r + add_ref[...]
            o_ref[...] = r.astype(o_ref.dtype)

        if nk == 1:
            finish(prod)
        else:
            @pl.when(k == 0)
            def _():
                acc[...] = prod

            @pl.when(k > 0)
            def _():
                acc[...] += prod

            @pl.when(k == nk - 1)
            def _():
                finish(acc[...])

    a_spec = pl.BlockSpec((tk, tm), lambda i, j, k: (k, i)) if ta else pl.BlockSpec((tm, tk), lambda i, j, k: (i, k))
    b_spec = pl.BlockSpec((tn, tk), lambda i, j, k: (j, k)) if tb else pl.BlockSpec((tk, tn), lambda i, j, k: (k, j))
    o_spec = pl.BlockSpec((tm, tn), lambda i, j, k: (i, j))
    in_specs = [a_spec, b_spec]
    args = [a, b]
    if add is not None:
        in_specs.append(o_spec)
        args.append(add)
    return pl.pallas_call(
        body, name=name, grid=(M // tm, N // tn, nk),
        in_specs=in_specs, out_specs=o_spec,
        out_shape=jax.ShapeDtypeStruct((M, N), out_dtype),
        scratch_shapes=[pltpu.VMEM((tm, tn) if nk > 1 else (8, 128), F32)],
        compiler_params=_cparams(("parallel", "parallel", "arbitrary")),
    )(*args)


def _rms_fwd(x, g, name):
    T = x.shape[0]
    tr = 512

    def body(x_ref, g_ref, h_ref):
        xv = x_ref[...]
        r = lax.rsqrt(jnp.mean(xv * xv, axis=1, keepdims=True) + EPS)
        h_ref[...] = (xv * r * g_ref[...]).astype(BF16)

    return pl.pallas_call(
        body, name=name, grid=(T // tr,),
        in_specs=[pl.BlockSpec((tr, D), lambda i: (i, 0)), pl.BlockSpec((1, D), lambda i: (0, 0))],
        out_specs=pl.BlockSpec((tr, D), lambda i: (i, 0)),
        out_shape=jax.ShapeDtypeStruct((T, D), BF16),
        compiler_params=_cparams(("parallel",)),
    )(x, g.reshape(1, D))


def _rms_bwd(x, g, dh, dres, name):
    T = x.shape[0]
    tr = 512

    def body(x_ref, g_ref, dh_ref, dres_ref, dx_ref, dxb_ref, dg_ref):
        i = pl.program_id(0)
        xv = x_ref[...]
        dhv = dh_ref[...].astype(F32)
        r = lax.rsqrt(jnp.mean(xv * xv, axis=1, keepdims=True) + EPS)
        a = dhv * g_ref[...]
        dx = dres_ref[...] + r * a - xv * (r * r * r * jnp.mean(xv * a, axis=1, keepdims=True))
        dx_ref[...] = dx
        dxb_ref[...] = dx.astype(BF16)
        part = jnp.sum(dhv * xv * r, axis=0, keepdims=True)

        @pl.when(i == 0)
        def _():
            dg_ref[...] = part

        @pl.when(i > 0)
        def _():
            dg_ref[...] += part

    row = pl.BlockSpec((tr, D), lambda i: (i, 0))
    vec = pl.BlockSpec((1, D), lambda i: (0, 0))
    return pl.pallas_call(
        body, name=name, grid=(T // tr,),
        in_specs=[row, vec, row, row], out_specs=[row, row, vec],
        out_shape=[jax.ShapeDtypeStruct((T, D), F32), jax.ShapeDtypeStruct((T, D), BF16),
                   jax.ShapeDtypeStruct((1, D), F32)],
        compiler_params=_cparams(("arbitrary",)),
    )(x, g.reshape(1, D), dh, dres)


def _loss_head(x, g, target, name):
    T = x.shape[0]
    tr = 512

    def body(x_ref, g_ref, t_ref, loss_ref, dx_ref, dxb_ref, dg_ref):
        i = pl.program_id(0)
        xv = x_ref[...]
        gv = g_ref[...]
        r = lax.rsqrt(jnp.mean(xv * xv, axis=1, keepdims=True) + EPS)
        n = xv * r
        err = n * gv - t_ref[...]
        lpart = 0.5 * jnp.sum(jnp.mean(err * err, axis=1, keepdims=True), axis=0, keepdims=True)
        dy = err * (1.0 / D)
        a = dy * gv
        dx = r * a - xv * (r * r * r * jnp.mean(xv * a, axis=1, keepdims=True))
        dx_ref[...] = dx
        dxb_ref[...] = dx.astype(BF16)
        part = jnp.sum(dy * n, axis=0, keepdims=True)
        lfull = jnp.broadcast_to(lpart, (8, LANE))

        @pl.when(i == 0)
        def _():
            dg_ref[...] = part
            loss_ref[...] = lfull

        @pl.when(i > 0)
        def _():
            dg_ref[...] += part
            loss_ref[...] += lfull

    row = pl.BlockSpec((tr, D), lambda i: (i, 0))
    vec = pl.BlockSpec((1, D), lambda i: (0, 0))
    lsp = pl.BlockSpec((8, LANE), lambda i: (0, 0))
    return pl.pallas_call(
        body, name=name, grid=(T // tr,),
        in_specs=[row, vec, row], out_specs=[lsp, row, row, vec],
        out_shape=[jax.ShapeDtypeStruct((8, LANE), F32), jax.ShapeDtypeStruct((T, D), F32),
                   jax.ShapeDtypeStruct((T, D), BF16), jax.ShapeDtypeStruct((1, D), F32)],
        compiler_params=_cparams(("arbitrary",)),
    )(x, g.reshape(1, D), target)


CUM_BLK = 256


def _tri(n, upper):
    r = lax.broadcasted_iota(jnp.int32, (n, n), 0)
    c = lax.broadcasted_iota(jnp.int32, (n, n), 1)
    return jnp.where((c >= r) if upper else (c <= r), 1.0, 0.0).astype(BF16)


def _tri_apply(tri, x):
    hi, mid, lo = _split3(x)
    out = jnp.dot(tri, hi, preferred_element_type=F32)
    out = out + jnp.dot(tri, mid, preferred_element_type=F32)
    return out + jnp.dot(tri, lo, preferred_element_type=F32)


def _forget_fwd(P, bf_pad, nb, name):
    nblk = S // CUM_BLK

    def body(fa_ref, b_ref, c_ref):
        tri = _tri(CUM_BLK, upper=False)
        carry = jnp.zeros((1, LANE), F32)
        for i in range(nblk):
            z = fa_ref[pl.ds(i * CUM_BLK, CUM_BLK), :] + b_ref[...]
            lf = jnp.minimum(z, 0.0) - jnp.log(1.0 + jnp.exp(-jnp.abs(z)))
            cb = _tri_apply(tri, lf) + carry
            c_ref[pl.ds(i * CUM_BLK, CUM_BLK), :] = cb
            carry = cb[CUM_BLK - 1:CUM_BLK, :]

    return pl.pallas_call(
        body, name=name, grid=(nb,),
        in_specs=[pl.BlockSpec((S, LANE), lambda b: (b, CB_FA)), pl.BlockSpec((1, LANE), lambda b: (0, 0))],
        out_specs=pl.BlockSpec((S, LANE), lambda b: (b, 0)),
        out_shape=jax.ShapeDtypeStruct((nb * S, LANE), F32),
        compiler_params=_cparams(("parallel",)),
    )(P, bf_pad)


def _forget_bwd(P, bf_pad, dcb, nb, name):
    nblk = S // CUM_BLK

    def body(fa_ref, b_ref, dc_ref, dfa_ref, db_ref):
        b = pl.program_id(0)
        tri = _tri(CUM_BLK, upper=True)
        lane = lax.broadcasted_iota(jnp.int32, (CUM_BLK, LANE), 1)
        carry = jnp.zeros((1, LANE), F32)
        dbacc = jnp.zeros((1, LANE), F32)
        for i in reversed(range(nblk)):
            rows = pl.ds(i * CUM_BLK, CUM_BLK)
            dc = jnp.zeros((CUM_BLK, LANE), F32)
            dcv = dc_ref[rows, :]
            for h in range(2 * NHP):
                dc = jnp.where(lane == h, -dcv[:, HD * h:HD * h + 1], dc)
            dl = _tri_apply(tri, dc) + carry
            carry = dl[0:1, :]
            z = fa_ref[rows, :] + b_ref[...]
            dz = jnp.where(lane < 2 * NHP, dl * (1.0 - _sigmoid(z)), 0.0)
            dfa_ref[rows, :] = dz.astype(BF16)
            dbacc = dbacc + jnp.sum(dz, axis=0, keepdims=True)

        dbfull = jnp.broadcast_to(dbacc, (8, LANE))

        @pl.when(b == 0)
        def _():
            db_ref[...] = dbfull

        @pl.when(b > 0)
        def _():
            db_ref[...] += dbfull

    return pl.pallas_call(
        body, name=name, grid=(nb,),
        in_specs=[pl.BlockSpec((S, LANE), lambda b: (b, CB_FA)), pl.BlockSpec((1, LANE), lambda b: (0, 0)),
                  pl.BlockSpec((S, WA), lambda b: (b, 0))],
        out_specs=[pl.BlockSpec((S, LANE), lambda b: (b, 0)), pl.BlockSpec((8, LANE), lambda b: (0, 0))],
        out_shape=[jax.ShapeDtypeStruct((nb * S, LANE), BF16), jax.ShapeDtypeStruct((8, LANE), F32)],
        compiler_params=_cparams(("arbitrary",)),
    )(P, bf_pad, dcb)


FQ = 256
NT_DIMS = (((1,), (1,)), ((), ()))
AUGW = 6


def _head_masks(shape):
    lane = lax.broadcasted_iota(jnp.int32, shape, 1)
    return lane < HD, lane >= HD


def _fox_bias_terms(c_ref, hp):
    lane = lax.broadcasted_iota(jnp.int32, (S, LANE), 1)
    cv = c_ref[...]
    return [_split3(jnp.sum(jnp.where(lane == 2 * hp + e, cv, 0.0), axis=1, keepdims=True)) for e in range(2)]


def _fox_ext(x, terms, side, heads, only):
    lane = lax.broadcasted_iota(jnp.int32, (S, LANE), 1)
    one = jnp.ones((S, 1), BF16)
    aug = jnp.zeros((S, LANE), BF16)
    for e in heads:
        hi, mid, lo = terms[e]
        cols = (hi, mid, lo, one, one, one) if side == "q" else (one, one, one, -hi, -mid, -lo)
        for i, col in enumerate(cols):
            aug = jnp.where(lane == AUGW * e + i, col, aug)
    if only is not None:
        x = jnp.where(_head_masks((S, LANE))[only], x, jnp.zeros_like(x))
    return jnp.concatenate([x, aug], axis=1)


def _halves(x, lane_mask):
    return jnp.where(lane_mask, x[0:FQ, 0:LANE], x[FQ:2 * FQ, 0:LANE])


def _fox_fwd(P, c, nb, name):
    def body(q_ref, k_ref, v_ref, c_ref, o_ref, lse_ref, qm0, qm1, kx, vx):
        hp = pl.program_id(1)
        terms = _fox_bias_terms(c_ref, hp)
        qv = (q_ref[...] * SCALE).astype(BF16)
        qm0[...] = _fox_ext(qv, terms, "q", (0,), 0)
        qm1[...] = _fox_ext(qv, terms, "q", (1,), 1)
        kx[...] = _fox_ext(k_ref[...].astype(BF16), terms, "k", (0, 1), None)
        lane = lax.broadcasted_iota(jnp.int32, (S, LANE), 1)
        vx[...] = jnp.concatenate([v_ref[...].astype(BF16), jnp.where(lane == 0, 1.0, 0.0).astype(BF16)], axis=1)
        tmask = _head_masks((FQ, LANE))[0]
        row = lax.broadcasted_iota(jnp.int32, (2 * FQ, FQ), 0) & (FQ - 1)
        col = lax.broadcasted_iota(jnp.int32, (2 * FQ, FQ), 1)
        for i in range(S // FQ):
            r0 = i * FQ
            qt = jnp.concatenate([qm0[pl.ds(r0, FQ), :], qm1[pl.ds(r0, FQ), :]], axis=0)
            sd = lax.dot_general(qt, kx[pl.ds(r0, FQ), :], NT_DIMS, preferred_element_type=F32)
            sd = jnp.where(col <= row, sd, NEG)
            m = jnp.max(sd, axis=1, keepdims=True)
            if i > 0:
                so = lax.dot_general(qt, kx[pl.ds(0, r0), :], NT_DIMS, preferred_element_type=F32)
                m = jnp.maximum(m, jnp.max(so, axis=1, keepdims=True))
            acc = jnp.dot(jnp.exp(sd - m).astype(BF16), vx[pl.ds(r0, FQ), :], preferred_element_type=F32)
            if i > 0:
                acc = acc + jnp.dot(jnp.exp(so - m).astype(BF16), vx[pl.ds(0, r0), :], preferred_element_type=F32)
            l = acc[:, LANE:LANE + 1]
            o_ref[pl.ds(r0, FQ), :] = _halves(acc / l, tmask)
            lse_ref[pl.ds(r0, FQ), :] = _halves(jnp.broadcast_to(m + jnp.log(l), (2 * FQ, LANE)), tmask)

    def colblk(off):
        return pl.BlockSpec((S, LANE), lambda b, hp: (b, off + hp))

    return pl.pallas_call(
        body, name=name, grid=(nb, NHP),
        in_specs=[colblk(CB_QA), colblk(CB_KA), colblk(CB_VA), pl.BlockSpec((S, LANE), lambda b, hp: (b, 0))],
        out_specs=[colblk(0), colblk(0)],
        out_shape=[jax.ShapeDtypeStruct((nb * S, WA), F32), jax.ShapeDtypeStruct((nb * S, WA), F32)],
        scratch_shapes=[pltpu.VMEM((S, 2 * LANE), BF16)] * 4,
        compiler_params=_cparams(("parallel", "parallel")),
    )(P, P, P, c)


def _fox_bwd(P, c, o, lse, do, nb, name):
    def body(q_ref, k_ref, v_ref, c_ref, o_ref, lse_ref, do_ref, dq_ref, dk_ref, dv_ref, dc_ref,
             km0, km1, qx, vm0, vm1, dob, kt0, kt1, rows, dqt, rsum):
        hp = pl.program_id(1)
        terms = _fox_bias_terms(c_ref, hp)
        kv = k_ref[...].astype(BF16)
        km0[...] = _fox_ext(kv, terms, "k", (0,), 0)
        km1[...] = _fox_ext(kv, terms, "k", (1,), 1)
        qx[...] = _fox_ext((q_ref[...] * SCALE).astype(BF16), terms, "q", (0, 1), None)
        masks = _head_masks((S, LANE))
        vv = v_ref[...].astype(BF16)
        zero = jnp.zeros((S, LANE), BF16)
        vm0[...] = jnp.where(masks[0], vv, zero)
        vm1[...] = jnp.where(masks[1], vv, zero)
        dov = do_ref[...]
        dob[...] = dov.astype(BF16)
        ktf = k_ref[...].T
        prodt = (dov * o_ref[...]).T
        lset = lse_ref[...].T
        hrow = lax.broadcasted_iota(jnp.int32, (LANE, S), 0)
        kt0[...] = jnp.where(hrow < HD, ktf, 0.0).astype(BF16)
        kt1[...] = jnp.where(hrow >= HD, ktf, 0.0).astype(BF16)
        for e in range(2):
            rows[e:e + 1, :] = lset[HD * e:HD * e + 1, :]
            rows[2 + e:3 + e, :] = jnp.sum(prodt[HD * e:HD * (e + 1), :], axis=0, keepdims=True)
        dqt[...] = jnp.zeros_like(dqt)
        rsum[...] = jnp.zeros_like(rsum)
        tmask = _head_masks((FQ, LANE))[0]
        row = lax.broadcasted_iota(jnp.int32, (2 * FQ, FQ), 0) & (FQ - 1)
        col = lax.broadcasted_iota(jnp.int32, (2 * FQ, FQ), 1)
        for j in range(S // FQ):
            k0 = j * FQ
            rest = S - k0 - FQ
            spans = [(k0, FQ)] + ([(k0 + FQ, rest)] if rest > 0 else [])
            kte = jnp.concatenate([km0[pl.ds(k0, FQ), :], km1[pl.ds(k0, FQ), :]], axis=0)
            vte = jnp.concatenate([vm0[pl.ds(k0, FQ), :], vm1[pl.ds(k0, FQ), :]], axis=0)
            ktt = jnp.concatenate([kt0[:, pl.ds(k0, FQ)], kt1[:, pl.ds(k0, FQ)]], axis=1)
            dke = jnp.zeros((2 * FQ, 2 * LANE), F32)
            dve = jnp.zeros((2 * FQ, LANE), F32)
            cse = jnp.zeros((2 * FQ, 1), F32)
            for si, (q0, n) in enumerate(spans):
                qs = qx[pl.ds(q0, n), :]
                dos = dob[pl.ds(q0, n), :]
                st = lax.dot_general(kte, qs, NT_DIMS, preferred_element_type=F32)
                if si == 0:
                    st = jnp.where(col >= row, st, NEG)
                dpt = lax.dot_general(vte, dos, NT_DIMS, preferred_element_type=F32)
                pts, dsts = [], []
                for e in range(2):
                    pe = jnp.exp(st[FQ * e:FQ * (e + 1), :] - rows[e:e + 1, pl.ds(q0, n)])
                    de = pe * (dpt[FQ * e:FQ * (e + 1), :] - rows[2 + e:3 + e, pl.ds(q0, n)])
                    rsum[HD * e:HD * e + 8, pl.ds(q0, n)] += _sum8(de)
                    pts.append(pe)
                    dsts.append(de)
                pt = jnp.concatenate(pts, axis=0)
                dst = jnp.concatenate(dsts, axis=0)
                dsb = dst.astype(BF16)
                dve = dve + jnp.dot(pt.astype(BF16), dos, preferred_element_type=F32)
                dke = dke + jnp.dot(dsb, qs, preferred_element_type=F32)
                dqt[:, pl.ds(q0, n)] += jnp.dot(ktt, dsb, preferred_element_type=F32)
                cse = cse + jnp.sum(dst, axis=1, keepdims=True)
            dk_ref[pl.ds(k0, FQ), :] = _halves(dke, tmask).astype(BF16)
            dv_ref[pl.ds(k0, FQ), :] = _halves(dve, tmask).astype(BF16)
            dc_ref[pl.ds(k0, FQ), :] = _halves(jnp.broadcast_to(cse, (2 * FQ, LANE)), tmask)
        dq_ref[...] = (dqt[...].T * SCALE).astype(BF16)
        tot = [jnp.sum(rsum[HD * e:HD * e + 8, :], axis=0, keepdims=True) for e in range(2)]
        dc_ref[...] = dc_ref[...] - jnp.where(hrow == 0, tot[0], jnp.where(hrow == HD, tot[1], 0.0)).T

    def colblk(off):
        return pl.BlockSpec((S, LANE), lambda b, hp: (b, off + hp))

    wide = pltpu.VMEM((S, 2 * LANE), BF16)
    half = pltpu.VMEM((S, LANE), BF16)
    return pl.pallas_call(
        body, name=name, grid=(nb, NHP),
        in_specs=[colblk(CB_QA), colblk(CB_KA), colblk(CB_VA), pl.BlockSpec((S, LANE), lambda b, hp: (b, 0)),
                  colblk(0), colblk(0), colblk(0)],
        out_specs=[colblk(0)] * 4,
        out_shape=[jax.ShapeDtypeStruct((nb * S, WA), BF16)] * 3 + [jax.ShapeDtypeStruct((nb * S, WA), F32)],
        scratch_shapes=[wide, wide, wide, half, half, half, pltpu.VMEM((LANE, S), BF16), pltpu.VMEM((LANE, S), BF16),
                        pltpu.VMEM((8, S), F32), pltpu.VMEM((LANE, S), F32), pltpu.VMEM((LANE, S), F32)],
        compiler_params=_cparams(("parallel", "parallel")),
    )(P, P, P, c, o, lse, do)


DILS = (1, 4, 16)
DB = 128


def _regroup_load(ref, d, scale=None):
    if d == 1:
        v = ref[...]
    else:
        L = S // d
        v = jnp.concatenate([ref[pl.ds(r, L, stride=d), :] for r in range(d)], axis=0)
    return v if scale is None else v * scale


def _regroup_store(ref, d, val_ref, accumulate):
    L = S // d
    for r in range(d):
        src = val_ref[pl.ds(r * L, L), :]
        dst = (slice(None), slice(None)) if d == 1 else (pl.ds(r, L, stride=d), slice(None))
        if accumulate:
            ref[dst] = ref[dst] + src
        else:
            ref[dst] = src


def _dil_bands():
    qi = lax.broadcasted_iota(jnp.int32, (DB, 2 * DB), 0)
    ki = lax.broadcasted_iota(jnp.int32, (DB, 2 * DB), 1)
    band = (ki >= qi) & (ki <= qi + DB)
    return band, band & (ki >= DB)


def _dil_valid(bands, bk, d):
    band, own = bands
    has_prev = (bk % ((S // d) // DB)) > 0
    return own | (band & has_prev)


def _stack_heads(x, masks):
    zero = jnp.zeros_like(x)
    return jnp.concatenate([jnp.where(masks[0], x, zero), jnp.where(masks[1], x, zero)], axis=0)


def _dil_fwd(P, nb, name):
    nblk = S // DB

    def body(q_ref, k_ref, v_ref, o_ref, lse_ref, qd, kd, vd, rnum, rm, rl, num_n, m_n, l_n):
        masks = _head_masks((DB, LANE))
        bands = _dil_bands()
        for bi, d in enumerate(DILS):
            qd[...] = _regroup_load(q_ref, d, SCALE).astype(BF16)
            kd[pl.ds(0, DB), :] = jnp.zeros((DB, LANE), BF16)
            vd[pl.ds(0, DB), :] = jnp.zeros((DB, LANE), BF16)
            kd[pl.ds(DB, S), :] = _regroup_load(k_ref, d).astype(BF16)
            vd[pl.ds(DB, S), :] = _regroup_load(v_ref, d).astype(BF16)

            def blk(bk, _, d=d):
                r0 = pl.multiple_of(bk * DB, DB)
                qt = qd[pl.ds(r0, DB), :]
                kk = kd[pl.ds(r0, 2 * DB), :]
                vv = vd[pl.ds(r0, 2 * DB), :]
                valid = _dil_valid(bands, bk, d)
                valid = jnp.concatenate([valid, valid], axis=0)
                qm = _stack_heads(qt, masks)
                s = lax.dot_general(qm, kk, NT_DIMS, preferred_element_type=F32)
                s = jnp.where(valid, s, NEG)
                m = jnp.max(s, axis=1, keepdims=True)
                p = jnp.exp(s - m)
                l = jnp.sum(p, axis=1, keepdims=True)
                num = jnp.dot(p.astype(BF16), vv, preferred_element_type=F32)
                rnum[pl.ds(r0, DB), :] = jnp.where(masks[0], num[0:DB], num[DB:2 * DB])
                rm[pl.ds(r0, DB), :] = jnp.where(masks[0], m[0:DB], m[DB:2 * DB])
                rl[pl.ds(r0, DB), :] = jnp.where(masks[0], l[0:DB], l[DB:2 * DB])
                return 0

            lax.fori_loop(0, nblk, blk, 0, unroll=4)
            _regroup_store(num_n.at[bi], d, rnum, False)
            _regroup_store(m_n.at[bi], d, rm, False)
            _regroup_store(l_n.at[bi], d, rl, False)

        m_all = jnp.maximum(jnp.maximum(m_n[0], m_n[1]), m_n[2])
        num = jnp.zeros((S, LANE), F32)
        den = jnp.zeros((S, LANE), F32)
        for bi in range(3):
            a = jnp.exp(m_n[bi] - m_all)
            num = num + a * num_n[bi]
            den = den + a * l_n[bi]
        o_ref[...] = num / den
        lse_ref[...] = m_all + jnp.log(den)

    def colblk(off):
        return pl.BlockSpec((S, LANE), lambda b, hp: (b, off + hp))

    return pl.pallas_call(
        body, name=name, grid=(nb, NHP),
        in_specs=[colblk(CB_QB), colblk(CB_KB), colblk(CB_VB)],
        out_specs=[colblk(0), colblk(0)],
        out_shape=[jax.ShapeDtypeStruct((nb * S, WA), F32), jax.ShapeDtypeStruct((nb * S, WA), F32)],
        scratch_shapes=[pltpu.VMEM((S, LANE), BF16), pltpu.VMEM((S + DB, LANE), BF16), pltpu.VMEM((S + DB, LANE), BF16),
                        pltpu.VMEM((S, LANE), F32), pltpu.VMEM((S, LANE), F32), pltpu.VMEM((S, LANE), F32),
                        pltpu.VMEM((3, S, LANE), F32), pltpu.VMEM((3, S, LANE), F32), pltpu.VMEM((3, S, LANE), F32)],
        compiler_params=_cparams(("parallel", "parallel")),
    )(P, P, P)


def _dil_bwd(P, o, lse, do, nb, name):
    nblk = S // DB

    def body(q_ref, k_ref, v_ref, o_ref, lse_ref, do_ref, dq_ref, dk_ref, dv_ref,
             qd, kd, vd, dod, lsed, dsd, dsum, dq_r, dk_r, dv_r, dq_n, dk_n, dv_n):
        masks = _head_masks((DB, LANE))
        fmask = _head_masks((S, LANE))
        prod = do_ref[...] * o_ref[...]
        d0 = jnp.sum(jnp.where(fmask[0], prod, 0.0), axis=1, keepdims=True)
        d1 = jnp.sum(jnp.where(fmask[1], prod, 0.0), axis=1, keepdims=True)
        dsum[...] = jnp.where(fmask[0], d0, d1)
        tn = (((0,), (0,)), ((), ()))
        bands = _dil_bands()
        for bi, d in enumerate(DILS):
            qd[...] = _regroup_load(q_ref, d, SCALE).astype(BF16)
            kd[pl.ds(0, DB), :] = jnp.zeros((DB, LANE), BF16)
            vd[pl.ds(0, DB), :] = jnp.zeros((DB, LANE), BF16)
            kd[pl.ds(DB, S), :] = _regroup_load(k_ref, d).astype(BF16)
            vd[pl.ds(DB, S), :] = _regroup_load(v_ref, d).astype(BF16)
            dod[...] = _regroup_load(do_ref, d).astype(BF16)
            lsed[...] = _regroup_load(lse_ref, d)
            dsd[...] = _regroup_load(dsum, d)
            dq_r[...] = jnp.zeros_like(dq_r)
            dk_r[...] = jnp.zeros_like(dk_r)
            dv_r[...] = jnp.zeros_like(dv_r)

            def blk(bk, _, d=d):
                r0 = pl.multiple_of(bk * DB, DB)
                qt = qd[pl.ds(r0, DB), :]
                dot = dod[pl.ds(r0, DB), :]
                lt = lsed[pl.ds(r0, DB), :]
                dt = dsd[pl.ds(r0, DB), :]
                kk = kd[pl.ds(r0, 2 * DB), :]
                vv = vd[pl.ds(r0, 2 * DB), :]
                valid = _dil_valid(bands, bk, d)
                valid = jnp.concatenate([valid, valid], axis=0)
                qm = _stack_heads(qt, masks)
                dom = _stack_heads(dot, masks)
                lcol = jnp.concatenate([lt[:, 0:1], lt[:, HD:HD + 1]], axis=0)
                dcol = jnp.concatenate([dt[:, 0:1], dt[:, HD:HD + 1]], axis=0)
                s = lax.dot_general(qm, kk, NT_DIMS, preferred_element_type=F32)
                s = jnp.where(valid, s, NEG)
                p = jnp.exp(s - lcol)
                dp = lax.dot_general(dom, vv, NT_DIMS, preferred_element_type=F32)
                ds = (p * (dp - dcol)).astype(BF16)
                dvt = lax.dot_general(p.astype(BF16), dom, tn, preferred_element_type=F32)
                dkt = lax.dot_general(ds, qm, tn, preferred_element_type=F32)
                dqt = jnp.dot(ds, kk, preferred_element_type=F32)
                dq_r[pl.ds(r0, DB), :] = jnp.where(masks[0], dqt[0:DB], dqt[DB:2 * DB])
                dk_r[pl.ds(r0, 2 * DB), :] += dkt
                dv_r[pl.ds(r0, 2 * DB), :] += dvt
                return 0

            lax.fori_loop(0, nblk, blk, 0, unroll=2)
            _regroup_store(dq_n, d, dq_r, bi > 0)
            _regroup_store(dk_n, d, dk_r.at[pl.ds(DB, S)], bi > 0)
            _regroup_store(dv_n, d, dv_r.at[pl.ds(DB, S)], bi > 0)

        dq_ref[...] = (dq_n[...] * SCALE).astype(BF16)
        dk_ref[...] = dk_n[...].astype(BF16)
        dv_ref[...] = dv_n[...].astype(BF16)

    def colblk(off):
        return pl.BlockSpec((S, LANE), lambda b, hp: (b, off + hp))

    big = pltpu.VMEM((S, LANE), F32)
    bigp = pltpu.VMEM((S + DB, LANE), F32)
    return pl.pallas_call(
        body, name=name, grid=(nb, NHP),
        in_specs=[colblk(CB_QB), colblk(CB_KB), colblk(CB_VB), colblk(0), colblk(0), colblk(0)],
        out_specs=[colblk(0)] * 3,
        out_shape=[jax.ShapeDtypeStruct((nb * S, WA), BF16)] * 3,
        scratch_shapes=[pltpu.VMEM((S, LANE), BF16), pltpu.VMEM((S + DB, LANE), BF16), pltpu.VMEM((S + DB, LANE), BF16),
                        pltpu.VMEM((S, LANE), BF16), big, big, big, big, bigp, bigp, big, big, big],
        compiler_params=_cparams(("parallel", "parallel")),
    )(P, P, P, o, lse, do)


RC = 256
NSHW = 4


def _window(src, start, buf):
    if start % 8 == 0:
        return src[pl.ds(start, RC), :]
    buf[...] = src[pl.ds(start, RC), :]
    return buf[...]
CPAD = 32


def _conv_chunk(gpad, r0, cw_ref, cb_ref):
    acc = jnp.zeros((RC, CC), F32) + cb_ref[...]
    for k in range(CK):
        acc = acc + cw_ref[k:k + 1, :] * gpad[pl.ds(r0 + CPAD - (CK - 1) + k, RC), :]
    return acc


def _cnorm(c0, cng_ref, cnb_ref):
    mu = jnp.mean(c0, axis=1, keepdims=True)
    xc = c0 - mu
    rstd = lax.rsqrt(jnp.mean(xc * xc, axis=1, keepdims=True) + EPS)
    n = xc * rstd
    return n, rstd, n * cng_ref[...] + cnb_ref[...]


NORM_ROWS = 512
YC_BLK = 2 * WA // CC


def _attn_norm_fwd(of, od, gof, god, name):
    T = of.shape[0]

    def body(of_ref, od_ref, gof_ref, god_ref, y_ref):
        for i, (src, g_ref) in enumerate(((of_ref, gof_ref), (od_ref, god_ref))):
            v = src[...]
            r = lax.rsqrt(jnp.mean(v * v, axis=1, keepdims=True) + EPS)
            y_ref[:, i * WA:(i + 1) * WA] = (v * r * g_ref[...]).astype(BF16)

    row = lambda w: pl.BlockSpec((NORM_ROWS, w), lambda i: (i, 0))
    par = pl.BlockSpec((1, WA), lambda i: (0, 0))
    return pl.pallas_call(
        body, name=name, grid=(T // NORM_ROWS,),
        in_specs=[row(WA), row(WA), par, par], out_specs=row(2 * WA),
        out_shape=jax.ShapeDtypeStruct((T, D), BF16),
        compiler_params=_cparams(("parallel",)),
    )(of, od, gof, god)


def _attn_norm_bwd(of, od, dy, gof, god, name):
    T = of.shape[0]

    def body(of_ref, od_ref, dy_ref, gof_ref, god_ref, dof_ref, dod_ref, dgo_ref):
        @pl.when(pl.program_id(0) == 0)
        def _():
            dgo_ref[...] = jnp.zeros_like(dgo_ref)

        for i, (src, g_ref, dst) in enumerate(((of_ref, gof_ref, dof_ref), (od_ref, god_ref, dod_ref))):
            v = src[...]
            dyv = dy_ref[:, i * WA:(i + 1) * WA].astype(F32)
            r = lax.rsqrt(jnp.mean(v * v, axis=1, keepdims=True) + EPS)
            a = dyv * g_ref[...]
            dst[...] = r * a - v * (r * r * r * jnp.mean(v * a, axis=1, keepdims=True))
            dgo_ref[i:i + 1, :] += jnp.sum(dyv * v * r, axis=0, keepdims=True)

    row = lambda w: pl.BlockSpec((NORM_ROWS, w), lambda i: (i, 0))
    par = pl.BlockSpec((1, WA), lambda i: (0, 0))
    return pl.pallas_call(
        body, name=name, grid=(T // NORM_ROWS,),
        in_specs=[row(WA), row(WA), row(2 * WA), par, par],
        out_specs=[row(WA), row(WA), pl.BlockSpec((8, WA), lambda i: (0, 0))],
        out_shape=[jax.ShapeDtypeStruct((T, WA), F32), jax.ShapeDtypeStruct((T, WA), F32),
                   jax.ShapeDtypeStruct((8, WA), F32)],
        compiler_params=_cparams(("arbitrary",)),
    )(of, od, dy, gof, god)


def _conv_specs():
    gblk = lambda off: pl.BlockSpec((S, CC), lambda b: (b, off))
    par = lambda r: pl.BlockSpec((r, CC), lambda b: (0, 0))
    return gblk, par


def _conv_fwd(P, y, cw, cb, cng, cnb, nb, name):
    def body(gv_ref, gg_ref, cw_ref, cb_ref, cng_ref, cnb_ref, y_in, y_ref, c0_ref, gpad):
        del y_in
        gpad[pl.ds(0, CPAD), :] = jnp.zeros((CPAD, CC), F32)
        gpad[pl.ds(CPAD, S), :] = gv_ref[...] * _sigmoid(gg_ref[...])
        for ci in range(S // RC):
            r0 = ci * RC
            c0 = _conv_chunk(gpad, r0, cw_ref, cb_ref)
            c0_ref[pl.ds(r0, RC), :] = c0
            _, _, z = _cnorm(c0, cng_ref, cnb_ref)
            y_ref[pl.ds(r0, RC), :] = (z * _sigmoid(z)).astype(BF16)

    gblk, par = _conv_specs()
    return pl.pallas_call(
        body, name=name, grid=(nb,),
        in_specs=[gblk(CB_GV), gblk(CB_GG), par(CPAD), par(1), par(1), par(1), pl.BlockSpec(memory_space=pl.ANY)],
        out_specs=[gblk(YC_BLK), gblk(0)],
        out_shape=[jax.ShapeDtypeStruct((nb * S, D), BF16), jax.ShapeDtypeStruct((nb * S, CC), F32)],
        input_output_aliases={6: 0},
        scratch_shapes=[pltpu.VMEM((S + CPAD, CC), F32)],
        compiler_params=_cparams(("parallel",)),
    )(P, P, cw, cb, cng, cnb, y)


def _conv_bwd(P, c0, dy, cw, cng, cnb, nb, name):
    def body(gv_ref, gg_ref, c0_ref, dy_ref, cw_ref, cng_ref, cnb_ref, dg_ref, dcw_ref, dsm_ref, dpad):
        @pl.when(pl.program_id(0) == 0)
        def _():
            dcw_ref[...] = jnp.zeros_like(dcw_ref)
            dsm_ref[...] = jnp.zeros_like(dsm_ref)

        dpad[pl.ds(S, CPAD), :] = jnp.zeros((CPAD, CC), F32)
        zero = jnp.zeros((8, CC), F32)
        dcb, dcng, dcnb = zero, zero, zero
        for ci in range(S // RC):
            r0 = ci * RC
            n, rstd, z = _cnorm(c0_ref[pl.ds(r0, RC), :], cng_ref, cnb_ref)
            sz = _sigmoid(z)
            dz = dy_ref[pl.ds(r0, RC), :].astype(F32) * (sz * (1.0 + z * (1.0 - sz)))
            dcng = dcng + _sum8(dz * n)
            dcnb = dcnb + _sum8(dz)
            dn = dz * cng_ref[...]
            dc0 = rstd * (dn - jnp.mean(dn, axis=1, keepdims=True) - n * jnp.mean(dn * n, axis=1, keepdims=True))
            dcb = dcb + _sum8(dc0)
            dpad[pl.ds(r0, RC), :] = dc0
        dsm_ref[0:1, :] += jnp.sum(dcb, axis=0, keepdims=True)
        dsm_ref[1:2, :] += jnp.sum(dcng, axis=0, keepdims=True)
        dsm_ref[2:3, :] += jnp.sum(dcnb, axis=0, keepdims=True)

        dws = [zero] * CK
        for ci in range(S // RC):
            r0 = ci * RC
            sg = _sigmoid(gg_ref[pl.ds(r0, RC), :])
            gvc = gv_ref[pl.ds(r0, RC), :]
            glu = gvc * sg
            dgl = jnp.zeros((RC, CC), F32)
            for k in range(CK):
                win = dpad[pl.ds(r0 + (CK - 1) - k, RC), :]
                dws[k] = dws[k] + _sum8(win * glu)
                dgl = dgl + cw_ref[k:k + 1, :] * win
            dg_ref[pl.ds(r0, RC), 0:CC] = (dgl * sg).astype(BF16)
            dg_ref[pl.ds(r0, RC), CC:2 * CC] = (dgl * gvc * sg * (1.0 - sg)).astype(BF16)
        for k in range(CK):
            dcw_ref[k:k + 1, :] += jnp.sum(dws[k], axis=0, keepdims=True)

    gblk, par = _conv_specs()
    return pl.pallas_call(
        body, name=name, grid=(nb,),
        in_specs=[gblk(CB_GV), gblk(CB_GG), gblk(0), gblk(YC_BLK), par(CPAD), par(1), par(1)],
        out_specs=[pl.BlockSpec((S, 2 * CC), lambda b: (b, 0)), par(CPAD), par(8)],
        out_shape=[jax.ShapeDtypeStruct((nb * S, 2 * CC), BF16), jax.ShapeDtypeStruct((CPAD, CC), F32),
                   jax.ShapeDtypeStruct((8, CC), F32)],
        scratch_shapes=[pltpu.VMEM((S + CPAD, CC), F32)],
        compiler_params=_cparams(("arbitrary",)),
    )(P, P, c0, dy, cw, cng, cnb)


FC = 512
FPAD = 8
NFB = 2 * DFF // FC


def _ffn_u2_chunk(upad, r0, fw_ref, fb_ref):
    acc = jnp.zeros((RC, FC), F32) + fb_ref[...]
    for k in range(FK):
        acc = acc + fw_ref[k:k + 1, :] * upad[pl.ds(r0 + FPAD - (FK - 1) + k, RC), :]
    return acc


def _ffn_fwd(U, fw, fb, nb, name):
    def body(u_ref, fw_ref, fb_ref, h_ref, u2_ref, upad):
        upad[pl.ds(0, FPAD), :] = jnp.zeros((FPAD, FC), F32)
        upad[pl.ds(FPAD, S), :] = u_ref[...].astype(F32)
        for ci in range(S // RC):
            r0 = ci * RC
            u2 = _ffn_u2_chunk(upad, r0, fw_ref, fb_ref)
            u2_ref[pl.ds(r0, RC), :] = u2.astype(BF16)
            a2, b2 = u2[:, :FC // 2], u2[:, FC // 2:]
            h_ref[pl.ds(r0, RC), :] = (a2 * _sigmoid(a2) * b2).astype(BF16)

    return pl.pallas_call(
        body, name=name, grid=(nb, NFB),
        in_specs=[pl.BlockSpec((S, FC), lambda b, j: (b, j)), pl.BlockSpec((8, FC), lambda b, j: (0, j)),
                  pl.BlockSpec((1, FC), lambda b, j: (0, j))],
        out_specs=[pl.BlockSpec((S, FC // 2), lambda b, j: (b, j)), pl.BlockSpec((S, FC), lambda b, j: (b, j))],
        out_shape=[jax.ShapeDtypeStruct((nb * S, DFF), BF16), jax.ShapeDtypeStruct((nb * S, 2 * DFF), BF16)],
        scratch_shapes=[pltpu.VMEM((S + FPAD, FC), F32)],
        compiler_params=_cparams(("parallel", "parallel")),
    )(U, fw, fb)


def _ffn_bwd(U, U2, dhid, fw, nb, name):
    def body(u_ref, u2_ref, dh_ref, fw_ref, du_ref, dfw_ref, dpad, shw):
        @pl.when(pl.program_id(1) == 0)
        def _():
            dfw_ref[...] = jnp.zeros_like(dfw_ref)

        dpad[pl.ds(S, FPAD), :] = jnp.zeros((FPAD, FC), F32)
        zero = jnp.zeros((8, FC), F32)
        dbias = zero
        for ci in range(S // RC):
            r0 = ci * RC
            u2 = u2_ref[pl.ds(r0, RC), :].astype(F32)
            a2, b2 = u2[:, :FC // 2], u2[:, FC // 2:]
            sa = _sigmoid(a2)
            dh = dh_ref[pl.ds(r0, RC), :].astype(F32)
            du2 = jnp.concatenate([dh * b2 * (sa * (1.0 + a2 * (1.0 - sa))), dh * a2 * sa], axis=1)
            dpad[pl.ds(r0, RC), :] = du2
            dbias = dbias + _sum8(du2)
        dws = [zero] * FK
        for ci in range(S // RC):
            r0 = ci * RC
            uc = u_ref[pl.ds(r0, RC), :].astype(F32)
            du = jnp.zeros((RC, FC), F32)
            for k in range(FK):
                win = _window(dpad, r0 + (FK - 1) - k, shw.at[k % NSHW])
                dws[k] = dws[k] + _sum8(win * uc)
                du = du + fw_ref[k:k + 1, :] * win
            du_ref[pl.ds(r0, RC), :] = du.astype(BF16)
        for k in range(FK):
            dfw_ref[k:k + 1, :] += jnp.sum(dws[k], axis=0, keepdims=True)
        dfw_ref[FK:FK + 1, :] += jnp.sum(dbias, axis=0, keepdims=True)

    blk = pl.BlockSpec((S, FC), lambda j, b: (b, j))
    return pl.pallas_call(
        body, name=name, grid=(NFB, nb),
        in_specs=[blk, blk, pl.BlockSpec((S, FC // 2), lambda j, b: (b, j)), pl.BlockSpec((8, FC), lambda j, b: (0, j))],
        out_specs=[blk, pl.BlockSpec((8, FC), lambda j, b: (0, j))],
        out_shape=[jax.ShapeDtypeStruct((nb * S, 2 * DFF), BF16), jax.ShapeDtypeStruct((8, 2 * DFF), F32)],
        scratch_shapes=[pltpu.VMEM((S + FPAD, FC), F32), pltpu.VMEM((NSHW, RC, FC), F32)],
        compiler_params=_cparams(("parallel", "arbitrary")),
    )(U, U2, dhid, fw)


def _matmul_ffn(a, b, mode, *, out_dtype=F32, tm=1024, tk=2048, name):
    HF = FC // 2
    if mode == "fwd":
        M, K = a.shape

        def body(a_ref, b1_ref, b2_ref, o_ref):
            av = a_ref[...]
            o_ref[:, :HF] = jnp.dot(av, b1_ref[...], preferred_element_type=F32).astype(o_ref.dtype)
            o_ref[:, HF:] = jnp.dot(av, b2_ref[...], preferred_element_type=F32).astype(o_ref.dtype)

        return pl.pallas_call(
            body, name=name, grid=(M // tm, NFB),
            in_specs=[pl.BlockSpec((tm, K), lambda i, j: (i, 0)), pl.BlockSpec((K, HF), lambda i, j: (0, j)),
                      pl.BlockSpec((K, HF), lambda i, j: (0, NFB + j))],
            out_specs=pl.BlockSpec((tm, FC), lambda i, j: (i, j)),
            out_shape=jax.ShapeDtypeStruct((M, 2 * DFF), out_dtype),
            compiler_params=_cparams(("parallel", "parallel")),
        )(a, b, b)
    if mode == "dx":
        M = a.shape[0]
        N = b.shape[0]
        tm = 512

        def body(a_ref, b_ref, o_ref):
            acc = None
            for j in range(NFB):
                for half in range(2):
                    av = a_ref[:, j * FC + half * HF:j * FC + (half + 1) * HF]
                    bv = b_ref[:, half * DFF + j * HF:half * DFF + (j + 1) * HF]
                    d = lax.dot_general(av, bv, NT_DIMS, preferred_element_type=F32)
                    acc = d if acc is None else acc + d
            o_ref[...] = acc.astype(o_ref.dtype)

        return pl.pallas_call(
            body, name=name, grid=(M // tm,),
            in_specs=[pl.BlockSpec((tm, 2 * DFF), lambda i: (i, 0)), pl.BlockSpec((N, 2 * DFF), lambda i: (0, 0))],
            out_specs=pl.BlockSpec((tm, N), lambda i: (i, 0)),
            out_shape=jax.ShapeDtypeStruct((M, N), out_dtype),
            compiler_params=_cparams(("parallel",)),
        )(a, b)
    assert mode == "dw"
    T, M = a.shape
    nk = T // tk

    def body(a_ref, g_ref, oa_ref, ob_ref, acc):
        k = pl.program_id(1)
        prod = lax.dot_general(a_ref[...], g_ref[...], (((0,), (0,)), ((), ())), preferred_element_type=F32)

        @pl.when(k == 0)
        def _():
            acc[...] = prod

        @pl.when(k > 0)
        def _():
            acc[...] += prod

        @pl.when(k == nk - 1)
        def _():
            oa_ref[...] = acc[:, :HF]
            ob_ref[...] = acc[:, HF:]

    half = pl.BlockSpec((M, HF), lambda j, k: (0, j))
    return pl.pallas_call(
        body, name=name, grid=(NFB, nk),
        in_specs=[pl.BlockSpec((tk, M), lambda j, k: (k, 0)), pl.BlockSpec((tk, FC), lambda j, k: (k, j))],
        out_specs=[half, half],
        out_shape=[jax.ShapeDtypeStruct((M, DFF), F32)] * 2,
        scratch_shapes=[pltpu.VMEM((M, FC), F32)],
        compiler_params=_cparams(("parallel", "arbitrary")),
    )(a, b)


def _adamw_body(w_ref, g_ref, m_ref, v_ref, d_ref, nm_ref, nv_ref):
    g = g_ref[...]
    m = ADAM_B1 * m_ref[...] + (1.0 - ADAM_B1) * g
    v = ADAM_B2 * v_ref[...] + (1.0 - ADAM_B2) * (g * g)
    m_hat = m / (1.0 - ADAM_B1 ** ADAM_STEP)
    v_hat = v / (1.0 - ADAM_B2 ** ADAM_STEP)
    d_ref[...] = -ADAM_LR * (m_hat / (jnp.sqrt(v_hat) + ADAM_EPS) + ADAM_WD * w_ref[...])
    nm_ref[...] = m
    nv_ref[...] = v


def _adamw(w, g, m, v, name):
    shape = w.shape
    R = 1
    for s in shape[:-1]:
        R *= s
    C = shape[-1]
    args = [a.reshape(R, C) for a in (w, g, m, v)]
    tr = R
    for cand in (512, 352, 256, 128, 64, 32, 16, 8):
        if R % cand == 0 and cand * C * 4 * 14 <= 24 * 1024 * 1024:
            tr = cand
            break
    blk = pl.BlockSpec((tr, C), lambda i: (i, 0))
    outs = pl.pallas_call(
        functools.partial(_adamw_body), name=name, grid=(R // tr,),
        in_specs=[blk] * 4, out_specs=[blk] * 3,
        out_shape=[jax.ShapeDtypeStruct((R, C), F32)] * 3,
        compiler_params=_cparams(("parallel",)),
    )(*args)
    return [o.reshape(shape) for o in outs]


def _rs_row_tile(H):
    th = 128 if H % 128 == 0 else 176
    assert H % th == 0
    return th


def _add_half(g, r1, place, name):
    _, R, C = g.shape
    H = R // 2
    th = _rs_row_tile(H)
    nh = H // th

    def body(s_ref, g_ref, r_ref, o_ref):
        o_ref[...] = (g_ref[...] + r_ref[...]).astype(BF16)

    grid_spec = pltpu.PrefetchScalarGridSpec(
        num_scalar_prefetch=1, grid=(NCHIP, nh),
        in_specs=[pl.BlockSpec((None, th, C), lambda p, i, s: (p, s[1] * nh + i, 0)),
                  pl.BlockSpec((None, th, C), lambda p, i, s: (p, i, 0))],
        out_specs=pl.BlockSpec((None, th, C), lambda p, i, s: (p, i, 0)))
    return pl.pallas_call(
        body, name=name, grid_spec=grid_spec, out_shape=jax.ShapeDtypeStruct((NCHIP, H, C), BF16),
        compiler_params=_cparams(("parallel", "parallel")),
    )(place, g, r1)


def _sum_slots(g, r1, r2, place, name):
    _, R, C = g.shape
    H = R // 2
    th = _rs_row_tile(H)
    nh = H // th

    def body(s_ref, g_ref, r1_ref, r2_ref, o_ref):
        acc = g_ref[...] + r1_ref[...]
        for j in range(NCHIP - 1):
            acc = acc + r2_ref[j].astype(F32)
        o_ref[...] = acc

    grid_spec = pltpu.PrefetchScalarGridSpec(
        num_scalar_prefetch=1, grid=(nh,),
        in_specs=[pl.BlockSpec((None, th, C), lambda i, s: (s[0], s[1] * nh + i, 0)),
                  pl.BlockSpec((None, th, C), lambda i, s: (s[0], i, 0)),
                  pl.BlockSpec((NCHIP - 1, th, C), lambda i, s: (0, i, 0))],
        out_specs=pl.BlockSpec((None, th, C), lambda i, s: (s[1], i, 0)))
    return pl.pallas_call(
        body, name=name, grid_spec=grid_spec, out_shape=jax.ShapeDtypeStruct((2, H, C), F32),
        compiler_params=_cparams(("parallel",)),
    )(place, g, r1, r2)


MESH = pl.DeviceIdType.MESH
HBM = pl.BlockSpec(memory_space=pltpu.HBM)


def _place():
    x, y, c = lax.axis_index("x"), lax.axis_index("y"), lax.axis_index("c")
    chips = [(1 - x, y), (x, 1 - y), (1 - x, 1 - y)]
    return x, y, c, chips


def _rcopy(src, dst, ssem, rsem, dev):
    return pltpu.make_async_remote_copy(src_ref=src, dst_ref=dst, send_sem=ssem, recv_sem=rsem,
                                        device_id=dev, device_id_type=MESH)


def _allgather(shards, split):
    n = len(shards)

    def body(*refs):
        ins, outs = refs[:n], refs[n:2 * n]
        ssem, rsem, fssem, frsem = refs[2 * n:]
        x, y, c, chips = _place()
        me = 2 * x + y
        sib = (x, y, 1 - c)

        def window(t, chip, half):
            if not split[t]:
                return outs[t].at[:, chip]
            H = shards[t].shape[1] // 2
            return outs[t].at[:, chip, pl.ds(half * H, H)]

        sends = []
        for t in range(n):
            H = shards[t].shape[1] // 2
            src = ins[t].at[:, pl.ds(c * H, H)] if split[t] else ins[t]
            for j, (cx, cy) in enumerate(chips):
                cp = _rcopy(src, window(t, me, c), ssem.at[3 * t + j], rsem.at[3 * t + j], (cx, cy, c))
                cp.start()
                sends.append(cp)
        for t in range(n):
            for j, (cx, cy) in enumerate(chips):
                win = window(t, 2 * cx + cy, c)
                _rcopy(win, win, ssem.at[3 * t + j], rsem.at[3 * t + j], (cx, cy, c)).wait_recv()
                if split[t]:
                    cp = _rcopy(win, win, fssem.at[3 * t + j], frsem.at[3 * t + j], sib)
                    cp.start()
                    sends.append(cp)
        for t in range(n):
            if split[t]:
                for j, (cx, cy) in enumerate(chips):
                    win = window(t, 2 * cx + cy, 1 - c)
                    _rcopy(win, win, fssem.at[3 * t + j], frsem.at[3 * t + j], sib).wait_recv()
        for cp in sends:
            cp.wait_send()

    out_shape = [jax.ShapeDtypeStruct((s.shape[0], NCHIP) + s.shape[1:], s.dtype) for s in shards]
    return pl.pallas_call(
        body, name="allgather_weights", in_specs=[HBM] * n, out_specs=[HBM] * n, out_shape=out_shape,
        scratch_shapes=[pltpu.SemaphoreType.DMA((3 * n,))] * 4,
    )(*shards)


def _rs_pair_exchange(gs):
    n = len(gs)

    def body(*refs):
        ins, outs = refs[:n], refs[n:2 * n]
        ssem, rsem = refs[2 * n:]
        x, y, c, _ = _place()
        cps = []
        for t in range(n):
            H = gs[t].shape[1] // 2
            cp = _rcopy(ins[t].at[:, pl.ds((1 - c) * H, H)], outs[t], ssem.at[t], rsem.at[t], (x, y, 1 - c))
            cp.start()
            cps.append(cp)
        for cp in cps:
            cp.wait_recv()
        for cp in cps:
            cp.wait_send()

    out_shape = [jax.ShapeDtypeStruct((NCHIP, g.shape[1] // 2, g.shape[2]), F32) for g in gs]
    return pl.pallas_call(
        body, name="rs_pair_exchange", in_specs=[HBM] * n, out_specs=[HBM] * n, out_shape=out_shape,
        scratch_shapes=[pltpu.SemaphoreType.DMA((n,))] * 2,
    )(*gs)


def _rs_chip_scatter(hs):
    n = len(hs)

    def body(*refs):
        ins, outs = refs[:n], refs[n:2 * n]
        ssem, rsem = refs[2 * n:]
        x, y, c, chips = _place()
        sends = []
        for t in range(n):
            for j, (cx, cy) in enumerate(chips):
                cp = _rcopy(ins[t].at[2 * cx + cy], outs[t].at[j], ssem.at[3 * t + j], rsem.at[3 * t + j], (cx, cy, c))
                cp.start()
                sends.append(cp)
        for cp in sends:
            cp.wait_recv()
        for cp in sends:
            cp.wait_send()

    out_shape = [jax.ShapeDtypeStruct((NCHIP - 1,) + h.shape[1:], h.dtype) for h in hs]
    return pl.pallas_call(
        body, name="rs_chip_scatter", in_specs=[HBM] * n, out_specs=[HBM] * n, out_shape=out_shape,
        scratch_shapes=[pltpu.SemaphoreType.DMA((3 * n,))] * 2,
    )(*hs)


def _rs_pair_gather(fs):
    n = len(fs)

    def body(*refs):
        bufs = refs[n:2 * n]
        ssem, rsem = refs[2 * n:]
        x, y, c, _ = _place()
        sends = []
        for t in range(n):
            cp = _rcopy(bufs[t].at[c], bufs[t].at[c], ssem.at[t], rsem.at[t], (x, y, 1 - c))
            cp.start()
            sends.append(cp)
        for t in range(n):
            win = bufs[t].at[1 - c]
            _rcopy(win, win, ssem.at[t], rsem.at[t], (x, y, 1 - c)).wait_recv()
        for cp in sends:
            cp.wait_send()

    out_shape = [jax.ShapeDtypeStruct(f.shape, F32) for f in fs]
    return pl.pallas_call(
        body, name="rs_pair_gather", in_specs=[HBM] * n, out_specs=[HBM] * n, out_shape=out_shape,
        input_output_aliases={t: t for t in range(n)},
        scratch_shapes=[pltpu.SemaphoreType.DMA((n,))] * 2,
    )(*fs)


def _allreduce_small(buf):
    R = buf.shape[0]

    def body(in_ref, out_ref, slots, ssem, rsem):
        x, y, c, _ = _place()
        me = 4 * x + 2 * y + c
        slots[me] = in_ref[...]
        cps = []
        for k in range(1, NDEV):
            px = 1 - x if k & 4 else x
            py = 1 - y if k & 2 else y
            pc = 1 - c if k & 1 else c
            cp = _rcopy(in_ref, slots.at[me], ssem.at[k - 1], rsem.at[k - 1], (px, py, pc))
            cp.start()
            cps.append((cp, 4 * px + 2 * py + pc))
        for k, (cp, peer) in enumerate(cps):
            _rcopy(in_ref, slots.at[peer], ssem.at[k], rsem.at[k], (x, y, c)).wait_recv()
        for cp, _ in cps:
            cp.wait_send()
        acc = slots[0]
        for p in range(1, NDEV):
            acc = acc + slots[p]
        out_ref[...] = acc

    return pl.pallas_call(
        body, name="allreduce_small", out_shape=jax.ShapeDtypeStruct((R, LANE), F32),
        in_specs=[pl.BlockSpec(memory_space=pltpu.VMEM)], out_specs=pl.BlockSpec(memory_space=pltpu.VMEM),
        scratch_shapes=[pltpu.VMEM((NDEV, R, LANE), F32), pltpu.SemaphoreType.DMA((NDEV - 1,)),
                        pltpu.SemaphoreType.DMA((NDEV - 1,))],
        compiler_params=pltpu.CompilerParams(vmem_limit_bytes=VMEM_LIMIT),
    )(buf)


def _interleave(a):
    lead = a.shape[:-1]
    return a.reshape(*lead, 2, NFB, FC // 2).swapaxes(-3, -2).reshape(*lead, 2 * DFF)


def _uninterleave(a):
    lead = a.shape[:-1]
    return a.reshape(*lead, NFB, 2, FC // 2).swapaxes(-3, -2).reshape(*lead, 2 * DFF)


N_QKV = 3 * WA
N_FG = 2 * NHP


def _pack_in_cols(w):
    pad = jnp.zeros(w.shape[:-1] + (NP - NIN,), w.dtype)
    return jnp.concatenate([w[..., :N_QKV], w[..., N_QKV + N_FG:], w[..., N_QKV:N_QKV + N_FG], pad], axis=-1)


def _unpack_in_cols(g):
    return jnp.concatenate([g[..., :N_QKV], g[..., NIN - N_FG:NIN], g[..., N_QKV:NIN - N_FG]], axis=-1)


def _train_compute(xt, tgt, W, nb):
    saved = []
    xc = xt
    for l in range(DEPTH):
        t = f"_l{l}"
        h = _rms_fwd(xc, W["ln1"][l], "rms1_fwd" + t)
        P = _matmul(h, W["in"][l], tm=1024, tn=1024, tk=D, name="proj_in" + t)
        c = _forget_fwd(P, W["bf"][l], nb, "forget_fwd" + t)
        of, lsef = _fox_fwd(P, c, nb, "fox_fwd" + t)
        od, lsed = _dil_fwd(P, nb, "dil_fwd" + t)
        convp = (W["cw"][l], W["cb"][l], W["cng"][l], W["cnb"][l])
        y = _attn_norm_fwd(of, od, W["gof"][l], W["god"][l], "attn_norm_fwd" + t)
        y, c0 = _conv_fwd(P, y, *convp, nb, "conv_fwd" + t)
        xm = _matmul(y, W["o"][l], add=xc, tm=1024, tn=1024, tk=D, name="proj_out" + t)
        h2 = _rms_fwd(xm, W["ln2"][l], "rms2_fwd" + t)
        U = _matmul_ffn(h2, W["up"][l], "fwd", out_dtype=BF16, name="ffn_up" + t)
        hid, U2 = _ffn_fwd(U, W["fw"][l], W["fb"][l], nb, "ffn_act_fwd" + t)
        xo = _matmul(hid, W["down"][l], add=xm, tm=1024, tn=512, tk=DFF, name="ffn_down" + t)
        saved.append((xc, h, P, c, of, lsef, od, lsed, convp, c0, y, xm, h2, U, U2, hid))
        xc = xo

    loss8, dx, dxb, dgfin = _loss_head(xc, W["gfin"], tgt, "loss_head")

    big = [None] * DEPTH
    small = [None] * DEPTH
    for l in reversed(range(DEPTH)):
        t = f"_l{l}"
        xin, h, P, c, of, lsef, od, lsed, convp, c0, y, xm, h2, U, U2, hid = saved[l]
        dhid = _matmul(dxb, W["down"][l], tb=True, out_dtype=BF16, tm=1024, tn=DFF // 2, tk=D, name="ffn_down_dx" + t)
        dWd = _matmul(hid, dxb, ta=True, tm=DFF // 2, tn=D, tk=2048, name="ffn_down_dw" + t)
        dU, dfw = _ffn_bwd(U, U2, dhid, W["fw"][l], nb, "ffn_act_bwd" + t)
        dh2 = _matmul_ffn(dU, W["up"][l], "dx", name="ffn_up_dx" + t)
        dWup = _matmul_ffn(h2, dU, "dw", name="ffn_up_dw" + t)
        dxm, dxmb, dln2 = _rms_bwd(xm, W["ln2"][l], dh2, dx, "rms2_bwd" + t)
        dy = _matmul(dxmb, W["o"][l], tb=True, out_dtype=BF16, tm=1024, tn=D, tk=D, name="proj_out_dx" + t)
        dWo = _matmul(y, dxmb, ta=True, tm=D, tn=D, tk=2048, name="proj_out_dw" + t)
        dof, dod, dgo = _attn_norm_bwd(of, od, dy, W["gof"][l], W["god"][l], "attn_norm_bwd" + t)
        dgvgg, dcw, dsm = _conv_bwd(P, c0, dy, convp[0], convp[2], convp[3], nb, "conv_bwd" + t)
        dqa, dka, dva, dcb = _fox_bwd(P, c, of, lsef, dof, nb, "fox_bwd" + t)
        dfa, dbf = _forget_bwd(P, W["bf"][l], dcb, nb, "forget_bwd" + t)
        dqb, dkb, dvb = _dil_bwd(P, od, lsed, dod, nb, "dil_bwd" + t)
        dP = jnp.concatenate([dqa, dka, dva, dqb, dkb, dvb, dgvgg, dfa, jnp.zeros_like(dfa)], axis=1)
        dh = _matmul(dP, W["in"][l], tb=True, tm=1024, tn=D, tk=NP, name="proj_in_dx" + t)
        dWin = _matmul(h, dP, ta=True, tm=D, tn=1024, tk=2048, name="proj_in_dw" + t)
        dx, dxb, dln1 = _rms_bwd(xin, W["ln1"][l], dh, dxm, "rms1_bwd" + t)
        big[l] = (dWin, dWo, dWup, dWd)
        small[l] = (dln1, dbf, dgo, dcw, dsm, dln2, dfw)
    return loss8, dx, big, small, dgfin


_SMALL_ROWS = (D // LANE, 8, 8 * WA // LANE, CPAD * CC // LANE, 8 * CC // LANE, D // LANE, 8 * 2 * DFF // LANE)


def kernel(x, ln1_g, w_in, b_forget, g_out_fox, g_out_dil, conv_w, conv_b, cnorm_g, cnorm_b, w_o, ln2_g, w_up, ffn_conv_w, ffn_conv_b, w_down, g_final, loss_target, m_ln1_g, m_w_in, m_b_forget, m_g_out_fox, m_g_out_dil, m_conv_w, m_conv_b, m_cnorm_g, m_cnorm_b, m_w_o, m_ln2_g, m_w_up, m_ffn_conv_w, m_ffn_conv_b, m_w_down, m_g_final, v_ln1_g, v_w_in, v_b_forget, v_g_out_fox, v_g_out_dil, v_conv_w, v_conv_b, v_cnorm_g, v_cnorm_b, v_w_o, v_ln2_g, v_w_up, v_ffn_conv_w, v_ffn_conv_b, v_w_down, v_g_final):
    names = ("ln1_g", "w_in", "b_forget", "g_out_fox", "g_out_dil", "conv_w", "conv_b", "cnorm_g", "cnorm_b",
             "w_o", "ln2_g", "w_up", "ffn_conv_w", "ffn_conv_b", "w_down", "g_final")
    w = dict(zip(names, (ln1_g, w_in, b_forget, g_out_fox, g_out_dil, conv_w, conv_b, cnorm_g, cnorm_b,
                         w_o, ln2_g, w_up, ffn_conv_w, ffn_conv_b, w_down, g_final)))
    m = dict(zip(names, (m_ln1_g, m_w_in, m_b_forget, m_g_out_fox, m_g_out_dil, m_conv_w, m_conv_b, m_cnorm_g,
                         m_cnorm_b, m_w_o, m_ln2_g, m_w_up, m_ffn_conv_w, m_ffn_conv_b, m_w_down, m_g_final)))
    v = dict(zip(names, (v_ln1_g, v_w_in, v_b_forget, v_g_out_fox, v_g_out_dil, v_conv_w, v_conv_b, v_cnorm_g,
                         v_cnorm_b, v_w_o, v_ln2_g, v_w_up, v_ffn_conv_w, v_ffn_conv_b, v_w_down, v_g_final)))
    nb = x.shape[0]
    T = nb * S
    xi, yi, ci = lax.axis_index("x"), lax.axis_index("y"), lax.axis_index("c")
    chip = 2 * xi + yi
    cw_cols = CC // NCHIP
    up_cols = 2 * DFF // NCHIP

    shards = [_pack_in_cols(w_in).astype(BF16), w_o.astype(BF16), w_up.astype(BF16), w_down.astype(BF16),
              jnp.pad(ffn_conv_w, ((0, 0), (0, 8 - FK), (0, 0))),
              jnp.pad(conv_w, ((0, 0), (0, CPAD - CK), (0, LANE - cw_cols)))]
    gathered = _allgather(shards, (True, True, True, True, False, False))
    g_in, g_o, g_up, g_dn, g_fw, g_cw = [
        lax.dynamic_update_slice(g, s[:, None], (0, chip, 0, 0)) for g, s in zip(gathered, shards)]
    up_full = g_up.transpose(0, 2, 1, 3).reshape(DEPTH, D, 2 * DFF)
    fw_full = _interleave(g_fw.transpose(0, 2, 1, 3).reshape(DEPTH, 8, 2 * DFF))
    cw_full = g_cw[..., :cw_cols].transpose(0, 2, 1, 3).reshape(DEPTH, CPAD, CC)
    fb_full = _interleave(ffn_conv_b)
    W = {
        "in": [g_in[l].reshape(D, NP) for l in range(DEPTH)],
        "o": [g_o[l].reshape(D, D) for l in range(DEPTH)],
        "up": [up_full[l] for l in range(DEPTH)],
        "down": [g_dn[l].reshape(DFF, D) for l in range(DEPTH)],
        "ln1": [ln1_g[l] for l in range(DEPTH)],
        "ln2": [ln2_g[l] for l in range(DEPTH)],
        "bf": [jnp.pad(b_forget[l], (0, LANE - N_FG)).reshape(1, LANE) for l in range(DEPTH)],
        "gof": [g_out_fox[l].reshape(1, WA) for l in range(DEPTH)],
        "god": [g_out_dil[l].reshape(1, WA) for l in range(DEPTH)],
        "cw": [cw_full[l] for l in range(DEPTH)],
        "cb": [conv_b[l].reshape(1, CC) for l in range(DEPTH)],
        "cng": [cnorm_g[l].reshape(1, CC) for l in range(DEPTH)],
        "cnb": [cnorm_b[l].reshape(1, CC) for l in range(DEPTH)],
        "fw": [fw_full[l] for l in range(DEPTH)],
        "fb": [fb_full[l].reshape(1, 2 * DFF) for l in range(DEPTH)],
        "gfin": g_final,
    }

    loss8, dx, big, small, dgfin = _train_compute(x.reshape(T, D), loss_target.reshape(T, D), W, nb)

    gs = []
    for l in range(DEPTH):
        dWin, dWo, dWup, dWd = big[l]
        gs += [dWin.reshape(NCHIP, D // NCHIP, NP), dWo.reshape(NCHIP, D // NCHIP, D),
               jnp.stack([half[:, i * up_cols:(i + 1) * up_cols] for half in dWup for i in range(2)]),
               dWd.reshape(NCHIP, DFF // NCHIP, D)]
    r1 = _rs_pair_exchange(gs)
    place = jnp.stack([chip, ci]).astype(jnp.int32)
    hs = [_add_half(g, r, place, f"rs_add_pair_{i}") for i, (g, r) in enumerate(zip(gs, r1))]
    r2 = _rs_chip_scatter(hs)
    fs = [_sum_slots(g, a, b, place, f"rs_add_chips_{i}") for i, (g, a, b) in enumerate(zip(gs, r1, r2))]
    red = _rs_pair_gather(fs)
    red = [r.reshape(r.shape[0] * r.shape[1], r.shape[2]) for r in red]
    grads = {
        "w_in": jnp.stack([_unpack_in_cols(red[4 * l]) for l in range(DEPTH)]),
        "w_o": jnp.stack([red[4 * l + 1] for l in range(DEPTH)]),
        "w_up": jnp.stack([red[4 * l + 2] for l in range(DEPTH)]),
        "w_down": jnp.stack([red[4 * l + 3] for l in range(DEPTH)]),
    }

    parts = []
    for l in range(DEPTH):
        parts += [p.reshape(-1, LANE) for p in small[l]]
    parts += [dgfin.reshape(-1, LANE), loss8]
    tot = _allreduce_small(jnp.concatenate(parts, axis=0))
    off = 0
    per_layer = []
    for l in range(DEPTH):
        vals = []
        for rows in _SMALL_ROWS:
            vals.append(tot[off:off + rows])
            off += rows
        per_layer.append(vals)
    gfin_sum = tot[off:off + D // LANE].reshape(D)
    loss = tot[off + D // LANE, 0]

    def layer_stack(fn):
        return jnp.stack([fn(*per_layer[l]) for l in range(DEPTH)])

    fw_sum = layer_stack(lambda a, b, c_, d, e, f, g: _uninterleave(g.reshape(8, 2 * DFF)))
    cw_sum = layer_stack(lambda a, b, c_, d, e, f, g: d.reshape(CPAD, CC)[:CK])
    sm_sum = layer_stack(lambda a, b, c_, d, e, f, g: e.reshape(8, CC))
    go_sum = layer_stack(lambda a, b, c_, d, e, f, g: c_.reshape(8, WA))
    grads.update({
        "ln1_g": layer_stack(lambda a, b, c_, d, e, f, g: a.reshape(D)),
        "b_forget": layer_stack(lambda a, b, c_, d, e, f, g: b[0, :N_FG]),
        "g_out_fox": go_sum[:, 0],
        "g_out_dil": go_sum[:, 1],
        "conv_w": lax.dynamic_slice_in_dim(cw_sum, chip * cw_cols, cw_cols, axis=2),
        "conv_b": sm_sum[:, 0],
        "cnorm_g": sm_sum[:, 1],
        "cnorm_b": sm_sum[:, 2],
        "ln2_g": layer_stack(lambda a, b, c_, d, e, f, g: f.reshape(D)),
        "ffn_conv_w": lax.dynamic_slice_in_dim(fw_sum[:, :FK], chip * up_cols, up_cols, axis=2),
        "ffn_conv_b": fw_sum[:, FK],
        "g_final": gfin_sum,
    })

    delta, new_m, new_v = {}, {}, {}
    for n in names:
        delta[n], new_m[n], new_v[n] = _adamw(w[n], grads[n], m[n], v[n], "adamw_" + n)
    return (loss, dx.reshape(nb, S, D), *[grads[n] for n in names], *[delta[n] for n in names],
            *[new_m[n] for n in names], *[new_v[n] for n in names])
```

```python
import functools

import jax
import jax.numpy as jnp
from jax import lax
from jax.experimental import pallas as pl
from jax.experimental.pallas import tpu as pltpu

F32 = jnp.float32
BF16 = jnp.bfloat16

D = 1024
S = 2048
DEPTH = 2
HD = 64
WA = 384
NHP = 3
CC = 256
CK = 31
FK = 3
DFF = 2816
NIN = 2822
NP = 3072
SCALE = 0.125
EPS = 1e-6
NEG = -1e30
NCHIP = 4
NDEV = 8
LANE = 128

CB_QA, CB_KA, CB_VA, CB_QB, CB_KB, CB_VB = 0, 3, 6, 9, 12, 15
CB_GV, CB_GG = 9, 10
CB_FA = 22

ADAM_LR, ADAM_B1, ADAM_B2, ADAM_EPS, ADAM_WD, ADAM_STEP = 0.001, 0.9, 0.999, 1e-08, 0.01, 10

VMEM_LIMIT = 56 * 1024 * 1024


def _cparams(sem=None):
    return pltpu.CompilerParams(dimension_semantics=sem, vmem_limit_bytes=VMEM_LIMIT)


def _split3(x):
    hi = x.astype(BF16)
    r1 = x - hi.astype(F32)
    mid = r1.astype(BF16)
    lo = (r1 - mid.astype(F32)).astype(BF16)
    return hi, mid, lo


def _sum8(x):
    r, c = x.shape
    return jnp.sum(x.reshape(r // 8, 8, c), axis=0)


def _sigmoid(z):
    return 0.5 * jnp.tanh(0.5 * z) + 0.5


def _matmul(a, b, *, ta=False, tb=False, out_dtype=F32, add=None, tm, tn, tk, name):
    M = a.shape[1] if ta else a.shape[0]
    K = a.shape[0] if ta else a.shape[1]
    N = b.shape[0] if tb else b.shape[1]
    assert (b.shape[1] if tb else b.shape[0]) == K
    assert M % tm == 0 and N % tn == 0 and K % tk == 0, (M, N, K, tm, tn, tk)
    nk = K // tk
    dn = (((0 if ta else 1,), (1 if tb else 0,)), ((), ()))

    def body(*refs):
        if add is not None:
            a_ref, b_ref, add_ref, o_ref, acc = refs
        else:
            a_ref, b_ref, o_ref, acc = refs
        k = pl.program_id(2)
        prod = lax.dot_general(a_ref[...].astype(BF16), b_ref[...].astype(BF16), dn, preferred_element_type=F32)

        def finish(r):
            if add is not None:
                r = r + add_ref[...]
            o_ref[...] = r.astype(o_ref.dtype)

        if nk == 1:
            finish(prod)
        else:
            @pl.when(k == 0)
            def _():
                acc[...] = prod

            @pl.when(k > 0)
            def _():
                acc[...] += prod

            @pl.when(k == nk - 1)
            def _():
                finish(acc[...])

    a_spec = pl.BlockSpec((tk, tm), lambda i, j, k: (k, i)) if ta else pl.BlockSpec((tm, tk), lambda i, j, k: (i, k))
    b_spec = pl.BlockSpec((tn, tk), lambda i, j, k: (j, k)) if tb else pl.BlockSpec((tk, tn), lambda i, j, k: (k, j))
    o_spec = pl.BlockSpec((tm, tn), lambda i, j, k: (i, j))
    in_specs = [a_spec, b_spec]
    args = [a, b]
    if add is not None:
        in_specs.append(o_spec)
        args.append(add)
    return pl.pallas_call(
        body, name=name, grid=(M // tm, N // tn, nk),
        in_specs=in_specs, out_specs=o_spec,
        out_shape=jax.ShapeDtypeStruct((M, N), out_dtype),
        scratch_shapes=[pltpu.VMEM((tm, tn) if nk > 1 else (8, 128), F32)],
        compiler_params=_cparams(("parallel", "parallel", "arbitrary")),
    )(*args)


def _rms_fwd(x, g, name):
    T = x.shape[0]
    tr = 512

    def body(x_ref, g_ref, h_ref):
        xv = x_ref[...]
        r = lax.rsqrt(jnp.mean(xv * xv, axis=1, keepdims=True) + EPS)
        h_ref[...] = (xv * r * g_ref[...]).astype(BF16)

    return pl.pallas_call(
        body, name=name, grid=(T // tr,),
        in_specs=[pl.BlockSpec((tr, D), lambda i: (i, 0)), pl.BlockSpec((1, D), lambda i: (0, 0))],
        out_specs=pl.BlockSpec((tr, D), lambda i: (i, 0)),
        out_shape=jax.ShapeDtypeStruct((T, D), BF16),
        compiler_params=_cparams(("parallel",)),
    )(x, g.reshape(1, D))


def _rms_bwd(x, g, dh, dres, name):
    T = x.shape[0]
    tr = 512

    def body(x_ref, g_ref, dh_ref, dres_ref, dx_ref, dxb_ref, dg_ref):
        i = pl.program_id(0)
        xv = x_ref[...]
        dhv = dh_ref[...].astype(F32)
        r = lax.rsqrt(jnp.mean(xv * xv, axis=1, keepdims=True) + EPS)
        a = dhv * g_ref[...]
        dx = dres_ref[...] + r * a - xv * (r * r * r * jnp.mean(xv * a, axis=1, keepdims=True))
        dx_ref[...] = dx
        dxb_ref[...] = dx.astype(BF16)
        part = jnp.sum(dhv * xv * r, axis=0, keepdims=True)

        @pl.when(i == 0)
        def _():
            dg_ref[...] = part

        @pl.when(i > 0)
        def _():
            dg_ref[...] += part

    row = pl.BlockSpec((tr, D), lambda i: (i, 0))
    vec = pl.BlockSpec((1, D), lambda i: (0, 0))
    return pl.pallas_call(
        body, name=name, grid=(T // tr,),
        in_specs=[row, vec, row, row], out_specs=[row, row, vec],
        out_shape=[jax.ShapeDtypeStruct((T, D), F32), jax.ShapeDtypeStruct((T, D), BF16),
                   jax.ShapeDtypeStruct((1, D), F32)],
        compiler_params=_cparams(("arbitrary",)),
    )(x, g.reshape(1, D), dh, dres)


def _loss_head(x, g, target, name):
    T = x.shape[0]
    tr = 512

    def body(x_ref, g_ref, t_ref, loss_ref, dx_ref, dxb_ref, dg_ref):
        i = pl.program_id(0)
        xv = x_ref[...]
        gv = g_ref[...]
        r = lax.rsqrt(jnp.mean(xv * xv, axis=1, keepdims=True) + EPS)
        n = xv * r
        err = n * gv - t_ref[...]
        lpart = 0.5 * jnp.sum(jnp.mean(err * err, axis=1, keepdims=True), axis=0, keepdims=True)
        dy = err * (1.0 / D)
        a = dy * gv
        dx = r * a - xv * (r * r * r * jnp.mean(xv * a, axis=1, keepdims=True))
        dx_ref[...] = dx
        dxb_ref[...] = dx.astype(BF16)
        part = jnp.sum(dy * n, axis=0, keepdims=True)
        lfull = jnp.broadcast_to(lpart, (8, LANE))

        @pl.when(i == 0)
        def _():
            dg_ref[...] = part
            loss_ref[...] = lfull

        @pl.when(i > 0)
        def _():
            dg_ref[...] += part
            loss_ref[...] += lfull

    row = pl.BlockSpec((tr, D), lambda i: (i, 0))
    vec = pl.BlockSpec((1, D), lambda i: (0, 0))
    lsp = pl.BlockSpec((8, LANE), lambda i: (0, 0))
    return pl.pallas_call(
        body, name=name, grid=(T // tr,),
        in_specs=[row, vec, row], out_specs=[lsp, row, row, vec],
        out_shape=[jax.ShapeDtypeStruct((8, LANE), F32), jax.ShapeDtypeStruct((T, D), F32),
                   jax.ShapeDtypeStruct((T, D), BF16), jax.ShapeDtypeStruct((1, D), F32)],
        compiler_params=_cparams(("arbitrary",)),
    )(x, g.reshape(1, D), target)


CUM_BLK = 256


def _tri(n, upper):
    r = lax.broadcasted_iota(jnp.int32, (n, n), 0)
    c = lax.broadcasted_iota(jnp.int32, (n, n), 1)
    return jnp.where((c >= r) if upper else (c <= r), 1.0, 0.0).astype(BF16)


def _tri_apply(tri, x):
    hi, mid, lo = _split3(x)
    out = jnp.dot(tri, hi, preferred_element_type=F32)
    out = out + jnp.dot(tri, mid, preferred_element_type=F32)
    return out + jnp.dot(tri, lo, preferred_element_type=F32)


def _forget_fwd(P, bf_pad, nb, name):
    nblk = S // CUM_BLK

    def body(fa_ref, b_ref, c_ref):
        tri = _tri(CUM_BLK, upper=False)
        carry = jnp.zeros((1, LANE), F32)
        for i in range(nblk):
            z = fa_ref[pl.ds(i * CUM_BLK, CUM_BLK), :] + b_ref[...]
            lf = jnp.minimum(z, 0.0) - jnp.log(1.0 + jnp.exp(-jnp.abs(z)))
            cb = _tri_apply(tri, lf) + carry
            c_ref[pl.ds(i * CUM_BLK, CUM_BLK), :] = cb
            carry = cb[CUM_BLK - 1:CUM_BLK, :]

    return pl.pallas_call(
        body, name=name, grid=(nb,),
        in_specs=[pl.BlockSpec((S, LANE), lambda b: (b, CB_FA)), pl.BlockSpec((1, LANE), lambda b: (0, 0))],
        out_specs=pl.BlockSpec((S, LANE), lambda b: (b, 0)),
        out_shape=jax.ShapeDtypeStruct((nb * S, LANE), F32),
        compiler_params=_cparams(("parallel",)),
    )(P, bf_pad)


def _forget_bwd(P, bf_pad, dcb, nb, name):
    nblk = S // CUM_BLK

    def body(fa_ref, b_ref, dc_ref, dfa_ref, db_ref):
        b = pl.program_id(0)
        tri = _tri(CUM_BLK, upper=True)
        lane = lax.broadcasted_iota(jnp.int32, (CUM_BLK, LANE), 1)
        carry = jnp.zeros((1, LANE), F32)
        dbacc = jnp.zeros((1, LANE), F32)
        for i in reversed(range(nblk)):
            rows = pl.ds(i * CUM_BLK, CUM_BLK)
            dc = jnp.zeros((CUM_BLK, LANE), F32)
            dcv = dc_ref[rows, :]
            for h in range(2 * NHP):
                dc = jnp.where(lane == h, -dcv[:, HD * h:HD * h + 1], dc)
            dl = _tri_apply(tri, dc) + carry
            carry = dl[0:1, :]
            z = fa_ref[rows, :] + b_ref[...]
            dz = jnp.where(lane < 2 * NHP, dl * (1.0 - _sigmoid(z)), 0.0)
            dfa_ref[rows, :] = dz.astype(BF16)
            dbacc = dbacc + jnp.sum(dz, axis=0, keepdims=True)

        dbfull = jnp.broadcast_to(dbacc, (8, LANE))

        @pl.when(b == 0)
        def _():
            db_ref[...] = dbfull

        @pl.when(b > 0)
        def _():
            db_ref[...] += dbfull

    return pl.pallas_call(
        body, name=name, grid=(nb,),
        in_specs=[pl.BlockSpec((S, LANE), lambda b: (b, CB_FA)), pl.BlockSpec((1, LANE), lambda b: (0, 0)),
                  pl.BlockSpec((S, WA), lambda b: (b, 0))],
        out_specs=[pl.BlockSpec((S, LANE), lambda b: (b, 0)), pl.BlockSpec((8, LANE), lambda b: (0, 0))],
        out_shape=[jax.ShapeDtypeStruct((nb * S, LANE), BF16), jax.ShapeDtypeStruct((8, LANE), F32)],
        compiler_params=_cparams(("arbitrary",)),
    )(P, bf_pad, dcb)


FQ = 256
NT_DIMS = (((1,), (1,)), ((), ()))
AUGW = 6


def _head_masks(shape):
    lane = lax.broadcasted_iota(jnp.int32, shape, 1)
    return lane < HD, lane >= HD


def _fox_bias_terms(c_ref, hp):
    lane = lax.broadcasted_iota(jnp.int32, (S, LANE), 1)
    cv = c_ref[...]
    return [_split3(jnp.sum(jnp.where(lane == 2 * hp + e, cv, 0.0), axis=1, keepdims=True)) for e in range(2)]


def _fox_ext(x, terms, side, heads, only):
    lane = lax.broadcasted_iota(jnp.int32, (S, LANE), 1)
    one = jnp.ones((S, 1), BF16)
    aug = jnp.zeros((S, LANE), BF16)
    for e in heads:
        hi, mid, lo = terms[e]
        cols = (hi, mid, lo, one, one, one) if side == "q" else (one, one, one, -hi, -mid, -lo)
        for i, col in enumerate(cols):
            aug = jnp.where(lane == AUGW * e + i, col, aug)
    if only is not None:
        x = jnp.where(_head_masks((S, LANE))[only], x, jnp.zeros_like(x))
    return jnp.concatenate([x, aug], axis=1)


def _halves(x, lane_mask):
    return jnp.where(lane_mask, x[0:FQ, 0:LANE], x[FQ:2 * FQ, 0:LANE])


def _fox_fwd(P, c, nb, name):
    def body(q_ref, k_ref, v_ref, c_ref, o_ref, lse_ref, qm0, qm1, kx, vx):
        hp = pl.program_id(1)
        terms = _fox_bias_terms(c_ref, hp)
        qv = (q_ref[...] * SCALE).astype(BF16)
        qm0[...] = _fox_ext(qv, terms, "q", (0,), 0)
        qm1[...] = _fox_ext(qv, terms, "q", (1,), 1)
        kx[...] = _fox_ext(k_ref[...].astype(BF16), terms, "k", (0, 1), None)
        lane = lax.broadcasted_iota(jnp.int32, (S, LANE), 1)
        vx[...] = jnp.concatenate([v_ref[...].astype(BF16), jnp.where(lane == 0, 1.0, 0.0).astype(BF16)], axis=1)
        tmask = _head_masks((FQ, LANE))[0]
        row = lax.broadcasted_iota(jnp.int32, (2 * FQ, FQ), 0) & (FQ - 1)
        col = lax.broadcasted_iota(jnp.int32, (2 * FQ, FQ), 1)
        for i in range(S // FQ):
            r0 = i * FQ
            qt = jnp.concatenate([qm0[pl.ds(r0, FQ), :], qm1[pl.ds(r0, FQ), :]], axis=0)
            sd = lax.dot_general(qt, kx[pl.ds(r0, FQ), :], NT_DIMS, preferred_element_type=F32)
            sd = jnp.where(col <= row, sd, NEG)
            m = jnp.max(sd, axis=1, keepdims=True)
            if i > 0:
                so = lax.dot_general(qt, kx[pl.ds(0, r0), :], NT_DIMS, preferred_element_type=F32)
                m = jnp.maximum(m, jnp.max(so, axis=1, keepdims=True))
            acc = jnp.dot(jnp.exp(sd - m).astype(BF16), vx[pl.ds(r0, FQ), :], preferred_element_type=F32)
            if i > 0:
                acc = acc + jnp.dot(jnp.exp(so - m).astype(BF16), vx[pl.ds(0, r0), :], preferred_element_type=F32)
            l = acc[:, LANE:LANE + 1]
            o_ref[pl.ds(r0, FQ), :] = _halves(acc / l, tmask)
            lse_ref[pl.ds(r0, FQ), :] = _halves(jnp.broadcast_to(m + jnp.log(l), (2 * FQ, LANE)), tmask)

    def colblk(off):
        return pl.BlockSpec((S, LANE), lambda b, hp: (b, off + hp))

    return pl.pallas_call(
        body, name=name, grid=(nb, NHP),
        in_specs=[colblk(CB_QA), colblk(CB_KA), colblk(CB_VA), pl.BlockSpec((S, LANE), lambda b, hp: (b, 0))],
        out_specs=[colblk(0), colblk(0)],
        out_shape=[jax.ShapeDtypeStruct((nb * S, WA), F32), jax.ShapeDtypeStruct((nb * S, WA), F32)],
        scratch_shapes=[pltpu.VMEM((S, 2 * LANE), BF16)] * 4,
        compiler_params=_cparams(("parallel", "parallel")),
    )(P, P, P, c)


def _fox_bwd(P, c, o, lse, do, nb, name):
    def body(q_ref, k_ref, v_ref, c_ref, o_ref, lse_ref, do_ref, dq_ref, dk_ref, dv_ref, dc_ref,
             km0, km1, qx, vm0, vm1, dob, kt0, kt1, rows, dqt, rsum):
        hp = pl.program_id(1)
        terms = _fox_bias_terms(c_ref, hp)
        kv = k_ref[...].astype(BF16)
        km0[...] = _fox_ext(kv, terms, "k", (0,), 0)
        km1[...] = _fox_ext(kv, terms, "k", (1,), 1)
        qx[...] = _fox_ext((q_ref[...] * SCALE).astype(BF16), terms, "q", (0, 1), None)
        masks = _head_masks((S, LANE))
        vv = v_ref[...].astype(BF16)
        zero = jnp.zeros((S, LANE), BF16)
        vm0[...] = jnp.where(masks[0], vv, zero)
        vm1[...] = jnp.where(masks[1], vv, zero)
        dov = do_ref[...]
        dob[...] = dov.astype(BF16)
        ktf = k_ref[...].T
        prodt = (dov * o_ref[...]).T
        lset = lse_ref[...].T
        hrow = lax.broadcasted_iota(jnp.int32, (LANE, S), 0)
        kt0[...] = jnp.where(hrow < HD, ktf, 0.0).astype(BF16)
        kt1[...] = jnp.where(hrow >= HD, ktf, 0.0).astype(BF16)
        for e in range(2):
            rows[e:e + 1, :] = lset[HD * e:HD * e + 1, :]
            rows[2 + e:3 + e, :] = jnp.sum(prodt[HD * e:HD * (e + 1), :], axis=0, keepdims=True)
        dqt[...] = jnp.zeros_like(dqt)
        rsum[...] = jnp.zeros_like(rsum)
        tmask = _head_masks((FQ, LANE))[0]
        row = lax.broadcasted_iota(jnp.int32, (2 * FQ, FQ), 0) & (FQ - 1)
        col = lax.broadcasted_iota(jnp.int32, (2 * FQ, FQ), 1)
        for j in range(S // FQ):
            k0 = j * FQ
            rest = S - k0 - FQ
            spans = [(k0, FQ)] + ([(k0 + FQ, rest)] if rest > 0 else [])
            kte = jnp.concatenate([km0[pl.ds(k0, FQ), :], km1[pl.ds(k0, FQ), :]], axis=0)
            vte = jnp.concatenate([vm0[pl.ds(k0, FQ), :], vm1[pl.ds(k0, FQ), :]], axis=0)
            ktt = jnp.concatenate([kt0[:, pl.ds(k0, FQ)], kt1[:, pl.ds(k0, FQ)]], axis=1)
            dke = jnp.zeros((2 * FQ, 2 * LANE), F32)
            dve = jnp.zeros((2 * FQ, LANE), F32)
            cse = jnp.zeros((2 * FQ, 1), F32)
            for si, (q0, n) in enumerate(spans):
                qs = qx[pl.ds(q0, n), :]
                dos = dob[pl.ds(q0, n), :]
                st = lax.dot_general(kte, qs, NT_DIMS, preferred_element_type=F32)
                if si == 0:
                    st = jnp.where(col >= row, st, NEG)
                dpt = lax.dot_general(vte, dos, NT_DIMS, preferred_element_type=F32)
                pts, dsts = [], []
                for e in range(2):
                    pe = jnp.exp(st[FQ * e:FQ * (e + 1), :] - rows[e:e + 1, pl.ds(q0, n)])
                    de = pe * (dpt[FQ * e:FQ * (e + 1), :] - rows[2 + e:3 + e, pl.ds(q0, n)])
                    rsum[HD * e:HD * e + 8, pl.ds(q0, n)] += _sum8(de)
                    pts.append(pe)
                    dsts.append(de)
                pt = jnp.concatenate(pts, axis=0)
                dst = jnp.concatenate(dsts, axis=0)
                dsb = dst.astype(BF16)
                dve = dve + jnp.dot(pt.astype(BF16), dos, preferred_element_type=F32)
                dke = dke + jnp.dot(dsb, qs, preferred_element_type=F32)
                dqt[:, pl.ds(q0, n)] += jnp.dot(ktt, dsb, preferred_element_type=F32)
                cse = cse + jnp.sum(dst, axis=1, keepdims=True)
            dk_ref[pl.ds(k0, FQ), :] = _halves(dke, tmask).astype(BF16)
            dv_ref[pl.ds(k0, FQ), :] = _halves(dve, tmask).astype(BF16)
            dc_ref[pl.ds(k0, FQ), :] = _halves(jnp.broadcast_to(cse, (2 * FQ, LANE)), tmask)
        dq_ref[...] = (dqt[...].T * SCALE).astype(BF16)
        tot = [jnp.sum(rsum[HD * e:HD * e + 8, :], axis=0, keepdims=True) for e in range(2)]
        dc_ref[...] = dc_ref[...] - jnp.where(hrow == 0, tot[0], jnp.where(hrow == HD, tot[1], 0.0)).T

    def colblk(off):
        return pl.BlockSpec((S, LANE), lambda b, hp: (b, off + hp))

    wide = pltpu.VMEM((S, 2 * LANE), BF16)
    half = pltpu.VMEM((S, LANE), BF16)
    return pl.pallas_call(
        body, name=name, grid=(nb, NHP),
        in_specs=[colblk(CB_QA), colblk(CB_KA), colblk(CB_VA), pl.BlockSpec((S, LANE), lambda b, hp: (b, 0)),
                  colblk(0), colblk(0), colblk(0)],
        out_specs=[colblk(0)] * 4,
        out_shape=[jax.ShapeDtypeStruct((nb * S, WA), BF16)] * 3 + [jax.ShapeDtypeStruct((nb * S, WA), F32)],
        scratch_shapes=[wide, wide, wide, half, half, half, pltpu.VMEM((LANE, S), BF16), pltpu.VMEM((LANE, S), BF16),
                        pltpu.VMEM((8, S), F32), pltpu.VMEM((LANE, S), F32), pltpu.VMEM((LANE, S), F32)],
        compiler_params=_cparams(("parallel", "parallel")),
    )(P, P, P, c, o, lse, do)


DILS = (1, 4, 16)
DB = 128


def _regroup_load(ref, d, scale=None):
    if d == 1:
        v = ref[...]
    else:
        L = S // d
        v = jnp.concatenate([ref[pl.ds(r, L, stride=d), :] for r in range(d)], axis=0)
    return v if scale is None else v * scale


def _regroup_store(ref, d, val_ref, accumulate):
    L = S // d
    for r in range(d):
        src = val_ref[pl.ds(r * L, L), :]
        dst = (slice(None), slice(None)) if d == 1 else (pl.ds(r, L, stride=d), slice(None))
        if accumulate:
            ref[dst] = ref[dst] + src
        else:
            ref[dst] = src


def _dil_bands():
    qi = lax.broadcasted_iota(jnp.int32, (DB, 2 * DB), 0)
    ki = lax.broadcasted_iota(jnp.int32, (DB, 2 * DB), 1)
    band = (ki >= qi) & (ki <= qi + DB)
    return band, band & (ki >= DB)


def _dil_valid(bands, bk, d):
    band, own = bands
    has_prev = (bk % ((S // d) // DB)) > 0
    return own | (band & has_prev)


def _dil_keys(kd, vd, r0, bands, bk, d):
    if S // d == DB:
        valid = bands[1][:, DB:]
        kk, vv = kd[pl.ds(r0 + DB, DB), :], vd[pl.ds(r0 + DB, DB), :]
    else:
        valid = _dil_valid(bands, bk, d)
        kk, vv = kd[pl.ds(r0, 2 * DB), :], vd[pl.ds(r0, 2 * DB), :]
    return kk, vv, jnp.concatenate([valid, valid], axis=0)


def _stack_heads(x, masks):
    zero = jnp.zeros_like(x)
    return jnp.concatenate([jnp.where(masks[0], x, zero), jnp.where(masks[1], x, zero)], axis=0)


def _dil_fwd(P, nb, name):
    nblk = S // DB

    def body(q_ref, k_ref, v_ref, o_ref, lse_ref, qd, kd, vd, rnum, rm, rl, num_n, m_n, l_n):
        masks = _head_masks((DB, LANE))
        bands = _dil_bands()
        for bi, d in enumerate(DILS):
            qd[...] = _regroup_load(q_ref, d, SCALE).astype(BF16)
            kd[pl.ds(0, DB), :] = jnp.zeros((DB, LANE), BF16)
            vd[pl.ds(0, DB), :] = jnp.zeros((DB, LANE), BF16)
            kd[pl.ds(DB, S), :] = _regroup_load(k_ref, d).astype(BF16)
            vd[pl.ds(DB, S), :] = _regroup_load(v_ref, d).astype(BF16)

            def blk(bk, _, d=d):
                r0 = pl.multiple_of(bk * DB, DB)
                qt = qd[pl.ds(r0, DB), :]
                kk, vv, valid = _dil_keys(kd, vd, r0, bands, bk, d)
                qm = _stack_heads(qt, masks)
                s = lax.dot_general(qm, kk, NT_DIMS, preferred_element_type=F32)
                s = jnp.where(valid, s, NEG)
                m = jnp.max(s, axis=1, keepdims=True)
                p = jnp.exp(s - m)
                l = jnp.sum(p, axis=1, keepdims=True)
                num = jnp.dot(p.astype(BF16), vv, preferred_element_type=F32)
                rnum[pl.ds(r0, DB), :] = jnp.where(masks[0], num[0:DB], num[DB:2 * DB])
                rm[pl.ds(r0, DB), :] = jnp.where(masks[0], m[0:DB], m[DB:2 * DB])
                rl[pl.ds(r0, DB), :] = jnp.where(masks[0], l[0:DB], l[DB:2 * DB])
                return 0

            lax.fori_loop(0, nblk, blk, 0, unroll=8)
            _regroup_store(num_n.at[bi], d, rnum, False)
            _regroup_store(m_n.at[bi], d, rm, False)
            _regroup_store(l_n.at[bi], d, rl, False)

        m_all = jnp.maximum(jnp.maximum(m_n[0], m_n[1]), m_n[2])
        num = jnp.zeros((S, LANE), F32)
        den = jnp.zeros((S, LANE), F32)
        for bi in range(3):
            a = jnp.exp(m_n[bi] - m_all)
            num = num + a * num_n[bi]
            den = den + a * l_n[bi]
        o_ref[...] = num / den
        lse_ref[...] = m_all + jnp.log(den)

    def colblk(off):
        return pl.BlockSpec((S, LANE), lambda b, hp: (b, off + hp))

    return pl.pallas_call(
        body, name=name, grid=(nb, NHP),
        in_specs=[colblk(CB_QB), colblk(CB_KB), colblk(CB_VB)],
        out_specs=[colblk(0), colblk(0)],
        out_shape=[jax.ShapeDtypeStruct((nb * S, WA), F32), jax.ShapeDtypeStruct((nb * S, WA), F32)],
        scratch_shapes=[pltpu.VMEM((S, LANE), BF16), pltpu.VMEM((S + DB, LANE), BF16), pltpu.VMEM((S + DB, LANE), BF16),
                        pltpu.VMEM((S, LANE), F32), pltpu.VMEM((S, LANE), F32), pltpu.VMEM((S, LANE), F32),
                        pltpu.VMEM((3, S, LANE), F32), pltpu.VMEM((3, S, LANE), F32), pltpu.VMEM((3, S, LANE), F32)],
        compiler_params=_cparams(("parallel", "parallel")),
    )(P, P, P)


def _dil_bwd(P, o, lse, do, nb, name):
    nblk = S // DB

    def body(q_ref, k_ref, v_ref, o_ref, lse_ref, do_ref, dq_ref, dk_ref, dv_ref,
             qd, kd, vd, dod, lsed, dsd, dsum, dq_r, dk_r, dv_r, dq_n, dk_n, dv_n):
        masks = _head_masks((DB, LANE))
        fmask = _head_masks((S, LANE))
        prod = do_ref[...] * o_ref[...]
        d0 = jnp.sum(jnp.where(fmask[0], prod, 0.0), axis=1, keepdims=True)
        d1 = jnp.sum(jnp.where(fmask[1], prod, 0.0), axis=1, keepdims=True)
        dsum[...] = jnp.where(fmask[0], d0, d1)
        tn = (((0,), (0,)), ((), ()))
        bands = _dil_bands()
        for bi, d in enumerate(DILS):
            qd[...] = _regroup_load(q_ref, d, SCALE).astype(BF16)
            kd[pl.ds(0, DB), :] = jnp.zeros((DB, LANE), BF16)
            vd[pl.ds(0, DB), :] = jnp.zeros((DB, LANE), BF16)
            kd[pl.ds(DB, S), :] = _regroup_load(k_ref, d).astype(BF16)
            vd[pl.ds(DB, S), :] = _regroup_load(v_ref, d).astype(BF16)
            dod[...] = _regroup_load(do_ref, d).astype(BF16)
            lsed[...] = _regroup_load(lse_ref, d)
            dsd[...] = _regroup_load(dsum, d)
            dq_r[...] = jnp.zeros_like(dq_r)
            dk_r[...] = jnp.zeros_like(dk_r)
            dv_r[...] = jnp.zeros_like(dv_r)

            def blk(bk, _, d=d):
                r0 = pl.multiple_of(bk * DB, DB)
                qt = qd[pl.ds(r0, DB), :]
                dot = dod[pl.ds(r0, DB), :]
                lt = lsed[pl.ds(r0, DB), :]
                dt = dsd[pl.ds(r0, DB), :]
                kk, vv, valid = _dil_keys(kd, vd, r0, bands, bk, d)
                kw = kk.shape[0]
                qm = _stack_heads(qt, masks)
                dom = _stack_heads(dot, masks)
                lcol = jnp.concatenate([lt[:, 0:1], lt[:, HD:HD + 1]], axis=0)
                dcol = jnp.concatenate([dt[:, 0:1], dt[:, HD:HD + 1]], axis=0)
                s = lax.dot_general(qm, kk, NT_DIMS, preferred_element_type=F32)
                s = jnp.where(valid, s, NEG)
                p = jnp.exp(s - lcol)
                dp = lax.dot_general(dom, vv, NT_DIMS, preferred_element_type=F32)
                ds = (p * (dp - dcol)).astype(BF16)
                dvt = lax.dot_general(p.astype(BF16), dom, tn, preferred_element_type=F32)
                dkt = lax.dot_general(ds, qm, tn, preferred_element_type=F32)
                dqt = jnp.dot(ds, kk, preferred_element_type=F32)
                dq_r[pl.ds(r0, DB), :] = jnp.where(masks[0], dqt[0:DB], dqt[DB:2 * DB])
                dk_r[pl.ds(r0 + 2 * DB - kw, kw), :] += dkt
                dv_r[pl.ds(r0 + 2 * DB - kw, kw), :] += dvt
                return 0

            lax.fori_loop(0, nblk, blk, 0, unroll=4)
            _regroup_store(dq_n, d, dq_r, bi > 0)
            _regroup_store(dk_n, d, dk_r.at[pl.ds(DB, S)], bi > 0)
            _regroup_store(dv_n, d, dv_r.at[pl.ds(DB, S)], bi > 0)

        dq_ref[...] = (dq_n[...] * SCALE).astype(BF16)
        dk_ref[...] = dk_n[...].astype(BF16)
        dv_ref[...] = dv_n[...].astype(BF16)

    def colblk(off):
        return pl.BlockSpec((S, LANE), lambda b, hp: (b, off + hp))

    big = pltpu.VMEM((S, LANE), F32)
    bigp = pltpu.VMEM((S + DB, LANE), F32)
    return pl.pallas_call(
        body, name=name, grid=(nb, NHP),
        in_specs=[colblk(CB_QB), colblk(CB_KB), colblk(CB_VB), colblk(0), colblk(0), colblk(0)],
        out_specs=[colblk(0)] * 3,
        out_shape=[jax.ShapeDtypeStruct((nb * S, WA), BF16)] * 3,
        scratch_shapes=[pltpu.VMEM((S, LANE), BF16), pltpu.VMEM((S + DB, LANE), BF16), pltpu.VMEM((S + DB, LANE), BF16),
                        pltpu.VMEM((S, LANE), BF16), big, big, big, big, bigp, bigp, big, big, big],
        compiler_params=_cparams(("parallel", "parallel")),
    )(P, P, P, o, lse, do)


RC = 256
NSHW = 4


def _window(src, start, buf):
    if start % 8 == 0:
        return src[pl.ds(start, RC), :]
    buf[...] = src[pl.ds(start, RC), :]
    return buf[...]
CPAD = 32


NSUB = 8
CROWS = S + CPAD


def _preshift(src, dst):
    for b in range(NSUB):
        dst[b] = src[pl.ds(b, CROWS), :]


def _shifted(dst, start):
    return dst[start % NSUB, pl.ds(start - start % NSUB, RC), :]


def _conv_chunk(gsh, r0, cw_ref, cb_ref):
    acc = jnp.zeros((RC, CC), F32) + cb_ref[...]
    for k in range(CK):
        acc = acc + cw_ref[k:k + 1, :] * _shifted(gsh, r0 + CPAD - (CK - 1) + k)
    return acc


def _cnorm(c0, cng_ref, cnb_ref):
    mu = jnp.mean(c0, axis=1, keepdims=True)
    xc = c0 - mu
    rstd = lax.rsqrt(jnp.mean(xc * xc, axis=1, keepdims=True) + EPS)
    n = xc * rstd
    return n, rstd, n * cng_ref[...] + cnb_ref[...]


NORM_ROWS = 512
YC_BLK = 2 * WA // CC


def _attn_norm_fwd(of, od, gof, god, name):
    T = of.shape[0]

    def body(of_ref, od_ref, gof_ref, god_ref, y_ref):
        for i, (src, g_ref) in enumerate(((of_ref, gof_ref), (od_ref, god_ref))):
            v = src[...]
            r = lax.rsqrt(jnp.mean(v * v, axis=1, keepdims=True) + EPS)
            y_ref[:, i * WA:(i + 1) * WA] = (v * r * g_ref[...]).astype(BF16)

    row = lambda w: pl.BlockSpec((NORM_ROWS, w), lambda i: (i, 0))
    par = pl.BlockSpec((1, WA), lambda i: (0, 0))
    return pl.pallas_call(
        body, name=name, grid=(T // NORM_ROWS,),
        in_specs=[row(WA), row(WA), par, par], out_specs=row(2 * WA),
        out_shape=jax.ShapeDtypeStruct((T, D), BF16),
        compiler_params=_cparams(("parallel",)),
    )(of, od, gof, god)


def _attn_norm_bwd(of, od, dy, gof, god, name):
    T = of.shape[0]

    def body(of_ref, od_ref, dy_ref, gof_ref, god_ref, dof_ref, dod_ref, dgo_ref):
        @pl.when(pl.program_id(0) == 0)
        def _():
            dgo_ref[...] = jnp.zeros_like(dgo_ref)

        for i, (src, g_ref, dst) in enumerate(((of_ref, gof_ref, dof_ref), (od_ref, god_ref, dod_ref))):
            v = src[...]
            dyv = dy_ref[:, i * WA:(i + 1) * WA].astype(F32)
            r = lax.rsqrt(jnp.mean(v * v, axis=1, keepdims=True) + EPS)
            a = dyv * g_ref[...]
            dst[...] = r * a - v * (r * r * r * jnp.mean(v * a, axis=1, keepdims=True))
            dgo_ref[i:i + 1, :] += jnp.sum(dyv * v * r, axis=0, keepdims=True)

    row = lambda w: pl.BlockSpec((NORM_ROWS, w), lambda i: (i, 0))
    par = pl.BlockSpec((1, WA), lambda i: (0, 0))
    return pl.pallas_call(
        body, name=name, grid=(T // NORM_ROWS,),
        in_specs=[row(WA), row(WA), row(2 * WA), par, par],
        out_specs=[row(WA), row(WA), pl.BlockSpec((8, WA), lambda i: (0, 0))],
        out_shape=[jax.ShapeDtypeStruct((T, WA), F32), jax.ShapeDtypeStruct((T, WA), F32),
                   jax.ShapeDtypeStruct((8, WA), F32)],
        compiler_params=_cparams(("arbitrary",)),
    )(of, od, dy, gof, god)


def _conv_specs():
    gblk = lambda off: pl.BlockSpec((S, CC), lambda b: (b, off))
    par = lambda r: pl.BlockSpec((r, CC), lambda b: (0, 0))
    return gblk, par


def _conv_fwd(P, y, cw, cb, cng, cnb, nb, name):
    def body(gv_ref, gg_ref, cw_ref, cb_ref, cng_ref, cnb_ref, y_in, y_ref, c0_ref, gpad, gsh):
        del y_in
        gpad[pl.ds(0, CPAD), :] = jnp.zeros((CPAD, CC), F32)
        gpad[pl.ds(CPAD, S), :] = gv_ref[...] * _sigmoid(gg_ref[...])
        gpad[pl.ds(CROWS, NSUB), :] = jnp.zeros((NSUB, CC), F32)
        _preshift(gpad, gsh)
        for ci in range(S // RC):
            r0 = ci * RC
            c0 = _conv_chunk(gsh, r0, cw_ref, cb_ref)
            c0_ref[pl.ds(r0, RC), :] = c0
            _, _, z = _cnorm(c0, cng_ref, cnb_ref)
            y_ref[pl.ds(r0, RC), :] = (z * _sigmoid(z)).astype(BF16)

    gblk, par = _conv_specs()
    return pl.pallas_call(
        body, name=name, grid=(nb,),
        in_specs=[gblk(CB_GV), gblk(CB_GG), par(CPAD), par(1), par(1), par(1), pl.BlockSpec(memory_space=pl.ANY)],
        out_specs=[gblk(YC_BLK), gblk(0)],
        out_shape=[jax.ShapeDtypeStruct((nb * S, D), BF16), jax.ShapeDtypeStruct((nb * S, CC), F32)],
        input_output_aliases={6: 0},
        scratch_shapes=[pltpu.VMEM((CROWS + NSUB, CC), F32), pltpu.VMEM((NSUB, CROWS, CC), F32)],
        compiler_params=_cparams(("parallel",)),
    )(P, P, cw, cb, cng, cnb, y)


def _conv_bwd(P, c0, dy, cw, cng, cnb, nb, name):
    def body(gv_ref, gg_ref, c0_ref, dy_ref, cw_ref, cng_ref, cnb_ref, dg_ref, dcw_ref, dsm_ref, dpad, dsh):
        @pl.when(pl.program_id(0) == 0)
        def _():
            dcw_ref[...] = jnp.zeros_like(dcw_ref)
            dsm_ref[...] = jnp.zeros_like(dsm_ref)

        dpad[pl.ds(S, CPAD + NSUB), :] = jnp.zeros((CPAD + NSUB, CC), F32)
        zero = jnp.zeros((8, CC), F32)
        dcb, dcng, dcnb = zero, zero, zero
        for ci in range(S // RC):
            r0 = ci * RC
            n, rstd, z = _cnorm(c0_ref[pl.ds(r0, RC), :], cng_ref, cnb_ref)
            sz = _sigmoid(z)
            dz = dy_ref[pl.ds(r0, RC), :].astype(F32) * (sz * (1.0 + z * (1.0 - sz)))
            dcng = dcng + _sum8(dz * n)
            dcnb = dcnb + _sum8(dz)
            dn = dz * cng_ref[...]
            dc0 = rstd * (dn - jnp.mean(dn, axis=1, keepdims=True) - n * jnp.mean(dn * n, axis=1, keepdims=True))
            dcb = dcb + _sum8(dc0)
            dpad[pl.ds(r0, RC), :] = dc0
        dsm_ref[0:1, :] += jnp.sum(dcb, axis=0, keepdims=True)
        dsm_ref[1:2, :] += jnp.sum(dcng, axis=0, keepdims=True)
        dsm_ref[2:3, :] += jnp.sum(dcnb, axis=0, keepdims=True)

        _preshift(dpad, dsh)
        dws = [zero] * CK
        for ci in range(S // RC):
            r0 = ci * RC
            sg = _sigmoid(gg_ref[pl.ds(r0, RC), :])
            gvc = gv_ref[pl.ds(r0, RC), :]
            glu = gvc * sg
            dgl = jnp.zeros((RC, CC), F32)
            for k in range(CK):
                win = _shifted(dsh, r0 + (CK - 1) - k)
                dws[k] = dws[k] + _sum8(win * glu)
                dgl = dgl + cw_ref[k:k + 1, :] * win
            dg_ref[pl.ds(r0, RC), 0:CC] = (dgl * sg).astype(BF16)
            dg_ref[pl.ds(r0, RC), CC:2 * CC] = (dgl * gvc * sg * (1.0 - sg)).astype(BF16)
        for k in range(CK):
            dcw_ref[k:k + 1, :] += jnp.sum(dws[k], axis=0, keepdims=True)

    gblk, par = _conv_specs()
    return pl.pallas_call(
        body, name=name, grid=(nb,),
        in_specs=[gblk(CB_GV), gblk(CB_GG), gblk(0), gblk(YC_BLK), par(CPAD), par(1), par(1)],
        out_specs=[pl.BlockSpec((S, 2 * CC), lambda b: (b, 0)), par(CPAD), par(8)],
        out_shape=[jax.ShapeDtypeStruct((nb * S, 2 * CC), BF16), jax.ShapeDtypeStruct((CPAD, CC), F32),
                   jax.ShapeDtypeStruct((8, CC), F32)],
        scratch_shapes=[pltpu.VMEM((CROWS + NSUB, CC), F32), pltpu.VMEM((NSUB, CROWS, CC), F32)],
        compiler_params=_cparams(("arbitrary",)),
    )(P, P, c0, dy, cw, cng, cnb)


FC = 512
FPAD = 8
NFB = 2 * DFF // FC


def _ffn_u2_chunk(upad, r0, fw_ref, fb_ref):
    acc = jnp.zeros((RC, FC), F32) + fb_ref[...]
    for k in range(FK):
        acc = acc + fw_ref[k:k + 1, :] * upad[pl.ds(r0 + FPAD - (FK - 1) + k, RC), :]
    return acc


def _ffn_fwd(U, fw, fb, nb, name):
    def body(u_ref, fw_ref, fb_ref, h_ref, u2_ref, upad):
        upad[pl.ds(0, FPAD), :] = jnp.zeros((FPAD, FC), F32)
        upad[pl.ds(FPAD, S), :] = u_ref[...].astype(F32)
        for ci in range(S // RC):
            r0 = ci * RC
            u2 = _ffn_u2_chunk(upad, r0, fw_ref, fb_ref)
            u2_ref[pl.ds(r0, RC), :] = u2.astype(BF16)
            a2, b2 = u2[:, :FC // 2], u2[:, FC // 2:]
            h_ref[pl.ds(r0, RC), :] = (a2 * _sigmoid(a2) * b2).astype(BF16)

    return pl.pallas_call(
        body, name=name, grid=(nb, NFB),
        in_specs=[pl.BlockSpec((S, FC), lambda b, j: (b, j)), pl.BlockSpec((8, FC), lambda b, j: (0, j)),
                  pl.BlockSpec((1, FC), lambda b, j: (0, j))],
        out_specs=[pl.BlockSpec((S, FC // 2), lambda b, j: (b, j)), pl.BlockSpec((S, FC), lambda b, j: (b, j))],
        out_shape=[jax.ShapeDtypeStruct((nb * S, DFF), BF16), jax.ShapeDtypeStruct((nb * S, 2 * DFF), BF16)],
        scratch_shapes=[pltpu.VMEM((S + FPAD, FC), F32)],
        compiler_params=_cparams(("parallel", "parallel")),
    )(U, fw, fb)


def _ffn_bwd(U, U2, dhid, fw, nb, name):
    def body(u_ref, u2_ref, dh_ref, fw_ref, du_ref, dfw_ref, dpad, shw):
        @pl.when(pl.program_id(1) == 0)
        def _():
            dfw_ref[...] = jnp.zeros_like(dfw_ref)

        dpad[pl.ds(S, FPAD), :] = jnp.zeros((FPAD, FC), F32)
        zero = jnp.zeros((8, FC), F32)
        dbias = zero
        for ci in range(S // RC):
            r0 = ci * RC
            u2 = u2_ref[pl.ds(r0, RC), :].astype(F32)
            a2, b2 = u2[:, :FC // 2], u2[:, FC // 2:]
            sa = _sigmoid(a2)
            dh = dh_ref[pl.ds(r0, RC), :].astype(F32)
            du2 = jnp.concatenate([dh * b2 * (sa * (1.0 + a2 * (1.0 - sa))), dh * a2 * sa], axis=1)
            dpad[pl.ds(r0, RC), :] = du2
            dbias = dbias + _sum8(du2)
        dws = [zero] * FK
        for ci in range(S // RC):
            r0 = ci * RC
            uc = u_ref[pl.ds(r0, RC), :].astype(F32)
            du = jnp.zeros((RC, FC), F32)
            for k in range(FK):
                win = _window(dpad, r0 + (FK - 1) - k, shw.at[k % NSHW])
                dws[k] = dws[k] + _sum8(win * uc)
                du = du + fw_ref[k:k + 1, :] * win
            du_ref[pl.ds(r0, RC), :] = du.astype(BF16)
        for k in range(FK):
            dfw_ref[k:k + 1, :] += jnp.sum(dws[k], axis=0, keepdims=True)
        dfw_ref[FK:FK + 1, :] += jnp.sum(dbias, axis=0, keepdims=True)

    blk = pl.BlockSpec((S, FC), lambda j, b: (b, j))
    return pl.pallas_call(
        body, name=name, grid=(NFB, nb),
        in_specs=[blk, blk, pl.BlockSpec((S, FC // 2), lambda j, b: (b, j)), pl.BlockSpec((8, FC), lambda j, b: (0, j))],
        out_specs=[blk, pl.BlockSpec((8, FC), lambda j, b: (0, j))],
        out_shape=[jax.ShapeDtypeStruct((nb * S, 2 * DFF), BF16), jax.ShapeDtypeStruct((8, 2 * DFF), F32)],
        scratch_shapes=[pltpu.VMEM((S + FPAD, FC), F32), pltpu.VMEM((NSHW, RC, FC), F32)],
        compiler_params=_cparams(("parallel", "arbitrary")),
    )(U, U2, dhid, fw)


def _matmul_ffn(a, b, mode, *, out_dtype=F32, tm=1024, tk=2048, name):
    HF = FC // 2
    if mode == "fwd":
        M, K = a.shape

        def body(a_ref, b1_ref, b2_ref, o_ref):
            av = a_ref[...]
            o_ref[:, :HF] = jnp.dot(av, b1_ref[...], preferred_element_type=F32).astype(o_ref.dtype)
            o_ref[:, HF:] = jnp.dot(av, b2_ref[...], preferred_element_type=F32).astype(o_ref.dtype)

        return pl.pallas_call(
            body, name=name, grid=(M // tm, NFB),
            in_specs=[pl.BlockSpec((tm, K), lambda i, j: (i, 0)), pl.BlockSpec((K, HF), lambda i, j: (0, j)),
                      pl.BlockSpec((K, HF), lambda i, j: (0, NFB + j))],
            out_specs=pl.BlockSpec((tm, FC), lambda i, j: (i, j)),
            out_shape=jax.ShapeDtypeStruct((M, 2 * DFF), out_dtype),
            compiler_params=_cparams(("parallel", "parallel")),
        )(a, b, b)
    if mode == "dx":
        M = a.shape[0]
        N = b.shape[0]
        tm = 512

        def body(a_ref, b_ref, o_ref):
            acc = None
            for j in range(NFB):
                for half in range(2):
                    av = a_ref[:, j * FC + half * HF:j * FC + (half + 1) * HF]
                    bv = b_ref[:, half * DFF + j * HF:half * DFF + (j + 1) * HF]
                    d = lax.dot_general(av, bv, NT_DIMS, preferred_element_type=F32)
                    acc = d if acc is None else acc + d
            o_ref[...] = acc.astype(o_ref.dtype)

        return pl.pallas_call(
            body, name=name, grid=(M // tm,),
            in_specs=[pl.BlockSpec((tm, 2 * DFF), lambda i: (i, 0)), pl.BlockSpec((N, 2 * DFF), lambda i: (0, 0))],
            out_specs=pl.BlockSpec((tm, N), lambda i: (i, 0)),
            out_shape=jax.ShapeDtypeStruct((M, N), out_dtype),
            compiler_params=_cparams(("parallel",)),
        )(a, b)
    assert mode == "dw"
    T, M = a.shape
    nk = T // tk

    def body(a_ref, g_ref, oa_ref, ob_ref, acc):
        k = pl.program_id(1)
        prod = lax.dot_general(a_ref[...], g_ref[...], (((0,), (0,)), ((), ())), preferred_element_type=F32)

        @pl.when(k == 0)
        def _():
            acc[...] = prod

        @pl.when(k > 0)
        def _():
            acc[...] += prod

        @pl.when(k == nk - 1)
        def _():
            oa_ref[...] = acc[:, :HF]
            ob_ref[...] = acc[:, HF:]

    half = pl.BlockSpec((M, HF), lambda j, k: (0, j))
    return pl.pallas_call(
        body, name=name, grid=(NFB, nk),
        in_specs=[pl.BlockSpec((tk, M), lambda j, k: (k, 0)), pl.BlockSpec((tk, FC), lambda j, k: (k, j))],
        out_specs=[half, half],
        out_shape=[jax.ShapeDtypeStruct((M, DFF), F32)] * 2,
        scratch_shapes=[pltpu.VMEM((M, FC), F32)],
        compiler_params=_cparams(("parallel", "arbitrary")),
    )(a, b)


def _adamw_body(w_ref, g_ref, m_ref, v_ref, d_ref, nm_ref, nv_ref):
    g = g_ref[...]
    m = ADAM_B1 * m_ref[...] + (1.0 - ADAM_B1) * g
    v = ADAM_B2 * v_ref[...] + (1.0 - ADAM_B2) * (g * g)
    m_hat = m / (1.0 - ADAM_B1 ** ADAM_STEP)
    v_hat = v / (1.0 - ADAM_B2 ** ADAM_STEP)
    d_ref[...] = -ADAM_LR * (m_hat / (jnp.sqrt(v_hat) + ADAM_EPS) + ADAM_WD * w_ref[...])
    nm_ref[...] = m
    nv_ref[...] = v


def _adamw(w, g, m, v, name):
    shape = w.shape
    R = 1
    for s in shape[:-1]:
        R *= s
    C = shape[-1]
    args = [a.reshape(R, C) for a in (w, g, m, v)]
    tr = R
    for cand in (512, 352, 256, 128, 64, 32, 16, 8):
        if R % cand == 0 and cand * C * 4 * 14 <= 24 * 1024 * 1024:
            tr = cand
            break
    blk = pl.BlockSpec((tr, C), lambda i: (i, 0))
    outs = pl.pallas_call(
        functools.partial(_adamw_body), name=name, grid=(R // tr,),
        in_specs=[blk] * 4, out_specs=[blk] * 3,
        out_shape=[jax.ShapeDtypeStruct((R, C), F32)] * 3,
        compiler_params=_cparams(("parallel",)),
    )(*args)
    return [o.reshape(shape) for o in outs]


def _rs_row_tile(H):
    th = 128 if H % 128 == 0 else 176
    assert H % th == 0
    return th


def _add_half(g, r1, place, name):
    _, R, C = g.shape
    H = R // 2
    th = _rs_row_tile(H)
    nh = H // th

    def body(s_ref, g_ref, r_ref, o_ref):
        o_ref[...] = (g_ref[...] + r_ref[...]).astype(BF16)

    grid_spec = pltpu.PrefetchScalarGridSpec(
        num_scalar_prefetch=1, grid=(NCHIP, nh),
        in_specs=[pl.BlockSpec((None, th, C), lambda p, i, s: (p, s[1] * nh + i, 0)),
                  pl.BlockSpec((None, th, C), lambda p, i, s: (p, i, 0))],
        out_specs=pl.BlockSpec((None, th, C), lambda p, i, s: (p, i, 0)))
    return pl.pallas_call(
        body, name=name, grid_spec=grid_spec, out_shape=jax.ShapeDtypeStruct((NCHIP, H, C), BF16),
        compiler_params=_cparams(("parallel", "parallel")),
    )(place, g, r1)


def _sum_slots(g, r1, r2, place, name):
    _, R, C = g.shape
    H = R // 2
    th = _rs_row_tile(H)
    nh = H // th

    def body(s_ref, g_ref, r1_ref, r2_ref, o_ref):
        acc = g_ref[...] + r1_ref[...]
        for j in range(NCHIP - 1):
            acc = acc + r2_ref[j].astype(F32)
        o_ref[...] = acc

    grid_spec = pltpu.PrefetchScalarGridSpec(
        num_scalar_prefetch=1, grid=(nh,),
        in_specs=[pl.BlockSpec((None, th, C), lambda i, s: (s[0], s[1] * nh + i, 0)),
                  pl.BlockSpec((None, th, C), lambda i, s: (s[0], i, 0)),
                  pl.BlockSpec((NCHIP - 1, th, C), lambda i, s: (0, i, 0))],
        out_specs=pl.BlockSpec((None, th, C), lambda i, s: (s[1], i, 0)))
    return pl.pallas_call(
        body, name=name, grid_spec=grid_spec, out_shape=jax.ShapeDtypeStruct((2, H, C), F32),
        compiler_params=_cparams(("parallel",)),
    )(place, g, r1, r2)


MESH = pl.DeviceIdType.MESH
HBM = pl.BlockSpec(memory_space=pltpu.HBM)


def _place():
    x, y, c = lax.axis_index("x"), lax.axis_index("y"), lax.axis_index("c")
    chips = [(1 - x, y), (x, 1 - y), (1 - x, 1 - y)]
    return x, y, c, chips


def _rcopy(src, dst, ssem, rsem, dev):
    return pltpu.make_async_remote_copy(src_ref=src, dst_ref=dst, send_sem=ssem, recv_sem=rsem,
                                        device_id=dev, device_id_type=MESH)


def _allgather(shards, split):
    n = len(shards)

    def body(*refs):
        ins, outs = refs[:n], refs[n:2 * n]
        ssem, rsem, fssem, frsem = refs[2 * n:]
        x, y, c, chips = _place()
        me = 2 * x + y
        sib = (x, y, 1 - c)

        def window(t, chip, half):
            if not split[t]:
                return outs[t].at[:, chip]
            H = shards[t].shape[1] // 2
            return outs[t].at[:, chip, pl.ds(half * H, H)]

        sends = []
        for t in range(n):
            H = shards[t].shape[1] // 2
            src = ins[t].at[:, pl.ds(c * H, H)] if split[t] else ins[t]
            for j, (cx, cy) in enumerate(chips):
                cp = _rcopy(src, window(t, me, c), ssem.at[3 * t + j], rsem.at[3 * t + j], (cx, cy, c))
                cp.start()
                sends.append(cp)
        for t in range(n):
            for j, (cx, cy) in enumerate(chips):
                win = window(t, 2 * cx + cy, c)
                _rcopy(win, win, ssem.at[3 * t + j], rsem.at[3 * t + j], (cx, cy, c)).wait_recv()
                if split[t]:
                    cp = _rcopy(win, win, fssem.at[3 * t + j], frsem.at[3 * t + j], sib)
                    cp.start()
                    sends.append(cp)
        for t in range(n):
            if split[t]:
                for j, (cx, cy) in enumerate(chips):
                    win = window(t, 2 * cx + cy, 1 - c)
                    _rcopy(win, win, fssem.at[3 * t + j], frsem.at[3 * t + j], sib).wait_recv()
        for cp in sends:
            cp.wait_send()

    out_shape = [jax.ShapeDtypeStruct((s.shape[0], NCHIP) + s.shape[1:], s.dtype) for s in shards]
    return pl.pallas_call(
        body, name="allgather_weights", in_specs=[HBM] * n, out_specs=[HBM] * n, out_shape=out_shape,
        scratch_shapes=[pltpu.SemaphoreType.DMA((3 * n,))] * 4,
    )(*shards)


def _rs_pair_exchange(gs):
    n = len(gs)

    def body(*refs):
        ins, outs = refs[:n], refs[n:2 * n]
        ssem, rsem = refs[2 * n:]
        x, y, c, _ = _place()
        cps = []
        for t in range(n):
            H = gs[t].shape[1] // 2
            cp = _rcopy(ins[t].at[:, pl.ds((1 - c) * H, H)], outs[t], ssem.at[t], rsem.at[t], (x, y, 1 - c))
            cp.start()
            cps.append(cp)
        for cp in cps:
            cp.wait_recv()
        for cp in cps:
            cp.wait_send()

    out_shape = [jax.ShapeDtypeStruct((NCHIP, g.shape[1] // 2, g.shape[2]), F32) for g in gs]
    return pl.pallas_call(
        body, name="rs_pair_exchange", in_specs=[HBM] * n, out_specs=[HBM] * n, out_shape=out_shape,
        scratch_shapes=[pltpu.SemaphoreType.DMA((n,))] * 2,
    )(*gs)


def _rs_chip_scatter(hs):
    n = len(hs)

    def body(*refs):
        ins, outs = refs[:n], refs[n:2 * n]
        ssem, rsem = refs[2 * n:]
        x, y, c, chips = _place()
        sends = []
        for t in range(n):
            for j, (cx, cy) in enumerate(chips):
                cp = _rcopy(ins[t].at[2 * cx + cy], outs[t].at[j], ssem.at[3 * t + j], rsem.at[3 * t + j], (cx, cy, c))
                cp.start()
                sends.append(cp)
        for cp in sends:
            cp.wait_recv()
        for cp in sends:
            cp.wait_send()

    out_shape = [jax.ShapeDtypeStruct((NCHIP - 1,) + h.shape[1:], h.dtype) for h in hs]
    return pl.pallas_call(
        body, name="rs_chip_scatter", in_specs=[HBM] * n, out_specs=[HBM] * n, out_shape=out_shape,
        scratch_shapes=[pltpu.SemaphoreType.DMA((3 * n,))] * 2,
    )(*hs)


def _rs_pair_gather(fs):
    n = len(fs)

    def body(*refs):
        bufs = refs[n:2 * n]
        ssem, rsem = refs[2 * n:]
        x, y, c, _ = _place()
        sends = []
        for t in range(n):
            cp = _rcopy(bufs[t].at[c], bufs[t].at[c], ssem.at[t], rsem.at[t], (x, y, 1 - c))
            cp.start()
            sends.append(cp)
        for t in range(n):
            win = bufs[t].at[1 - c]
            _rcopy(win, win, ssem.at[t], rsem.at[t], (x, y, 1 - c)).wait_recv()
        for cp in sends:
            cp.wait_send()

    out_shape = [jax.ShapeDtypeStruct(f.shape, F32) for f in fs]
    return pl.pallas_call(
        body, name="rs_pair_gather", in_specs=[HBM] * n, out_specs=[HBM] * n, out_shape=out_shape,
        input_output_aliases={t: t for t in range(n)},
        scratch_shapes=[pltpu.SemaphoreType.DMA((n,))] * 2,
    )(*fs)


def _allreduce_small(buf):
    R = buf.shape[0]

    def body(in_ref, out_ref, slots, ssem, rsem):
        x, y, c, _ = _place()
        me = 4 * x + 2 * y + c
        slots[me] = in_ref[...]
        cps = []
        for k in range(1, NDEV):
            px = 1 - x if k & 4 else x
            py = 1 - y if k & 2 else y
            pc = 1 - c if k & 1 else c
            cp = _rcopy(in_ref, slots.at[me], ssem.at[k - 1], rsem.at[k - 1], (px, py, pc))
            cp.start()
            cps.append((cp, 4 * px + 2 * py + pc))
        for k, (cp, peer) in enumerate(cps):
            _rcopy(in_ref, slots.at[peer], ssem.at[k], rsem.at[k], (x, y, c)).wait_recv()
        for cp, _ in cps:
            cp.wait_send()
        acc = slots[0]
        for p in range(1, NDEV):
            acc = acc + slots[p]
        out_ref[...] = acc

    return pl.pallas_call(
        body, name="allreduce_small", out_shape=jax.ShapeDtypeStruct((R, LANE), F32),
        in_specs=[pl.BlockSpec(memory_space=pltpu.VMEM)], out_specs=pl.BlockSpec(memory_space=pltpu.VMEM),
        scratch_shapes=[pltpu.VMEM((NDEV, R, LANE), F32), pltpu.SemaphoreType.DMA((NDEV - 1,)),
                        pltpu.SemaphoreType.DMA((NDEV - 1,))],
        compiler_params=pltpu.CompilerParams(vmem_limit_bytes=VMEM_LIMIT),
    )(buf)


def _interleave(a):
    lead = a.shape[:-1]
    return a.reshape(*lead, 2, NFB, FC // 2).swapaxes(-3, -2).reshape(*lead, 2 * DFF)


def _uninterleave(a):
    lead = a.shape[:-1]
    return a.reshape(*lead, NFB, 2, FC // 2).swapaxes(-3, -2).reshape(*lead, 2 * DFF)


N_QKV = 3 * WA
N_FG = 2 * NHP


def _pack_in_cols(w):
    pad = jnp.zeros(w.shape[:-1] + (NP - NIN,), w.dtype)
    return jnp.concatenate([w[..., :N_QKV], w[..., N_QKV + N_FG:], w[..., N_QKV:N_QKV + N_FG], pad], axis=-1)


def _unpack_in_cols(g):
    return jnp.concatenate([g[..., :N_QKV], g[..., NIN - N_FG:NIN], g[..., N_QKV:NIN - N_FG]], axis=-1)


def _train_compute(xt, tgt, W, nb):
    saved = []
    xc = xt
    for l in range(DEPTH):
        t = f"_l{l}"
        h = _rms_fwd(xc, W["ln1"][l], "rms1_fwd" + t)
        P = _matmul(h, W["in"][l], tm=1024, tn=1024, tk=D, name="proj_in" + t)
        c = _forget_fwd(P, W["bf"][l], nb, "forget_fwd" + t)
        of, lsef = _fox_fwd(P, c, nb, "fox_fwd" + t)
        od, lsed = _dil_fwd(P, nb, "dil_fwd" + t)
        convp = (W["cw"][l], W["cb"][l], W["cng"][l], W["cnb"][l])
        y = _attn_norm_fwd(of, od, W["gof"][l], W["god"][l], "attn_norm_fwd" + t)
        y, c0 = _conv_fwd(P, y, *convp, nb, "conv_fwd" + t)
        xm = _matmul(y, W["o"][l], add=xc, tm=1024, tn=1024, tk=D, name="proj_out" + t)
        h2 = _rms_fwd(xm, W["ln2"][l], "rms2_fwd" + t)
        U = _matmul_ffn(h2, W["up"][l], "fwd", out_dtype=BF16, name="ffn_up" + t)
        hid, U2 = _ffn_fwd(U, W["fw"][l], W["fb"][l], nb, "ffn_act_fwd" + t)
        xo = _matmul(hid, W["down"][l], add=xm, tm=1024, tn=512, tk=DFF, name="ffn_down" + t)
        saved.append((xc, h, P, c, of, lsef, od, lsed, convp, c0, y, xm, h2, U, U2, hid))
        xc = xo

    loss8, dx, dxb, dgfin = _loss_head(xc, W["gfin"], tgt, "loss_head")

    big = [None] * DEPTH
    small = [None] * DEPTH
    for l in reversed(range(DEPTH)):
        t = f"_l{l}"
        xin, h, P, c, of, lsef, od, lsed, convp, c0, y, xm, h2, U, U2, hid = saved[l]
        dhid = _matmul(dxb, W["down"][l], tb=True, out_dtype=BF16, tm=1024, tn=DFF // 2, tk=D, name="ffn_down_dx" + t)
        dWd = _matmul(hid, dxb, ta=True, tm=DFF // 2, tn=D, tk=2048, name="ffn_down_dw" + t)
        dU, dfw = _ffn_bwd(U, U2, dhid, W["fw"][l], nb, "ffn_act_bwd" + t)
        dh2 = _matmul_ffn(dU, W["up"][l], "dx", name="ffn_up_dx" + t)
        dWup = _matmul_ffn(h2, dU, "dw", name="ffn_up_dw" + t)
        dxm, dxmb, dln2 = _rms_bwd(xm, W["ln2"][l], dh2, dx, "rms2_bwd" + t)
        dy = _matmul(dxmb, W["o"][l], tb=True, out_dtype=BF16, tm=1024, tn=D, tk=D, name="proj_out_dx" + t)
        dWo = _matmul(y, dxmb, ta=True, tm=D, tn=D, tk=2048, name="proj_out_dw" + t)
        dof, dod, dgo = _attn_norm_bwd(of, od, dy, W["gof"][l], W["god"][l], "attn_norm_bwd" + t)
        dgvgg, dcw, dsm = _conv_bwd(P, c0, dy, convp[0], convp[2], convp[3], nb, "conv_bwd" + t)
        dqa, dka, dva, dcb = _fox_bwd(P, c, of, lsef, dof, nb, "fox_bwd" + t)
        dfa, dbf = _forget_bwd(P, W["bf"][l], dcb, nb, "forget_bwd" + t)
        dqb, dkb, dvb = _dil_bwd(P, od, lsed, dod, nb, "dil_bwd" + t)
        dP = jnp.concatenate([dqa, dka, dva, dqb, dkb, dvb, dgvgg, dfa, jnp.zeros_like(dfa)], axis=1)
        dh = _matmul(dP, W["in"][l], tb=True, tm=1024, tn=D, tk=NP, name="proj_in_dx" + t)
        dWin = _matmul(h, dP, ta=True, tm=D, tn=1024, tk=2048, name="proj_in_dw" + t)
        dx, dxb, dln1 = _rms_bwd(xin, W["ln1"][l], dh, dxm, "rms1_bwd" + t)
        big[l] = (dWin, dWo, dWup, dWd)
        small[l] = (dln1, dbf, dgo, dcw, dsm, dln2, dfw)
    return loss8, dx, big, small, dgfin


_SMALL_ROWS = (D // LANE, 8, 8 * WA // LANE, CPAD * CC // LANE, 8 * CC // LANE, D // LANE, 8 * 2 * DFF // LANE)


def kernel(x, ln1_g, w_in, b_forget, g_out_fox, g_out_dil, conv_w, conv_b, cnorm_g, cnorm_b, w_o, ln2_g, w_up, ffn_conv_w, ffn_conv_b, w_down, g_final, loss_target, m_ln1_g, m_w_in, m_b_forget, m_g_out_fox, m_g_out_dil, m_conv_w, m_conv_b, m_cnorm_g, m_cnorm_b, m_w_o, m_ln2_g, m_w_up, m_ffn_conv_w, m_ffn_conv_b, m_w_down, m_g_final, v_ln1_g, v_w_in, v_b_forget, v_g_out_fox, v_g_out_dil, v_conv_w, v_conv_b, v_cnorm_g, v_cnorm_b, v_w_o, v_ln2_g, v_w_up, v_ffn_conv_w, v_ffn_conv_b, v_w_down, v_g_final):
    names = ("ln1_g", "w_in", "b_forget", "g_out_fox", "g_out_dil", "conv_w", "conv_b", "cnorm_g", "cnorm_b",
             "w_o", "ln2_g", "w_up", "ffn_conv_w", "ffn_conv_b", "w_down", "g_final")
    w = dict(zip(names, (ln1_g, w_in, b_forget, g_out_fox, g_out_dil, conv_w, conv_b, cnorm_g, cnorm_b,
                         w_o, ln2_g, w_up, ffn_conv_w, ffn_conv_b, w_down, g_final)))
    m = dict(zip(names, (m_ln1_g, m_w_in, m_b_forget, m_g_out_fox, m_g_out_dil, m_conv_w, m_conv_b, m_cnorm_g,
                         m_cnorm_b, m_w_o, m_ln2_g, m_w_up, m_ffn_conv_w, m_ffn_conv_b, m_w_down, m_g_final)))
    v = dict(zip(names, (v_ln1_g, v_w_in, v_b_forget, v_g_out_fox, v_g_out_dil, v_conv_w, v_conv_b, v_cnorm_g,
                         v_cnorm_b, v_w_o, v_ln2_g, v_w_up, v_ffn_conv_w, v_ffn_conv_b, v_w_down, v_g_final)))
    nb = x.shape[0]
    T = nb * S
    xi, yi, ci = lax.axis_index("x"), lax.axis_index("y"), lax.axis_index("c")
    chip = 2 * xi + yi
    cw_cols = CC // NCHIP
    up_cols = 2 * DFF // NCHIP

    shards = [_pack_in_cols(w_in).astype(BF16), w_o.astype(BF16), w_up.astype(BF16), w_down.astype(BF16),
              jnp.pad(ffn_conv_w, ((0, 0), (0, 8 - FK), (0, 0))),
              jnp.pad(conv_w, ((0, 0), (0, CPAD - CK), (0, LANE - cw_cols)))]
    gathered = _allgather(shards, (True, True, True, True, False, False))
    g_in, g_o, g_up, g_dn, g_fw, g_cw = [
        lax.dynamic_update_slice(g, s[:, None], (0, chip, 0, 0)) for g, s in zip(gathered, shards)]
    up_full = g_up.transpose(0, 2, 1, 3).reshape(DEPTH, D, 2 * DFF)
    fw_full = _interleave(g_fw.transpose(0, 2, 1, 3).reshape(DEPTH, 8, 2 * DFF))
    cw_full = g_cw[..., :cw_cols].transpose(0, 2, 1, 3).reshape(DEPTH, CPAD, CC)
    fb_full = _interleave(ffn_conv_b)
    W = {
        "in": [g_in[l].reshape(D, NP) for l in range(DEPTH)],
        "o": [g_o[l].reshape(D, D) for l in range(DEPTH)],
        "up": [up_full[l] for l in range(DEPTH)],
        "down": [g_dn[l].reshape(DFF, D) for l in range(DEPTH)],
        "ln1": [ln1_g[l] for l in range(DEPTH)],
        "ln2": [ln2_g[l] for l in range(DEPTH)],
        "bf": [jnp.pad(b_forget[l], (0, LANE - N_FG)).reshape(1, LANE) for l in range(DEPTH)],
        "gof": [g_out_fox[l].reshape(1, WA) for l in range(DEPTH)],
        "god": [g_out_dil[l].reshape(1, WA) for l in range(DEPTH)],
        "cw": [cw_full[l] for l in range(DEPTH)],
        "cb": [conv_b[l].reshape(1, CC) for l in range(DEPTH)],
        "cng": [cnorm_g[l].reshape(1, CC) for l in range(DEPTH)],
        "cnb": [cnorm_b[l].reshape(1, CC) for l in range(DEPTH)],
        "fw": [fw_full[l] for l in range(DEPTH)],
        "fb": [fb_full[l].reshape(1, 2 * DFF) for l in range(DEPTH)],
        "gfin": g_final,
    }

    loss8, dx, big, small, dgfin = _train_compute(x.reshape(T, D), loss_target.reshape(T, D), W, nb)

    gs = []
    for l in range(DEPTH):
        dWin, dWo, dWup, dWd = big[l]
        gs += [dWin.reshape(NCHIP, D // NCHIP, NP), dWo.reshape(NCHIP, D // NCHIP, D),
               jnp.stack([half[:, i * up_cols:(i + 1) * up_cols] for half in dWup for i in range(2)]),
               dWd.reshape(NCHIP, DFF // NCHIP, D)]
    r1 = _rs_pair_exchange(gs)
    place = jnp.stack([chip, ci]).astype(jnp.int32)
    hs = [_add_half(g, r, place, f"rs_add_pair_{i}") for i, (g, r) in enumerate(zip(gs, r1))]
    r2 = _rs_chip_scatter(hs)
    fs = [_sum_slots(g, a, b, place, f"rs_add_chips_{i}") for i, (g, a, b) in enumerate(zip(gs, r1, r2))]
    red = _rs_pair_gather(fs)
    red = [r.reshape(r.shape[0] * r.shape[1], r.shape[2]) for r in red]
    grads = {
        "w_in": jnp.stack([_unpack_in_cols(red[4 * l]) for l in range(DEPTH)]),
        "w_o": jnp.stack([red[4 * l + 1] for l in range(DEPTH)]),
        "w_up": jnp.stack([red[4 * l + 2] for l in range(DEPTH)]),
        "w_down": jnp.stack([red[4 * l + 3] for l in range(DEPTH)]),
    }

    parts = []
    for l in range(DEPTH):
        parts += [p.reshape(-1, LANE) for p in small[l]]
    parts += [dgfin.reshape(-1, LANE), loss8]
    tot = _allreduce_small(jnp.concatenate(parts, axis=0))
    off = 0
    per_layer = []
    for l in range(DEPTH):
        vals = []
        for rows in _SMALL_ROWS:
            vals.append(tot[off:off + rows])
            off += rows
        per_layer.append(vals)
    gfin_sum = tot[off:off + D // LANE].reshape(D)
    loss = tot[off + D // LANE, 0]

    def layer_stack(fn):
        return jnp.stack([fn(*per_layer[l]) for l in range(DEPTH)])

    fw_sum = layer_stack(lambda a, b, c_, d, e, f, g: _uninterleave(g.reshape(8, 2 * DFF)))
    cw_sum = layer_stack(lambda a, b, c_, d, e, f, g: d.reshape(CPAD, CC)[:CK])
    sm_sum = layer_stack(lambda a, b, c_, d, e, f, g: e.reshape(8, CC))
    go_sum = layer_stack(lambda a, b, c_, d, e, f, g: c_.reshape(8, WA))
    grads.update({
        "ln1_g": layer_stack(lambda a, b, c_, d, e, f, g: a.reshape(D)),
        "b_forget": layer_stack(lambda a, b, c_, d, e, f, g: b[0, :N_FG]),
        "g_out_fox": go_sum[:, 0],
        "g_out_dil": go_sum[:, 1],
        "conv_w": lax.dynamic_slice_in_dim(cw_sum, chip * cw_cols, cw_cols, axis=2),
        "conv_b": sm_sum[:, 0],
        "cnorm_g": sm_sum[:, 1],
        "cnorm_b": sm_sum[:, 2],
        "ln2_g": layer_stack(lambda a, b, c_, d, e, f, g: f.reshape(D)),
        "ffn_conv_w": lax.dynamic_slice_in_dim(fw_sum[:, :FK], chip * up_cols, up_cols, axis=2),
        "ffn_conv_b": fw_sum[:, FK],
        "g_final": gfin_sum,
    })

    delta, new_m, new_v = {}, {}, {}
    for n in names:
        delta[n], new_m[n], new_v[n] = _adamw(w[n], grads[n], m[n], v[n], "adamw_" + n)
    return (loss, dx.reshape(nb, S, D), *[grads[n] for n in names], *[delta[n] for n in names],
            *[new_m[n] for n in names], *[new_v[n] for n in names])
```

```python
import functools

import jax
import jax.numpy as jnp
from jax import lax
from jax.experimental import pallas as pl
from jax.experimental.pallas import tpu as pltpu

F32 = jnp.float32
BF16 = jnp.bfloat16

D = 1024
S = 2048
DEPTH = 2
HD = 64
WA = 384
NHP = 3
CC = 256
CK = 31
FK = 3
DFF = 2816
NIN = 2822
NP = 3072
SCALE = 0.125
EPS = 1e-6
NEG = -1e30
NCHIP = 4
NDEV = 8
LANE = 128

CB_QA, CB_KA, CB_VA, CB_QB, CB_KB, CB_VB = 0, 3, 6, 9, 12, 15
CB_GV, CB_GG = 9, 10
CB_FA = 22

ADAM_LR, ADAM_B1, ADAM_B2, ADAM_EPS, ADAM_WD, ADAM_STEP = 0.001, 0.9, 0.999, 1e-08, 0.01, 10

VMEM_LIMIT = 56 * 1024 * 1024


def _cparams(sem=None):
    return pltpu.CompilerParams(dimension_semantics=sem, vmem_limit_bytes=VMEM_LIMIT)


def _split3(x):
    hi = x.astype(BF16)
    r1 = x - hi.astype(F32)
    mid = r1.astype(BF16)
    lo = (r1 - mid.astype(F32)).astype(BF16)
    return hi, mid, lo


def _sum8(x):
    r, c = x.shape
    return jnp.sum(x.reshape(r // 8, 8, c), axis=0)


def _sigmoid(z):
    return 0.5 * jnp.tanh(0.5 * z) + 0.5


def _matmul(a, b, *, ta=False, tb=False, out_dtype=F32, add=None, tm, tn, tk, name):
    M = a.shape[1] if ta else a.shape[0]
    K = a.shape[0] if ta else a.shape[1]
    N = b.shape[0] if tb else b.shape[1]
    assert (b.shape[1] if tb else b.shape[0]) == K
    assert M % tm == 0 and N % tn == 0 and K % tk == 0, (M, N, K, tm, tn, tk)
    nk = K // tk
    dn = (((0 if ta else 1,), (1 if tb else 0,)), ((), ()))

    def body(*refs):
        if add is not None:
            a_ref, b_ref, add_ref, o_ref, acc = refs
        else:
            a_ref, b_ref, o_ref, acc = refs
        k = pl.program_id(2)
        prod = lax.dot_general(a_ref[...].astype(BF16), b_ref[...].astype(BF16), dn, preferred_element_type=F32)

        def finish(r):
            if add is not None:
                r = r + add_ref[...]
            o_ref[...] = r.astype(o_ref.dtype)

        if nk == 1:
            finish(prod)
        else:
            @pl.when(k == 0)
            def _():
                acc[...] = prod

            @pl.when(k > 0)
            def _():
                acc[...] += prod

            @pl.when(k == nk - 1)
            def _():
                finish(acc[...])

    a_spec = pl.BlockSpec((tk, tm), lambda i, j, k: (k, i)) if ta else pl.BlockSpec((tm, tk), lambda i, j, k: (i, k))
    b_spec = pl.BlockSpec((tn, tk), lambda i, j, k: (j, k)) if tb else pl.BlockSpec((tk, tn), lambda i, j, k: (k, j))
    o_spec = pl.BlockSpec((tm, tn), lambda i, j, k: (i, j))
    in_specs = [a_spec, b_spec]
    args = [a, b]
    if add is not None:
        in_specs.append(o_spec)
        args.append(add)
    return pl.pallas_call(
        body, name=name, grid=(M // tm, N // tn, nk),
        in_specs=in_specs, out_specs=o_spec,
        out_shape=jax.ShapeDtypeStruct((M, N), out_dtype),
        scratch_shapes=[pltpu.VMEM((tm, tn) if nk > 1 else (8, 128), F32)],
        compiler_params=_cparams(("parallel", "parallel", "arbitrary")),
    )(*args)


def _rms_rows(xv, g_ref):
    r = lax.rsqrt(jnp.mean(xv * xv, axis=1, keepdims=True) + EPS)
    return (xv * r * g_ref[...]).astype(BF16)


def _rms_matmul(x, g, b, *, tm, name):
    T, K = x.shape
    N = b.shape[1]

    def body(x_ref, g_ref, b_ref, h_ref, o_ref):
        h = _rms_rows(x_ref[...], g_ref)
        h_ref[...] = h
        o_ref[...] = jnp.dot(h, b_ref[...], preferred_element_type=F32)

    return pl.pallas_call(
        body, name=name, grid=(T // tm,),
        in_specs=[pl.BlockSpec((tm, K), lambda i: (i, 0)), pl.BlockSpec((1, K), lambda i: (0, 0)),
                  pl.BlockSpec((K, N), lambda i: (0, 0))],
        out_specs=[pl.BlockSpec((tm, K), lambda i: (i, 0)), pl.BlockSpec((tm, N), lambda i: (i, 0))],
        out_shape=[jax.ShapeDtypeStruct((T, K), BF16), jax.ShapeDtypeStruct((T, N), F32)],
        compiler_params=_cparams(("parallel",)),
    )(x, g.reshape(1, K), b)


def _rms_bwd(x, g, dh, dres, name):
    T = x.shape[0]
    tr = 512

    def body(x_ref, g_ref, dh_ref, dres_ref, dx_ref, dxb_ref, dg_ref):
        i = pl.program_id(0)
        xv = x_ref[...]
        dhv = dh_ref[...].astype(F32)
        r = lax.rsqrt(jnp.mean(xv * xv, axis=1, keepdims=True) + EPS)
        a = dhv * g_ref[...]
        dx = dres_ref[...] + r * a - xv * (r * r * r * jnp.mean(xv * a, axis=1, keepdims=True))
        dx_ref[...] = dx
        dxb_ref[...] = dx.astype(BF16)
        part = jnp.sum(dhv * xv * r, axis=0, keepdims=True)

        @pl.when(i == 0)
        def _():
            dg_ref[...] = part

        @pl.when(i > 0)
        def _():
            dg_ref[...] += part

    row = pl.BlockSpec((tr, D), lambda i: (i, 0))
    vec = pl.BlockSpec((1, D), lambda i: (0, 0))
    return pl.pallas_call(
        body, name=name, grid=(T // tr,),
        in_specs=[row, vec, row, row], out_specs=[row, row, vec],
        out_shape=[jax.ShapeDtypeStruct((T, D), F32), jax.ShapeDtypeStruct((T, D), BF16),
                   jax.ShapeDtypeStruct((1, D), F32)],
        compiler_params=_cparams(("arbitrary",)),
    )(x, g.reshape(1, D), dh, dres)


def _loss_head(x, g, target, name):
    T = x.shape[0]
    tr = 512

    def body(x_ref, g_ref, t_ref, loss_ref, dx_ref, dxb_ref, dg_ref):
        i = pl.program_id(0)
        xv = x_ref[...]
        gv = g_ref[...]
        r = lax.rsqrt(jnp.mean(xv * xv, axis=1, keepdims=True) + EPS)
        n = xv * r
        err = n * gv - t_ref[...]
        lpart = 0.5 * jnp.sum(jnp.mean(err * err, axis=1, keepdims=True), axis=0, keepdims=True)
        dy = err * (1.0 / D)
        a = dy * gv
        dx = r * a - xv * (r * r * r * jnp.mean(xv * a, axis=1, keepdims=True))
        dx_ref[...] = dx
        dxb_ref[...] = dx.astype(BF16)
        part = jnp.sum(dy * n, axis=0, keepdims=True)
        lfull = jnp.broadcast_to(lpart, (8, LANE))

        @pl.when(i == 0)
        def _():
            dg_ref[...] = part
            loss_ref[...] = lfull

        @pl.when(i > 0)
        def _():
            dg_ref[...] += part
            loss_ref[...] += lfull

    row = pl.BlockSpec((tr, D), lambda i: (i, 0))
    vec = pl.BlockSpec((1, D), lambda i: (0, 0))
    lsp = pl.BlockSpec((8, LANE), lambda i: (0, 0))
    return pl.pallas_call(
        body, name=name, grid=(T // tr,),
        in_specs=[row, vec, row], out_specs=[lsp, row, row, vec],
        out_shape=[jax.ShapeDtypeStruct((8, LANE), F32), jax.ShapeDtypeStruct((T, D), F32),
                   jax.ShapeDtypeStruct((T, D), BF16), jax.ShapeDtypeStruct((1, D), F32)],
        compiler_params=_cparams(("arbitrary",)),
    )(x, g.reshape(1, D), target)


CUM_BLK = 256


def _tri(n, upper):
    r = lax.broadcasted_iota(jnp.int32, (n, n), 0)
    c = lax.broadcasted_iota(jnp.int32, (n, n), 1)
    return jnp.where((c >= r) if upper else (c <= r), 1.0, 0.0).astype(BF16)


def _tri_apply(tri, x):
    hi, mid, lo = _split3(x)
    out = jnp.dot(tri, hi, preferred_element_type=F32)
    out = out + jnp.dot(tri, mid, preferred_element_type=F32)
    return out + jnp.dot(tri, lo, preferred_element_type=F32)


def _forget_fwd(P, bf_pad, nb, name):
    nblk = S // CUM_BLK

    def body(fa_ref, b_ref, c_ref):
        tri = _tri(CUM_BLK, upper=False)
        carry = jnp.zeros((1, LANE), F32)
        for i in range(nblk):
            z = fa_ref[pl.ds(i * CUM_BLK, CUM_BLK), :] + b_ref[...]
            lf = jnp.minimum(z, 0.0) - jnp.log(1.0 + jnp.exp(-jnp.abs(z)))
            cb = _tri_apply(tri, lf) + carry
            c_ref[pl.ds(i * CUM_BLK, CUM_BLK), :] = cb
            carry = cb[CUM_BLK - 1:CUM_BLK, :]

    return pl.pallas_call(
        body, name=name, grid=(nb,),
        in_specs=[pl.BlockSpec((S, LANE), lambda b: (b, CB_FA)), pl.BlockSpec((1, LANE), lambda b: (0, 0))],
        out_specs=pl.BlockSpec((S, LANE), lambda b: (b, 0)),
        out_shape=jax.ShapeDtypeStruct((nb * S, LANE), F32),
        compiler_params=_cparams(("parallel",)),
    )(P, bf_pad)


def _forget_bwd(P, bf_pad, dcb, nb, name):
    nblk = S // CUM_BLK

    def body(fa_ref, b_ref, dc_ref, dfa_ref, db_ref):
        b = pl.program_id(0)
        tri = _tri(CUM_BLK, upper=True)
        lane = lax.broadcasted_iota(jnp.int32, (CUM_BLK, LANE), 1)
        carry = jnp.zeros((1, LANE), F32)
        dbacc = jnp.zeros((1, LANE), F32)
        for i in reversed(range(nblk)):
            rows = pl.ds(i * CUM_BLK, CUM_BLK)
            dc = jnp.zeros((CUM_BLK, LANE), F32)
            dcv = dc_ref[rows, :]
            for h in range(2 * NHP):
                dc = jnp.where(lane == h, -dcv[:, HD * h:HD * h + 1], dc)
            dl = _tri_apply(tri, dc) + carry
            carry = dl[0:1, :]
            z = fa_ref[rows, :] + b_ref[...]
            dz = jnp.where(lane < 2 * NHP, dl * (1.0 - _sigmoid(z)), 0.0)
            dfa_ref[rows, :] = dz.astype(BF16)
            dbacc = dbacc + jnp.sum(dz, axis=0, keepdims=True)

        dbfull = jnp.broadcast_to(dbacc, (8, LANE))

        @pl.when(b == 0)
        def _():
            db_ref[...] = dbfull

        @pl.when(b > 0)
        def _():
            db_ref[...] += dbfull

    return pl.pallas_call(
        body, name=name, grid=(nb,),
        in_specs=[pl.BlockSpec((S, LANE), lambda b: (b, CB_FA)), pl.BlockSpec((1, LANE), lambda b: (0, 0)),
                  pl.BlockSpec((S, WA), lambda b: (b, 0))],
        out_specs=[pl.BlockSpec((S, LANE), lambda b: (b, 0)), pl.BlockSpec((8, LANE), lambda b: (0, 0))],
        out_shape=[jax.ShapeDtypeStruct((nb * S, LANE), BF16), jax.ShapeDtypeStruct((8, LANE), F32)],
        compiler_params=_cparams(("arbitrary",)),
    )(P, bf_pad, dcb)


FQ = 256
NT_DIMS = (((1,), (1,)), ((), ()))
AUGW = 6


def _head_masks(shape):
    lane = lax.broadcasted_iota(jnp.int32, shape, 1)
    return lane < HD, lane >= HD


def _fox_bias_terms(c_ref, hp):
    lane = lax.broadcasted_iota(jnp.int32, (S, LANE), 1)
    cv = c_ref[...]
    return [_split3(jnp.sum(jnp.where(lane == 2 * hp + e, cv, 0.0), axis=1, keepdims=True)) for e in range(2)]


def _fox_ext(x, terms, side, heads, only):
    lane = lax.broadcasted_iota(jnp.int32, (S, LANE), 1)
    one = jnp.ones((S, 1), BF16)
    aug = jnp.zeros((S, LANE), BF16)
    for e in heads:
        hi, mid, lo = terms[e]
        cols = (hi, mid, lo, one, one, one) if side == "q" else (one, one, one, -hi, -mid, -lo)
        for i, col in enumerate(cols):
            aug = jnp.where(lane == AUGW * e + i, col, aug)
    if only is not None:
        x = jnp.where(_head_masks((S, LANE))[only], x, jnp.zeros_like(x))
    return jnp.concatenate([x, aug], axis=1)


def _halves(x, lane_mask):
    return jnp.where(lane_mask, x[0:FQ, 0:LANE], x[FQ:2 * FQ, 0:LANE])


def _fox_fwd(P, c, nb, name):
    def body(q_ref, k_ref, v_ref, c_ref, o_ref, lse_ref, qm0, qm1, kx, vx):
        hp = pl.program_id(1)
        terms = _fox_bias_terms(c_ref, hp)
        qv = (q_ref[...] * SCALE).astype(BF16)
        qm0[...] = _fox_ext(qv, terms, "q", (0,), 0)
        qm1[...] = _fox_ext(qv, terms, "q", (1,), 1)
        kx[...] = _fox_ext(k_ref[...].astype(BF16), terms, "k", (0, 1), None)
        lane = lax.broadcasted_iota(jnp.int32, (S, LANE), 1)
        vx[...] = jnp.concatenate([v_ref[...].astype(BF16), jnp.where(lane == 0, 1.0, 0.0).astype(BF16)], axis=1)
        tmask = _head_masks((FQ, LANE))[0]
        row = lax.broadcasted_iota(jnp.int32, (2 * FQ, FQ), 0) & (FQ - 1)
        col = lax.broadcasted_iota(jnp.int32, (2 * FQ, FQ), 1)
        for i in range(S // FQ):
            r0 = i * FQ
            qt = jnp.concatenate([qm0[pl.ds(r0, FQ), :], qm1[pl.ds(r0, FQ), :]], axis=0)
            sd = lax.dot_general(qt, kx[pl.ds(r0, FQ), :], NT_DIMS, preferred_element_type=F32)
            sd = jnp.where(col <= row, sd, NEG)
            m = jnp.max(sd, axis=1, keepdims=True)
            if i > 0:
                so = lax.dot_general(qt, kx[pl.ds(0, r0), :], NT_DIMS, preferred_element_type=F32)
                m = jnp.maximum(m, jnp.max(so, axis=1, keepdims=True))
            acc = jnp.dot(jnp.exp(sd - m).astype(BF16), vx[pl.ds(r0, FQ), :], preferred_element_type=F32)
            if i > 0:
                acc = acc + jnp.dot(jnp.exp(so - m).astype(BF16), vx[pl.ds(0, r0), :], preferred_element_type=F32)
            l = acc[:, LANE:LANE + 1]
            o_ref[pl.ds(r0, FQ), :] = _halves(acc / l, tmask)
            lse_ref[pl.ds(r0, FQ), :] = _halves(jnp.broadcast_to(m + jnp.log(l), (2 * FQ, LANE)), tmask)

    def colblk(off):
        return pl.BlockSpec((S, LANE), lambda b, hp: (b, off + hp))

    return pl.pallas_call(
        body, name=name, grid=(nb, NHP),
        in_specs=[colblk(CB_QA), colblk(CB_KA), colblk(CB_VA), pl.BlockSpec((S, LANE), lambda b, hp: (b, 0))],
        out_specs=[colblk(0), colblk(0)],
        out_shape=[jax.ShapeDtypeStruct((nb * S, WA), F32), jax.ShapeDtypeStruct((nb * S, WA), F32)],
        scratch_shapes=[pltpu.VMEM((S, 2 * LANE), BF16)] * 4,
        compiler_params=_cparams(("parallel", "parallel")),
    )(P, P, P, c)


def _fox_bwd(P, c, o, lse, do, nb, name):
    def body(q_ref, k_ref, v_ref, c_ref, o_ref, lse_ref, do_ref, dq_ref, dk_ref, dv_ref, dc_ref,
             km0, km1, qx, vm0, vm1, dob, kt0, kt1, rows, dqt, rsum):
        hp = pl.program_id(1)
        terms = _fox_bias_terms(c_ref, hp)
        kv = k_ref[...].astype(BF16)
        km0[...] = _fox_ext(kv, terms, "k", (0,), 0)
        km1[...] = _fox_ext(kv, terms, "k", (1,), 1)
        qx[...] = _fox_ext((q_ref[...] * SCALE).astype(BF16), terms, "q", (0, 1), None)
        masks = _head_masks((S, LANE))
        vv = v_ref[...].astype(BF16)
        zero = jnp.zeros((S, LANE), BF16)
        vm0[...] = jnp.where(masks[0], vv, zero)
        vm1[...] = jnp.where(masks[1], vv, zero)
        dov = do_ref[...]
        dob[...] = dov.astype(BF16)
        ktf = k_ref[...].T
        prodt = (dov * o_ref[...]).T
        lset = lse_ref[...].T
        hrow = lax.broadcasted_iota(jnp.int32, (LANE, S), 0)
        kt0[...] = jnp.where(hrow < HD, ktf, 0.0).astype(BF16)
        kt1[...] = jnp.where(hrow >= HD, ktf, 0.0).astype(BF16)
        for e in range(2):
            rows[e:e + 1, :] = lset[HD * e:HD * e + 1, :]
            rows[2 + e:3 + e, :] = jnp.sum(prodt[HD * e:HD * (e + 1), :], axis=0, keepdims=True)
        dqt[...] = jnp.zeros_like(dqt)
        rsum[...] = jnp.zeros_like(rsum)
        tmask = _head_masks((FQ, LANE))[0]
        row = lax.broadcasted_iota(jnp.int32, (2 * FQ, FQ), 0) & (FQ - 1)
        col = lax.broadcasted_iota(jnp.int32, (2 * FQ, FQ), 1)
        for j in range(S // FQ):
            k0 = j * FQ
            rest = S - k0 - FQ
            spans = [(k0, FQ)] + ([(k0 + FQ, rest)] if rest > 0 else [])
            kte = jnp.concatenate([km0[pl.ds(k0, FQ), :], km1[pl.ds(k0, FQ), :]], axis=0)
            vte = jnp.concatenate([vm0[pl.ds(k0, FQ), :], vm1[pl.ds(k0, FQ), :]], axis=0)
            ktt = jnp.concatenate([kt0[:, pl.ds(k0, FQ)], kt1[:, pl.ds(k0, FQ)]], axis=1)
            dke = jnp.zeros((2 * FQ, 2 * LANE), F32)
            dve = jnp.zeros((2 * FQ, LANE), F32)
            cse = jnp.zeros((2 * FQ, 1), F32)
            for si, (q0, n) in enumerate(spans):
                qs = qx[pl.ds(q0, n), :]
                dos = dob[pl.ds(q0, n), :]
                st = lax.dot_general(kte, qs, NT_DIMS, preferred_element_type=F32)
                if si == 0:
                    st = jnp.where(col >= row, st, NEG)
                dpt = lax.dot_general(vte, dos, NT_DIMS, preferred_element_type=F32)
                pts, dsts = [], []
                for e in range(2):
                    pe = jnp.exp(st[FQ * e:FQ * (e + 1), :] - rows[e:e + 1, pl.ds(q0, n)])
                    de = pe * (dpt[FQ * e:FQ * (e + 1), :] - rows[2 + e:3 + e, pl.ds(q0, n)])
                    rsum[HD * e:HD * e + 8, pl.ds(q0, n)] += _sum8(de)
                    pts.append(pe)
                    dsts.append(de)
                pt = jnp.concatenate(pts, axis=0)
                dst = jnp.concatenate(dsts, axis=0)
                dsb = dst.astype(BF16)
                dve = dve + jnp.dot(pt.astype(BF16), dos, preferred_element_type=F32)
                dke = dke + jnp.dot(dsb, qs, preferred_element_type=F32)
                dqt[:, pl.ds(q0, n)] += jnp.dot(ktt, dsb, preferred_element_type=F32)
                cse = cse + jnp.sum(dst, axis=1, keepdims=True)
            dk_ref[pl.ds(k0, FQ), :] = _halves(dke, tmask).astype(BF16)
            dv_ref[pl.ds(k0, FQ), :] = _halves(dve, tmask).astype(BF16)
            dc_ref[pl.ds(k0, FQ), :] = _halves(jnp.broadcast_to(cse, (2 * FQ, LANE)), tmask)
        dq_ref[...] = (dqt[...].T * SCALE).astype(BF16)
        tot = [jnp.sum(rsum[HD * e:HD * e + 8, :], axis=0, keepdims=True) for e in range(2)]
        dc_ref[...] = dc_ref[...] - jnp.where(hrow == 0, tot[0], jnp.where(hrow == HD, tot[1], 0.0)).T

    def colblk(off):
        return pl.BlockSpec((S, LANE), lambda b, hp: (b, off + hp))

    wide = pltpu.VMEM((S, 2 * LANE), BF16)
    half = pltpu.VMEM((S, LANE), BF16)
    return pl.pallas_call(
        body, name=name, grid=(nb, NHP),
        in_specs=[colblk(CB_QA), colblk(CB_KA), colblk(CB_VA), pl.BlockSpec((S, LANE), lambda b, hp: (b, 0)),
                  colblk(0), colblk(0), colblk(0)],
        out_specs=[colblk(0)] * 4,
        out_shape=[jax.ShapeDtypeStruct((nb * S, WA), BF16)] * 3 + [jax.ShapeDtypeStruct((nb * S, WA), F32)],
        scratch_shapes=[wide, wide, wide, half, half, half, pltpu.VMEM((LANE, S), BF16), pltpu.VMEM((LANE, S), BF16),
                        pltpu.VMEM((8, S), F32), pltpu.VMEM((LANE, S), F32), pltpu.VMEM((LANE, S), F32)],
        compiler_params=_cparams(("parallel", "parallel")),
    )(P, P, P, c, o, lse, do)


DILS = (1, 4, 16)
DB = 128


def _regroup_load(ref, d, scale=None):
    if d == 1:
        v = ref[...]
    else:
        L = S // d
        v = jnp.concatenate([ref[pl.ds(r, L, stride=d), :] for r in range(d)], axis=0)
    return v if scale is None else v * scale


def _regroup_store(ref, d, val_ref, accumulate):
    L = S // d
    for r in range(d):
        src = val_ref[pl.ds(r * L, L), :]
        dst = (slice(None), slice(None)) if d == 1 else (pl.ds(r, L, stride=d), slice(None))
        if accumulate:
            ref[dst] = ref[dst] + src
        else:
            ref[dst] = src


def _dil_bands():
    qi = lax.broadcasted_iota(jnp.int32, (DB, 2 * DB), 0)
    ki = lax.broadcasted_iota(jnp.int32, (DB, 2 * DB), 1)
    band = (ki >= qi) & (ki <= qi + DB)
    return band, band & (ki >= DB)


def _dil_valid(bands, bk, d):
    band, own = bands
    has_prev = (bk % ((S // d) // DB)) > 0
    return own | (band & has_prev)


def _dil_keys(kd, vd, r0, bands, bk, d):
    if S // d == DB:
        valid = bands[1][:, DB:]
        kk, vv = kd[pl.ds(r0 + DB, DB), :], vd[pl.ds(r0 + DB, DB), :]
    else:
        valid = _dil_valid(bands, bk, d)
        kk, vv = kd[pl.ds(r0, 2 * DB), :], vd[pl.ds(r0, 2 * DB), :]
    return kk, vv, jnp.concatenate([valid, valid], axis=0)


def _stack_heads(x, masks):
    zero = jnp.zeros_like(x)
    return jnp.concatenate([jnp.where(masks[0], x, zero), jnp.where(masks[1], x, zero)], axis=0)


def _dil_fwd(P, nb, name):
    nblk = S // DB

    def body(q_ref, k_ref, v_ref, o_ref, lse_ref, qd, kd, vd, rnum, rm, rl, num_n, m_n, l_n):
        masks = _head_masks((DB, LANE))
        bands = _dil_bands()
        for bi, d in enumerate(DILS):
            qd[...] = _regroup_load(q_ref, d, SCALE).astype(BF16)
            kd[pl.ds(0, DB), :] = jnp.zeros((DB, LANE), BF16)
            vd[pl.ds(0, DB), :] = jnp.zeros((DB, LANE), BF16)
            kd[pl.ds(DB, S), :] = _regroup_load(k_ref, d).astype(BF16)
            vd[pl.ds(DB, S), :] = _regroup_load(v_ref, d).astype(BF16)

            def blk(bk, _, d=d):
                r0 = pl.multiple_of(bk * DB, DB)
                qt = qd[pl.ds(r0, DB), :]
                kk, vv, valid = _dil_keys(kd, vd, r0, bands, bk, d)
                qm = _stack_heads(qt, masks)
                s = lax.dot_general(qm, kk, NT_DIMS, preferred_element_type=F32)
                s = jnp.where(valid, s, NEG)
                m = jnp.max(s, axis=1, keepdims=True)
                p = jnp.exp(s - m)
                l = jnp.sum(p, axis=1, keepdims=True)
                num = jnp.dot(p.astype(BF16), vv, preferred_element_type=F32)
                rnum[pl.ds(r0, DB), :] = jnp.where(masks[0], num[0:DB], num[DB:2 * DB])
                rm[pl.ds(r0, DB), :] = jnp.where(masks[0], m[0:DB], m[DB:2 * DB])
                rl[pl.ds(r0, DB), :] = jnp.where(masks[0], l[0:DB], l[DB:2 * DB])
                return 0

            lax.fori_loop(0, nblk, blk, 0, unroll=8)
            _regroup_store(num_n.at[bi], d, rnum, False)
            _regroup_store(m_n.at[bi], d, rm, False)
            _regroup_store(l_n.at[bi], d, rl, False)

        m_all = jnp.maximum(jnp.maximum(m_n[0], m_n[1]), m_n[2])
        num = jnp.zeros((S, LANE), F32)
        den = jnp.zeros((S, LANE), F32)
        for bi in range(3):
            a = jnp.exp(m_n[bi] - m_all)
            num = num + a * num_n[bi]
            den = den + a * l_n[bi]
        o_ref[...] = num / den
        lse_ref[...] = m_all + jnp.log(den)

    def colblk(off):
        return pl.BlockSpec((S, LANE), lambda b, hp: (b, off + hp))

    return pl.pallas_call(
        body, name=name, grid=(nb, NHP),
        in_specs=[colblk(CB_QB), colblk(CB_KB), colblk(CB_VB)],
        out_specs=[colblk(0), colblk(0)],
        out_shape=[jax.ShapeDtypeStruct((nb * S, WA), F32), jax.ShapeDtypeStruct((nb * S, WA), F32)],
        scratch_shapes=[pltpu.VMEM((S, LANE), BF16), pltpu.VMEM((S + DB, LANE), BF16), pltpu.VMEM((S + DB, LANE), BF16),
                        pltpu.VMEM((S, LANE), F32), pltpu.VMEM((S, LANE), F32), pltpu.VMEM((S, LANE), F32),
                        pltpu.VMEM((3, S, LANE), F32), pltpu.VMEM((3, S, LANE), F32), pltpu.VMEM((3, S, LANE), F32)],
        compiler_params=_cparams(("parallel", "parallel")),
    )(P, P, P)


def _dil_bwd(P, o, lse, do, nb, name):
    nblk = S // DB

    def body(q_ref, k_ref, v_ref, o_ref, lse_ref, do_ref, dq_ref, dk_ref, dv_ref,
             qd, kd, vd, dod, lsed, dsd, dsum, dq_r, dk_r, dv_r, dq_n, dk_n, dv_n):
        masks = _head_masks((DB, LANE))
        fmask = _head_masks((S, LANE))
        prod = do_ref[...] * o_ref[...]
        d0 = jnp.sum(jnp.where(fmask[0], prod, 0.0), axis=1, keepdims=True)
        d1 = jnp.sum(jnp.where(fmask[1], prod, 0.0), axis=1, keepdims=True)
        dsum[...] = jnp.where(fmask[0], d0, d1)
        tn = (((0,), (0,)), ((), ()))
        bands = _dil_bands()
        for bi, d in enumerate(DILS):
            qd[...] = _regroup_load(q_ref, d, SCALE).astype(BF16)
            kd[pl.ds(0, DB), :] = jnp.zeros((DB, LANE), BF16)
            vd[pl.ds(0, DB), :] = jnp.zeros((DB, LANE), BF16)
            kd[pl.ds(DB, S), :] = _regroup_load(k_ref, d).astype(BF16)
            vd[pl.ds(DB, S), :] = _regroup_load(v_ref, d).astype(BF16)
            dod[...] = _regroup_load(do_ref, d).astype(BF16)
            lsed[...] = _regroup_load(lse_ref, d)
            dsd[...] = _regroup_load(dsum, d)
            dq_r[...] = jnp.zeros_like(dq_r)
            dk_r[...] = jnp.zeros_like(dk_r)
            dv_r[...] = jnp.zeros_like(dv_r)

            def blk(bk, _, d=d):
                r0 = pl.multiple_of(bk * DB, DB)
                qt = qd[pl.ds(r0, DB), :]
                dot = dod[pl.ds(r0, DB), :]
                lt = lsed[pl.ds(r0, DB), :]
                dt = dsd[pl.ds(r0, DB), :]
                kk, vv, valid = _dil_keys(kd, vd, r0, bands, bk, d)
                kw = kk.shape[0]
                qm = _stack_heads(qt, masks)
                dom = _stack_heads(dot, masks)
                lcol = jnp.concatenate([lt[:, 0:1], lt[:, HD:HD + 1]], axis=0)
                dcol = jnp.concatenate([dt[:, 0:1], dt[:, HD:HD + 1]], axis=0)
                s = lax.dot_general(qm, kk, NT_DIMS, preferred_element_type=F32)
                s = jnp.where(valid, s, NEG)
                p = jnp.exp(s - lcol)
                dp = lax.dot_general(dom, vv, NT_DIMS, preferred_element_type=F32)
                ds = (p * (dp - dcol)).astype(BF16)
                dvt = lax.dot_general(p.astype(BF16), dom, tn, preferred_element_type=F32)
                dkt = lax.dot_general(ds, qm, tn, preferred_element_type=F32)
                dqt = jnp.dot(ds, kk, preferred_element_type=F32)
                dq_r[pl.ds(r0, DB), :] = jnp.where(masks[0], dqt[0:DB], dqt[DB:2 * DB])
                dk_r[pl.ds(r0 + 2 * DB - kw, kw), :] += dkt
                dv_r[pl.ds(r0 + 2 * DB - kw, kw), :] += dvt
                return 0

            lax.fori_loop(0, nblk, blk, 0, unroll=4)
            _regroup_store(dq_n, d, dq_r, bi > 0)
            _regroup_store(dk_n, d, dk_r.at[pl.ds(DB, S)], bi > 0)
            _regroup_store(dv_n, d, dv_r.at[pl.ds(DB, S)], bi > 0)

        dq_ref[...] = (dq_n[...] * SCALE).astype(BF16)
        dk_ref[...] = dk_n[...].astype(BF16)
        dv_ref[...] = dv_n[...].astype(BF16)

    def colblk(off):
        return pl.BlockSpec((S, LANE), lambda b, hp: (b, off + hp))

    big = pltpu.VMEM((S, LANE), F32)
    bigp = pltpu.VMEM((S + DB, LANE), F32)
    return pl.pallas_call(
        body, name=name, grid=(nb, NHP),
        in_specs=[colblk(CB_QB), colblk(CB_KB), colblk(CB_VB), colblk(0), colblk(0), colblk(0)],
        out_specs=[colblk(0)] * 3,
        out_shape=[jax.ShapeDtypeStruct((nb * S, WA), BF16)] * 3,
        scratch_shapes=[pltpu.VMEM((S, LANE), BF16), pltpu.VMEM((S + DB, LANE), BF16), pltpu.VMEM((S + DB, LANE), BF16),
                        pltpu.VMEM((S, LANE), BF16), big, big, big, big, bigp, bigp, big, big, big],
        compiler_params=_cparams(("parallel", "parallel")),
    )(P, P, P, o, lse, do)


RC = 256
NSHW = 4


def _window(src, start, buf):
    if start % 8 == 0:
        return src[pl.ds(start, RC), :]
    buf[...] = src[pl.ds(start, RC), :]
    return buf[...]
CPAD = 32


NSUB = 8
CROWS = S + CPAD


def _preshift(src, dst):
    for b in range(NSUB):
        dst[b] = src[pl.ds(b, CROWS), :]


def _shifted(dst, start):
    return dst[start % NSUB, pl.ds(start - start % NSUB, RC), :]


def _conv_chunk(gsh, r0, cw_ref, cb_ref):
    acc = jnp.zeros((RC, CC), F32) + cb_ref[...]
    for k in range(CK):
        acc = acc + cw_ref[k:k + 1, :] * _shifted(gsh, r0 + CPAD - (CK - 1) + k)
    return acc


def _cnorm(c0, cng_ref, cnb_ref):
    mu = jnp.mean(c0, axis=1, keepdims=True)
    xc = c0 - mu
    rstd = lax.rsqrt(jnp.mean(xc * xc, axis=1, keepdims=True) + EPS)
    n = xc * rstd
    return n, rstd, n * cng_ref[...] + cnb_ref[...]


NORM_ROWS = 512
YC_BLK = 2 * WA // CC


def _attn_norm_fwd(of, od, gof, god, name):
    T = of.shape[0]

    def body(of_ref, od_ref, gof_ref, god_ref, y_ref):
        for i, (src, g_ref) in enumerate(((of_ref, gof_ref), (od_ref, god_ref))):
            v = src[...]
            r = lax.rsqrt(jnp.mean(v * v, axis=1, keepdims=True) + EPS)
            y_ref[:, i * WA:(i + 1) * WA] = (v * r * g_ref[...]).astype(BF16)

    row = lambda w: pl.BlockSpec((NORM_ROWS, w), lambda i: (i, 0))
    par = pl.BlockSpec((1, WA), lambda i: (0, 0))
    return pl.pallas_call(
        body, name=name, grid=(T // NORM_ROWS,),
        in_specs=[row(WA), row(WA), par, par], out_specs=row(2 * WA),
        out_shape=jax.ShapeDtypeStruct((T, D), BF16),
        compiler_params=_cparams(("parallel",)),
    )(of, od, gof, god)


def _attn_norm_bwd(of, od, dy, gof, god, name):
    T = of.shape[0]

    def body(of_ref, od_ref, dy_ref, gof_ref, god_ref, dof_ref, dod_ref, dgo_ref):
        @pl.when(pl.program_id(0) == 0)
        def _():
            dgo_ref[...] = jnp.zeros_like(dgo_ref)

        for i, (src, g_ref, dst) in enumerate(((of_ref, gof_ref, dof_ref), (od_ref, god_ref, dod_ref))):
            v = src[...]
            dyv = dy_ref[:, i * WA:(i + 1) * WA].astype(F32)
            r = lax.rsqrt(jnp.mean(v * v, axis=1, keepdims=True) + EPS)
            a = dyv * g_ref[...]
            dst[...] = r * a - v * (r * r * r * jnp.mean(v * a, axis=1, keepdims=True))
            dgo_ref[i:i + 1, :] += jnp.sum(dyv * v * r, axis=0, keepdims=True)

    row = lambda w: pl.BlockSpec((NORM_ROWS, w), lambda i: (i, 0))
    par = pl.BlockSpec((1, WA), lambda i: (0, 0))
    return pl.pallas_call(
        body, name=name, grid=(T // NORM_ROWS,),
        in_specs=[row(WA), row(WA), row(2 * WA), par, par],
        out_specs=[row(WA), row(WA), pl.BlockSpec((8, WA), lambda i: (0, 0))],
        out_shape=[jax.ShapeDtypeStruct((T, WA), F32), jax.ShapeDtypeStruct((T, WA), F32),
                   jax.ShapeDtypeStruct((8, WA), F32)],
        compiler_params=_cparams(("arbitrary",)),
    )(of, od, dy, gof, god)


def _conv_specs():
    gblk = lambda off: pl.BlockSpec((S, CC), lambda b: (b, off))
    par = lambda r: pl.BlockSpec((r, CC), lambda b: (0, 0))
    return gblk, par


def _conv_fwd(P, y, cw, cb, cng, cnb, nb, name):
    def body(gv_ref, gg_ref, cw_ref, cb_ref, cng_ref, cnb_ref, y_in, y_ref, c0_ref, gpad, gsh):
        del y_in
        gpad[pl.ds(0, CPAD), :] = jnp.zeros((CPAD, CC), F32)
        gpad[pl.ds(CPAD, S), :] = gv_ref[...] * _sigmoid(gg_ref[...])
        gpad[pl.ds(CROWS, NSUB), :] = jnp.zeros((NSUB, CC), F32)
        _preshift(gpad, gsh)
        for ci in range(S // RC):
            r0 = ci * RC
            c0 = _conv_chunk(gsh, r0, cw_ref, cb_ref)
            c0_ref[pl.ds(r0, RC), :] = c0
            _, _, z = _cnorm(c0, cng_ref, cnb_ref)
            y_ref[pl.ds(r0, RC), :] = (z * _sigmoid(z)).astype(BF16)

    gblk, par = _conv_specs()
    return pl.pallas_call(
        body, name=name, grid=(nb,),
        in_specs=[gblk(CB_GV), gblk(CB_GG), par(CPAD), par(1), par(1), par(1), pl.BlockSpec(memory_space=pl.ANY)],
        out_specs=[gblk(YC_BLK), gblk(0)],
        out_shape=[jax.ShapeDtypeStruct((nb * S, D), BF16), jax.ShapeDtypeStruct((nb * S, CC), F32)],
        input_output_aliases={6: 0},
        scratch_shapes=[pltpu.VMEM((CROWS + NSUB, CC), F32), pltpu.VMEM((NSUB, CROWS, CC), F32)],
        compiler_params=_cparams(("parallel",)),
    )(P, P, cw, cb, cng, cnb, y)


def _conv_bwd(P, c0, dy, cw, cng, cnb, nb, name):
    def body(gv_ref, gg_ref, c0_ref, dy_ref, cw_ref, cng_ref, cnb_ref, dg_ref, dcw_ref, dsm_ref, dpad, dsh):
        @pl.when(pl.program_id(0) == 0)
        def _():
            dcw_ref[...] = jnp.zeros_like(dcw_ref)
            dsm_ref[...] = jnp.zeros_like(dsm_ref)

        dpad[pl.ds(S, CPAD + NSUB), :] = jnp.zeros((CPAD + NSUB, CC), F32)
        zero = jnp.zeros((8, CC), F32)
        dcb, dcng, dcnb = zero, zero, zero
        for ci in range(S // RC):
            r0 = ci * RC
            n, rstd, z = _cnorm(c0_ref[pl.ds(r0, RC), :], cng_ref, cnb_ref)
            sz = _sigmoid(z)
            dz = dy_ref[pl.ds(r0, RC), :].astype(F32) * (sz * (1.0 + z * (1.0 - sz)))
            dcng = dcng + _sum8(dz * n)
            dcnb = dcnb + _sum8(dz)
            dn = dz * cng_ref[...]
            dc0 = rstd * (dn - jnp.mean(dn, axis=1, keepdims=True) - n * jnp.mean(dn * n, axis=1, keepdims=True))
            dcb = dcb + _sum8(dc0)
            dpad[pl.ds(r0, RC), :] = dc0
        dsm_ref[0:1, :] += jnp.sum(dcb, axis=0, keepdims=True)
        dsm_ref[1:2, :] += jnp.sum(dcng, axis=0, keepdims=True)
        dsm_ref[2:3, :] += jnp.sum(dcnb, axis=0, keepdims=True)

        _preshift(dpad, dsh)
        dws = [zero] * CK
        for ci in range(S // RC):
            r0 = ci * RC
            sg = _sigmoid(gg_ref[pl.ds(r0, RC), :])
            gvc = gv_ref[pl.ds(r0, RC), :]
            glu = gvc * sg
            dgl = jnp.zeros((RC, CC), F32)
            for k in range(CK):
                win = _shifted(dsh, r0 + (CK - 1) - k)
                dws[k] = dws[k] + _sum8(win * glu)
                dgl = dgl + cw_ref[k:k + 1, :] * win
            dg_ref[pl.ds(r0, RC), 0:CC] = (dgl * sg).astype(BF16)
            dg_ref[pl.ds(r0, RC), CC:2 * CC] = (dgl * gvc * sg * (1.0 - sg)).astype(BF16)
        for k in range(CK):
            dcw_ref[k:k + 1, :] += jnp.sum(dws[k], axis=0, keepdims=True)

    gblk, par = _conv_specs()
    return pl.pallas_call(
        body, name=name, grid=(nb,),
        in_specs=[gblk(CB_GV), gblk(CB_GG), gblk(0), gblk(YC_BLK), par(CPAD), par(1), par(1)],
        out_specs=[pl.BlockSpec((S, 2 * CC), lambda b: (b, 0)), par(CPAD), par(8)],
        out_shape=[jax.ShapeDtypeStruct((nb * S, 2 * CC), BF16), jax.ShapeDtypeStruct((CPAD, CC), F32),
                   jax.ShapeDtypeStruct((8, CC), F32)],
        scratch_shapes=[pltpu.VMEM((CROWS + NSUB, CC), F32), pltpu.VMEM((NSUB, CROWS, CC), F32)],
        compiler_params=_cparams(("arbitrary",)),
    )(P, P, c0, dy, cw, cng, cnb)


FC = 512
FPAD = 8
NFB = 2 * DFF // FC


def _ffn_u2_chunk(upad, r0, fw_ref, fb_ref):
    acc = jnp.zeros((RC, FC), F32) + fb_ref[...]
    for k in range(FK):
        acc = acc + fw_ref[k:k + 1, :] * upad[pl.ds(r0 + FPAD - (FK - 1) + k, RC), :]
    return acc


def _ffn_fwd(U, fw, fb, nb, name):
    def body(u_ref, fw_ref, fb_ref, h_ref, u2_ref, upad):
        upad[pl.ds(0, FPAD), :] = jnp.zeros((FPAD, FC), F32)
        upad[pl.ds(FPAD, S), :] = u_ref[...].astype(F32)
        for ci in range(S // RC):
            r0 = ci * RC
            u2 = _ffn_u2_chunk(upad, r0, fw_ref, fb_ref)
            u2_ref[pl.ds(r0, RC), :] = u2.astype(BF16)
            a2, b2 = u2[:, :FC // 2], u2[:, FC // 2:]
            h_ref[pl.ds(r0, RC), :] = (a2 * _sigmoid(a2) * b2).astype(BF16)

    return pl.pallas_call(
        body, name=name, grid=(nb, NFB),
        in_specs=[pl.BlockSpec((S, FC), lambda b, j: (b, j)), pl.BlockSpec((8, FC), lambda b, j: (0, j)),
                  pl.BlockSpec((1, FC), lambda b, j: (0, j))],
        out_specs=[pl.BlockSpec((S, FC // 2), lambda b, j: (b, j)), pl.BlockSpec((S, FC), lambda b, j: (b, j))],
        out_shape=[jax.ShapeDtypeStruct((nb * S, DFF), BF16), jax.ShapeDtypeStruct((nb * S, 2 * DFF), BF16)],
        scratch_shapes=[pltpu.VMEM((S + FPAD, FC), F32)],
        compiler_params=_cparams(("parallel", "parallel")),
    )(U, fw, fb)


def _ffn_bwd(U, U2, dhid, fw, nb, name):
    def body(u_ref, u2_ref, dh_ref, fw_ref, du_ref, dfw_ref, dpad, shw):
        @pl.when(pl.program_id(1) == 0)
        def _():
            dfw_ref[...] = jnp.zeros_like(dfw_ref)

        dpad[pl.ds(S, FPAD), :] = jnp.zeros((FPAD, FC), F32)
        zero = jnp.zeros((8, FC), F32)
        dbias = zero
        for ci in range(S // RC):
            r0 = ci * RC
            u2 = u2_ref[pl.ds(r0, RC), :].astype(F32)
            a2, b2 = u2[:, :FC // 2], u2[:, FC // 2:]
            sa = _sigmoid(a2)
            dh = dh_ref[pl.ds(r0, RC), :].astype(F32)
            du2 = jnp.concatenate([dh * b2 * (sa * (1.0 + a2 * (1.0 - sa))), dh * a2 * sa], axis=1)
            dpad[pl.ds(r0, RC), :] = du2
            dbias = dbias + _sum8(du2)
        dws = [zero] * FK
        for ci in range(S // RC):
            r0 = ci * RC
            uc = u_ref[pl.ds(r0, RC), :].astype(F32)
            du = jnp.zeros((RC, FC), F32)
            for k in range(FK):
                win = _window(dpad, r0 + (FK - 1) - k, shw.at[k % NSHW])
                dws[k] = dws[k] + _sum8(win * uc)
                du = du + fw_ref[k:k + 1, :] * win
            du_ref[pl.ds(r0, RC), :] = du.astype(BF16)
        for k in range(FK):
            dfw_ref[k:k + 1, :] += jnp.sum(dws[k], axis=0, keepdims=True)
        dfw_ref[FK:FK + 1, :] += jnp.sum(dbias, axis=0, keepdims=True)

    blk = pl.BlockSpec((S, FC), lambda j, b: (b, j))
    return pl.pallas_call(
        body, name=name, grid=(NFB, nb),
        in_specs=[blk, blk, pl.BlockSpec((S, FC // 2), lambda j, b: (b, j)), pl.BlockSpec((8, FC), lambda j, b: (0, j))],
        out_specs=[blk, pl.BlockSpec((8, FC), lambda j, b: (0, j))],
        out_shape=[jax.ShapeDtypeStruct((nb * S, 2 * DFF), BF16), jax.ShapeDtypeStruct((8, 2 * DFF), F32)],
        scratch_shapes=[pltpu.VMEM((S + FPAD, FC), F32), pltpu.VMEM((NSHW, RC, FC), F32)],
        compiler_params=_cparams(("parallel", "arbitrary")),
    )(U, U2, dhid, fw)


def _matmul_ffn(a, b, mode, *, out_dtype=F32, tm=1024, tk=2048, norm_g=None, name):
    HF = FC // 2
    if mode == "fwd":
        M, K = a.shape
        tm = 512

        def body(a_ref, g_ref, b_ref, h_ref, o_ref):
            av = _rms_rows(a_ref[...], g_ref)
            h_ref[...] = av
            for j in range(NFB):
                for half in range(2):
                    bv = b_ref[:, half * DFF + j * HF:half * DFF + (j + 1) * HF]
                    o_ref[:, j * FC + half * HF:j * FC + (half + 1) * HF] = jnp.dot(
                        av, bv, preferred_element_type=F32).astype(o_ref.dtype)

        return pl.pallas_call(
            body, name=name, grid=(M // tm,),
            in_specs=[pl.BlockSpec((tm, K), lambda i: (i, 0)), pl.BlockSpec((1, K), lambda i: (0, 0)),
                      pl.BlockSpec((K, 2 * DFF), lambda i: (0, 0))],
            out_specs=[pl.BlockSpec((tm, K), lambda i: (i, 0)), pl.BlockSpec((tm, 2 * DFF), lambda i: (i, 0))],
            out_shape=[jax.ShapeDtypeStruct((M, K), BF16), jax.ShapeDtypeStruct((M, 2 * DFF), out_dtype)],
            compiler_params=_cparams(("parallel",)),
        )(a, norm_g.reshape(1, K), b)
    if mode == "dx":
        M = a.shape[0]
        N = b.shape[0]
        tm = 512

        def body(a_ref, b_ref, o_ref):
            acc = None
            for j in range(NFB):
                for half in range(2):
                    av = a_ref[:, j * FC + half * HF:j * FC + (half + 1) * HF]
                    bv = b_ref[:, half * DFF + j * HF:half * DFF + (j + 1) * HF]
                    d = lax.dot_general(av, bv, NT_DIMS, preferred_element_type=F32)
                    acc = d if acc is None else acc + d
            o_ref[...] = acc.astype(o_ref.dtype)

        return pl.pallas_call(
            body, name=name, grid=(M // tm,),
            in_specs=[pl.BlockSpec((tm, 2 * DFF), lambda i: (i, 0)), pl.BlockSpec((N, 2 * DFF), lambda i: (0, 0))],
            out_specs=pl.BlockSpec((tm, N), lambda i: (i, 0)),
            out_shape=jax.ShapeDtypeStruct((M, N), out_dtype),
            compiler_params=_cparams(("parallel",)),
        )(a, b)
    assert mode == "dw"
    T, M = a.shape
    nk = T // tk

    def body(a_ref, g_ref, oa_ref, ob_ref, acc):
        k = pl.program_id(1)
        prod = lax.dot_general(a_ref[...], g_ref[...], (((0,), (0,)), ((), ())), preferred_element_type=F32)

        @pl.when(k == 0)
        def _():
            acc[...] = prod

        @pl.when(k > 0)
        def _():
            acc[...] += prod

        @pl.when(k == nk - 1)
        def _():
            oa_ref[...] = acc[:, :HF]
            ob_ref[...] = acc[:, HF:]

    half = pl.BlockSpec((M, HF), lambda j, k: (0, j))
    return pl.pallas_call(
        body, name=name, grid=(NFB, nk),
        in_specs=[pl.BlockSpec((tk, M), lambda j, k: (k, 0)), pl.BlockSpec((tk, FC), lambda j, k: (k, j))],
        out_specs=[half, half],
        out_shape=[jax.ShapeDtypeStruct((M, DFF), F32)] * 2,
        scratch_shapes=[pltpu.VMEM((M, FC), F32)],
        compiler_params=_cparams(("parallel", "arbitrary")),
    )(a, b)


def _adamw_body(w_ref, g_ref, m_ref, v_ref, d_ref, nm_ref, nv_ref):
    g = g_ref[...]
    m = ADAM_B1 * m_ref[...] + (1.0 - ADAM_B1) * g
    v = ADAM_B2 * v_ref[...] + (1.0 - ADAM_B2) * (g * g)
    m_hat = m / (1.0 - ADAM_B1 ** ADAM_STEP)
    v_hat = v / (1.0 - ADAM_B2 ** ADAM_STEP)
    d_ref[...] = -ADAM_LR * (m_hat / (jnp.sqrt(v_hat) + ADAM_EPS) + ADAM_WD * w_ref[...])
    nm_ref[...] = m
    nv_ref[...] = v


def _adamw(w, g, m, v, name):
    shape = w.shape
    R = 1
    for s in shape[:-1]:
        R *= s
    C = shape[-1]
    args = [a.reshape(R, C) for a in (w, g, m, v)]
    tr = R
    for cand in (512, 352, 256, 128, 64, 32, 16, 8):
        if R % cand == 0 and cand * C * 4 * 14 <= 24 * 1024 * 1024:
            tr = cand
            break
    blk = pl.BlockSpec((tr, C), lambda i: (i, 0))
    outs = pl.pallas_call(
        functools.partial(_adamw_body), name=name, grid=(R // tr,),
        in_specs=[blk] * 4, out_specs=[blk] * 3,
        out_shape=[jax.ShapeDtypeStruct((R, C), F32)] * 3,
        compiler_params=_cparams(("parallel",)),
    )(*args)
    return [o.reshape(shape) for o in outs]


def _rs_row_tile(H):
    th = 128 if H % 128 == 0 else 176
    assert H % th == 0
    return th


def _add_half(g, r1, place, name):
    _, R, C = g.shape
    H = R // 2
    th = _rs_row_tile(H)
    nh = H // th

    def body(s_ref, g_ref, r_ref, o_ref):
        o_ref[...] = (g_ref[...] + r_ref[...]).astype(BF16)

    grid_spec = pltpu.PrefetchScalarGridSpec(
        num_scalar_prefetch=1, grid=(NCHIP, nh),
        in_specs=[pl.BlockSpec((None, th, C), lambda p, i, s: (p, s[1] * nh + i, 0)),
                  pl.BlockSpec((None, th, C), lambda p, i, s: (p, i, 0))],
        out_specs=pl.BlockSpec((None, th, C), lambda p, i, s: (p, i, 0)))
    return pl.pallas_call(
        body, name=name, grid_spec=grid_spec, out_shape=jax.ShapeDtypeStruct((NCHIP, H, C), BF16),
        compiler_params=_cparams(("parallel", "parallel")),
    )(place, g, r1)


def _sum_slots(g, r1, r2, place, name):
    _, R, C = g.shape
    H = R // 2
    th = _rs_row_tile(H)
    nh = H // th

    def body(s_ref, g_ref, r1_ref, r2_ref, o_ref):
        acc = g_ref[...] + r1_ref[...]
        for j in range(NCHIP - 1):
            acc = acc + r2_ref[j].astype(F32)
        o_ref[...] = acc

    grid_spec = pltpu.PrefetchScalarGridSpec(
        num_scalar_prefetch=1, grid=(nh,),
        in_specs=[pl.BlockSpec((None, th, C), lambda i, s: (s[0], s[1] * nh + i, 0)),
                  pl.BlockSpec((None, th, C), lambda i, s: (s[0], i, 0)),
                  pl.BlockSpec((NCHIP - 1, th, C), lambda i, s: (0, i, 0))],
        out_specs=pl.BlockSpec((None, th, C), lambda i, s: (s[1], i, 0)))
    return pl.pallas_call(
        body, name=name, grid_spec=grid_spec, out_shape=jax.ShapeDtypeStruct((2, H, C), F32),
        compiler_params=_cparams(("parallel",)),
    )(place, g, r1, r2)


MESH = pl.DeviceIdType.MESH
HBM = pl.BlockSpec(memory_space=pltpu.HBM)


def _place():
    x, y, c = lax.axis_index("x"), lax.axis_index("y"), lax.axis_index("c")
    chips = [(1 - x, y), (x, 1 - y), (1 - x, 1 - y)]
    return x, y, c, chips


def _rcopy(src, dst, ssem, rsem, dev):
    return pltpu.make_async_remote_copy(src_ref=src, dst_ref=dst, send_sem=ssem, recv_sem=rsem,
                                        device_id=dev, device_id_type=MESH)


AG_CHUNK_BYTES = 1 << 20


def _stage_rows(R, C, dtype):
    rows = R
    while rows * C * jnp.dtype(dtype).itemsize > AG_CHUNK_BYTES and rows % 32 == 0:
        rows //= 2
    return rows


def _allgather(shards, split):
    n = len(shards)
    nout = n * DEPTH
    rows = [_stage_rows(s.shape[1], s.shape[2], s.dtype) for s in shards]

    def body(*refs):
        ins, outs = refs[:n], refs[n:n + nout]
        stages = refs[n + nout:2 * n + nout]
        ssem, rsem, fssem, frsem, isem, osem = refs[2 * n + nout:]
        x, y, c, chips = _place()
        me = 2 * x + y
        sib = (x, y, 1 - c)

        def window(t, l, chip, half):
            if not split[t]:
                return outs[t * DEPTH + l].at[chip]
            H = shards[t].shape[1] // 2
            return outs[t * DEPTH + l].at[chip, pl.ds(half * H, H)]

        sends = []
        for t in range(n):
            H = shards[t].shape[1] // 2
            for l in range(DEPTH):
                src = ins[t].at[l, pl.ds(c * H, H)] if split[t] else ins[t].at[l]
                for j, (cx, cy) in enumerate(chips):
                    k = (t * DEPTH + l) * 3 + j
                    cp = _rcopy(src, window(t, l, me, c), ssem.at[k], rsem.at[k], (cx, cy, c))
                    cp.start()
                    sends.append(cp)
        for t in range(n):
            nch = shards[t].shape[1] // rows[t]
            outc = []
            for l in range(DEPTH):
                for i in range(nch):
                    slot = len(outc) % 2
                    if len(outc) >= 2:
                        outc[-2].wait()
                    rs = pl.ds(i * rows[t], rows[t])
                    cin = pltpu.make_async_copy(ins[t].at[l, rs], stages[t].at[slot], isem.at[t])
                    cin.start()
                    cin.wait()
                    co = pltpu.make_async_copy(stages[t].at[slot], outs[t * DEPTH + l].at[me, rs], osem.at[2 * t + slot])
                    co.start()
                    outc.append(co)
            for co in outc[-2:]:
                co.wait()
        for t in range(n):
            for l in range(DEPTH):
                for j, (cx, cy) in enumerate(chips):
                    k = (t * DEPTH + l) * 3 + j
                    win = window(t, l, 2 * cx + cy, c)
                    _rcopy(win, win, ssem.at[k], rsem.at[k], (cx, cy, c)).wait_recv()
                    if split[t]:
                        cp = _rcopy(win, win, fssem.at[k], frsem.at[k], sib)
                        cp.start()
                        sends.append(cp)
        for t in range(n):
            if split[t]:
                for l in range(DEPTH):
                    for j, (cx, cy) in enumerate(chips):
                        k = (t * DEPTH + l) * 3 + j
                        win = window(t, l, 2 * cx + cy, 1 - c)
                        _rcopy(win, win, fssem.at[k], frsem.at[k], sib).wait_recv()
        for cp in sends:
            cp.wait_send()

    out_shape = [jax.ShapeDtypeStruct((NCHIP,) + s.shape[1:], s.dtype) for s in shards for _ in range(DEPTH)]
    outs = pl.pallas_call(
        body, name="allgather_weights", in_specs=[HBM] * n, out_specs=[HBM] * nout, out_shape=out_shape,
        scratch_shapes=[pltpu.VMEM((2, r, s.shape[2]), s.dtype) for r, s in zip(rows, shards)]
        + [pltpu.SemaphoreType.DMA((3 * nout,))] * 4 + [pltpu.SemaphoreType.DMA((n,)), pltpu.SemaphoreType.DMA((2 * n,))],
        compiler_params=pltpu.CompilerParams(vmem_limit_bytes=VMEM_LIMIT),
    )(*shards)
    return [outs[t * DEPTH:(t + 1) * DEPTH] for t in range(n)]


def _rs_pair_exchange(gs):
    n = len(gs)

    def body(*refs):
        ins, outs = refs[:n], refs[n:2 * n]
        ssem, rsem = refs[2 * n:]
        x, y, c, _ = _place()
        cps = []
        for t in range(n):
            H = gs[t].shape[1] // 2
            cp = _rcopy(ins[t].at[:, pl.ds((1 - c) * H, H)], outs[t], ssem.at[t], rsem.at[t], (x, y, 1 - c))
            cp.start()
            cps.append(cp)
        for cp in cps:
            cp.wait_recv()
        for cp in cps:
            cp.wait_send()

    out_shape = [jax.ShapeDtypeStruct((NCHIP, g.shape[1] // 2, g.shape[2]), F32) for g in gs]
    return pl.pallas_call(
        body, name="rs_pair_exchange", in_specs=[HBM] * n, out_specs=[HBM] * n, out_shape=out_shape,
        scratch_shapes=[pltpu.SemaphoreType.DMA((n,))] * 2,
    )(*gs)


def _rs_chip_scatter(hs):
    n = len(hs)

    def body(*refs):
        ins, outs = refs[:n], refs[n:2 * n]
        ssem, rsem = refs[2 * n:]
        x, y, c, chips = _place()
        sends = []
        for t in range(n):
            for j, (cx, cy) in enumerate(chips):
                cp = _rcopy(ins[t].at[2 * cx + cy], outs[t].at[j], ssem.at[3 * t + j], rsem.at[3 * t + j], (cx, cy, c))
                cp.start()
                sends.append(cp)
        for cp in sends:
            cp.wait_recv()
        for cp in sends:
            cp.wait_send()

    out_shape = [jax.ShapeDtypeStruct((NCHIP - 1,) + h.shape[1:], h.dtype) for h in hs]
    return pl.pallas_call(
        body, name="rs_chip_scatter", in_specs=[HBM] * n, out_specs=[HBM] * n, out_shape=out_shape,
        scratch_shapes=[pltpu.SemaphoreType.DMA((3 * n,))] * 2,
    )(*hs)


def _rs_pair_gather(fs):
    n = len(fs)

    def body(*refs):
        bufs = refs[n:2 * n]
        ssem, rsem = refs[2 * n:]
        x, y, c, _ = _place()
        sends = []
        for t in range(n):
            cp = _rcopy(bufs[t].at[c], bufs[t].at[c], ssem.at[t], rsem.at[t], (x, y, 1 - c))
            cp.start()
            sends.append(cp)
        for t in range(n):
            win = bufs[t].at[1 - c]
            _rcopy(win, win, ssem.at[t], rsem.at[t], (x, y, 1 - c)).wait_recv()
        for cp in sends:
            cp.wait_send()

    out_shape = [jax.ShapeDtypeStruct(f.shape, F32) for f in fs]
    return pl.pallas_call(
        body, name="rs_pair_gather", in_specs=[HBM] * n, out_specs=[HBM] * n, out_shape=out_shape,
        input_output_aliases={t: t for t in range(n)},
        scratch_shapes=[pltpu.SemaphoreType.DMA((n,))] * 2,
    )(*fs)


def _allreduce_small(buf):
    R = buf.shape[0]

    def body(in_ref, out_ref, slots, ssem, rsem):
        x, y, c, _ = _place()
        me = 4 * x + 2 * y + c
        slots[me] = in_ref[...]
        cps = []
        for k in range(1, NDEV):
            px = 1 - x if k & 4 else x
            py = 1 - y if k & 2 else y
            pc = 1 - c if k & 1 else c
            cp = _rcopy(in_ref, slots.at[me], ssem.at[k - 1], rsem.at[k - 1], (px, py, pc))
            cp.start()
            cps.append((cp, 4 * px + 2 * py + pc))
        for k, (cp, peer) in enumerate(cps):
            _rcopy(in_ref, slots.at[peer], ssem.at[k], rsem.at[k], (x, y, c)).wait_recv()
        for cp, _ in cps:
            cp.wait_send()
        acc = slots[0]
        for p in range(1, NDEV):
            acc = acc + slots[p]
        out_ref[...] = acc

    return pl.pallas_call(
        body, name="allreduce_small", out_shape=jax.ShapeDtypeStruct((R, LANE), F32),
        in_specs=[pl.BlockSpec(memory_space=pltpu.VMEM)], out_specs=pl.BlockSpec(memory_space=pltpu.VMEM),
        scratch_shapes=[pltpu.VMEM((NDEV, R, LANE), F32), pltpu.SemaphoreType.DMA((NDEV - 1,)),
                        pltpu.SemaphoreType.DMA((NDEV - 1,))],
        compiler_params=pltpu.CompilerParams(vmem_limit_bytes=VMEM_LIMIT),
    )(buf)


def _interleave(a):
    lead = a.shape[:-1]
    return a.reshape(*lead, 2, NFB, FC // 2).swapaxes(-3, -2).reshape(*lead, 2 * DFF)


def _uninterleave(a):
    lead = a.shape[:-1]
    return a.reshape(*lead, NFB, 2, FC // 2).swapaxes(-3, -2).reshape(*lead, 2 * DFF)


N_QKV = 3 * WA
N_FG = 2 * NHP


def _pack_in_cols(w):
    pad = jnp.zeros(w.shape[:-1] + (NP - NIN,), w.dtype)
    return jnp.concatenate([w[..., :N_QKV], w[..., N_QKV + N_FG:], w[..., N_QKV:N_QKV + N_FG], pad], axis=-1)


def _unpack_in_cols(g):
    return jnp.concatenate([g[..., :N_QKV], g[..., NIN - N_FG:NIN], g[..., N_QKV:NIN - N_FG]], axis=-1)


def _train_compute(xt, tgt, W, nb):
    saved = []
    xc = xt
    for l in range(DEPTH):
        t = f"_l{l}"
        h, P = _rms_matmul(xc, W["ln1"][l], W["in"][l], tm=512, name="proj_in" + t)
        c = _forget_fwd(P, W["bf"][l], nb, "forget_fwd" + t)
        of, lsef = _fox_fwd(P, c, nb, "fox_fwd" + t)
        od, lsed = _dil_fwd(P, nb, "dil_fwd" + t)
        convp = (W["cw"][l], W["cb"][l], W["cng"][l], W["cnb"][l])
        y = _attn_norm_fwd(of, od, W["gof"][l], W["god"][l], "attn_norm_fwd" + t)
        y, c0 = _conv_fwd(P, y, *convp, nb, "conv_fwd" + t)
        xm = _matmul(y, W["o"][l], add=xc, tm=1024, tn=1024, tk=D, name="proj_out" + t)
        h2, U = _matmul_ffn(xm, W["up"][l], "fwd", out_dtype=BF16, norm_g=W["ln2"][l], name="ffn_up" + t)
        hid, U2 = _ffn_fwd(U, W["fw"][l], W["fb"][l], nb, "ffn_act_fwd" + t)
        xo = _matmul(hid, W["down"][l], add=xm, tm=512, tn=D, tk=DFF, name="ffn_down" + t)
        saved.append((xc, h, P, c, of, lsef, od, lsed, convp, c0, y, xm, h2, U, U2, hid))
        xc = xo

    loss8, dx, dxb, dgfin = _loss_head(xc, W["gfin"], tgt, "loss_head")

    big = [None] * DEPTH
    small = [None] * DEPTH
    for l in reversed(range(DEPTH)):
        t = f"_l{l}"
        xin, h, P, c, of, lsef, od, lsed, convp, c0, y, xm, h2, U, U2, hid = saved[l]
        dhid = _matmul(dxb, W["down"][l], tb=True, out_dtype=BF16, tm=1024, tn=DFF, tk=D, name="ffn_down_dx" + t)
        dWd = _matmul(hid, dxb, ta=True, tm=DFF // 2, tn=D, tk=2048, name="ffn_down_dw" + t)
        dU, dfw = _ffn_bwd(U, U2, dhid, W["fw"][l], nb, "ffn_act_bwd" + t)
        dh2 = _matmul_ffn(dU, W["up"][l], "dx", name="ffn_up_dx" + t)
        dWup = _matmul_ffn(h2, dU, "dw", name="ffn_up_dw" + t)
        dxm, dxmb, dln2 = _rms_bwd(xm, W["ln2"][l], dh2, dx, "rms2_bwd" + t)
        dy = _matmul(dxmb, W["o"][l], tb=True, out_dtype=BF16, tm=1024, tn=D, tk=D, name="proj_out_dx" + t)
        dWo = _matmul(y, dxmb, ta=True, tm=D, tn=D, tk=2048, name="proj_out_dw" + t)
        dof, dod, dgo = _attn_norm_bwd(of, od, dy, W["gof"][l], W["god"][l], "attn_norm_bwd" + t)
        dgvgg, dcw, dsm = _conv_bwd(P, c0, dy, convp[0], convp[2], convp[3], nb, "conv_bwd" + t)
        dqa, dka, dva, dcb = _fox_bwd(P, c, of, lsef, dof, nb, "fox_bwd" + t)
        dfa, dbf = _forget_bwd(P, W["bf"][l], dcb, nb, "forget_bwd" + t)
        dqb, dkb, dvb = _dil_bwd(P, od, lsed, dod, nb, "dil_bwd" + t)
        dP = jnp.concatenate([dqa, dka, dva, dqb, dkb, dvb, dgvgg, dfa, jnp.zeros_like(dfa)], axis=1)
        dh = _matmul(dP, W["in"][l], tb=True, tm=1024, tn=D, tk=NP, name="proj_in_dx" + t)
        dWin = _matmul(h, dP, ta=True, tm=D, tn=1024, tk=2048, name="proj_in_dw" + t)
        dx, dxb, dln1 = _rms_bwd(xin, W["ln1"][l], dh, dxm, "rms1_bwd" + t)
        big[l] = (dWin, dWo, dWup, dWd)
        small[l] = (dln1, dbf, dgo, dcw, dsm, dln2, dfw)
    return loss8, dx, big, small, dgfin


_SMALL_ROWS = (D // LANE, 8, 8 * WA // LANE, CPAD * CC // LANE, 8 * CC // LANE, D // LANE, 8 * 2 * DFF // LANE)


def kernel(x, ln1_g, w_in, b_forget, g_out_fox, g_out_dil, conv_w, conv_b, cnorm_g, cnorm_b, w_o, ln2_g, w_up, ffn_conv_w, ffn_conv_b, w_down, g_final, loss_target, m_ln1_g, m_w_in, m_b_forget, m_g_out_fox, m_g_out_dil, m_conv_w, m_conv_b, m_cnorm_g, m_cnorm_b, m_w_o, m_ln2_g, m_w_up, m_ffn_conv_w, m_ffn_conv_b, m_w_down, m_g_final, v_ln1_g, v_w_in, v_b_forget, v_g_out_fox, v_g_out_dil, v_conv_w, v_conv_b, v_cnorm_g, v_cnorm_b, v_w_o, v_ln2_g, v_w_up, v_ffn_conv_w, v_ffn_conv_b, v_w_down, v_g_final):
    names = ("ln1_g", "w_in", "b_forget", "g_out_fox", "g_out_dil", "conv_w", "conv_b", "cnorm_g", "cnorm_b",
             "w_o", "ln2_g", "w_up", "ffn_conv_w", "ffn_conv_b", "w_down", "g_final")
    w = dict(zip(names, (ln1_g, w_in, b_forget, g_out_fox, g_out_dil, conv_w, conv_b, cnorm_g, cnorm_b,
                         w_o, ln2_g, w_up, ffn_conv_w, ffn_conv_b, w_down, g_final)))
    m = dict(zip(names, (m_ln1_g, m_w_in, m_b_forget, m_g_out_fox, m_g_out_dil, m_conv_w, m_conv_b, m_cnorm_g,
                         m_cnorm_b, m_w_o, m_ln2_g, m_w_up, m_ffn_conv_w, m_ffn_conv_b, m_w_down, m_g_final)))
    v = dict(zip(names, (v_ln1_g, v_w_in, v_b_forget, v_g_out_fox, v_g_out_dil, v_conv_w, v_conv_b, v_cnorm_g,
                         v_cnorm_b, v_w_o, v_ln2_g, v_w_up, v_ffn_conv_w, v_ffn_conv_b, v_w_down, v_g_final)))
    nb = x.shape[0]
    T = nb * S
    xi, yi, ci = lax.axis_index("x"), lax.axis_index("y"), lax.axis_index("c")
    chip = 2 * xi + yi
    cw_cols = CC // NCHIP
    up_cols = 2 * DFF // NCHIP

    shards = [_pack_in_cols(w_in).astype(BF16), w_o.astype(BF16), w_up.astype(BF16), w_down.astype(BF16),
              jnp.pad(ffn_conv_w, ((0, 0), (0, 8 - FK), (0, 0))),
              jnp.pad(conv_w, ((0, 0), (0, CPAD - CK), (0, LANE - cw_cols)))]
    g_in, g_o, g_up, g_dn, g_fw, g_cw = _allgather(shards, (True, True, True, True, False, False))
    fb_full = _interleave(ffn_conv_b)
    W = {
        "in": [g.reshape(D, NP) for g in g_in],
        "o": [g.reshape(D, D) for g in g_o],
        "up": [g.transpose(1, 0, 2).reshape(D, 2 * DFF) for g in g_up],
        "down": [g.reshape(DFF, D) for g in g_dn],
        "ln1": [ln1_g[l] for l in range(DEPTH)],
        "ln2": [ln2_g[l] for l in range(DEPTH)],
        "bf": [jnp.pad(b_forget[l], (0, LANE - N_FG)).reshape(1, LANE) for l in range(DEPTH)],
        "gof": [g_out_fox[l].reshape(1, WA) for l in range(DEPTH)],
        "god": [g_out_dil[l].reshape(1, WA) for l in range(DEPTH)],
        "cw": [g[..., :cw_cols].transpose(1, 0, 2).reshape(CPAD, CC) for g in g_cw],
        "cb": [conv_b[l].reshape(1, CC) for l in range(DEPTH)],
        "cng": [cnorm_g[l].reshape(1, CC) for l in range(DEPTH)],
        "cnb": [cnorm_b[l].reshape(1, CC) for l in range(DEPTH)],
        "fw": [_interleave(g.transpose(1, 0, 2).reshape(8, 2 * DFF)) for g in g_fw],
        "fb": [fb_full[l].reshape(1, 2 * DFF) for l in range(DEPTH)],
        "gfin": g_final,
    }

    loss8, dx, big, small, dgfin = _train_compute(x.reshape(T, D), loss_target.reshape(T, D), W, nb)

    gs = []
    for l in range(DEPTH):
        dWin, dWo, dWup, dWd = big[l]
        gs += [dWin.reshape(NCHIP, D // NCHIP, NP), dWo.reshape(NCHIP, D // NCHIP, D),
               jnp.stack([half[:, i * up_cols:(i + 1) * up_cols] for half in dWup for i in range(2)]),
               dWd.reshape(NCHIP, DFF // NCHIP, D)]
    r1 = _rs_pair_exchange(gs)
    place = jnp.stack([chip, ci]).astype(jnp.int32)
    hs = [_add_half(g, r, place, f"rs_add_pair_{i}") for i, (g, r) in enumerate(zip(gs, r1))]
    r2 = _rs_chip_scatter(hs)
    fs = [_sum_slots(g, a, b, place, f"rs_add_chips_{i}") for i, (g, a, b) in enumerate(zip(gs, r1, r2))]
    red = _rs_pair_gather(fs)
    red = [r.reshape(r.shape[0] * r.shape[1], r.shape[2]) for r in red]
    grads = {
        "w_in": jnp.stack([_unpack_in_cols(red[4 * l]) for l in range(DEPTH)]),
        "w_o": jnp.stack([red[4 * l + 1] for l in range(DEPTH)]),
        "w_up": jnp.stack([red[4 * l + 2] for l in range(DEPTH)]),
        "w_down": jnp.stack([red[4 * l + 3] for l in range(DEPTH)]),
    }

    parts = []
    for l in range(DEPTH):
        parts += [p.reshape(-1, LANE) for p in small[l]]
    parts += [dgfin.reshape(-1, LANE), loss8]
    tot = _allreduce_small(jnp.concatenate(parts, axis=0))
    off = 0
    per_layer = []
    for l in range(DEPTH):
        vals = []
        for rows in _SMALL_ROWS:
            vals.append(tot[off:off + rows])
            off += rows
        per_layer.append(vals)
    gfin_sum = tot[off:off + D // LANE].reshape(D)
    loss = tot[off + D // LANE, 0]

    def layer_stack(fn):
        return jnp.stack([fn(*per_layer[l]) for l in range(DEPTH)])

    fw_sum = layer_stack(lambda a, b, c_, d, e, f, g: _uninterleave(g.reshape(8, 2 * DFF)))
    cw_sum = layer_stack(lambda a, b, c_, d, e, f, g: d.reshape(CPAD, CC)[:CK])
    sm_sum = layer_stack(lambda a, b, c_, d, e, f, g: e.reshape(8, CC))
    go_sum = layer_stack(lambda a, b, c_, d, e, f, g: c_.reshape(8, WA))
    grads.update({
        "ln1_g": layer_stack(lambda a, b, c_, d, e, f, g: a.reshape(D)),
        "b_forget": layer_stack(lambda a, b, c_, d, e, f, g: b[0, :N_FG]),
        "g_out_fox": go_sum[:, 0],
        "g_out_dil": go_sum[:, 1],
        "conv_w": lax.dynamic_slice_in_dim(cw_sum, chip * cw_cols, cw_cols, axis=2),
        "conv_b": sm_sum[:, 0],
        "cnorm_g": sm_sum[:, 1],
        "cnorm_b": sm_sum[:, 2],
        "ln2_g": layer_stack(lambda a, b, c_, d, e, f, g: f.reshape(D)),
        "ffn_conv_w": lax.dynamic_slice_in_dim(fw_sum[:, :FK], chip * up_cols, up_cols, axis=2),
        "ffn_conv_b": fw_sum[:, FK],
        "g_final": gfin_sum,
    })

    delta, new_m, new_v = {}, {}, {}
    for n in names:
        delta[n], new_m[n], new_v[n] = _adamw(w[n], grads[n], m[n], v[n], "adamw_" + n)
    return (loss, dx.reshape(nb, S, D), *[grads[n] for n in names], *[delta[n] for n in names],
            *[new_m[n] for n in names], *[new_v[n] for n in names])
```

```python
import functools

import jax
import jax.numpy as jnp
from jax import lax
from jax.experimental import pallas as pl
from jax.experimental.pallas import tpu as pltpu

F32 = jnp.float32
BF16 = jnp.bfloat16

D = 1024
S = 2048
DEPTH = 2
HD = 64
WA = 384
NHP = 3
CC = 256
CK = 31
FK = 3
DFF = 2816
NIN = 2822
NP = 3072
SCALE = 0.125
EPS = 1e-6
NEG = -1e30
NCHIP = 4
NDEV = 8
LANE = 128

CB_QA, CB_KA, CB_VA, CB_QB, CB_KB, CB_VB = 0, 3, 6, 9, 12, 15
CB_GV, CB_GG = 9, 10
CB_FA = 22

ADAM_LR, ADAM_B1, ADAM_B2, ADAM_EPS, ADAM_WD, ADAM_STEP = 0.001, 0.9, 0.999, 1e-08, 0.01, 10

N_QKV = 3 * WA
N_FG = 2 * NHP

VMEM_LIMIT = 56 * 1024 * 1024


def _cparams(sem=None):
    return pltpu.CompilerParams(dimension_semantics=sem, vmem_limit_bytes=VMEM_LIMIT)


def _split3(x):
    hi = x.astype(BF16)
    r1 = x - hi.astype(F32)
    mid = r1.astype(BF16)
    lo = (r1 - mid.astype(F32)).astype(BF16)
    return hi, mid, lo


def _sum8(x):
    r, c = x.shape
    return jnp.sum(x.reshape(r // 8, 8, c), axis=0)


def _sigmoid(z):
    return 0.5 * jnp.tanh(0.5 * z) + 0.5


def _matmul(a, b, *, ta=False, tb=False, out_dtype=F32, add=None, tm, tn, tk, name):
    M = a.shape[1] if ta else a.shape[0]
    K = a.shape[0] if ta else a.shape[1]
    N = b.shape[0] if tb else b.shape[1]
    assert (b.shape[1] if tb else b.shape[0]) == K
    assert M % tm == 0 and N % tn == 0 and K % tk == 0, (M, N, K, tm, tn, tk)
    nk = K // tk
    dn = (((0 if ta else 1,), (1 if tb else 0,)), ((), ()))

    def body(*refs):
        if add is not None:
            a_ref, b_ref, add_ref, o_ref, acc = refs
        else:
            a_ref, b_ref, o_ref, acc = refs
        k = pl.program_id(2)
        prod = lax.dot_general(a_ref[...].astype(BF16), b_ref[...].astype(BF16), dn, preferred_element_type=F32)

        def finish(r):
            if add is not None:
                r = r + add_ref[...]
            o_ref[...] = r.astype(o_ref.dtype)

        if nk == 1:
            finish(prod)
        else:
            @pl.when(k == 0)
            def _():
                acc[...] = prod

            @pl.when(k > 0)
            def _():
                acc[...] += prod

            @pl.when(k == nk - 1)
            def _():
                finish(acc[...])

    a_spec = pl.BlockSpec((tk, tm), lambda i, j, k: (k, i)) if ta else pl.BlockSpec((tm, tk), lambda i, j, k: (i, k))
    b_spec = pl.BlockSpec((tn, tk), lambda i, j, k: (j, k)) if tb else pl.BlockSpec((tk, tn), lambda i, j, k: (k, j))
    o_spec = pl.BlockSpec((tm, tn), lambda i, j, k: (i, j))
    in_specs = [a_spec, b_spec]
    args = [a, b]
    if add is not None:
        in_specs.append(o_spec)
        args.append(add)
    return pl.pallas_call(
        body, name=name, grid=(M // tm, N // tn, nk),
        in_specs=in_specs, out_specs=o_spec,
        out_shape=jax.ShapeDtypeStruct((M, N), out_dtype),
        scratch_shapes=[pltpu.VMEM((tm, tn) if nk > 1 else (8, 128), F32)],
        compiler_params=_cparams(("parallel", "parallel", "arbitrary")),
    )(*args)


def _rms_rows(xv, g_ref):
    r = lax.rsqrt(jnp.mean(xv * xv, axis=1, keepdims=True) + EPS)
    return (xv * r * g_ref[...]).astype(BF16)


def _rms_matmul(x, g, b, *, tm, name):
    T, K = x.shape
    N = b.shape[1]

    def body(x_ref, g_ref, b_ref, h_ref, o_ref):
        h = _rms_rows(x_ref[...], g_ref)
        h_ref[...] = h
        o_ref[...] = jnp.dot(h, b_ref[...], preferred_element_type=F32)

    return pl.pallas_call(
        body, name=name, grid=(T // tm,),
        in_specs=[pl.BlockSpec((tm, K), lambda i: (i, 0)), pl.BlockSpec((1, K), lambda i: (0, 0)),
                  pl.BlockSpec((K, N), lambda i: (0, 0))],
        out_specs=[pl.BlockSpec((tm, K), lambda i: (i, 0)), pl.BlockSpec((tm, N), lambda i: (i, 0))],
        out_shape=[jax.ShapeDtypeStruct((T, K), BF16), jax.ShapeDtypeStruct((T, N), F32)],
        compiler_params=_cparams(("parallel",)),
    )(x, g.reshape(1, K), b)


def _rms_bwd(x, g, dh, dres, name):
    T = x.shape[0]
    tr = 512

    def body(x_ref, g_ref, dh_ref, dres_ref, dx_ref, dxb_ref, dg_ref):
        i = pl.program_id(0)
        xv = x_ref[...]
        dhv = dh_ref[...].astype(F32)
        r = lax.rsqrt(jnp.mean(xv * xv, axis=1, keepdims=True) + EPS)
        a = dhv * g_ref[...]
        dx = dres_ref[...] + r * a - xv * (r * r * r * jnp.mean(xv * a, axis=1, keepdims=True))
        dx_ref[...] = dx
        dxb_ref[...] = dx.astype(BF16)
        part = jnp.sum(dhv * xv * r, axis=0, keepdims=True)

        @pl.when(i == 0)
        def _():
            dg_ref[...] = part

        @pl.when(i > 0)
        def _():
            dg_ref[...] += part

    row = pl.BlockSpec((tr, D), lambda i: (i, 0))
    vec = pl.BlockSpec((1, D), lambda i: (0, 0))
    return pl.pallas_call(
        body, name=name, grid=(T // tr,),
        in_specs=[row, vec, row, row], out_specs=[row, row, vec],
        out_shape=[jax.ShapeDtypeStruct((T, D), F32), jax.ShapeDtypeStruct((T, D), BF16),
                   jax.ShapeDtypeStruct((1, D), F32)],
        compiler_params=_cparams(("arbitrary",)),
    )(x, g.reshape(1, D), dh, dres)


def _loss_head(x, g, target, name):
    T = x.shape[0]
    tr = 512

    def body(x_ref, g_ref, t_ref, loss_ref, dx_ref, dxb_ref, dg_ref):
        i = pl.program_id(0)
        xv = x_ref[...]
        gv = g_ref[...]
        r = lax.rsqrt(jnp.mean(xv * xv, axis=1, keepdims=True) + EPS)
        n = xv * r
        err = n * gv - t_ref[...]
        lpart = 0.5 * jnp.sum(jnp.mean(err * err, axis=1, keepdims=True), axis=0, keepdims=True)
        dy = err * (1.0 / D)
        a = dy * gv
        dx = r * a - xv * (r * r * r * jnp.mean(xv * a, axis=1, keepdims=True))
        dx_ref[...] = dx
        dxb_ref[...] = dx.astype(BF16)
        part = jnp.sum(dy * n, axis=0, keepdims=True)
        lfull = jnp.broadcast_to(lpart, (8, LANE))

        @pl.when(i == 0)
        def _():
            dg_ref[...] = part
            loss_ref[...] = lfull

        @pl.when(i > 0)
        def _():
            dg_ref[...] += part
            loss_ref[...] += lfull

    row = pl.BlockSpec((tr, D), lambda i: (i, 0))
    vec = pl.BlockSpec((1, D), lambda i: (0, 0))
    lsp = pl.BlockSpec((8, LANE), lambda i: (0, 0))
    return pl.pallas_call(
        body, name=name, grid=(T // tr,),
        in_specs=[row, vec, row], out_specs=[lsp, row, row, vec],
        out_shape=[jax.ShapeDtypeStruct((8, LANE), F32), jax.ShapeDtypeStruct((T, D), F32),
                   jax.ShapeDtypeStruct((T, D), BF16), jax.ShapeDtypeStruct((1, D), F32)],
        compiler_params=_cparams(("arbitrary",)),
    )(x, g.reshape(1, D), target)


CUM_BLK = 256


def _tri(n, upper):
    r = lax.broadcasted_iota(jnp.int32, (n, n), 0)
    c = lax.broadcasted_iota(jnp.int32, (n, n), 1)
    return jnp.where((c >= r) if upper else (c <= r), 1.0, 0.0).astype(BF16)


def _tri_apply(tri, x):
    hi, mid, lo = _split3(x)
    out = jnp.dot(tri, hi, preferred_element_type=F32)
    out = out + jnp.dot(tri, mid, preferred_element_type=F32)
    return out + jnp.dot(tri, lo, preferred_element_type=F32)


def _forget_fwd(P, bf_pad, nb, name):
    nblk = S // CUM_BLK

    def body(fa_ref, b_ref, c_ref):
        tri = _tri(CUM_BLK, upper=False)
        carry = jnp.zeros((1, LANE), F32)
        for i in range(nblk):
            z = fa_ref[pl.ds(i * CUM_BLK, CUM_BLK), :] + b_ref[...]
            lf = jnp.minimum(z, 0.0) - jnp.log(1.0 + jnp.exp(-jnp.abs(z)))
            cb = _tri_apply(tri, lf) + carry
            c_ref[pl.ds(i * CUM_BLK, CUM_BLK), :] = cb
            carry = cb[CUM_BLK - 1:CUM_BLK, :]

    return pl.pallas_call(
        body, name=name, grid=(nb,),
        in_specs=[pl.BlockSpec((S, LANE), lambda b: (b, CB_FA)), pl.BlockSpec((1, LANE), lambda b: (0, 0))],
        out_specs=pl.BlockSpec((S, LANE), lambda b: (b, 0)),
        out_shape=jax.ShapeDtypeStruct((nb * S, LANE), F32),
        compiler_params=_cparams(("parallel",)),
    )(P, bf_pad)


def _forget_bwd(P, bf_pad, dcb, nb, name):
    nblk = S // CUM_BLK

    def body(fa_ref, b_ref, dc_ref, dfa_ref, db_ref):
        b = pl.program_id(0)
        tri = _tri(CUM_BLK, upper=True)
        lane = lax.broadcasted_iota(jnp.int32, (CUM_BLK, LANE), 1)
        carry = jnp.zeros((1, LANE), F32)
        dbacc = jnp.zeros((1, LANE), F32)
        for i in reversed(range(nblk)):
            rows = pl.ds(i * CUM_BLK, CUM_BLK)
            dc = jnp.zeros((CUM_BLK, LANE), F32)
            dcv = dc_ref[rows, :]
            for h in range(2 * NHP):
                dc = jnp.where(lane == h, -dcv[:, HD * h:HD * h + 1], dc)
            dl = _tri_apply(tri, dc) + carry
            carry = dl[0:1, :]
            z = fa_ref[rows, :] + b_ref[...]
            dz = jnp.where(lane < 2 * NHP, dl * (1.0 - _sigmoid(z)), 0.0)
            dfa_ref[rows, :] = dz.astype(BF16)
            dbacc = dbacc + jnp.sum(dz, axis=0, keepdims=True)

        dbfull = jnp.broadcast_to(dbacc, (8, LANE))

        @pl.when(b == 0)
        def _():
            db_ref[...] = dbfull

        @pl.when(b > 0)
        def _():
            db_ref[...] += dbfull

    return pl.pallas_call(
        body, name=name, grid=(nb,),
        in_specs=[pl.BlockSpec((S, LANE), lambda b: (b, CB_FA)), pl.BlockSpec((1, LANE), lambda b: (0, 0)),
                  pl.BlockSpec((S, WA), lambda b: (b, 0))],
        out_specs=[pl.BlockSpec((S, LANE), lambda b: (b, 0)), pl.BlockSpec((8, LANE), lambda b: (0, 0))],
        out_shape=[jax.ShapeDtypeStruct((nb * S, LANE), BF16), jax.ShapeDtypeStruct((8, LANE), F32)],
        compiler_params=_cparams(("arbitrary",)),
    )(P, bf_pad, dcb)


FQ = 256
NT_DIMS = (((1,), (1,)), ((), ()))
AUGW = 6


def _head_masks(shape):
    lane = lax.broadcasted_iota(jnp.int32, shape, 1)
    return lane < HD, lane >= HD


def _fox_bias_terms(c_ref, hp):
    lane = lax.broadcasted_iota(jnp.int32, (S, LANE), 1)
    cv = c_ref[...]
    return [_split3(jnp.sum(jnp.where(lane == 2 * hp + e, cv, 0.0), axis=1, keepdims=True)) for e in range(2)]


def _fox_ext(x, terms, side, heads, only):
    lane = lax.broadcasted_iota(jnp.int32, (S, LANE), 1)
    one = jnp.ones((S, 1), BF16)
    aug = jnp.zeros((S, LANE), BF16)
    for e in heads:
        hi, mid, lo = terms[e]
        cols = (hi, mid, lo, one, one, one) if side == "q" else (one, one, one, -hi, -mid, -lo)
        for i, col in enumerate(cols):
            aug = jnp.where(lane == AUGW * e + i, col, aug)
    if only is not None:
        x = jnp.where(_head_masks((S, LANE))[only], x, jnp.zeros_like(x))
    return jnp.concatenate([x, aug], axis=1)


def _halves(x, lane_mask):
    return jnp.where(lane_mask, x[0:FQ, 0:LANE], x[FQ:2 * FQ, 0:LANE])


def _fox_fwd(P, c, nb, name):
    def body(q_ref, k_ref, v_ref, c_ref, o_ref, lse_ref, qm0, qm1, kx, vx):
        hp = pl.program_id(1)
        terms = _fox_bias_terms(c_ref, hp)
        qv = (q_ref[...] * SCALE).astype(BF16)
        qm0[...] = _fox_ext(qv, terms, "q", (0,), 0)
        qm1[...] = _fox_ext(qv, terms, "q", (1,), 1)
        kx[...] = _fox_ext(k_ref[...].astype(BF16), terms, "k", (0, 1), None)
        lane = lax.broadcasted_iota(jnp.int32, (S, LANE), 1)
        vx[...] = jnp.concatenate([v_ref[...].astype(BF16), jnp.where(lane == 0, 1.0, 0.0).astype(BF16)], axis=1)
        tmask = _head_masks((FQ, LANE))[0]
        row = lax.broadcasted_iota(jnp.int32, (2 * FQ, FQ), 0) & (FQ - 1)
        col = lax.broadcasted_iota(jnp.int32, (2 * FQ, FQ), 1)
        for i in range(S // FQ):
            r0 = i * FQ
            qt = jnp.concatenate([qm0[pl.ds(r0, FQ), :], qm1[pl.ds(r0, FQ), :]], axis=0)
            sd = lax.dot_general(qt, kx[pl.ds(r0, FQ), :], NT_DIMS, preferred_element_type=F32)
            sd = jnp.where(col <= row, sd, NEG)
            m = jnp.max(sd, axis=1, keepdims=True)
            if i > 0:
                so = lax.dot_general(qt, kx[pl.ds(0, r0), :], NT_DIMS, preferred_element_type=F32)
                m = jnp.maximum(m, jnp.max(so, axis=1, keepdims=True))
            acc = jnp.dot(jnp.exp(sd - m).astype(BF16), vx[pl.ds(r0, FQ), :], preferred_element_type=F32)
            if i > 0:
                acc = acc + jnp.dot(jnp.exp(so - m).astype(BF16), vx[pl.ds(0, r0), :], preferred_element_type=F32)
            l = acc[:, LANE:LANE + 1]
            o_ref[pl.ds(r0, FQ), :] = _halves(acc / l, tmask)
            lse_ref[pl.ds(r0, FQ), :] = _halves(jnp.broadcast_to(m + jnp.log(l), (2 * FQ, LANE)), tmask)

    def colblk(off):
        return pl.BlockSpec((S, LANE), lambda b, hp: (b, off + hp))

    return pl.pallas_call(
        body, name=name, grid=(nb, NHP),
        in_specs=[colblk(CB_QA), colblk(CB_KA), colblk(CB_VA), pl.BlockSpec((S, LANE), lambda b, hp: (b, 0))],
        out_specs=[colblk(0), colblk(0)],
        out_shape=[jax.ShapeDtypeStruct((nb * S, WA), F32), jax.ShapeDtypeStruct((nb * S, WA), F32)],
        scratch_shapes=[pltpu.VMEM((S, 2 * LANE), BF16)] * 4,
        compiler_params=_cparams(("parallel", "parallel")),
    )(P, P, P, c)


def _fox_bwd(P, c, o, lse, do, nb, name):
    def body(q_ref, k_ref, v_ref, c_ref, o_ref, lse_ref, do_ref, dq_ref, dk_ref, dv_ref, dc_ref,
             km0, km1, qx, vm0, vm1, dob, kt0, kt1, rows, dqt, rsum):
        hp = pl.program_id(1)
        terms = _fox_bias_terms(c_ref, hp)
        kv = k_ref[...].astype(BF16)
        km0[...] = _fox_ext(kv, terms, "k", (0,), 0)
        km1[...] = _fox_ext(kv, terms, "k", (1,), 1)
        qx[...] = _fox_ext((q_ref[...] * SCALE).astype(BF16), terms, "q", (0, 1), None)
        masks = _head_masks((S, LANE))
        vv = v_ref[...].astype(BF16)
        zero = jnp.zeros((S, LANE), BF16)
        vm0[...] = jnp.where(masks[0], vv, zero)
        vm1[...] = jnp.where(masks[1], vv, zero)
        dov = do_ref[...]
        dob[...] = dov.astype(BF16)
        ktf = k_ref[...].T
        prodt = (dov * o_ref[...]).T
        lset = lse_ref[...].T
        hrow = lax.broadcasted_iota(jnp.int32, (LANE, S), 0)
        kt0[...] = jnp.where(hrow < HD, ktf, 0.0).astype(BF16)
        kt1[...] = jnp.where(hrow >= HD, ktf, 0.0).astype(BF16)
        for e in range(2):
            rows[e:e + 1, :] = lset[HD * e:HD * e + 1, :]
            rows[2 + e:3 + e, :] = jnp.sum(prodt[HD * e:HD * (e + 1), :], axis=0, keepdims=True)
        dqt[...] = jnp.zeros_like(dqt)
        rsum[...] = jnp.zeros_like(rsum)
        tmask = _head_masks((FQ, LANE))[0]
        row = lax.broadcasted_iota(jnp.int32, (2 * FQ, FQ), 0) & (FQ - 1)
        col = lax.broadcasted_iota(jnp.int32, (2 * FQ, FQ), 1)
        for j in range(S // FQ):
            k0 = j * FQ
            rest = S - k0 - FQ
            spans = [(k0, FQ)] + ([(k0 + FQ, rest)] if rest > 0 else [])
            kte = jnp.concatenate([km0[pl.ds(k0, FQ), :], km1[pl.ds(k0, FQ), :]], axis=0)
            vte = jnp.concatenate([vm0[pl.ds(k0, FQ), :], vm1[pl.ds(k0, FQ), :]], axis=0)
            ktt = jnp.concatenate([kt0[:, pl.ds(k0, FQ)], kt1[:, pl.ds(k0, FQ)]], axis=1)
            dke = jnp.zeros((2 * FQ, 2 * LANE), F32)
            dve = jnp.zeros((2 * FQ, LANE), F32)
            cse = jnp.zeros((2 * FQ, 1), F32)
            for si, (q0, n) in enumerate(spans):
                qs = qx[pl.ds(q0, n), :]
                dos = dob[pl.ds(q0, n), :]
                st = lax.dot_general(kte, qs, NT_DIMS, preferred_element_type=F32)
                if si == 0:
                    st = jnp.where(col >= row, st, NEG)
                dpt = lax.dot_general(vte, dos, NT_DIMS, preferred_element_type=F32)
                pts, dsts = [], []
                for e in range(2):
                    pe = jnp.exp(st[FQ * e:FQ * (e + 1), :] - rows[e:e + 1, pl.ds(q0, n)])
                    de = pe * (dpt[FQ * e:FQ * (e + 1), :] - rows[2 + e:3 + e, pl.ds(q0, n)])
                    rsum[HD * e:HD * e + 8, pl.ds(q0, n)] += _sum8(de)
                    pts.append(pe)
                    dsts.append(de)
                pt = jnp.concatenate(pts, axis=0)
                dst = jnp.concatenate(dsts, axis=0)
                dsb = dst.astype(BF16)
                dve = dve + jnp.dot(pt.astype(BF16), dos, preferred_element_type=F32)
                dke = dke + jnp.dot(dsb, qs, preferred_element_type=F32)
                dqt[:, pl.ds(q0, n)] += jnp.dot(ktt, dsb, preferred_element_type=F32)
                cse = cse + jnp.sum(dst, axis=1, keepdims=True)
            dk_ref[pl.ds(k0, FQ), :] = _halves(dke, tmask).astype(BF16)
            dv_ref[pl.ds(k0, FQ), :] = _halves(dve, tmask).astype(BF16)
            dc_ref[pl.ds(k0, FQ), :] = _halves(jnp.broadcast_to(cse, (2 * FQ, LANE)), tmask)
        dq_ref[...] = (dqt[...].T * SCALE).astype(BF16)
        tot = [jnp.sum(rsum[HD * e:HD * e + 8, :], axis=0, keepdims=True) for e in range(2)]
        dc_ref[...] = dc_ref[...] - jnp.where(hrow == 0, tot[0], jnp.where(hrow == HD, tot[1], 0.0)).T

    def colblk(off):
        return pl.BlockSpec((S, LANE), lambda b, hp: (b, off + hp))

    wide = pltpu.VMEM((S, 2 * LANE), BF16)
    half = pltpu.VMEM((S, LANE), BF16)
    return pl.pallas_call(
        body, name=name, grid=(nb, NHP),
        in_specs=[colblk(CB_QA), colblk(CB_KA), colblk(CB_VA), pl.BlockSpec((S, LANE), lambda b, hp: (b, 0)),
                  colblk(0), colblk(0), colblk(0)],
        out_specs=[colblk(0)] * 4,
        out_shape=[jax.ShapeDtypeStruct((nb * S, WA), BF16)] * 3 + [jax.ShapeDtypeStruct((nb * S, WA), F32)],
        scratch_shapes=[wide, wide, wide, half, half, half, pltpu.VMEM((LANE, S), BF16), pltpu.VMEM((LANE, S), BF16),
                        pltpu.VMEM((8, S), F32), pltpu.VMEM((LANE, S), F32), pltpu.VMEM((LANE, S), F32)],
        compiler_params=_cparams(("parallel", "parallel")),
    )(P, P, P, c, o, lse, do)


DILS = (1, 4, 16)
DB = 128


def _regroup_load(ref, d, scale=None):
    if d == 1:
        v = ref[...]
    else:
        L = S // d
        v = jnp.concatenate([ref[pl.ds(r, L, stride=d), :] for r in range(d)], axis=0)
    return v if scale is None else v * scale


def _regroup_store(ref, d, val_ref, accumulate):
    L = S // d
    for r in range(d):
        src = val_ref[pl.ds(r * L, L), :]
        dst = (slice(None), slice(None)) if d == 1 else (pl.ds(r, L, stride=d), slice(None))
        if accumulate:
            ref[dst] = ref[dst] + src
        else:
            ref[dst] = src


def _dil_bands():
    qi = lax.broadcasted_iota(jnp.int32, (DB, 2 * DB), 0)
    ki = lax.broadcasted_iota(jnp.int32, (DB, 2 * DB), 1)
    band = (ki >= qi) & (ki <= qi + DB)
    return band, band & (ki >= DB)


def _dil_valid(bands, bk, d):
    band, own = bands
    has_prev = (bk % ((S // d) // DB)) > 0
    return own | (band & has_prev)


def _dil_keys(kd, vd, r0, bands, bk, d):
    if S // d == DB:
        valid = bands[1][:, DB:]
        kk, vv = kd[pl.ds(r0 + DB, DB), :], vd[pl.ds(r0 + DB, DB), :]
    else:
        valid = _dil_valid(bands, bk, d)
        kk, vv = kd[pl.ds(r0, 2 * DB), :], vd[pl.ds(r0, 2 * DB), :]
    return kk, vv, jnp.concatenate([valid, valid], axis=0)


def _stack_heads(x, masks):
    zero = jnp.zeros_like(x)
    return jnp.concatenate([jnp.where(masks[0], x, zero), jnp.where(masks[1], x, zero)], axis=0)


def _dil_fwd(P, nb, name):
    nblk = S // DB

    def body(q_ref, k_ref, v_ref, o_ref, lse_ref, qd, kd, vd, rnum, rm, rl, num_n, m_n, l_n):
        masks = _head_masks((DB, LANE))
        bands = _dil_bands()
        for bi, d in enumerate(DILS):
            qd[...] = _regroup_load(q_ref, d, SCALE).astype(BF16)
            kd[pl.ds(0, DB), :] = jnp.zeros((DB, LANE), BF16)
            vd[pl.ds(0, DB), :] = jnp.zeros((DB, LANE), BF16)
            kd[pl.ds(DB, S), :] = _regroup_load(k_ref, d).astype(BF16)
            vd[pl.ds(DB, S), :] = _regroup_load(v_ref, d).astype(BF16)

            def blk(bk, _, d=d):
                r0 = pl.multiple_of(bk * DB, DB)
                qt = qd[pl.ds(r0, DB), :]
                kk, vv, valid = _dil_keys(kd, vd, r0, bands, bk, d)
                qm = _stack_heads(qt, masks)
                s = lax.dot_general(qm, kk, NT_DIMS, preferred_element_type=F32)
                s = jnp.where(valid, s, NEG)
                m = jnp.max(s, axis=1, keepdims=True)
                p = jnp.exp(s - m)
                l = jnp.sum(p, axis=1, keepdims=True)
                num = jnp.dot(p.astype(BF16), vv, preferred_element_type=F32)
                rnum[pl.ds(r0, DB), :] = jnp.where(masks[0], num[0:DB], num[DB:2 * DB])
                rm[pl.ds(r0, DB), :] = jnp.where(masks[0], m[0:DB], m[DB:2 * DB])
                rl[pl.ds(r0, DB), :] = jnp.where(masks[0], l[0:DB], l[DB:2 * DB])
                return 0

            lax.fori_loop(0, nblk, blk, 0, unroll=8)
            _regroup_store(num_n.at[bi], d, rnum, False)
            _regroup_store(m_n.at[bi], d, rm, False)
            _regroup_store(l_n.at[bi], d, rl, False)

        m_all = jnp.maximum(jnp.maximum(m_n[0], m_n[1]), m_n[2])
        num = jnp.zeros((S, LANE), F32)
        den = jnp.zeros((S, LANE), F32)
        for bi in range(3):
            a = jnp.exp(m_n[bi] - m_all)
            num = num + a * num_n[bi]
            den = den + a * l_n[bi]
        o_ref[...] = num / den
        lse_ref[...] = m_all + jnp.log(den)

    def colblk(off):
        return pl.BlockSpec((S, LANE), lambda b, hp: (b, off + hp))

    return pl.pallas_call(
        body, name=name, grid=(nb, NHP),
        in_specs=[colblk(CB_QB), colblk(CB_KB), colblk(CB_VB)],
        out_specs=[colblk(0), colblk(0)],
        out_shape=[jax.ShapeDtypeStruct((nb * S, WA), F32), jax.ShapeDtypeStruct((nb * S, WA), F32)],
        scratch_shapes=[pltpu.VMEM((S, LANE), BF16), pltpu.VMEM((S + DB, LANE), BF16), pltpu.VMEM((S + DB, LANE), BF16),
                        pltpu.VMEM((S, LANE), F32), pltpu.VMEM((S, LANE), F32), pltpu.VMEM((S, LANE), F32),
                        pltpu.VMEM((3, S, LANE), F32), pltpu.VMEM((3, S, LANE), F32), pltpu.VMEM((3, S, LANE), F32)],
        compiler_params=_cparams(("parallel", "parallel")),
    )(P, P, P)


def _dil_bwd(P, o, lse, do, nb, name):
    nblk = S // DB

    def body(q_ref, k_ref, v_ref, o_ref, lse_ref, do_ref, dq_ref, dk_ref, dv_ref,
             qd, kd, vd, dod, lsed, dsd, dsum, dq_r, dk_r, dv_r, dq_n, dk_n, dv_n):
        masks = _head_masks((DB, LANE))
        fmask = _head_masks((S, LANE))
        prod = do_ref[...] * o_ref[...]
        d0 = jnp.sum(jnp.where(fmask[0], prod, 0.0), axis=1, keepdims=True)
        d1 = jnp.sum(jnp.where(fmask[1], prod, 0.0), axis=1, keepdims=True)
        dsum[...] = jnp.where(fmask[0], d0, d1)
        tn = (((0,), (0,)), ((), ()))
        bands = _dil_bands()
        for bi, d in enumerate(DILS):
            qd[...] = _regroup_load(q_ref, d, SCALE).astype(BF16)
            kd[pl.ds(0, DB), :] = jnp.zeros((DB, LANE), BF16)
            vd[pl.ds(0, DB), :] = jnp.zeros((DB, LANE), BF16)
            kd[pl.ds(DB, S), :] = _regroup_load(k_ref, d).astype(BF16)
            vd[pl.ds(DB, S), :] = _regroup_load(v_ref, d).astype(BF16)
            dod[...] = _regroup_load(do_ref, d).astype(BF16)
            lsed[...] = _regroup_load(lse_ref, d)
            dsd[...] = _regroup_load(dsum, d)
            dq_r[...] = jnp.zeros_like(dq_r)
            dk_r[...] = jnp.zeros_like(dk_r)
            dv_r[...] = jnp.zeros_like(dv_r)

            def blk(bk, _, d=d):
                r0 = pl.multiple_of(bk * DB, DB)
                qt = qd[pl.ds(r0, DB), :]
                dot = dod[pl.ds(r0, DB), :]
                lt = lsed[pl.ds(r0, DB), :]
                dt = dsd[pl.ds(r0, DB), :]
                kk, vv, valid = _dil_keys(kd, vd, r0, bands, bk, d)
                kw = kk.shape[0]
                qm = _stack_heads(qt, masks)
                dom = _stack_heads(dot, masks)
                lcol = jnp.concatenate([lt[:, 0:1], lt[:, HD:HD + 1]], axis=0)
                dcol = jnp.concatenate([dt[:, 0:1], dt[:, HD:HD + 1]], axis=0)
                s = lax.dot_general(qm, kk, NT_DIMS, preferred_element_type=F32)
                s = jnp.where(valid, s, NEG)
                p = jnp.exp(s - lcol)
                dp = lax.dot_general(dom, vv, NT_DIMS, preferred_element_type=F32)
                ds = (p * (dp - dcol)).astype(BF16)
                dvt = lax.dot_general(p.astype(BF16), dom, tn, preferred_element_type=F32)
                dkt = lax.dot_general(ds, qm, tn, preferred_element_type=F32)
                dqt = jnp.dot(ds, kk, preferred_element_type=F32)
                dq_r[pl.ds(r0, DB), :] = jnp.where(masks[0], dqt[0:DB], dqt[DB:2 * DB])
                dk_r[pl.ds(r0 + 2 * DB - kw, kw), :] += dkt
                dv_r[pl.ds(r0 + 2 * DB - kw, kw), :] += dvt
                return 0

            lax.fori_loop(0, nblk, blk, 0, unroll=8)
            _regroup_store(dq_n, d, dq_r, bi > 0)
            _regroup_store(dk_n, d, dk_r.at[pl.ds(DB, S)], bi > 0)
            _regroup_store(dv_n, d, dv_r.at[pl.ds(DB, S)], bi > 0)

        dq_ref[...] = (dq_n[...] * SCALE).astype(BF16)
        dk_ref[...] = dk_n[...].astype(BF16)
        dv_ref[...] = dv_n[...].astype(BF16)

    def colblk(off):
        return pl.BlockSpec((S, LANE), lambda b, hp: (b, off + hp))

    big = pltpu.VMEM((S, LANE), F32)
    bigp = pltpu.VMEM((S + DB, LANE), F32)
    return pl.pallas_call(
        body, name=name, grid=(nb, NHP),
        in_specs=[colblk(CB_QB), colblk(CB_KB), colblk(CB_VB), colblk(0), colblk(0), colblk(0)],
        out_specs=[colblk(0)] * 3,
        out_shape=[jax.ShapeDtypeStruct((nb * S, WA), BF16)] * 3,
        scratch_shapes=[pltpu.VMEM((S, LANE), BF16), pltpu.VMEM((S + DB, LANE), BF16), pltpu.VMEM((S + DB, LANE), BF16),
                        pltpu.VMEM((S, LANE), BF16), big, big, big, big, bigp, bigp, big, big, big],
        compiler_params=_cparams(("parallel", "parallel")),
    )(P, P, P, o, lse, do)


RC = 256
NSHW = 4


def _window(src, start, buf):
    if start % 8 == 0:
        return src[pl.ds(start, RC), :]
    buf[...] = src[pl.ds(start, RC), :]
    return buf[...]
CPAD = 32


NSUB = 8
CROWS = S + CPAD


def _preshift(src, dst):
    for b in range(NSUB):
        dst[b] = src[pl.ds(b, CROWS), :]


def _shifted(dst, start):
    return dst[start % NSUB, pl.ds(start - start % NSUB, RC), :]


def _conv_chunk(gsh, r0, cw_ref, cb_ref):
    acc = jnp.zeros((RC, CC), F32) + cb_ref[...]
    for k in range(CK):
        acc = acc + cw_ref[k:k + 1, :] * _shifted(gsh, r0 + CPAD - (CK - 1) + k)
    return acc


def _cnorm(c0, cng_ref, cnb_ref):
    mu = jnp.mean(c0, axis=1, keepdims=True)
    xc = c0 - mu
    rstd = lax.rsqrt(jnp.mean(xc * xc, axis=1, keepdims=True) + EPS)
    n = xc * rstd
    return n, rstd, n * cng_ref[...] + cnb_ref[...]


NORM_ROWS = 512
YC_BLK = 2 * WA // CC


def _attn_norm_fwd(of, od, gof, god, name):
    T = of.shape[0]

    def body(of_ref, od_ref, gof_ref, god_ref, y_ref):
        for i, (src, g_ref) in enumerate(((of_ref, gof_ref), (od_ref, god_ref))):
            v = src[...]
            r = lax.rsqrt(jnp.mean(v * v, axis=1, keepdims=True) + EPS)
            y_ref[:, i * WA:(i + 1) * WA] = (v * r * g_ref[...]).astype(BF16)

    row = lambda w: pl.BlockSpec((NORM_ROWS, w), lambda i: (i, 0))
    par = pl.BlockSpec((1, WA), lambda i: (0, 0))
    return pl.pallas_call(
        body, name=name, grid=(T // NORM_ROWS,),
        in_specs=[row(WA), row(WA), par, par], out_specs=row(2 * WA),
        out_shape=jax.ShapeDtypeStruct((T, D), BF16),
        compiler_params=_cparams(("parallel",)),
    )(of, od, gof, god)


def _attn_norm_bwd(of, od, dy, gof, god, name):
    T = of.shape[0]

    def body(of_ref, od_ref, dy_ref, gof_ref, god_ref, dof_ref, dod_ref, dgo_ref):
        @pl.when(pl.program_id(0) == 0)
        def _():
            dgo_ref[...] = jnp.zeros_like(dgo_ref)

        for i, (src, g_ref, dst) in enumerate(((of_ref, gof_ref, dof_ref), (od_ref, god_ref, dod_ref))):
            v = src[...]
            dyv = dy_ref[:, i * WA:(i + 1) * WA].astype(F32)
            r = lax.rsqrt(jnp.mean(v * v, axis=1, keepdims=True) + EPS)
            a = dyv * g_ref[...]
            dst[...] = r * a - v * (r * r * r * jnp.mean(v * a, axis=1, keepdims=True))
            dgo_ref[i:i + 1, :] += jnp.sum(dyv * v * r, axis=0, keepdims=True)

    row = lambda w: pl.BlockSpec((NORM_ROWS, w), lambda i: (i, 0))
    par = pl.BlockSpec((1, WA), lambda i: (0, 0))
    return pl.pallas_call(
        body, name=name, grid=(T // NORM_ROWS,),
        in_specs=[row(WA), row(WA), row(2 * WA), par, par],
        out_specs=[row(WA), row(WA), pl.BlockSpec((8, WA), lambda i: (0, 0))],
        out_shape=[jax.ShapeDtypeStruct((T, WA), F32), jax.ShapeDtypeStruct((T, WA), F32),
                   jax.ShapeDtypeStruct((8, WA), F32)],
        compiler_params=_cparams(("arbitrary",)),
    )(of, od, dy, gof, god)


def _conv_specs():
    gblk = lambda off: pl.BlockSpec((S, CC), lambda b: (b, off))
    par = lambda r: pl.BlockSpec((r, CC), lambda b: (0, 0))
    return gblk, par


def _conv_fwd(P, y, cw, cb, cng, cnb, nb, name):
    def body(gv_ref, gg_ref, cw_ref, cb_ref, cng_ref, cnb_ref, y_in, y_ref, c0_ref, gpad, gsh):
        del y_in
        gpad[pl.ds(0, CPAD), :] = jnp.zeros((CPAD, CC), F32)
        gpad[pl.ds(CPAD, S), :] = gv_ref[...] * _sigmoid(gg_ref[...])
        gpad[pl.ds(CROWS, NSUB), :] = jnp.zeros((NSUB, CC), F32)
        _preshift(gpad, gsh)
        for ci in range(S // RC):
            r0 = ci * RC
            c0 = _conv_chunk(gsh, r0, cw_ref, cb_ref)
            c0_ref[pl.ds(r0, RC), :] = c0
            _, _, z = _cnorm(c0, cng_ref, cnb_ref)
            y_ref[pl.ds(r0, RC), :] = (z * _sigmoid(z)).astype(BF16)

    gblk, par = _conv_specs()
    return pl.pallas_call(
        body, name=name, grid=(nb,),
        in_specs=[gblk(CB_GV), gblk(CB_GG), par(CPAD), par(1), par(1), par(1), pl.BlockSpec(memory_space=pl.ANY)],
        out_specs=[gblk(YC_BLK), gblk(0)],
        out_shape=[jax.ShapeDtypeStruct((nb * S, D), BF16), jax.ShapeDtypeStruct((nb * S, CC), F32)],
        input_output_aliases={6: 0},
        scratch_shapes=[pltpu.VMEM((CROWS + NSUB, CC), F32), pltpu.VMEM((NSUB, CROWS, CC), F32)],
        compiler_params=_cparams(("parallel",)),
    )(P, P, cw, cb, cng, cnb, y)


def _conv_bwd(P, c0, dy, cw, cng, cnb, nb, name):
    def body(gv_ref, gg_ref, c0_ref, dy_ref, cw_ref, cng_ref, cnb_ref, dg_ref, dcw_ref, dsm_ref, dpad, dsh):
        @pl.when(pl.program_id(0) == 0)
        def _():
            dcw_ref[...] = jnp.zeros_like(dcw_ref)
            dsm_ref[...] = jnp.zeros_like(dsm_ref)

        dpad[pl.ds(S, CPAD + NSUB), :] = jnp.zeros((CPAD + NSUB, CC), F32)
        zero = jnp.zeros((8, CC), F32)
        dcb, dcng, dcnb = zero, zero, zero
        for ci in range(S // RC):
            r0 = ci * RC
            n, rstd, z = _cnorm(c0_ref[pl.ds(r0, RC), :], cng_ref, cnb_ref)
            sz = _sigmoid(z)
            dz = dy_ref[pl.ds(r0, RC), :].astype(F32) * (sz * (1.0 + z * (1.0 - sz)))
            dcng = dcng + _sum8(dz * n)
            dcnb = dcnb + _sum8(dz)
            dn = dz * cng_ref[...]
            dc0 = rstd * (dn - jnp.mean(dn, axis=1, keepdims=True) - n * jnp.mean(dn * n, axis=1, keepdims=True))
            dcb = dcb + _sum8(dc0)
            dpad[pl.ds(r0, RC), :] = dc0
        dsm_ref[0:1, :] += jnp.sum(dcb, axis=0, keepdims=True)
        dsm_ref[1:2, :] += jnp.sum(dcng, axis=0, keepdims=True)
        dsm_ref[2:3, :] += jnp.sum(dcnb, axis=0, keepdims=True)

        _preshift(dpad, dsh)
        dws = [zero] * CK
        for ci in range(S // RC):
            r0 = ci * RC
            sg = _sigmoid(gg_ref[pl.ds(r0, RC), :])
            gvc = gv_ref[pl.ds(r0, RC), :]
            glu = gvc * sg
            dgl = jnp.zeros((RC, CC), F32)
            for k in range(CK):
                win = _shifted(dsh, r0 + (CK - 1) - k)
                dws[k] = dws[k] + _sum8(win * glu)
                dgl = dgl + cw_ref[k:k + 1, :] * win
            dg_ref[pl.ds(r0, RC), 0:CC] = (dgl * sg).astype(BF16)
            dg_ref[pl.ds(r0, RC), CC:2 * CC] = (dgl * gvc * sg * (1.0 - sg)).astype(BF16)
        for k in range(CK):
            dcw_ref[k:k + 1, :] += jnp.sum(dws[k], axis=0, keepdims=True)

    gblk, par = _conv_specs()
    return pl.pallas_call(
        body, name=name, grid=(nb,),
        in_specs=[gblk(CB_GV), gblk(CB_GG), gblk(0), gblk(YC_BLK), par(CPAD), par(1), par(1)],
        out_specs=[pl.BlockSpec((S, 2 * CC), lambda b: (b, 0)), par(CPAD), par(8)],
        out_shape=[jax.ShapeDtypeStruct((nb * S, 2 * CC), BF16), jax.ShapeDtypeStruct((CPAD, CC), F32),
                   jax.ShapeDtypeStruct((8, CC), F32)],
        scratch_shapes=[pltpu.VMEM((CROWS + NSUB, CC), F32), pltpu.VMEM((NSUB, CROWS, CC), F32)],
        compiler_params=_cparams(("arbitrary",)),
    )(P, P, c0, dy, cw, cng, cnb)


FC = 512
FPAD = 8
NFB = 2 * DFF // FC


def _ffn_u2_chunk(upad, r0, fw_ref, fb_ref):
    acc = jnp.zeros((RC, FC), F32) + fb_ref[...]
    for k in range(FK):
        acc = acc + fw_ref[k:k + 1, :] * upad[pl.ds(r0 + FPAD - (FK - 1) + k, RC), :]
    return acc


def _ffn_fwd(U, fw, fb, nb, name):
    def body(u_ref, fw_ref, fb_ref, h_ref, u2_ref, upad):
        upad[pl.ds(0, FPAD), :] = jnp.zeros((FPAD, FC), F32)
        upad[pl.ds(FPAD, S), :] = u_ref[...].astype(F32)
        for ci in range(S // RC):
            r0 = ci * RC
            u2 = _ffn_u2_chunk(upad, r0, fw_ref, fb_ref)
            u2_ref[pl.ds(r0, RC), :] = u2.astype(BF16)
            a2, b2 = u2[:, :FC // 2], u2[:, FC // 2:]
            h_ref[pl.ds(r0, RC), :] = (a2 * _sigmoid(a2) * b2).astype(BF16)

    return pl.pallas_call(
        body, name=name, grid=(nb, NFB),
        in_specs=[pl.BlockSpec((S, FC), lambda b, j: (b, j)), pl.BlockSpec((8, FC), lambda b, j: (0, j)),
                  pl.BlockSpec((1, FC), lambda b, j: (0, j))],
        out_specs=[pl.BlockSpec((S, FC // 2), lambda b, j: (b, j)), pl.BlockSpec((S, FC), lambda b, j: (b, j))],
        out_shape=[jax.ShapeDtypeStruct((nb * S, DFF), BF16), jax.ShapeDtypeStruct((nb * S, 2 * DFF), BF16)],
        scratch_shapes=[pltpu.VMEM((S + FPAD, FC), F32)],
        compiler_params=_cparams(("parallel", "parallel")),
    )(U, fw, fb)


def _ffn_bwd(U, U2, dhid, fw, nb, name):
    def body(u_ref, u2_ref, dh_ref, fw_ref, du_ref, dfw_ref, dpad, shw):
        @pl.when(pl.program_id(1) == 0)
        def _():
            dfw_ref[...] = jnp.zeros_like(dfw_ref)

        dpad[pl.ds(S, FPAD), :] = jnp.zeros((FPAD, FC), F32)
        zero = jnp.zeros((8, FC), F32)
        dbias = zero
        for ci in range(S // RC):
            r0 = ci * RC
            u2 = u2_ref[pl.ds(r0, RC), :].astype(F32)
            a2, b2 = u2[:, :FC // 2], u2[:, FC // 2:]
            sa = _sigmoid(a2)
            dh = dh_ref[pl.ds(r0, RC), :].astype(F32)
            du2 = jnp.concatenate([dh * b2 * (sa * (1.0 + a2 * (1.0 - sa))), dh * a2 * sa], axis=1)
            dpad[pl.ds(r0, RC), :] = du2
            dbias = dbias + _sum8(du2)
        dws = [zero] * FK
        for ci in range(S // RC):
            r0 = ci * RC
            uc = u_ref[pl.ds(r0, RC), :].astype(F32)
            du = jnp.zeros((RC, FC), F32)
            for k in range(FK):
                win = _window(dpad, r0 + (FK - 1) - k, shw.at[k % NSHW])
                dws[k] = dws[k] + _sum8(win * uc)
                du = du + fw_ref[k:k + 1, :] * win
            du_ref[pl.ds(r0, RC), :] = du.astype(BF16)
        for k in range(FK):
            dfw_ref[k:k + 1, :] += jnp.sum(dws[k], axis=0, keepdims=True)
        dfw_ref[FK:FK + 1, :] += jnp.sum(dbias, axis=0, keepdims=True)

    blk = pl.BlockSpec((S, FC), lambda j, b: (b, j))
    return pl.pallas_call(
        body, name=name, grid=(NFB, nb),
        in_specs=[blk, blk, pl.BlockSpec((S, FC // 2), lambda j, b: (b, j)), pl.BlockSpec((8, FC), lambda j, b: (0, j))],
        out_specs=[blk, pl.BlockSpec((8, FC), lambda j, b: (0, j))],
        out_shape=[jax.ShapeDtypeStruct((nb * S, 2 * DFF), BF16), jax.ShapeDtypeStruct((8, 2 * DFF), F32)],
        scratch_shapes=[pltpu.VMEM((S + FPAD, FC), F32), pltpu.VMEM((NSHW, RC, FC), F32)],
        compiler_params=_cparams(("parallel", "arbitrary")),
    )(U, U2, dhid, fw)


def _matmul_ffn(a, b, mode, *, out_dtype=F32, tm=1024, tk=2048, norm_g=None, name):
    HF = FC // 2
    if mode == "fwd":
        M, K = a.shape
        tm = 512

        def body(a_ref, g_ref, b_ref, h_ref, o_ref):
            av = _rms_rows(a_ref[...], g_ref)
            h_ref[...] = av
            for j in range(NFB):
                for half in range(2):
                    bv = b_ref[:, half * DFF + j * HF:half * DFF + (j + 1) * HF]
                    o_ref[:, j * FC + half * HF:j * FC + (half + 1) * HF] = jnp.dot(
                        av, bv, preferred_element_type=F32).astype(o_ref.dtype)

        return pl.pallas_call(
            body, name=name, grid=(M // tm,),
            in_specs=[pl.BlockSpec((tm, K), lambda i: (i, 0)), pl.BlockSpec((1, K), lambda i: (0, 0)),
                      pl.BlockSpec((K, 2 * DFF), lambda i: (0, 0))],
            out_specs=[pl.BlockSpec((tm, K), lambda i: (i, 0)), pl.BlockSpec((tm, 2 * DFF), lambda i: (i, 0))],
            out_shape=[jax.ShapeDtypeStruct((M, K), BF16), jax.ShapeDtypeStruct((M, 2 * DFF), out_dtype)],
            compiler_params=_cparams(("parallel",)),
        )(a, norm_g.reshape(1, K), b)
    if mode == "dx":
        M = a.shape[0]
        N = b.shape[0]
        tm = 512

        def body(a_ref, b_ref, o_ref):
            acc = None
            for j in range(NFB):
                for half in range(2):
                    av = a_ref[:, j * FC + half * HF:j * FC + (half + 1) * HF]
                    bv = b_ref[:, half * DFF + j * HF:half * DFF + (j + 1) * HF]
                    d = lax.dot_general(av, bv, NT_DIMS, preferred_element_type=F32)
                    acc = d if acc is None else acc + d
            o_ref[...] = acc.astype(o_ref.dtype)

        return pl.pallas_call(
            body, name=name, grid=(M // tm,),
            in_specs=[pl.BlockSpec((tm, 2 * DFF), lambda i: (i, 0)), pl.BlockSpec((N, 2 * DFF), lambda i: (0, 0))],
            out_specs=pl.BlockSpec((tm, N), lambda i: (i, 0)),
            out_shape=jax.ShapeDtypeStruct((M, N), out_dtype),
            compiler_params=_cparams(("parallel",)),
        )(a, b)
    assert mode == "dw"
    T, M = a.shape
    nk = T // tk

    def body(a_ref, g_ref, oa_ref, ob_ref, acc):
        k = pl.program_id(1)
        prod = lax.dot_general(a_ref[...], g_ref[...], (((0,), (0,)), ((), ())), preferred_element_type=F32)

        @pl.when(k == 0)
        def _():
            acc[...] = prod

        @pl.when(k > 0)
        def _():
            acc[...] += prod

        @pl.when(k == nk - 1)
        def _():
            oa_ref[...] = acc[:, :HF]
            ob_ref[...] = acc[:, HF:]

    half = pl.BlockSpec((M, HF), lambda j, k: (0, j))
    return pl.pallas_call(
        body, name=name, grid=(NFB, nk),
        in_specs=[pl.BlockSpec((tk, M), lambda j, k: (k, 0)), pl.BlockSpec((tk, FC), lambda j, k: (k, j))],
        out_specs=[half, half],
        out_shape=[jax.ShapeDtypeStruct((M, DFF), F32)] * 2,
        scratch_shapes=[pltpu.VMEM((M, FC), F32)],
        compiler_params=_cparams(("parallel", "arbitrary")),
    )(a, b)


def _adamw_body(w_ref, g_ref, m_ref, v_ref, d_ref, nm_ref, nv_ref):
    g = g_ref[...]
    m = ADAM_B1 * m_ref[...] + (1.0 - ADAM_B1) * g
    v = ADAM_B2 * v_ref[...] + (1.0 - ADAM_B2) * (g * g)
    m_hat = m / (1.0 - ADAM_B1 ** ADAM_STEP)
    v_hat = v / (1.0 - ADAM_B2 ** ADAM_STEP)
    d_ref[...] = -ADAM_LR * (m_hat / (jnp.sqrt(v_hat) + ADAM_EPS) + ADAM_WD * w_ref[...])
    nm_ref[...] = m
    nv_ref[...] = v


def _adamw(w, g, m, v, name):
    shape = w.shape
    R = 1
    for s in shape[:-1]:
        R *= s
    C = shape[-1]
    args = [a.reshape(R, C) for a in (w, g, m, v)]
    tr = R
    for cand in (512, 352, 256, 128, 64, 32, 16, 8):
        if R % cand == 0 and cand * C * 4 * 14 <= 24 * 1024 * 1024:
            tr = cand
            break
    blk = pl.BlockSpec((tr, C), lambda i: (i, 0))
    outs = pl.pallas_call(
        functools.partial(_adamw_body), name=name, grid=(R // tr,),
        in_specs=[blk] * 4, out_specs=[blk] * 3,
        out_shape=[jax.ShapeDtypeStruct((R, C), F32)] * 3,
        compiler_params=_cparams(("parallel",)),
    )(*args)
    return [o.reshape(shape) for o in outs]


def _unpack_cols_value(g):
    return jnp.concatenate([g[:, :N_QKV], g[:, NIN - N_FG:NIN], g[:, N_QKV:NIN - N_FG]], axis=1)


def _adamw_layers(w, gs, m, v, name, packed=False):
    _, R, C = w.shape
    Cg = gs[0].shape[1]
    tr = 128 if R % 128 == 0 else 176
    assert R % tr == 0 and len(gs) == DEPTH == 2

    def body(w_ref, g0_ref, g1_ref, m_ref, v_ref, g_out, d_ref, nm_ref, nv_ref):
        g = jnp.where(pl.program_id(0) == 0, g0_ref[...], g1_ref[...])
        if packed:
            g = _unpack_cols_value(g)
        g_out[...] = g
        mn = ADAM_B1 * m_ref[...] + (1.0 - ADAM_B1) * g
        vn = ADAM_B2 * v_ref[...] + (1.0 - ADAM_B2) * (g * g)
        m_hat = mn / (1.0 - ADAM_B1 ** ADAM_STEP)
        v_hat = vn / (1.0 - ADAM_B2 ** ADAM_STEP)
        d_ref[...] = -ADAM_LR * (m_hat / (jnp.sqrt(v_hat) + ADAM_EPS) + ADAM_WD * w_ref[...])
        nm_ref[...] = mn
        nv_ref[...] = vn

    lay = pl.BlockSpec((None, tr, C), lambda l, i: (l, i, 0))
    gsp = pl.BlockSpec((tr, Cg), lambda l, i: (i, 0))
    return pl.pallas_call(
        body, name=name, grid=(DEPTH, R // tr),
        in_specs=[lay, gsp, gsp, lay, lay], out_specs=[lay] * 4,
        out_shape=[jax.ShapeDtypeStruct((DEPTH, R, C), F32)] * 4,
        compiler_params=_cparams(("parallel", "parallel")),
    )(w, gs[0], gs[1], m, v)


def _pack_w_in(w_in):
    _, R, _ = w_in.shape

    def body(w_ref, o_ref):
        wv = w_ref[...]
        pad = jnp.zeros((R, NP - NIN), F32)
        o_ref[...] = jnp.concatenate(
            [wv[:, :N_QKV], wv[:, N_QKV + N_FG:], wv[:, N_QKV:N_QKV + N_FG], pad], axis=1).astype(BF16)

    return pl.pallas_call(
        body, name="pack_w_in", grid=(DEPTH,),
        in_specs=[pl.BlockSpec((None, R, NIN), lambda l: (l, 0, 0))],
        out_specs=pl.BlockSpec((None, R, NP), lambda l: (l, 0, 0)),
        out_shape=jax.ShapeDtypeStruct((DEPTH, R, NP), BF16),
        compiler_params=_cparams(("parallel",)),
    )(w_in)


def _rs_row_tile(H):
    th = 128 if H % 128 == 0 else 176
    assert H % th == 0
    return th


def _add_half(g, r1, place, name):
    _, R, C = g.shape
    H = R // 2
    th = _rs_row_tile(H)
    nh = H // th

    def body(s_ref, g_ref, r_ref, o_ref):
        o_ref[...] = (g_ref[...] + r_ref[...]).astype(BF16)

    grid_spec = pltpu.PrefetchScalarGridSpec(
        num_scalar_prefetch=1, grid=(NCHIP, nh),
        in_specs=[pl.BlockSpec((None, th, C), lambda p, i, s: (p, s[1] * nh + i, 0)),
                  pl.BlockSpec((None, th, C), lambda p, i, s: (p, i, 0))],
        out_specs=pl.BlockSpec((None, th, C), lambda p, i, s: (p, i, 0)))
    return pl.pallas_call(
        body, name=name, grid_spec=grid_spec, out_shape=jax.ShapeDtypeStruct((NCHIP, H, C), BF16),
        compiler_params=_cparams(("parallel", "parallel")),
    )(place, g, r1)


def _sum_slots(g, r1, r2, place, name):
    _, R, C = g.shape
    H = R // 2
    th = _rs_row_tile(H)
    nh = H // th

    def body(s_ref, g_ref, r1_ref, r2_ref, o_ref):
        acc = g_ref[...] + r1_ref[...]
        for j in range(NCHIP - 1):
            acc = acc + r2_ref[j].astype(F32)
        o_ref[...] = acc

    grid_spec = pltpu.PrefetchScalarGridSpec(
        num_scalar_prefetch=1, grid=(nh,),
        in_specs=[pl.BlockSpec((None, th, C), lambda i, s: (s[0], s[1] * nh + i, 0)),
                  pl.BlockSpec((None, th, C), lambda i, s: (s[0], i, 0)),
                  pl.BlockSpec((NCHIP - 1, th, C), lambda i, s: (0, i, 0))],
        out_specs=pl.BlockSpec((None, th, C), lambda i, s: (s[1], i, 0)))
    return pl.pallas_call(
        body, name=name, grid_spec=grid_spec, out_shape=jax.ShapeDtypeStruct((2, H, C), F32),
        compiler_params=_cparams(("parallel",)),
    )(place, g, r1, r2)


MESH = pl.DeviceIdType.MESH
HBM = pl.BlockSpec(memory_space=pltpu.HBM)


def _place():
    x, y, c = lax.axis_index("x"), lax.axis_index("y"), lax.axis_index("c")
    chips = [(1 - x, y), (x, 1 - y), (1 - x, 1 - y)]
    return x, y, c, chips


def _rcopy(src, dst, ssem, rsem, dev):
    return pltpu.make_async_remote_copy(src_ref=src, dst_ref=dst, send_sem=ssem, recv_sem=rsem,
                                        device_id=dev, device_id_type=MESH)


AG_CHUNK_BYTES = 1 << 20


def _stage_rows(R, C, dtype):
    rows = R
    while rows * C * jnp.dtype(dtype).itemsize > AG_CHUNK_BYTES and rows % 32 == 0:
        rows //= 2
    return rows


def _allgather(shards, split):
    n = len(shards)
    nout = n * DEPTH
    rows = [_stage_rows(s.shape[1], s.shape[2], s.dtype) for s in shards]

    def body(*refs):
        ins, outs = refs[:n], refs[n:n + nout]
        stages = refs[n + nout:2 * n + nout]
        ssem, rsem, fssem, frsem, isem, osem = refs[2 * n + nout:]
        x, y, c, chips = _place()
        me = 2 * x + y
        sib = (x, y, 1 - c)

        def window(t, l, chip, half):
            if not split[t]:
                return outs[t * DEPTH + l].at[chip]
            H = shards[t].shape[1] // 2
            return outs[t * DEPTH + l].at[chip, pl.ds(half * H, H)]

        sends = []
        for t in range(n):
            H = shards[t].shape[1] // 2
            for l in range(DEPTH):
                src = ins[t].at[l, pl.ds(c * H, H)] if split[t] else ins[t].at[l]
                for j, (cx, cy) in enumerate(chips):
                    k = (t * DEPTH + l) * 3 + j
                    cp = _rcopy(src, window(t, l, me, c), ssem.at[k], rsem.at[k], (cx, cy, c))
                    cp.start()
                    sends.append(cp)
        for t in range(n):
            nch = shards[t].shape[1] // rows[t]
            outc = []
            for l in range(DEPTH):
                for i in range(nch):
                    slot = len(outc) % 2
                    if len(outc) >= 2:
                        outc[-2].wait()
                    rs = pl.ds(i * rows[t], rows[t])
                    cin = pltpu.make_async_copy(ins[t].at[l, rs], stages[t].at[slot], isem.at[t])
                    cin.start()
                    cin.wait()
                    co = pltpu.make_async_copy(stages[t].at[slot], outs[t * DEPTH + l].at[me, rs], osem.at[2 * t + slot])
                    co.start()
                    outc.append(co)
            for co in outc[-2:]:
                co.wait()
        for t in range(n):
            for l in range(DEPTH):
                for j, (cx, cy) in enumerate(chips):
                    k = (t * DEPTH + l) * 3 + j
                    win = window(t, l, 2 * cx + cy, c)
                    _rcopy(win, win, ssem.at[k], rsem.at[k], (cx, cy, c)).wait_recv()
                    if split[t]:
                        cp = _rcopy(win, win, fssem.at[k], frsem.at[k], sib)
                        cp.start()
                        sends.append(cp)
        for t in range(n):
            if split[t]:
                for l in range(DEPTH):
                    for j, (cx, cy) in enumerate(chips):
                        k = (t * DEPTH + l) * 3 + j
                        win = window(t, l, 2 * cx + cy, 1 - c)
                        _rcopy(win, win, fssem.at[k], frsem.at[k], sib).wait_recv()
        for cp in sends:
            cp.wait_send()

    out_shape = [jax.ShapeDtypeStruct((NCHIP,) + s.shape[1:], s.dtype) for s in shards for _ in range(DEPTH)]
    outs = pl.pallas_call(
        body, name="allgather_weights", in_specs=[HBM] * n, out_specs=[HBM] * nout, out_shape=out_shape,
        scratch_shapes=[pltpu.VMEM((2, r, s.shape[2]), s.dtype) for r, s in zip(rows, shards)]
        + [pltpu.SemaphoreType.DMA((3 * nout,))] * 4 + [pltpu.SemaphoreType.DMA((n,)), pltpu.SemaphoreType.DMA((2 * n,))],
        compiler_params=pltpu.CompilerParams(vmem_limit_bytes=VMEM_LIMIT),
    )(*shards)
    return [outs[t * DEPTH:(t + 1) * DEPTH] for t in range(n)]


def _rs_pair_exchange(gs):
    n = len(gs)

    def body(*refs):
        ins, outs = refs[:n], refs[n:2 * n]
        ssem, rsem = refs[2 * n:]
        x, y, c, _ = _place()
        cps = []
        for t in range(n):
            H = gs[t].shape[1] // 2
            cp = _rcopy(ins[t].at[:, pl.ds((1 - c) * H, H)], outs[t], ssem.at[t], rsem.at[t], (x, y, 1 - c))
            cp.start()
            cps.append(cp)
        for cp in cps:
            cp.wait_recv()
        for cp in cps:
            cp.wait_send()

    out_shape = [jax.ShapeDtypeStruct((NCHIP, g.shape[1] // 2, g.shape[2]), F32) for g in gs]
    return pl.pallas_call(
        body, name="rs_pair_exchange", in_specs=[HBM] * n, out_specs=[HBM] * n, out_shape=out_shape,
        scratch_shapes=[pltpu.SemaphoreType.DMA((n,))] * 2,
    )(*gs)


def _rs_chip_scatter(hs):
    n = len(hs)

    def body(*refs):
        ins, outs = refs[:n], refs[n:2 * n]
        ssem, rsem = refs[2 * n:]
        x, y, c, chips = _place()
        sends = []
        for t in range(n):
            for j, (cx, cy) in enumerate(chips):
                cp = _rcopy(ins[t].at[2 * cx + cy], outs[t].at[j], ssem.at[3 * t + j], rsem.at[3 * t + j], (cx, cy, c))
                cp.start()
                sends.append(cp)
        for cp in sends:
            cp.wait_recv()
        for cp in sends:
            cp.wait_send()

    out_shape = [jax.ShapeDtypeStruct((NCHIP - 1,) + h.shape[1:], h.dtype) for h in hs]
    return pl.pallas_call(
        body, name="rs_chip_scatter", in_specs=[HBM] * n, out_specs=[HBM] * n, out_shape=out_shape,
        scratch_shapes=[pltpu.SemaphoreType.DMA((3 * n,))] * 2,
    )(*hs)


def _rs_pair_gather(fs):
    n = len(fs)

    def body(*refs):
        bufs = refs[n:2 * n]
        ssem, rsem = refs[2 * n:]
        x, y, c, _ = _place()
        sends = []
        for t in range(n):
            cp = _rcopy(bufs[t].at[c], bufs[t].at[c], ssem.at[t], rsem.at[t], (x, y, 1 - c))
            cp.start()
            sends.append(cp)
        for t in range(n):
            win = bufs[t].at[1 - c]
            _rcopy(win, win, ssem.at[t], rsem.at[t], (x, y, 1 - c)).wait_recv()
        for cp in sends:
            cp.wait_send()

    out_shape = [jax.ShapeDtypeStruct(f.shape, F32) for f in fs]
    return pl.pallas_call(
        body, name="rs_pair_gather", in_specs=[HBM] * n, out_specs=[HBM] * n, out_shape=out_shape,
        input_output_aliases={t: t for t in range(n)},
        scratch_shapes=[pltpu.SemaphoreType.DMA((n,))] * 2,
    )(*fs)


def _allreduce_small(buf):
    R = buf.shape[0]

    def body(in_ref, out_ref, slots, ssem, rsem):
        x, y, c, _ = _place()
        me = 4 * x + 2 * y + c
        slots[me] = in_ref[...]
        cps = []
        for k in range(1, NDEV):
            px = 1 - x if k & 4 else x
            py = 1 - y if k & 2 else y
            pc = 1 - c if k & 1 else c
            cp = _rcopy(in_ref, slots.at[me], ssem.at[k - 1], rsem.at[k - 1], (px, py, pc))
            cp.start()
            cps.append((cp, 4 * px + 2 * py + pc))
        for k, (cp, peer) in enumerate(cps):
            _rcopy(in_ref, slots.at[peer], ssem.at[k], rsem.at[k], (x, y, c)).wait_recv()
        for cp, _ in cps:
            cp.wait_send()
        acc = slots[0]
        for p in range(1, NDEV):
            acc = acc + slots[p]
        out_ref[...] = acc

    return pl.pallas_call(
        body, name="allreduce_small", out_shape=jax.ShapeDtypeStruct((R, LANE), F32),
        in_specs=[pl.BlockSpec(memory_space=pltpu.VMEM)], out_specs=pl.BlockSpec(memory_space=pltpu.VMEM),
        scratch_shapes=[pltpu.VMEM((NDEV, R, LANE), F32), pltpu.SemaphoreType.DMA((NDEV - 1,)),
                        pltpu.SemaphoreType.DMA((NDEV - 1,))],
        compiler_params=pltpu.CompilerParams(vmem_limit_bytes=VMEM_LIMIT),
    )(buf)


def _interleave(a):
    lead = a.shape[:-1]
    return a.reshape(*lead, 2, NFB, FC // 2).swapaxes(-3, -2).reshape(*lead, 2 * DFF)


def _uninterleave(a):
    lead = a.shape[:-1]
    return a.reshape(*lead, NFB, 2, FC // 2).swapaxes(-3, -2).reshape(*lead, 2 * DFF)


def _train_compute(xt, tgt, W, nb):
    saved = []
    xc = xt
    for l in range(DEPTH):
        t = f"_l{l}"
        h, P = _rms_matmul(xc, W["ln1"][l], W["in"][l], tm=512, name="proj_in" + t)
        c = _forget_fwd(P, W["bf"][l], nb, "forget_fwd" + t)
        of, lsef = _fox_fwd(P, c, nb, "fox_fwd" + t)
        od, lsed = _dil_fwd(P, nb, "dil_fwd" + t)
        convp = (W["cw"][l], W["cb"][l], W["cng"][l], W["cnb"][l])
        y = _attn_norm_fwd(of, od, W["gof"][l], W["god"][l], "attn_norm_fwd" + t)
        y, c0 = _conv_fwd(P, y, *convp, nb, "conv_fwd" + t)
        xm = _matmul(y, W["o"][l], add=xc, tm=1024, tn=1024, tk=D, name="proj_out" + t)
        h2, U = _matmul_ffn(xm, W["up"][l], "fwd", out_dtype=BF16, norm_g=W["ln2"][l], name="ffn_up" + t)
        hid, U2 = _ffn_fwd(U, W["fw"][l], W["fb"][l], nb, "ffn_act_fwd" + t)
        xo = _matmul(hid, W["down"][l], add=xm, tm=512, tn=D, tk=DFF, name="ffn_down" + t)
        saved.append((xc, h, P, c, of, lsef, od, lsed, convp, c0, y, xm, h2, U, U2, hid))
        xc = xo

    loss8, dx, dxb, dgfin = _loss_head(xc, W["gfin"], tgt, "loss_head")

    big = [None] * DEPTH
    small = [None] * DEPTH
    for l in reversed(range(DEPTH)):
        t = f"_l{l}"
        xin, h, P, c, of, lsef, od, lsed, convp, c0, y, xm, h2, U, U2, hid = saved[l]
        dhid = _matmul(dxb, W["down"][l], tb=True, out_dtype=BF16, tm=1024, tn=DFF, tk=D, name="ffn_down_dx" + t)
        dWd = _matmul(hid, dxb, ta=True, tm=DFF // 2, tn=D, tk=2048, name="ffn_down_dw" + t)
        dU, dfw = _ffn_bwd(U, U2, dhid, W["fw"][l], nb, "ffn_act_bwd" + t)
        dh2 = _matmul_ffn(dU, W["up"][l], "dx", out_dtype=BF16, name="ffn_up_dx" + t)
        dWup = _matmul_ffn(h2, dU, "dw", name="ffn_up_dw" + t)
        dxm, dxmb, dln2 = _rms_bwd(xm, W["ln2"][l], dh2, dx, "rms2_bwd" + t)
        dy = _matmul(dxmb, W["o"][l], tb=True, out_dtype=BF16, tm=1024, tn=D, tk=D, name="proj_out_dx" + t)
        dWo = _matmul(y, dxmb, ta=True, tm=D, tn=D, tk=2048, name="proj_out_dw" + t)
        dof, dod, dgo = _attn_norm_bwd(of, od, dy, W["gof"][l], W["god"][l], "attn_norm_bwd" + t)
        dgvgg, dcw, dsm = _conv_bwd(P, c0, dy, convp[0], convp[2], convp[3], nb, "conv_bwd" + t)
        dqa, dka, dva, dcb = _fox_bwd(P, c, of, lsef, dof, nb, "fox_bwd" + t)
        dfa, dbf = _forget_bwd(P, W["bf"][l], dcb, nb, "forget_bwd" + t)
        dqb, dkb, dvb = _dil_bwd(P, od, lsed, dod, nb, "dil_bwd" + t)
        dP = jnp.concatenate([dqa, dka, dva, dqb, dkb, dvb, dgvgg, dfa, jnp.zeros_like(dfa)], axis=1)
        dh = _matmul(dP, W["in"][l], tb=True, out_dtype=BF16, tm=1024, tn=D, tk=NP, name="proj_in_dx" + t)
        dWin = _matmul(h, dP, ta=True, tm=D, tn=1024, tk=2048, name="proj_in_dw" + t)
        dx, dxb, dln1 = _rms_bwd(xin, W["ln1"][l], dh, dxm, "rms1_bwd" + t)
        big[l] = (dWin, dWo, dWup, dWd)
        small[l] = (dln1, dbf, dgo, dcw, dsm, dln2, dfw)
    return loss8, dx, big, small, dgfin


_SMALL_ROWS = (D // LANE, 8, 8 * WA // LANE, CPAD * CC // LANE, 8 * CC // LANE, D // LANE, 8 * 2 * DFF // LANE)


def kernel(x, ln1_g, w_in, b_forget, g_out_fox, g_out_dil, conv_w, conv_b, cnorm_g, cnorm_b, w_o, ln2_g, w_up, ffn_conv_w, ffn_conv_b, w_down, g_final, loss_target, m_ln1_g, m_w_in, m_b_forget, m_g_out_fox, m_g_out_dil, m_conv_w, m_conv_b, m_cnorm_g, m_cnorm_b, m_w_o, m_ln2_g, m_w_up, m_ffn_conv_w, m_ffn_conv_b, m_w_down, m_g_final, v_ln1_g, v_w_in, v_b_forget, v_g_out_fox, v_g_out_dil, v_conv_w, v_conv_b, v_cnorm_g, v_cnorm_b, v_w_o, v_ln2_g, v_w_up, v_ffn_conv_w, v_ffn_conv_b, v_w_down, v_g_final):
    names = ("ln1_g", "w_in", "b_forget", "g_out_fox", "g_out_dil", "conv_w", "conv_b", "cnorm_g", "cnorm_b",
             "w_o", "ln2_g", "w_up", "ffn_conv_w", "ffn_conv_b", "w_down", "g_final")
    w = dict(zip(names, (ln1_g, w_in, b_forget, g_out_fox, g_out_dil, conv_w, conv_b, cnorm_g, cnorm_b,
                         w_o, ln2_g, w_up, ffn_conv_w, ffn_conv_b, w_down, g_final)))
    m = dict(zip(names, (m_ln1_g, m_w_in, m_b_forget, m_g_out_fox, m_g_out_dil, m_conv_w, m_conv_b, m_cnorm_g,
                         m_cnorm_b, m_w_o, m_ln2_g, m_w_up, m_ffn_conv_w, m_ffn_conv_b, m_w_down, m_g_final)))
    v = dict(zip(names, (v_ln1_g, v_w_in, v_b_forget, v_g_out_fox, v_g_out_dil, v_conv_w, v_conv_b, v_cnorm_g,
                         v_cnorm_b, v_w_o, v_ln2_g, v_w_up, v_ffn_conv_w, v_ffn_conv_b, v_w_down, v_g_final)))
    nb = x.shape[0]
    T = nb * S
    xi, yi, ci = lax.axis_index("x"), lax.axis_index("y"), lax.axis_index("c")
    chip = 2 * xi + yi
    cw_cols = CC // NCHIP
    up_cols = 2 * DFF // NCHIP

    shards = [_pack_w_in(w_in), w_o.astype(BF16), w_up.astype(BF16), w_down.astype(BF16),
              jnp.pad(ffn_conv_w, ((0, 0), (0, 8 - FK), (0, 0))),
              jnp.pad(conv_w, ((0, 0), (0, CPAD - CK), (0, LANE - cw_cols)))]
    g_in, g_o, g_up, g_dn, g_fw, g_cw = _allgather(shards, (True, True, True, True, False, False))
    fb_full = _interleave(ffn_conv_b)
    W = {
        "in": [g.reshape(D, NP) for g in g_in],
        "o": [g.reshape(D, D) for g in g_o],
        "up": [g.transpose(1, 0, 2).reshape(D, 2 * DFF) for g in g_up],
        "down": [g.reshape(DFF, D) for g in g_dn],
        "ln1": [ln1_g[l] for l in range(DEPTH)],
        "ln2": [ln2_g[l] for l in range(DEPTH)],
        "bf": [jnp.pad(b_forget[l], (0, LANE - N_FG)).reshape(1, LANE) for l in range(DEPTH)],
        "gof": [g_out_fox[l].reshape(1, WA) for l in range(DEPTH)],
        "god": [g_out_dil[l].reshape(1, WA) for l in range(DEPTH)],
        "cw": [g[..., :cw_cols].transpose(1, 0, 2).reshape(CPAD, CC) for g in g_cw],
        "cb": [conv_b[l].reshape(1, CC) for l in range(DEPTH)],
        "cng": [cnorm_g[l].reshape(1, CC) for l in range(DEPTH)],
        "cnb": [cnorm_b[l].reshape(1, CC) for l in range(DEPTH)],
        "fw": [_interleave(g.transpose(1, 0, 2).reshape(8, 2 * DFF)) for g in g_fw],
        "fb": [fb_full[l].reshape(1, 2 * DFF) for l in range(DEPTH)],
        "gfin": g_final,
    }

    loss8, dx, big, small, dgfin = _train_compute(x.reshape(T, D), loss_target.reshape(T, D), W, nb)

    gs = []
    for l in range(DEPTH):
        dWin, dWo, dWup, dWd = big[l]
        gs += [dWin.reshape(NCHIP, D // NCHIP, NP), dWo.reshape(NCHIP, D // NCHIP, D),
               jnp.stack([half[:, i * up_cols:(i + 1) * up_cols] for half in dWup for i in range(2)]),
               dWd.reshape(NCHIP, DFF // NCHIP, D)]
    r1 = _rs_pair_exchange(gs)
    place = jnp.stack([chip, ci]).astype(jnp.int32)
    hs = [_add_half(g, r, place, f"rs_add_pair_{i}") for i, (g, r) in enumerate(zip(gs, r1))]
    r2 = _rs_chip_scatter(hs)
    fs = [_sum_slots(g, a, b, place, f"rs_add_chips_{i}") for i, (g, a, b) in enumerate(zip(gs, r1, r2))]
    red = _rs_pair_gather(fs)
    red = [r.reshape(r.shape[0] * r.shape[1], r.shape[2]) for r in red]
    grads, delta, new_m, new_v = {}, {}, {}, {}
    for i, n in enumerate(("w_in", "w_o", "w_up", "w_down")):
        grads[n], delta[n], new_m[n], new_v[n] = _adamw_layers(
            w[n], [red[4 * l + i] for l in range(DEPTH)], m[n], v[n], "adamw_" + n, packed=(n == "w_in"))

    parts = []
    for l in range(DEPTH):
        parts += [p.reshape(-1, LANE) for p in small[l]]
    parts += [dgfin.reshape(-1, LANE), loss8]
    tot = _allreduce_small(jnp.concatenate(parts, axis=0))
    off = 0
    per_layer = []
    for l in range(DEPTH):
        vals = []
        for rows in _SMALL_ROWS:
            vals.append(tot[off:off + rows])
            off += rows
        per_layer.append(vals)
    gfin_sum = tot[off:off + D // LANE].reshape(D)
    loss = tot[off + D // LANE, 0]

    def layer_stack(fn):
        return jnp.stack([fn(*per_layer[l]) for l in range(DEPTH)])

    fw_sum = layer_stack(lambda a, b, c_, d, e, f, g: _uninterleave(g.reshape(8, 2 * DFF)))
    cw_sum = layer_stack(lambda a, b, c_, d, e, f, g: d.reshape(CPAD, CC)[:CK])
    sm_sum = layer_stack(lambda a, b, c_, d, e, f, g: e.reshape(8, CC))
    go_sum = layer_stack(lambda a, b, c_, d, e, f, g: c_.reshape(8, WA))
    grads.update({
        "ln1_g": layer_stack(lambda a, b, c_, d, e, f, g: a.reshape(D)),
        "b_forget": layer_stack(lambda a, b, c_, d, e, f, g: b[0, :N_FG]),
        "g_out_fox": go_sum[:, 0],
        "g_out_dil": go_sum[:, 1],
        "conv_w": lax.dynamic_slice_in_dim(cw_sum, chip * cw_cols, cw_cols, axis=2),
        "conv_b": sm_sum[:, 0],
        "cnorm_g": sm_sum[:, 1],
        "cnorm_b": sm_sum[:, 2],
        "ln2_g": layer_stack(lambda a, b, c_, d, e, f, g: f.reshape(D)),
        "ffn_conv_w": lax.dynamic_slice_in_dim(fw_sum[:, :FK], chip * up_cols, up_cols, axis=2),
        "ffn_conv_b": fw_sum[:, FK],
        "g_final": gfin_sum,
    })

    for n in names:
        if n not in delta:
            delta[n], new_m[n], new_v[n] = _adamw(w[n], grads[n], m[n], v[n], "adamw_" + n)
    return (loss, dx.reshape(nb, S, D), *[grads[n] for n in names], *[delta[n] for n in names],
            *[new_m[n] for n in names], *[new_v[n] for n in names])
```

```python
import functools

import jax
import jax.numpy as jnp
from jax import lax
from jax.experimental import pallas as pl
from jax.experimental.pallas import tpu as pltpu

F32 = jnp.float32
BF16 = jnp.bfloat16

D = 1024
S = 2048
DEPTH = 2
HD = 64
WA = 384
NHP = 3
CC = 256
CK = 31
FK = 3
DFF = 2816
NIN = 2822
NP = 3072
SCALE = 0.125
EPS = 1e-6
NEG = -1e30
NCHIP = 4
NDEV = 8
LANE = 128

CB_FOX, CB_DIL = 0, 9
CB_GV, CB_GG = 9, 10
CB_FA = 22

ADAM_LR, ADAM_B1, ADAM_B2, ADAM_EPS, ADAM_WD, ADAM_STEP = 0.001, 0.9, 0.999, 1e-08, 0.01, 10

N_QKV = 3 * WA
N_FG = 2 * NHP

VMEM_LIMIT = 56 * 1024 * 1024


def _cparams(sem=None):
    return pltpu.CompilerParams(dimension_semantics=sem, vmem_limit_bytes=VMEM_LIMIT)


def _split3(x):
    hi = x.astype(BF16)
    r1 = x - hi.astype(F32)
    mid = r1.astype(BF16)
    lo = (r1 - mid.astype(F32)).astype(BF16)
    return hi, mid, lo


def _sum8(x):
    r, c = x.shape
    return jnp.sum(x.reshape(r // 8, 8, c), axis=0)


def _sigmoid(z):
    return 0.5 * jnp.tanh(0.5 * z) + 0.5


def _matmul(a, b, *, ta=False, tb=False, out_dtype=F32, add=None, tm, tn, tk, name):
    M = a.shape[1] if ta else a.shape[0]
    K = a.shape[0] if ta else a.shape[1]
    N = b.shape[0] if tb else b.shape[1]
    assert (b.shape[1] if tb else b.shape[0]) == K
    assert M % tm == 0 and N % tn == 0 and K % tk == 0, (M, N, K, tm, tn, tk)
    nk = K // tk
    dn = (((0 if ta else 1,), (1 if tb else 0,)), ((), ()))

    def body(*refs):
        if add is not None:
            a_ref, b_ref, add_ref, o_ref, acc = refs
        else:
            a_ref, b_ref, o_ref, acc = refs
        k = pl.program_id(2)
        prod = lax.dot_general(a_ref[...].astype(BF16), b_ref[...].astype(BF16), dn, preferred_element_type=F32)

        def finish(r):
            if add is not None:
                r = r + add_ref[...]
            o_ref[...] = r.astype(o_ref.dtype)

        if nk == 1:
            finish(prod)
        else:
            @pl.when(k == 0)
            def _():
                acc[...] = prod

            @pl.when(k > 0)
            def _():
                acc[...] += prod

            @pl.when(k == nk - 1)
            def _():
                finish(acc[...])

    a_spec = pl.BlockSpec((tk, tm), lambda i, j, k: (k, i)) if ta else pl.BlockSpec((tm, tk), lambda i, j, k: (i, k))
    b_spec = pl.BlockSpec((tn, tk), lambda i, j, k: (j, k)) if tb else pl.BlockSpec((tk, tn), lambda i, j, k: (k, j))
    o_spec = pl.BlockSpec((tm, tn), lambda i, j, k: (i, j))
    in_specs = [a_spec, b_spec]
    args = [a, b]
    if add is not None:
        in_specs.append(o_spec)
        args.append(add)
    return pl.pallas_call(
        body, name=name, grid=(M // tm, N // tn, nk),
        in_specs=in_specs, out_specs=o_spec,
        out_shape=jax.ShapeDtypeStruct((M, N), out_dtype),
        scratch_shapes=[pltpu.VMEM((tm, tn) if nk > 1 else (8, 128), F32)],
        compiler_params=_cparams(("parallel", "parallel", "arbitrary")),
    )(*args)


def _rms_rows(xv, g_ref):
    r = lax.rsqrt(jnp.mean(xv * xv, axis=1, keepdims=True) + EPS)
    return (xv * r * g_ref[...]).astype(BF16)


def _rms_matmul(x, g, b, *, tm, name):
    T, K = x.shape
    N = b.shape[1]

    def body(x_ref, g_ref, b_ref, h_ref, o_ref):
        h = _rms_rows(x_ref[...], g_ref)
        h_ref[...] = h
        o_ref[...] = jnp.dot(h, b_ref[...], preferred_element_type=F32)

    return pl.pallas_call(
        body, name=name, grid=(T // tm,),
        in_specs=[pl.BlockSpec((tm, K), lambda i: (i, 0)), pl.BlockSpec((1, K), lambda i: (0, 0)),
                  pl.BlockSpec((K, N), lambda i: (0, 0))],
        out_specs=[pl.BlockSpec((tm, K), lambda i: (i, 0)), pl.BlockSpec((tm, N), lambda i: (i, 0))],
        out_shape=[jax.ShapeDtypeStruct((T, K), BF16), jax.ShapeDtypeStruct((T, N), F32)],
        compiler_params=_cparams(("parallel",)),
    )(x, g.reshape(1, K), b)


def _rms_bwd(x, g, dh, dres, name):
    T = x.shape[0]
    tr = 512

    def body(x_ref, g_ref, dh_ref, dres_ref, dx_ref, dxb_ref, dg_ref):
        i = pl.program_id(0)
        xv = x_ref[...]
        dhv = dh_ref[...].astype(F32)
        r = lax.rsqrt(jnp.mean(xv * xv, axis=1, keepdims=True) + EPS)
        a = dhv * g_ref[...]
        dx = dres_ref[...] + r * a - xv * (r * r * r * jnp.mean(xv * a, axis=1, keepdims=True))
        dx_ref[...] = dx
        dxb_ref[...] = dx.astype(BF16)
        part = jnp.sum(dhv * xv * r, axis=0, keepdims=True)

        @pl.when(i == 0)
        def _():
            dg_ref[...] = part

        @pl.when(i > 0)
        def _():
            dg_ref[...] += part

    row = pl.BlockSpec((tr, D), lambda i: (i, 0))
    vec = pl.BlockSpec((1, D), lambda i: (0, 0))
    return pl.pallas_call(
        body, name=name, grid=(T // tr,),
        in_specs=[row, vec, row, row], out_specs=[row, row, vec],
        out_shape=[jax.ShapeDtypeStruct((T, D), F32), jax.ShapeDtypeStruct((T, D), BF16),
                   jax.ShapeDtypeStruct((1, D), F32)],
        compiler_params=_cparams(("arbitrary",)),
    )(x, g.reshape(1, D), dh, dres)


def _loss_head(x, g, target, name):
    T = x.shape[0]
    tr = 512

    def body(x_ref, g_ref, t_ref, loss_ref, dx_ref, dxb_ref, dg_ref):
        i = pl.program_id(0)
        xv = x_ref[...]
        gv = g_ref[...]
        r = lax.rsqrt(jnp.mean(xv * xv, axis=1, keepdims=True) + EPS)
        n = xv * r
        err = n * gv - t_ref[...]
        lpart = 0.5 * jnp.sum(jnp.mean(err * err, axis=1, keepdims=True), axis=0, keepdims=True)
        dy = err * (1.0 / D)
        a = dy * gv
        dx = r * a - xv * (r * r * r * jnp.mean(xv * a, axis=1, keepdims=True))
        dx_ref[...] = dx
        dxb_ref[...] = dx.astype(BF16)
        part = jnp.sum(dy * n, axis=0, keepdims=True)
        lfull = jnp.broadcast_to(lpart, (8, LANE))

        @pl.when(i == 0)
        def _():
            dg_ref[...] = part
            loss_ref[...] = lfull

        @pl.when(i > 0)
        def _():
            dg_ref[...] += part
            loss_ref[...] += lfull

    row = pl.BlockSpec((tr, D), lambda i: (i, 0))
    vec = pl.BlockSpec((1, D), lambda i: (0, 0))
    lsp = pl.BlockSpec((8, LANE), lambda i: (0, 0))
    return pl.pallas_call(
        body, name=name, grid=(T // tr,),
        in_specs=[row, vec, row], out_specs=[lsp, row, row, vec],
        out_shape=[jax.ShapeDtypeStruct((8, LANE), F32), jax.ShapeDtypeStruct((T, D), F32),
                   jax.ShapeDtypeStruct((T, D), BF16), jax.ShapeDtypeStruct((1, D), F32)],
        compiler_params=_cparams(("arbitrary",)),
    )(x, g.reshape(1, D), target)


CUM_BLK = 256


def _tri(n, upper):
    r = lax.broadcasted_iota(jnp.int32, (n, n), 0)
    c = lax.broadcasted_iota(jnp.int32, (n, n), 1)
    return jnp.where((c >= r) if upper else (c <= r), 1.0, 0.0).astype(BF16)


def _tri_apply(tri, x):
    hi, mid, lo = _split3(x)
    out = jnp.dot(tri, hi, preferred_element_type=F32)
    out = out + jnp.dot(tri, mid, preferred_element_type=F32)
    return out + jnp.dot(tri, lo, preferred_element_type=F32)


def _forget_fwd(P, bf_pad, nb, name):
    nblk = S // CUM_BLK

    def body(fa_ref, b_ref, c_ref):
        tri = _tri(CUM_BLK, upper=False)
        carry = jnp.zeros((1, LANE), F32)
        for i in range(nblk):
            z = fa_ref[pl.ds(i * CUM_BLK, CUM_BLK), :] + b_ref[...]
            lf = jnp.minimum(z, 0.0) - jnp.log(1.0 + jnp.exp(-jnp.abs(z)))
            cb = _tri_apply(tri, lf) + carry
            c_ref[pl.ds(i * CUM_BLK, CUM_BLK), :] = cb
            carry = cb[CUM_BLK - 1:CUM_BLK, :]

    return pl.pallas_call(
        body, name=name, grid=(nb,),
        in_specs=[pl.BlockSpec((S, LANE), lambda b: (b, CB_FA)), pl.BlockSpec((1, LANE), lambda b: (0, 0))],
        out_specs=pl.BlockSpec((S, LANE), lambda b: (b, 0)),
        out_shape=jax.ShapeDtypeStruct((nb * S, LANE), F32),
        compiler_params=_cparams(("parallel",)),
    )(P, bf_pad)


def _forget_bwd(P, bf_pad, dcb, nb, name):
    nblk = S // CUM_BLK

    def body(fa_ref, b_ref, dc_ref, dfa_ref, db_ref):
        b = pl.program_id(0)
        tri = _tri(CUM_BLK, upper=True)
        lane = lax.broadcasted_iota(jnp.int32, (CUM_BLK, LANE), 1)
        carry = jnp.zeros((1, LANE), F32)
        dbacc = jnp.zeros((1, LANE), F32)
        for i in reversed(range(nblk)):
            rows = pl.ds(i * CUM_BLK, CUM_BLK)
            dc = jnp.zeros((CUM_BLK, LANE), F32)
            dcv = dc_ref[rows, :]
            for h in range(2 * NHP):
                dc = jnp.where(lane == h, -dcv[:, HD * h:HD * h + 1], dc)
            dl = _tri_apply(tri, dc) + carry
            carry = dl[0:1, :]
            z = fa_ref[rows, :] + b_ref[...]
            dz = jnp.where(lane < 2 * NHP, dl * (1.0 - _sigmoid(z)), 0.0)
            dfa_ref[rows, :] = dz.astype(BF16)
            dbacc = dbacc + jnp.sum(dz, axis=0, keepdims=True)

        dbfull = jnp.broadcast_to(dbacc, (8, LANE))

        @pl.when(b == 0)
        def _():
            db_ref[...] = dbfull

        @pl.when(b > 0)
        def _():
            db_ref[...] += dbfull

    return pl.pallas_call(
        body, name=name, grid=(nb,),
        in_specs=[pl.BlockSpec((S, LANE), lambda b: (b, CB_FA)), pl.BlockSpec((1, LANE), lambda b: (0, 0)),
                  pl.BlockSpec((S, WA), lambda b: (b, 0))],
        out_specs=[pl.BlockSpec((S, LANE), lambda b: (b, 0)), pl.BlockSpec((8, LANE), lambda b: (0, 0))],
        out_shape=[jax.ShapeDtypeStruct((nb * S, LANE), BF16), jax.ShapeDtypeStruct((8, LANE), F32)],
        compiler_params=_cparams(("arbitrary",)),
    )(P, bf_pad, dcb)


FQ = 256
NT_DIMS = (((1,), (1,)), ((), ()))
AUGW = 6


def _qkv_blk(base, i):
    return pl.BlockSpec((S, LANE), lambda b, hp: (b, base + 3 * hp + i))


def _dqkv_blk(base):
    return pl.BlockSpec((S, 3 * LANE), lambda b, hp: (b, base // 3 + hp))


def _head_masks(shape):
    lane = lax.broadcasted_iota(jnp.int32, shape, 1)
    return lane < HD, lane >= HD


def _fox_bias_terms(c_ref, hp):
    lane = lax.broadcasted_iota(jnp.int32, (S, LANE), 1)
    cv = c_ref[...]
    return [_split3(jnp.sum(jnp.where(lane == 2 * hp + e, cv, 0.0), axis=1, keepdims=True)) for e in range(2)]


def _fox_ext(x, terms, side, heads, only):
    lane = lax.broadcasted_iota(jnp.int32, (S, LANE), 1)
    one = jnp.ones((S, 1), BF16)
    aug = jnp.zeros((S, LANE), BF16)
    for e in heads:
        hi, mid, lo = terms[e]
        cols = (hi, mid, lo, one, one, one) if side == "q" else (one, one, one, -hi, -mid, -lo)
        for i, col in enumerate(cols):
            aug = jnp.where(lane == AUGW * e + i, col, aug)
    if only is not None:
        x = jnp.where(_head_masks((S, LANE))[only], x, jnp.zeros_like(x))
    return jnp.concatenate([x, aug], axis=1)


def _halves(x, lane_mask):
    return jnp.where(lane_mask, x[0:FQ, 0:LANE], x[FQ:2 * FQ, 0:LANE])


def _fox_fwd(P, c, nb, name):
    def body(q_ref, k_ref, v_ref, c_ref, o_ref, lse_ref, qm0, qm1, kx, vx):
        hp = pl.program_id(1)
        terms = _fox_bias_terms(c_ref, hp)
        qv = (q_ref[...] * SCALE).astype(BF16)
        qm0[...] = _fox_ext(qv, terms, "q", (0,), 0)
        qm1[...] = _fox_ext(qv, terms, "q", (1,), 1)
        kx[...] = _fox_ext(k_ref[...].astype(BF16), terms, "k", (0, 1), None)
        lane = lax.broadcasted_iota(jnp.int32, (S, LANE), 1)
        vx[...] = jnp.concatenate([v_ref[...].astype(BF16), jnp.where(lane == 0, 1.0, 0.0).astype(BF16)], axis=1)
        tmask = _head_masks((FQ, LANE))[0]
        row = lax.broadcasted_iota(jnp.int32, (2 * FQ, FQ), 0) & (FQ - 1)
        col = lax.broadcasted_iota(jnp.int32, (2 * FQ, FQ), 1)
        for i in range(S // FQ):
            r0 = i * FQ
            qt = jnp.concatenate([qm0[pl.ds(r0, FQ), :], qm1[pl.ds(r0, FQ), :]], axis=0)
            sd = lax.dot_general(qt, kx[pl.ds(r0, FQ), :], NT_DIMS, preferred_element_type=F32)
            sd = jnp.where(col <= row, sd, NEG)
            m = jnp.max(sd, axis=1, keepdims=True)
            if i > 0:
                so = lax.dot_general(qt, kx[pl.ds(0, r0), :], NT_DIMS, preferred_element_type=F32)
                m = jnp.maximum(m, jnp.max(so, axis=1, keepdims=True))
            acc = jnp.dot(jnp.exp(sd - m).astype(BF16), vx[pl.ds(r0, FQ), :], preferred_element_type=F32)
            if i > 0:
                acc = acc + jnp.dot(jnp.exp(so - m).astype(BF16), vx[pl.ds(0, r0), :], preferred_element_type=F32)
            l = acc[:, LANE:LANE + 1]
            o_ref[pl.ds(r0, FQ), :] = _halves(acc / l, tmask)
            lse_ref[pl.ds(r0, FQ), :] = _halves(jnp.broadcast_to(m + jnp.log(l), (2 * FQ, LANE)), tmask)

    def colblk(off):
        return pl.BlockSpec((S, LANE), lambda b, hp: (b, off + hp))

    return pl.pallas_call(
        body, name=name, grid=(nb, NHP),
        in_specs=[_qkv_blk(CB_FOX, 0), _qkv_blk(CB_FOX, 1), _qkv_blk(CB_FOX, 2),
                  pl.BlockSpec((S, LANE), lambda b, hp: (b, 0))],
        out_specs=[colblk(0), colblk(0)],
        out_shape=[jax.ShapeDtypeStruct((nb * S, WA), F32), jax.ShapeDtypeStruct((nb * S, WA), F32)],
        scratch_shapes=[pltpu.VMEM((S, 2 * LANE), BF16)] * 4,
        compiler_params=_cparams(("parallel", "parallel")),
    )(P, P, P, c)


def _fox_bwd(P, c, o, lse, do, nb, name):
    def body(q_ref, k_ref, v_ref, c_ref, o_ref, lse_ref, do_ref, dp_ref, dc_ref,
             km0, km1, qx, vm0, vm1, dob, kt0, kt1, rows, dqt, rsum):
        hp = pl.program_id(1)
        terms = _fox_bias_terms(c_ref, hp)
        kv = k_ref[...].astype(BF16)
        km0[...] = _fox_ext(kv, terms, "k", (0,), 0)
        km1[...] = _fox_ext(kv, terms, "k", (1,), 1)
        qx[...] = _fox_ext((q_ref[...] * SCALE).astype(BF16), terms, "q", (0, 1), None)
        masks = _head_masks((S, LANE))
        vv = v_ref[...].astype(BF16)
        zero = jnp.zeros((S, LANE), BF16)
        vm0[...] = jnp.where(masks[0], vv, zero)
        vm1[...] = jnp.where(masks[1], vv, zero)
        dov = do_ref[...]
        dob[...] = dov.astype(BF16)
        ktf = k_ref[...].T
        prodt = (dov * o_ref[...]).T
        lset = lse_ref[...].T
        hrow = lax.broadcasted_iota(jnp.int32, (LANE, S), 0)
        kt0[...] = jnp.where(hrow < HD, ktf, 0.0).astype(BF16)
        kt1[...] = jnp.where(hrow >= HD, ktf, 0.0).astype(BF16)
        for e in range(2):
            rows[e:e + 1, :] = lset[HD * e:HD * e + 1, :]
            rows[2 + e:3 + e, :] = jnp.sum(prodt[HD * e:HD * (e + 1), :], axis=0, keepdims=True)
        dqt[...] = jnp.zeros_like(dqt)
        rsum[...] = jnp.zeros_like(rsum)
        tmask = _head_masks((FQ, LANE))[0]
        row = lax.broadcasted_iota(jnp.int32, (2 * FQ, FQ), 0) & (FQ - 1)
        col = lax.broadcasted_iota(jnp.int32, (2 * FQ, FQ), 1)
        for j in range(S // FQ):
            k0 = j * FQ
            rest = S - k0 - FQ
            spans = [(k0, FQ)] + ([(k0 + FQ, rest)] if rest > 0 else [])
            kte = jnp.concatenate([km0[pl.ds(k0, FQ), :], km1[pl.ds(k0, FQ), :]], axis=0)
            vte = jnp.concatenate([vm0[pl.ds(k0, FQ), :], vm1[pl.ds(k0, FQ), :]], axis=0)
            ktt = jnp.concatenate([kt0[:, pl.ds(k0, FQ)], kt1[:, pl.ds(k0, FQ)]], axis=1)
            dke = jnp.zeros((2 * FQ, 2 * LANE), F32)
            dve = jnp.zeros((2 * FQ, LANE), F32)
            cse = jnp.zeros((2 * FQ, 1), F32)
            for si, (q0, n) in enumerate(spans):
                qs = qx[pl.ds(q0, n), :]
                dos = dob[pl.ds(q0, n), :]
                st = lax.dot_general(kte, qs, NT_DIMS, preferred_element_type=F32)
                if si == 0:
                    st = jnp.where(col >= row, st, NEG)
                dpt = lax.dot_general(vte, dos, NT_DIMS, preferred_element_type=F32)
                pts, dsts = [], []
                for e in range(2):
                    pe = jnp.exp(st[FQ * e:FQ * (e + 1), :] - rows[e:e + 1, pl.ds(q0, n)])
                    de = pe * (dpt[FQ * e:FQ * (e + 1), :] - rows[2 + e:3 + e, pl.ds(q0, n)])
                    rsum[HD * e:HD * e + 8, pl.ds(q0, n)] += _sum8(de)
                    pts.append(pe)
                    dsts.append(de)
                pt = jnp.concatenate(pts, axis=0)
                dst = jnp.concatenate(dsts, axis=0)
                dsb = dst.astype(BF16)
                dve = dve + jnp.dot(pt.astype(BF16), dos, preferred_element_type=F32)
                dke = dke + jnp.dot(dsb, qs, preferred_element_type=F32)
                dqt[:, pl.ds(q0, n)] += jnp.dot(ktt, dsb, preferred_element_type=F32)
                cse = cse + jnp.sum(dst, axis=1, keepdims=True)
            dp_ref[pl.ds(k0, FQ), LANE:2 * LANE] = _halves(dke, tmask).astype(BF16)
            dp_ref[pl.ds(k0, FQ), 2 * LANE:3 * LANE] = _halves(dve, tmask).astype(BF16)
            dc_ref[pl.ds(k0, FQ), :] = _halves(jnp.broadcast_to(cse, (2 * FQ, LANE)), tmask)
        dp_ref[:, 0:LANE] = (dqt[...].T * SCALE).astype(BF16)
        tot = [jnp.sum(rsum[HD * e:HD * e + 8, :], axis=0, keepdims=True) for e in range(2)]
        dc_ref[...] = dc_ref[...] - jnp.where(hrow == 0, tot[0], jnp.where(hrow == HD, tot[1], 0.0)).T

    def colblk(off):
        return pl.BlockSpec((S, LANE), lambda b, hp: (b, off + hp))

    wide = pltpu.VMEM((S, 2 * LANE), BF16)
    half = pltpu.VMEM((S, LANE), BF16)
    return pl.pallas_call(
        body, name=name, grid=(nb, NHP),
        in_specs=[_qkv_blk(CB_FOX, 0), _qkv_blk(CB_FOX, 1), _qkv_blk(CB_FOX, 2),
                  pl.BlockSpec((S, LANE), lambda b, hp: (b, 0)), colblk(0), colblk(0), colblk(0)],
        out_specs=[_dqkv_blk(CB_FOX), colblk(0)],
        out_shape=[jax.ShapeDtypeStruct((nb * S, NP), BF16), jax.ShapeDtypeStruct((nb * S, WA), F32)],
        scratch_shapes=[wide, wide, wide, half, half, half, pltpu.VMEM((LANE, S), BF16), pltpu.VMEM((LANE, S), BF16),
                        pltpu.VMEM((8, S), F32), pltpu.VMEM((LANE, S), F32), pltpu.VMEM((LANE, S), F32)],
        compiler_params=_cparams(("parallel", "parallel")),
    )(P, P, P, c, o, lse, do)


DILS = (1, 4, 16)
DB = 128


def _regroup_load(ref, d, scale=None):
    if d == 1:
        v = ref[...]
    else:
        L = S // d
        v = jnp.concatenate([ref[pl.ds(r, L, stride=d), :] for r in range(d)], axis=0)
    return v if scale is None else v * scale


def _regroup_store(ref, d, val_ref, accumulate):
    L = S // d
    for r in range(d):
        src = val_ref[pl.ds(r * L, L), :]
        dst = (slice(None), slice(None)) if d == 1 else (pl.ds(r, L, stride=d), slice(None))
        if accumulate:
            ref[dst] = ref[dst] + src
        else:
            ref[dst] = src


def _dil_bands():
    qi = lax.broadcasted_iota(jnp.int32, (DB, 2 * DB), 0)
    ki = lax.broadcasted_iota(jnp.int32, (DB, 2 * DB), 1)
    band = (ki >= qi) & (ki <= qi + DB)
    return band, band & (ki >= DB)


def _dil_valid(bands, bk, d):
    band, own = bands
    has_prev = (bk % ((S // d) // DB)) > 0
    return own | (band & has_prev)


def _dil_keys(kd, vd, r0, bands, bk, d):
    if S // d == DB:
        valid = bands[1][:, DB:]
        kk, vv = kd[pl.ds(r0 + DB, DB), :], vd[pl.ds(r0 + DB, DB), :]
    else:
        valid = _dil_valid(bands, bk, d)
        kk, vv = kd[pl.ds(r0, 2 * DB), :], vd[pl.ds(r0, 2 * DB), :]
    return kk, vv, jnp.concatenate([valid, valid], axis=0)


def _stack_heads(x, masks):
    zero = jnp.zeros_like(x)
    return jnp.concatenate([jnp.where(masks[0], x, zero), jnp.where(masks[1], x, zero)], axis=0)


def _dil_fwd(P, nb, name):
    nblk = S // DB

    def body(q_ref, k_ref, v_ref, o_ref, lse_ref, qd, kd, vd, rnum, rm, rl, num_n, m_n, l_n):
        masks = _head_masks((DB, LANE))
        bands = _dil_bands()
        for bi, d in enumerate(DILS):
            qd[...] = _regroup_load(q_ref, d, SCALE).astype(BF16)
            kd[pl.ds(0, DB), :] = jnp.zeros((DB, LANE), BF16)
            vd[pl.ds(0, DB), :] = jnp.zeros((DB, LANE), BF16)
            kd[pl.ds(DB, S), :] = _regroup_load(k_ref, d).astype(BF16)
            vd[pl.ds(DB, S), :] = _regroup_load(v_ref, d).astype(BF16)

            def blk(bk, _, d=d):
                r0 = pl.multiple_of(bk * DB, DB)
                qt = qd[pl.ds(r0, DB), :]
                kk, vv, valid = _dil_keys(kd, vd, r0, bands, bk, d)
                qm = _stack_heads(qt, masks)
                s = lax.dot_general(qm, kk, NT_DIMS, preferred_element_type=F32)
                s = jnp.where(valid, s, NEG)
                m = jnp.max(s, axis=1, keepdims=True)
                p = jnp.exp(s - m)
                l = jnp.sum(p, axis=1, keepdims=True)
                num = jnp.dot(p.astype(BF16), vv, preferred_element_type=F32)
                rnum[pl.ds(r0, DB), :] = jnp.where(masks[0], num[0:DB], num[DB:2 * DB])
                rm[pl.ds(r0, DB), :] = jnp.where(masks[0], m[0:DB], m[DB:2 * DB])
                rl[pl.ds(r0, DB), :] = jnp.where(masks[0], l[0:DB], l[DB:2 * DB])
                return 0

            lax.fori_loop(0, nblk, blk, 0, unroll=8)
            _regroup_store(num_n.at[bi], d, rnum, False)
            _regroup_store(m_n.at[bi], d, rm, False)
            _regroup_store(l_n.at[bi], d, rl, False)

        m_all = jnp.maximum(jnp.maximum(m_n[0], m_n[1]), m_n[2])
        num = jnp.zeros((S, LANE), F32)
        den = jnp.zeros((S, LANE), F32)
        for bi in range(3):
            a = jnp.exp(m_n[bi] - m_all)
            num = num + a * num_n[bi]
            den = den + a * l_n[bi]
        o_ref[...] = num / den
        lse_ref[...] = m_all + jnp.log(den)

    def colblk(off):
        return pl.BlockSpec((S, LANE), lambda b, hp: (b, off + hp))

    return pl.pallas_call(
        body, name=name, grid=(nb, NHP),
        in_specs=[_qkv_blk(CB_DIL, 0), _qkv_blk(CB_DIL, 1), _qkv_blk(CB_DIL, 2)],
        out_specs=[colblk(0), colblk(0)],
        out_shape=[jax.ShapeDtypeStruct((nb * S, WA), F32), jax.ShapeDtypeStruct((nb * S, WA), F32)],
        scratch_shapes=[pltpu.VMEM((S, LANE), BF16), pltpu.VMEM((S + DB, LANE), BF16), pltpu.VMEM((S + DB, LANE), BF16),
                        pltpu.VMEM((S, LANE), F32), pltpu.VMEM((S, LANE), F32), pltpu.VMEM((S, LANE), F32),
                        pltpu.VMEM((3, S, LANE), F32), pltpu.VMEM((3, S, LANE), F32), pltpu.VMEM((3, S, LANE), F32)],
        compiler_params=_cparams(("parallel", "parallel")),
    )(P, P, P)


def _dil_bwd(P, o, lse, do, dP, nb, name):
    nblk = S // DB

    def body(q_ref, k_ref, v_ref, o_ref, lse_ref, do_ref, dp_in, dp_ref,
             qd, kd, vd, dod, lsed, dsd, dsum, dq_r, dk_r, dv_r, dq_n, dk_n, dv_n):
        del dp_in
        masks = _head_masks((DB, LANE))
        fmask = _head_masks((S, LANE))
        prod = do_ref[...] * o_ref[...]
        d0 = jnp.sum(jnp.where(fmask[0], prod, 0.0), axis=1, keepdims=True)
        d1 = jnp.sum(jnp.where(fmask[1], prod, 0.0), axis=1, keepdims=True)
        dsum[...] = jnp.where(fmask[0], d0, d1)
        tn = (((0,), (0,)), ((), ()))
        bands = _dil_bands()
        for bi, d in enumerate(DILS):
            qd[...] = _regroup_load(q_ref, d, SCALE).astype(BF16)
            kd[pl.ds(0, DB), :] = jnp.zeros((DB, LANE), BF16)
            vd[pl.ds(0, DB), :] = jnp.zeros((DB, LANE), BF16)
            kd[pl.ds(DB, S), :] = _regroup_load(k_ref, d).astype(BF16)
            vd[pl.ds(DB, S), :] = _regroup_load(v_ref, d).astype(BF16)
            dod[...] = _regroup_load(do_ref, d).astype(BF16)
            lsed[...] = _regroup_load(lse_ref, d)
            dsd[...] = _regroup_load(dsum, d)
            dq_r[...] = jnp.zeros_like(dq_r)
            dk_r[...] = jnp.zeros_like(dk_r)
            dv_r[...] = jnp.zeros_like(dv_r)

            def blk(bk, _, d=d):
                r0 = pl.multiple_of(bk * DB, DB)
                qt = qd[pl.ds(r0, DB), :]
                dot = dod[pl.ds(r0, DB), :]
                lt = lsed[pl.ds(r0, DB), :]
                dt = dsd[pl.ds(r0, DB), :]
                kk, vv, valid = _dil_keys(kd, vd, r0, bands, bk, d)
                kw = kk.shape[0]
                qm = _stack_heads(qt, masks)
                dom = _stack_heads(dot, masks)
                lcol = jnp.concatenate([lt[:, 0:1], lt[:, HD:HD + 1]], axis=0)
                dcol = jnp.concatenate([dt[:, 0:1], dt[:, HD:HD + 1]], axis=0)
                s = lax.dot_general(qm, kk, NT_DIMS, preferred_element_type=F32)
                s = jnp.where(valid, s, NEG)
                p = jnp.exp(s - lcol)
                dp = lax.dot_general(dom, vv, NT_DIMS, preferred_element_type=F32)
                ds = (p * (dp - dcol)).astype(BF16)
                dvt = lax.dot_general(p.astype(BF16), dom, tn, preferred_element_type=F32)
                dkt = lax.dot_general(ds, qm, tn, preferred_element_type=F32)
                dqt = jnp.dot(ds, kk, preferred_element_type=F32)
                dq_r[pl.ds(r0, DB), :] = jnp.where(masks[0], dqt[0:DB], dqt[DB:2 * DB])
                dk_r[pl.ds(r0 + 2 * DB - kw, kw), :] += dkt
                dv_r[pl.ds(r0 + 2 * DB - kw, kw), :] += dvt
                return 0

            lax.fori_loop(0, nblk, blk, 0, unroll=8)
            _regroup_store(dq_n, d, dq_r, bi > 0)
            _regroup_store(dk_n, d, dk_r.at[pl.ds(DB, S)], bi > 0)
            _regroup_store(dv_n, d, dv_r.at[pl.ds(DB, S)], bi > 0)

        dp_ref[:, 0:LANE] = (dq_n[...] * SCALE).astype(BF16)
        dp_ref[:, LANE:2 * LANE] = dk_n[...].astype(BF16)
        dp_ref[:, 2 * LANE:3 * LANE] = dv_n[...].astype(BF16)

    def colblk(off):
        return pl.BlockSpec((S, LANE), lambda b, hp: (b, off + hp))

    big = pltpu.VMEM((S, LANE), F32)
    bigp = pltpu.VMEM((S + DB, LANE), F32)
    return pl.pallas_call(
        body, name=name, grid=(nb, NHP),
        in_specs=[_qkv_blk(CB_DIL, 0), _qkv_blk(CB_DIL, 1), _qkv_blk(CB_DIL, 2), colblk(0), colblk(0), colblk(0),
                  pl.BlockSpec(memory_space=pl.ANY)],
        out_specs=_dqkv_blk(CB_DIL), out_shape=jax.ShapeDtypeStruct((nb * S, NP), BF16),
        input_output_aliases={6: 0},
        scratch_shapes=[pltpu.VMEM((S, LANE), BF16), pltpu.VMEM((S + DB, LANE), BF16), pltpu.VMEM((S + DB, LANE), BF16),
                        pltpu.VMEM((S, LANE), BF16), big, big, big, big, bigp, bigp, big, big, big],
        compiler_params=_cparams(("parallel", "parallel")),
    )(P, P, P, o, lse, do, dP)


RC = 256
NSHW = 4


def _window(src, start, buf):
    if start % 8 == 0:
        return src[pl.ds(start, RC), :]
    buf[...] = src[pl.ds(start, RC), :]
    return buf[...]
CPAD = 32


NSUB = 8
CROWS = S + CPAD


def _preshift(src, dst):
    for b in range(NSUB):
        dst[b] = src[pl.ds(b, CROWS), :]


def _shifted(dst, start):
    return dst[start % NSUB, pl.ds(start - start % NSUB, RC), :]


def _conv_chunk(gsh, r0, cw_ref, cb_ref):
    acc = jnp.zeros((RC, CC), F32) + cb_ref[...]
    for k in range(CK):
        acc = acc + cw_ref[k:k + 1, :] * _shifted(gsh, r0 + CPAD - (CK - 1) + k)
    return acc


def _cnorm(c0, cng_ref, cnb_ref):
    mu = jnp.mean(c0, axis=1, keepdims=True)
    xc = c0 - mu
    rstd = lax.rsqrt(jnp.mean(xc * xc, axis=1, keepdims=True) + EPS)
    n = xc * rstd
    return n, rstd, n * cng_ref[...] + cnb_ref[...]


NORM_ROWS = 512
YC_BLK = 2 * WA // CC


def _attn_norm_fwd(of, od, gof, god, name):
    T = of.shape[0]

    def body(of_ref, od_ref, gof_ref, god_ref, y_ref):
        for i, (src, g_ref) in enumerate(((of_ref, gof_ref), (od_ref, god_ref))):
            v = src[...]
            r = lax.rsqrt(jnp.mean(v * v, axis=1, keepdims=True) + EPS)
            y_ref[:, i * WA:(i + 1) * WA] = (v * r * g_ref[...]).astype(BF16)

    row = lambda w: pl.BlockSpec((NORM_ROWS, w), lambda i: (i, 0))
    par = pl.BlockSpec((1, WA), lambda i: (0, 0))
    return pl.pallas_call(
        body, name=name, grid=(T // NORM_ROWS,),
        in_specs=[row(WA), row(WA), par, par], out_specs=row(2 * WA),
        out_shape=jax.ShapeDtypeStruct((T, D), BF16),
        compiler_params=_cparams(("parallel",)),
    )(of, od, gof, god)


def _attn_norm_bwd(of, od, dy, gof, god, name):
    T = of.shape[0]

    def body(of_ref, od_ref, dy_ref, gof_ref, god_ref, dof_ref, dod_ref, dgo_ref):
        @pl.when(pl.program_id(0) == 0)
        def _():
            dgo_ref[...] = jnp.zeros_like(dgo_ref)

        for i, (src, g_ref, dst) in enumerate(((of_ref, gof_ref, dof_ref), (od_ref, god_ref, dod_ref))):
            v = src[...]
            dyv = dy_ref[:, i * WA:(i + 1) * WA].astype(F32)
            r = lax.rsqrt(jnp.mean(v * v, axis=1, keepdims=True) + EPS)
            a = dyv * g_ref[...]
            dst[...] = r * a - v * (r * r * r * jnp.mean(v * a, axis=1, keepdims=True))
            dgo_ref[i:i + 1, :] += jnp.sum(dyv * v * r, axis=0, keepdims=True)

    row = lambda w: pl.BlockSpec((NORM_ROWS, w), lambda i: (i, 0))
    par = pl.BlockSpec((1, WA), lambda i: (0, 0))
    return pl.pallas_call(
        body, name=name, grid=(T // NORM_ROWS,),
        in_specs=[row(WA), row(WA), row(2 * WA), par, par],
        out_specs=[row(WA), row(WA), pl.BlockSpec((8, WA), lambda i: (0, 0))],
        out_shape=[jax.ShapeDtypeStruct((T, WA), F32), jax.ShapeDtypeStruct((T, WA), F32),
                   jax.ShapeDtypeStruct((8, WA), F32)],
        compiler_params=_cparams(("arbitrary",)),
    )(of, od, dy, gof, god)


def _conv_specs():
    gblk = lambda off: pl.BlockSpec((S, CC), lambda b: (b, off))
    par = lambda r: pl.BlockSpec((r, CC), lambda b: (0, 0))
    return gblk, par


def _conv_fwd(P, y, cw, cb, cng, cnb, nb, name):
    def body(gv_ref, gg_ref, cw_ref, cb_ref, cng_ref, cnb_ref, y_in, y_ref, c0_ref, gpad, gsh):
        del y_in
        gpad[pl.ds(0, CPAD), :] = jnp.zeros((CPAD, CC), F32)
        gpad[pl.ds(CPAD, S), :] = gv_ref[...] * _sigmoid(gg_ref[...])
        gpad[pl.ds(CROWS, NSUB), :] = jnp.zeros((NSUB, CC), F32)
        _preshift(gpad, gsh)
        for ci in range(S // RC):
            r0 = ci * RC
            c0 = _conv_chunk(gsh, r0, cw_ref, cb_ref)
            c0_ref[pl.ds(r0, RC), :] = c0
            _, _, z = _cnorm(c0, cng_ref, cnb_ref)
            y_ref[pl.ds(r0, RC), :] = (z * _sigmoid(z)).astype(BF16)

    gblk, par = _conv_specs()
    return pl.pallas_call(
        body, name=name, grid=(nb,),
        in_specs=[gblk(CB_GV), gblk(CB_GG), par(CPAD), par(1), par(1), par(1), pl.BlockSpec(memory_space=pl.ANY)],
        out_specs=[gblk(YC_BLK), gblk(0)],
        out_shape=[jax.ShapeDtypeStruct((nb * S, D), BF16), jax.ShapeDtypeStruct((nb * S, CC), F32)],
        input_output_aliases={6: 0},
        scratch_shapes=[pltpu.VMEM((CROWS + NSUB, CC), F32), pltpu.VMEM((NSUB, CROWS, CC), F32)],
        compiler_params=_cparams(("parallel",)),
    )(P, P, cw, cb, cng, cnb, y)


def _conv_bwd(P, c0, dy, dfa, dP, cw, cng, cnb, nb, name):
    def body(gv_ref, gg_ref, c0_ref, dy_ref, dfa_ref, cw_ref, cng_ref, cnb_ref, dp_in, dg_ref, dcw_ref, dsm_ref,
             dpad, dsh):
        del dp_in
        dg_ref[:, 2 * CC:2 * CC + LANE] = dfa_ref[...]
        dg_ref[:, 2 * CC + LANE:3 * CC] = jnp.zeros((S, CC - LANE), BF16)
        @pl.when(pl.program_id(0) == 0)
        def _():
            dcw_ref[...] = jnp.zeros_like(dcw_ref)
            dsm_ref[...] = jnp.zeros_like(dsm_ref)

        dpad[pl.ds(S, CPAD + NSUB), :] = jnp.zeros((CPAD + NSUB, CC), F32)
        zero = jnp.zeros((8, CC), F32)
        dcb, dcng, dcnb = zero, zero, zero
        for ci in range(S // RC):
            r0 = ci * RC
            n, rstd, z = _cnorm(c0_ref[pl.ds(r0, RC), :], cng_ref, cnb_ref)
            sz = _sigmoid(z)
            dz = dy_ref[pl.ds(r0, RC), :].astype(F32) * (sz * (1.0 + z * (1.0 - sz)))
            dcng = dcng + _sum8(dz * n)
            dcnb = dcnb + _sum8(dz)
            dn = dz * cng_ref[...]
            dc0 = rstd * (dn - jnp.mean(dn, axis=1, keepdims=True) - n * jnp.mean(dn * n, axis=1, keepdims=True))
            dcb = dcb + _sum8(dc0)
            dpad[pl.ds(r0, RC), :] = dc0
        dsm_ref[0:1, :] += jnp.sum(dcb, axis=0, keepdims=True)
        dsm_ref[1:2, :] += jnp.sum(dcng, axis=0, keepdims=True)
        dsm_ref[2:3, :] += jnp.sum(dcnb, axis=0, keepdims=True)

        _preshift(dpad, dsh)
        dws = [zero] * CK
        for ci in range(S // RC):
            r0 = ci * RC
            sg = _sigmoid(gg_ref[pl.ds(r0, RC), :])
            gvc = gv_ref[pl.ds(r0, RC), :]
            glu = gvc * sg
            dgl = jnp.zeros((RC, CC), F32)
            for k in range(CK):
                win = _shifted(dsh, r0 + (CK - 1) - k)
                dws[k] = dws[k] + _sum8(win * glu)
                dgl = dgl + cw_ref[k:k + 1, :] * win
            dg_ref[pl.ds(r0, RC), 0:CC] = (dgl * sg).astype(BF16)
            dg_ref[pl.ds(r0, RC), CC:2 * CC] = (dgl * gvc * sg * (1.0 - sg)).astype(BF16)
        for k in range(CK):
            dcw_ref[k:k + 1, :] += jnp.sum(dws[k], axis=0, keepdims=True)

    gblk, par = _conv_specs()
    return pl.pallas_call(
        body, name=name, grid=(nb,),
        in_specs=[gblk(CB_GV), gblk(CB_GG), gblk(0), gblk(YC_BLK), pl.BlockSpec((S, LANE), lambda b: (b, 0)),
                  par(CPAD), par(1), par(1), pl.BlockSpec(memory_space=pl.ANY)],
        out_specs=[pl.BlockSpec((S, 3 * CC), lambda b: (b, CB_GV // 3)), par(CPAD), par(8)],
        out_shape=[jax.ShapeDtypeStruct((nb * S, NP), BF16), jax.ShapeDtypeStruct((CPAD, CC), F32),
                   jax.ShapeDtypeStruct((8, CC), F32)],
        input_output_aliases={8: 0},
        scratch_shapes=[pltpu.VMEM((CROWS + NSUB, CC), F32), pltpu.VMEM((NSUB, CROWS, CC), F32)],
        compiler_params=_cparams(("arbitrary",)),
    )(P, P, c0, dy, dfa, cw, cng, cnb, dP)


FC = 512
FPAD = 8
NFB = 2 * DFF // FC


def _ffn_u2_chunk(upad, r0, fw_ref, fb_ref):
    acc = jnp.zeros((RC, FC), F32) + fb_ref[...]
    for k in range(FK):
        acc = acc + fw_ref[k:k + 1, :] * upad[pl.ds(r0 + FPAD - (FK - 1) + k, RC), :]
    return acc


def _ffn_fwd(U, fw, fb, nb, name):
    def body(u_ref, fw_ref, fb_ref, h_ref, u2_ref, upad):
        upad[pl.ds(0, FPAD), :] = jnp.zeros((FPAD, FC), F32)
        upad[pl.ds(FPAD, S), :] = u_ref[...].astype(F32)
        for ci in range(S // RC):
            r0 = ci * RC
            u2 = _ffn_u2_chunk(upad, r0, fw_ref, fb_ref)
            u2_ref[pl.ds(r0, RC), :] = u2.astype(BF16)
            a2, b2 = u2[:, :FC // 2], u2[:, FC // 2:]
            h_ref[pl.ds(r0, RC), :] = (a2 * _sigmoid(a2) * b2).astype(BF16)

    return pl.pallas_call(
        body, name=name, grid=(nb, NFB),
        in_specs=[pl.BlockSpec((S, FC), lambda b, j: (b, j)), pl.BlockSpec((8, FC), lambda b, j: (0, j)),
                  pl.BlockSpec((1, FC), lambda b, j: (0, j))],
        out_specs=[pl.BlockSpec((S, FC // 2), lambda b, j: (b, j)), pl.BlockSpec((S, FC), lambda b, j: (b, j))],
        out_shape=[jax.ShapeDtypeStruct((nb * S, DFF), BF16), jax.ShapeDtypeStruct((nb * S, 2 * DFF), BF16)],
        scratch_shapes=[pltpu.VMEM((S + FPAD, FC), F32)],
        compiler_params=_cparams(("parallel", "parallel")),
    )(U, fw, fb)


def _ffn_bwd(U, U2, dhid, fw, nb, name):
    def body(u_ref, u2_ref, dh_ref, fw_ref, du_ref, dfw_ref, dpad, shw):
        @pl.when(pl.program_id(1) == 0)
        def _():
            dfw_ref[...] = jnp.zeros_like(dfw_ref)

        dpad[pl.ds(S, FPAD), :] = jnp.zeros((FPAD, FC), F32)
        zero = jnp.zeros((8, FC), F32)
        dbias = zero
        for ci in range(S // RC):
            r0 = ci * RC
            u2 = u2_ref[pl.ds(r0, RC), :].astype(F32)
            a2, b2 = u2[:, :FC // 2], u2[:, FC // 2:]
            sa = _sigmoid(a2)
            dh = dh_ref[pl.ds(r0, RC), :].astype(F32)
            du2 = jnp.concatenate([dh * b2 * (sa * (1.0 + a2 * (1.0 - sa))), dh * a2 * sa], axis=1)
            dpad[pl.ds(r0, RC), :] = du2
            dbias = dbias + _sum8(du2)
        dws = [zero] * FK
        for ci in range(S // RC):
            r0 = ci * RC
            uc = u_ref[pl.ds(r0, RC), :].astype(F32)
            du = jnp.zeros((RC, FC), F32)
            for k in range(FK):
                win = _window(dpad, r0 + (FK - 1) - k, shw.at[k % NSHW])
                dws[k] = dws[k] + _sum8(win * uc)
                du = du + fw_ref[k:k + 1, :] * win
            du_ref[pl.ds(r0, RC), :] = du.astype(BF16)
        for k in range(FK):
            dfw_ref[k:k + 1, :] += jnp.sum(dws[k], axis=0, keepdims=True)
        dfw_ref[FK:FK + 1, :] += jnp.sum(dbias, axis=0, keepdims=True)

    blk = pl.BlockSpec((S, FC), lambda j, b: (b, j))
    return pl.pallas_call(
        body, name=name, grid=(NFB, nb),
        in_specs=[blk, blk, pl.BlockSpec((S, FC // 2), lambda j, b: (b, j)), pl.BlockSpec((8, FC), lambda j, b: (0, j))],
        out_specs=[blk, pl.BlockSpec((8, FC), lambda j, b: (0, j))],
        out_shape=[jax.ShapeDtypeStruct((nb * S, 2 * DFF), BF16), jax.ShapeDtypeStruct((8, 2 * DFF), F32)],
        scratch_shapes=[pltpu.VMEM((S + FPAD, FC), F32), pltpu.VMEM((NSHW, RC, FC), F32)],
        compiler_params=_cparams(("parallel", "arbitrary")),
    )(U, U2, dhid, fw)


def _matmul_ffn(a, b, mode, *, out_dtype=F32, tm=1024, tk=2048, norm_g=None, name):
    HF = FC // 2
    if mode == "fwd":
        M, K = a.shape
        tm = 512

        def body(a_ref, g_ref, b_ref, h_ref, o_ref):
            av = _rms_rows(a_ref[...], g_ref)
            h_ref[...] = av
            for j in range(NFB):
                for half in range(2):
                    bv = b_ref[:, half * DFF + j * HF:half * DFF + (j + 1) * HF]
                    o_ref[:, j * FC + half * HF:j * FC + (half + 1) * HF] = jnp.dot(
                        av, bv, preferred_element_type=F32).astype(o_ref.dtype)

        return pl.pallas_call(
            body, name=name, grid=(M // tm,),
            in_specs=[pl.BlockSpec((tm, K), lambda i: (i, 0)), pl.BlockSpec((1, K), lambda i: (0, 0)),
                      pl.BlockSpec((K, 2 * DFF), lambda i: (0, 0))],
            out_specs=[pl.BlockSpec((tm, K), lambda i: (i, 0)), pl.BlockSpec((tm, 2 * DFF), lambda i: (i, 0))],
            out_shape=[jax.ShapeDtypeStruct((M, K), BF16), jax.ShapeDtypeStruct((M, 2 * DFF), out_dtype)],
            compiler_params=_cparams(("parallel",)),
        )(a, norm_g.reshape(1, K), b)
    if mode == "dx":
        M = a.shape[0]
        N = b.shape[0]
        tm = 512

        def body(a_ref, b_ref, o_ref):
            acc = None
            for j in range(NFB):
                for half in range(2):
                    av = a_ref[:, j * FC + half * HF:j * FC + (half + 1) * HF]
                    bv = b_ref[:, half * DFF + j * HF:half * DFF + (j + 1) * HF]
                    d = lax.dot_general(av, bv, NT_DIMS, preferred_element_type=F32)
                    acc = d if acc is None else acc + d
            o_ref[...] = acc.astype(o_ref.dtype)

        return pl.pallas_call(
            body, name=name, grid=(M // tm,),
            in_specs=[pl.BlockSpec((tm, 2 * DFF), lambda i: (i, 0)), pl.BlockSpec((N, 2 * DFF), lambda i: (0, 0))],
            out_specs=pl.BlockSpec((tm, N), lambda i: (i, 0)),
            out_shape=jax.ShapeDtypeStruct((M, N), out_dtype),
            compiler_params=_cparams(("parallel",)),
        )(a, b)
    assert mode == "dw"
    T, M = a.shape
    nk = T // tk

    def body(a_ref, g_ref, oa_ref, ob_ref, acc):
        k = pl.program_id(1)
        prod = lax.dot_general(a_ref[...], g_ref[...], (((0,), (0,)), ((), ())), preferred_element_type=F32)

        @pl.when(k == 0)
        def _():
            acc[...] = prod

        @pl.when(k > 0)
        def _():
            acc[...] += prod

        @pl.when(k == nk - 1)
        def _():
            oa_ref[...] = acc[:, :HF]
            ob_ref[...] = acc[:, HF:]

    half = pl.BlockSpec((M, HF), lambda j, k: (0, j))
    return pl.pallas_call(
        body, name=name, grid=(NFB, nk),
        in_specs=[pl.BlockSpec((tk, M), lambda j, k: (k, 0)), pl.BlockSpec((tk, FC), lambda j, k: (k, j))],
        out_specs=[half, half],
        out_shape=[jax.ShapeDtypeStruct((M, DFF), F32)] * 2,
        scratch_shapes=[pltpu.VMEM((M, FC), F32)],
        compiler_params=_cparams(("parallel", "arbitrary")),
    )(a, b)


def _adamw_body(w_ref, g_ref, m_ref, v_ref, d_ref, nm_ref, nv_ref):
    g = g_ref[...]
    m = ADAM_B1 * m_ref[...] + (1.0 - ADAM_B1) * g
    v = ADAM_B2 * v_ref[...] + (1.0 - ADAM_B2) * (g * g)
    m_hat = m / (1.0 - ADAM_B1 ** ADAM_STEP)
    v_hat = v / (1.0 - ADAM_B2 ** ADAM_STEP)
    d_ref[...] = -ADAM_LR * (m_hat / (jnp.sqrt(v_hat) + ADAM_EPS) + ADAM_WD * w_ref[...])
    nm_ref[...] = m
    nv_ref[...] = v


def _adamw(w, g, m, v, name):
    shape = w.shape
    R = 1
    for s in shape[:-1]:
        R *= s
    C = shape[-1]
    args = [a.reshape(R, C) for a in (w, g, m, v)]
    tr = R
    for cand in (512, 352, 256, 128, 64, 32, 16, 8):
        if R % cand == 0 and cand * C * 4 * 14 <= 24 * 1024 * 1024:
            tr = cand
            break
    blk = pl.BlockSpec((tr, C), lambda i: (i, 0))
    outs = pl.pallas_call(
        functools.partial(_adamw_body), name=name, grid=(R // tr,),
        in_specs=[blk] * 4, out_specs=[blk] * 3,
        out_shape=[jax.ShapeDtypeStruct((R, C), F32)] * 3,
        compiler_params=_cparams(("parallel",)),
    )(*args)
    return [o.reshape(shape) for o in outs]


REF_FOX, REF_DIL, REF_GATE = 0, N_QKV + N_FG, 2 * N_QKV + N_FG


def _pack_cols_value(w):
    parts = []
    for ref0 in (REF_FOX, REF_DIL):
        for hp in range(NHP):
            parts += [w[:, ref0 + i * WA + hp * LANE:ref0 + i * WA + (hp + 1) * LANE] for i in range(3)]
    parts += [w[:, REF_GATE:NIN], w[:, N_QKV:N_QKV + N_FG], jnp.zeros((w.shape[0], NP - NIN), w.dtype)]
    return jnp.concatenate(parts, axis=1)


def _unpack_cols_value(g):
    def group(cb):
        return [g[:, (cb + 3 * hp + i) * LANE:(cb + 3 * hp + i + 1) * LANE] for i in range(3) for hp in range(NHP)]
    fa0 = 2 * N_QKV + 2 * CC
    return jnp.concatenate(group(CB_FOX) + [g[:, fa0:fa0 + N_FG]] + group(CB_DIL) + [g[:, 2 * N_QKV:fa0]], axis=1)


def _adamw_layers(w, gs, m, v, name, packed=False):
    _, R, C = w.shape
    Cg = gs[0].shape[1]
    tr = 128 if R % 128 == 0 else 176
    assert R % tr == 0 and len(gs) == DEPTH == 2

    def body(w_ref, g0_ref, g1_ref, m_ref, v_ref, g_out, d_ref, nm_ref, nv_ref):
        g = jnp.where(pl.program_id(0) == 0, g0_ref[...], g1_ref[...])
        if packed:
            g = _unpack_cols_value(g)
        g_out[...] = g
        mn = ADAM_B1 * m_ref[...] + (1.0 - ADAM_B1) * g
        vn = ADAM_B2 * v_ref[...] + (1.0 - ADAM_B2) * (g * g)
        m_hat = mn / (1.0 - ADAM_B1 ** ADAM_STEP)
        v_hat = vn / (1.0 - ADAM_B2 ** ADAM_STEP)
        d_ref[...] = -ADAM_LR * (m_hat / (jnp.sqrt(v_hat) + ADAM_EPS) + ADAM_WD * w_ref[...])
        nm_ref[...] = mn
        nv_ref[...] = vn

    lay = pl.BlockSpec((None, tr, C), lambda l, i: (l, i, 0))
    gsp = pl.BlockSpec((tr, Cg), lambda l, i: (i, 0))
    return pl.pallas_call(
        body, name=name, grid=(DEPTH, R // tr),
        in_specs=[lay, gsp, gsp, lay, lay], out_specs=[lay] * 4,
        out_shape=[jax.ShapeDtypeStruct((DEPTH, R, C), F32)] * 4,
        compiler_params=_cparams(("parallel", "parallel")),
    )(w, gs[0], gs[1], m, v)


def _pack_w_in(w_in):
    _, R, _ = w_in.shape

    def body(w_ref, o_ref):
        o_ref[...] = _pack_cols_value(w_ref[...]).astype(BF16)

    return pl.pallas_call(
        body, name="pack_w_in", grid=(DEPTH,),
        in_specs=[pl.BlockSpec((None, R, NIN), lambda l: (l, 0, 0))],
        out_specs=pl.BlockSpec((None, R, NP), lambda l: (l, 0, 0)),
        out_shape=jax.ShapeDtypeStruct((DEPTH, R, NP), BF16),
        compiler_params=_cparams(("parallel",)),
    )(w_in)


def _rs_row_tile(H):
    th = 128 if H % 128 == 0 else 176
    assert H % th == 0
    return th


def _add_half(g, r1, place, name):
    _, R, C = g.shape
    H = R // 2
    th = _rs_row_tile(H)
    nh = H // th

    def body(s_ref, g_ref, r_ref, o_ref):
        o_ref[...] = (g_ref[...] + r_ref[...]).astype(BF16)

    grid_spec = pltpu.PrefetchScalarGridSpec(
        num_scalar_prefetch=1, grid=(NCHIP, nh),
        in_specs=[pl.BlockSpec((None, th, C), lambda p, i, s: (p, s[1] * nh + i, 0)),
                  pl.BlockSpec((None, th, C), lambda p, i, s: (p, i, 0))],
        out_specs=pl.BlockSpec((None, th, C), lambda p, i, s: (p, i, 0)))
    return pl.pallas_call(
        body, name=name, grid_spec=grid_spec, out_shape=jax.ShapeDtypeStruct((NCHIP, H, C), BF16),
        compiler_params=_cparams(("parallel", "parallel")),
    )(place, g, r1)


def _sum_slots(g, r1, r2, place, name):
    _, R, C = g.shape
    H = R // 2
    th = _rs_row_tile(H)
    nh = H // th

    def body(s_ref, g_ref, r1_ref, r2_ref, o_ref):
        acc = g_ref[...] + r1_ref[...]
        for j in range(NCHIP - 1):
            acc = acc + r2_ref[j].astype(F32)
        o_ref[...] = acc

    grid_spec = pltpu.PrefetchScalarGridSpec(
        num_scalar_prefetch=1, grid=(nh,),
        in_specs=[pl.BlockSpec((None, th, C), lambda i, s: (s[0], s[1] * nh + i, 0)),
                  pl.BlockSpec((None, th, C), lambda i, s: (s[0], i, 0)),
                  pl.BlockSpec((NCHIP - 1, th, C), lambda i, s: (0, i, 0))],
        out_specs=pl.BlockSpec((None, th, C), lambda i, s: (s[1], i, 0)))
    return pl.pallas_call(
        body, name=name, grid_spec=grid_spec, out_shape=jax.ShapeDtypeStruct((2, H, C), F32),
        compiler_params=_cparams(("parallel",)),
    )(place, g, r1, r2)


MESH = pl.DeviceIdType.MESH
HBM = pl.BlockSpec(memory_space=pltpu.HBM)


def _place():
    x, y, c = lax.axis_index("x"), lax.axis_index("y"), lax.axis_index("c")
    chips = [(1 - x, y), (x, 1 - y), (1 - x, 1 - y)]
    return x, y, c, chips


def _rcopy(src, dst, ssem, rsem, dev):
    return pltpu.make_async_remote_copy(src_ref=src, dst_ref=dst, send_sem=ssem, recv_sem=rsem,
                                        device_id=dev, device_id_type=MESH)


AG_CHUNK_BYTES = 1 << 20


def _stage_rows(R, C, dtype):
    rows = R
    while rows * C * jnp.dtype(dtype).itemsize > AG_CHUNK_BYTES and rows % 32 == 0:
        rows //= 2
    return rows


def _allgather(shards, split):
    n = len(shards)
    nout = n * DEPTH
    rows = [_stage_rows(s.shape[1], s.shape[2], s.dtype) for s in shards]

    def body(*refs):
        ins, outs = refs[:n], refs[n:n + nout]
        stages = refs[n + nout:2 * n + nout]
        ssem, rsem, fssem, frsem, isem, osem = refs[2 * n + nout:]
        x, y, c, chips = _place()
        me = 2 * x + y
        sib = (x, y, 1 - c)

        def window(t, l, chip, half):
            if not split[t]:
                return outs[t * DEPTH + l].at[chip]
            H = shards[t].shape[1] // 2
            return outs[t * DEPTH + l].at[chip, pl.ds(half * H, H)]

        sends = []
        for t in range(n):
            H = shards[t].shape[1] // 2
            for l in range(DEPTH):
                src = ins[t].at[l, pl.ds(c * H, H)] if split[t] else ins[t].at[l]
                for j, (cx, cy) in enumerate(chips):
                    k = (t * DEPTH + l) * 3 + j
                    cp = _rcopy(src, window(t, l, me, c), ssem.at[k], rsem.at[k], (cx, cy, c))
                    cp.start()
                    sends.append(cp)
        for t in range(n):
            nch = shards[t].shape[1] // rows[t]
            outc = []
            for l in range(DEPTH):
                for i in range(nch):
                    slot = len(outc) % 2
                    if len(outc) >= 2:
                        outc[-2].wait()
                    rs = pl.ds(i * rows[t], rows[t])
                    cin = pltpu.make_async_copy(ins[t].at[l, rs], stages[t].at[slot], isem.at[t])
                    cin.start()
                    cin.wait()
                    co = pltpu.make_async_copy(stages[t].at[slot], outs[t * DEPTH + l].at[me, rs], osem.at[2 * t + slot])
                    co.start()
                    outc.append(co)
            for co in outc[-2:]:
                co.wait()
        for t in range(n):
            for l in range(DEPTH):
                for j, (cx, cy) in enumerate(chips):
                    k = (t * DEPTH + l) * 3 + j
                    win = window(t, l, 2 * cx + cy, c)
                    _rcopy(win, win, ssem.at[k], rsem.at[k], (cx, cy, c)).wait_recv()
                    if split[t]:
                        cp = _rcopy(win, win, fssem.at[k], frsem.at[k], sib)
                        cp.start()
                        sends.append(cp)
        for t in range(n):
            if split[t]:
                for l in range(DEPTH):
                    for j, (cx, cy) in enumerate(chips):
                        k = (t * DEPTH + l) * 3 + j
                        win = window(t, l, 2 * cx + cy, 1 - c)
                        _rcopy(win, win, fssem.at[k], frsem.at[k], sib).wait_recv()
        for cp in sends:
            cp.wait_send()

    out_shape = [jax.ShapeDtypeStruct((NCHIP,) + s.shape[1:], s.dtype) for s in shards for _ in range(DEPTH)]
    outs = pl.pallas_call(
        body, name="allgather_weights", in_specs=[HBM] * n, out_specs=[HBM] * nout, out_shape=out_shape,
        scratch_shapes=[pltpu.VMEM((2, r, s.shape[2]), s.dtype) for r, s in zip(rows, shards)]
        + [pltpu.SemaphoreType.DMA((3 * nout,))] * 4 + [pltpu.SemaphoreType.DMA((n,)), pltpu.SemaphoreType.DMA((2 * n,))],
        compiler_params=pltpu.CompilerParams(vmem_limit_bytes=VMEM_LIMIT),
    )(*shards)
    return [outs[t * DEPTH:(t + 1) * DEPTH] for t in range(n)]


def _rs_pair_exchange(gs):
    n = len(gs)

    def body(*refs):
        ins, outs = refs[:n], refs[n:2 * n]
        ssem, rsem = refs[2 * n:]
        x, y, c, _ = _place()
        cps = []
        for t in range(n):
            H = gs[t].shape[1] // 2
            cp = _rcopy(ins[t].at[:, pl.ds((1 - c) * H, H)], outs[t], ssem.at[t], rsem.at[t], (x, y, 1 - c))
            cp.start()
            cps.append(cp)
        for cp in cps:
            cp.wait_recv()
        for cp in cps:
            cp.wait_send()

    out_shape = [jax.ShapeDtypeStruct((NCHIP, g.shape[1] // 2, g.shape[2]), F32) for g in gs]
    return pl.pallas_call(
        body, name="rs_pair_exchange", in_specs=[HBM] * n, out_specs=[HBM] * n, out_shape=out_shape,
        scratch_shapes=[pltpu.SemaphoreType.DMA((n,))] * 2,
    )(*gs)


def _rs_chip_scatter(hs):
    n = len(hs)

    def body(*refs):
        ins, outs = refs[:n], refs[n:2 * n]
        ssem, rsem = refs[2 * n:]
        x, y, c, chips = _place()
        sends = []
        for t in range(n):
            for j, (cx, cy) in enumerate(chips):
                cp = _rcopy(ins[t].at[2 * cx + cy], outs[t].at[j], ssem.at[3 * t + j], rsem.at[3 * t + j], (cx, cy, c))
                cp.start()
                sends.append(cp)
        for cp in sends:
            cp.wait_recv()
        for cp in sends:
            cp.wait_send()

    out_shape = [jax.ShapeDtypeStruct((NCHIP - 1,) + h.shape[1:], h.dtype) for h in hs]
    return pl.pallas_call(
        body, name="rs_chip_scatter", in_specs=[HBM] * n, out_specs=[HBM] * n, out_shape=out_shape,
        scratch_shapes=[pltpu.SemaphoreType.DMA((3 * n,))] * 2,
    )(*hs)


def _rs_pair_gather(fs):
    n = len(fs)

    def body(*refs):
        bufs = refs[n:2 * n]
        ssem, rsem = refs[2 * n:]
        x, y, c, _ = _place()
        sends = []
        for t in range(n):
            cp = _rcopy(bufs[t].at[c], bufs[t].at[c], ssem.at[t], rsem.at[t], (x, y, 1 - c))
            cp.start()
            sends.append(cp)
        for t in range(n):
            win = bufs[t].at[1 - c]
            _rcopy(win, win, ssem.at[t], rsem.at[t], (x, y, 1 - c)).wait_recv()
        for cp in sends:
            cp.wait_send()

    out_shape = [jax.ShapeDtypeStruct(f.shape, F32) for f in fs]
    return pl.pallas_call(
        body, name="rs_pair_gather", in_specs=[HBM] * n, out_specs=[HBM] * n, out_shape=out_shape,
        input_output_aliases={t: t for t in range(n)},
        scratch_shapes=[pltpu.SemaphoreType.DMA((n,))] * 2,
    )(*fs)


def _allreduce_small(buf):
    R = buf.shape[0]

    def body(in_ref, out_ref, slots, ssem, rsem):
        x, y, c, _ = _place()
        me = 4 * x + 2 * y + c
        slots[me] = in_ref[...]
        cps = []
        for k in range(1, NDEV):
            px = 1 - x if k & 4 else x
            py = 1 - y if k & 2 else y
            pc = 1 - c if k & 1 else c
            cp = _rcopy(in_ref, slots.at[me], ssem.at[k - 1], rsem.at[k - 1], (px, py, pc))
            cp.start()
            cps.append((cp, 4 * px + 2 * py + pc))
        for k, (cp, peer) in enumerate(cps):
            _rcopy(in_ref, slots.at[peer], ssem.at[k], rsem.at[k], (x, y, c)).wait_recv()
        for cp, _ in cps:
            cp.wait_send()
        acc = slots[0]
        for p in range(1, NDEV):
            acc = acc + slots[p]
        out_ref[...] = acc

    return pl.pallas_call(
        body, name="allreduce_small", out_shape=jax.ShapeDtypeStruct((R, LANE), F32),
        in_specs=[pl.BlockSpec(memory_space=pltpu.VMEM)], out_specs=pl.BlockSpec(memory_space=pltpu.VMEM),
        scratch_shapes=[pltpu.VMEM((NDEV, R, LANE), F32), pltpu.SemaphoreType.DMA((NDEV - 1,)),
                        pltpu.SemaphoreType.DMA((NDEV - 1,))],
        compiler_params=pltpu.CompilerParams(vmem_limit_bytes=VMEM_LIMIT),
    )(buf)


def _interleave(a):
    lead = a.shape[:-1]
    return a.reshape(*lead, 2, NFB, FC // 2).swapaxes(-3, -2).reshape(*lead, 2 * DFF)


def _uninterleave(a):
    lead = a.shape[:-1]
    return a.reshape(*lead, NFB, 2, FC // 2).swapaxes(-3, -2).reshape(*lead, 2 * DFF)


def _train_compute(xt, tgt, W, nb):
    saved = []
    xc = xt
    for l in range(DEPTH):
        t = f"_l{l}"
        h, P = _rms_matmul(xc, W["ln1"][l], W["in"][l], tm=512, name="proj_in" + t)
        c = _forget_fwd(P, W["bf"][l], nb, "forget_fwd" + t)
        of, lsef = _fox_fwd(P, c, nb, "fox_fwd" + t)
        od, lsed = _dil_fwd(P, nb, "dil_fwd" + t)
        convp = (W["cw"][l], W["cb"][l], W["cng"][l], W["cnb"][l])
        y = _attn_norm_fwd(of, od, W["gof"][l], W["god"][l], "attn_norm_fwd" + t)
        y, c0 = _conv_fwd(P, y, *convp, nb, "conv_fwd" + t)
        xm = _matmul(y, W["o"][l], add=xc, tm=1024, tn=1024, tk=D, name="proj_out" + t)
        h2, U = _matmul_ffn(xm, W["up"][l], "fwd", out_dtype=BF16, norm_g=W["ln2"][l], name="ffn_up" + t)
        hid, U2 = _ffn_fwd(U, W["fw"][l], W["fb"][l], nb, "ffn_act_fwd" + t)
        xo = _matmul(hid, W["down"][l], add=xm, tm=512, tn=D, tk=DFF, name="ffn_down" + t)
        saved.append((xc, h, P, c, of, lsef, od, lsed, convp, c0, y, xm, h2, U, U2, hid))
        xc = xo

    loss8, dx, dxb, dgfin = _loss_head(xc, W["gfin"], tgt, "loss_head")

    big = [None] * DEPTH
    small = [None] * DEPTH
    for l in reversed(range(DEPTH)):
        t = f"_l{l}"
        xin, h, P, c, of, lsef, od, lsed, convp, c0, y, xm, h2, U, U2, hid = saved[l]
        dhid = _matmul(dxb, W["down"][l], tb=True, out_dtype=BF16, tm=1024, tn=DFF, tk=D, name="ffn_down_dx" + t)
        dWd = _matmul(hid, dxb, ta=True, tm=DFF // 2, tn=D, tk=2048, name="ffn_down_dw" + t)
        dU, dfw = _ffn_bwd(U, U2, dhid, W["fw"][l], nb, "ffn_act_bwd" + t)
        dh2 = _matmul_ffn(dU, W["up"][l], "dx", out_dtype=BF16, name="ffn_up_dx" + t)
        dWup = _matmul_ffn(h2, dU, "dw", name="ffn_up_dw" + t)
        dxm, dxmb, dln2 = _rms_bwd(xm, W["ln2"][l], dh2, dx, "rms2_bwd" + t)
        dy = _matmul(dxmb, W["o"][l], tb=True, out_dtype=BF16, tm=1024, tn=D, tk=D, name="proj_out_dx" + t)
        dWo = _matmul(y, dxmb, ta=True, tm=D, tn=D, tk=2048, name="proj_out_dw" + t)
        dof, dod, dgo = _attn_norm_bwd(of, od, dy, W["gof"][l], W["god"][l], "attn_norm_bwd" + t)
        dP, dcb = _fox_bwd(P, c, of, lsef, dof, nb, "fox_bwd" + t)
        dfa, dbf = _forget_bwd(P, W["bf"][l], dcb, nb, "forget_bwd" + t)
        dP = _dil_bwd(P, od, lsed, dod, dP, nb, "dil_bwd" + t)
        dP, dcw, dsm = _conv_bwd(P, c0, dy, dfa, dP, convp[0], convp[2], convp[3], nb, "conv_bwd" + t)
        dh = _matmul(dP, W["in"][l], tb=True, out_dtype=BF16, tm=1024, tn=D, tk=NP, name="proj_in_dx" + t)
        dWin = _matmul(h, dP, ta=True, tm=D, tn=1024, tk=2048, name="proj_in_dw" + t)
        dx, dxb, dln1 = _rms_bwd(xin, W["ln1"][l], dh, dxm, "rms1_bwd" + t)
        big[l] = (dWin, dWo, dWup, dWd)
        small[l] = (dln1, dbf, dgo, dcw, dsm, dln2, dfw)
    return loss8, dx, big, small, dgfin


_SMALL_ROWS = (D // LANE, 8, 8 * WA // LANE, CPAD * CC // LANE, 8 * CC // LANE, D // LANE, 8 * 2 * DFF // LANE)


def kernel(x, ln1_g, w_in, b_forget, g_out_fox, g_out_dil, conv_w, conv_b, cnorm_g, cnorm_b, w_o, ln2_g, w_up, ffn_conv_w, ffn_conv_b, w_down, g_final, loss_target, m_ln1_g, m_w_in, m_b_forget, m_g_out_fox, m_g_out_dil, m_conv_w, m_conv_b, m_cnorm_g, m_cnorm_b, m_w_o, m_ln2_g, m_w_up, m_ffn_conv_w, m_ffn_conv_b, m_w_down, m_g_final, v_ln1_g, v_w_in, v_b_forget, v_g_out_fox, v_g_out_dil, v_conv_w, v_conv_b, v_cnorm_g, v_cnorm_b, v_w_o, v_ln2_g, v_w_up, v_ffn_conv_w, v_ffn_conv_b, v_w_down, v_g_final):
    names = ("ln1_g", "w_in", "b_forget", "g_out_fox", "g_out_dil", "conv_w", "conv_b", "cnorm_g", "cnorm_b",
             "w_o", "ln2_g", "w_up", "ffn_conv_w", "ffn_conv_b", "w_down", "g_final")
    w = dict(zip(names, (ln1_g, w_in, b_forget, g_out_fox, g_out_dil, conv_w, conv_b, cnorm_g, cnorm_b,
                         w_o, ln2_g, w_up, ffn_conv_w, ffn_conv_b, w_down, g_final)))
    m = dict(zip(names, (m_ln1_g, m_w_in, m_b_forget, m_g_out_fox, m_g_out_dil, m_conv_w, m_conv_b, m_cnorm_g,
                         m_cnorm_b, m_w_o, m_ln2_g, m_w_up, m_ffn_conv_w, m_ffn_conv_b, m_w_down, m_g_final)))
    v = dict(zip(names, (v_ln1_g, v_w_in, v_b_forget, v_g_out_fox, v_g_out_dil, v_conv_w, v_conv_b, v_cnorm_g,
                         v_cnorm_b, v_w_o, v_ln2_g, v_w_up, v_ffn_conv_w, v_ffn_conv_b, v_w_down, v_g_final)))
    nb = x.shape[0]
    T = nb * S
    xi, yi, ci = lax.axis_index("x"), lax.axis_index("y"), lax.axis_index("c")
    chip = 2 * xi + yi
    cw_cols = CC // NCHIP
    up_cols = 2 * DFF // NCHIP

    shards = [_pack_w_in(w_in), w_o.astype(BF16), w_up.astype(BF16), w_down.astype(BF16),
              jnp.pad(ffn_conv_w, ((0, 0), (0, 8 - FK), (0, 0))),
              jnp.pad(conv_w, ((0, 0), (0, CPAD - CK), (0, LANE - cw_cols)))]
    g_in, g_o, g_up, g_dn, g_fw, g_cw = _allgather(shards, (True, True, True, True, False, False))
    fb_full = _interleave(ffn_conv_b)
    W = {
        "in": [g.reshape(D, NP) for g in g_in],
        "o": [g.reshape(D, D) for g in g_o],
        "up": [g.transpose(1, 0, 2).reshape(D, 2 * DFF) for g in g_up],
        "down": [g.reshape(DFF, D) for g in g_dn],
        "ln1": [ln1_g[l] for l in range(DEPTH)],
        "ln2": [ln2_g[l] for l in range(DEPTH)],
        "bf": [jnp.pad(b_forget[l], (0, LANE - N_FG)).reshape(1, LANE) for l in range(DEPTH)],
        "gof": [g_out_fox[l].reshape(1, WA) for l in range(DEPTH)],
        "god": [g_out_dil[l].reshape(1, WA) for l in range(DEPTH)],
        "cw": [g[..., :cw_cols].transpose(1, 0, 2).reshape(CPAD, CC) for g in g_cw],
        "cb": [conv_b[l].reshape(1, CC) for l in range(DEPTH)],
        "cng": [cnorm_g[l].reshape(1, CC) for l in range(DEPTH)],
        "cnb": [cnorm_b[l].reshape(1, CC) for l in range(DEPTH)],
        "fw": [_interleave(g.transpose(1, 0, 2).reshape(8, 2 * DFF)) for g in g_fw],
        "fb": [fb_full[l].reshape(1, 2 * DFF) for l in range(DEPTH)],
        "gfin": g_final,
    }

    loss8, dx, big, small, dgfin = _train_compute(x.reshape(T, D), loss_target.reshape(T, D), W, nb)

    gs = []
    for l in range(DEPTH):
        dWin, dWo, dWup, dWd = big[l]
        gs += [dWin.reshape(NCHIP, D // NCHIP, NP), dWo.reshape(NCHIP, D // NCHIP, D),
               jnp.stack([half[:, i * up_cols:(i + 1) * up_cols] for half in dWup for i in range(2)]),
               dWd.reshape(NCHIP, DFF // NCHIP, D)]
    r1 = _rs_pair_exchange(gs)
    place = jnp.stack([chip, ci]).astype(jnp.int32)
    hs = [_add_half(g, r, place, f"rs_add_pair_{i}") for i, (g, r) in enumerate(zip(gs, r1))]
    r2 = _rs_chip_scatter(hs)
    fs = [_sum_slots(g, a, b, place, f"rs_add_chips_{i}") for i, (g, a, b) in enumerate(zip(gs, r1, r2))]
    red = _rs_pair_gather(fs)
    red = [r.reshape(r.shape[0] * r.shape[1], r.shape[2]) for r in red]
    grads, delta, new_m, new_v = {}, {}, {}, {}
    for i, n in enumerate(("w_in", "w_o", "w_up", "w_down")):
        grads[n], delta[n], new_m[n], new_v[n] = _adamw_layers(
            w[n], [red[4 * l + i] for l in range(DEPTH)], m[n], v[n], "adamw_" + n, packed=(n == "w_in"))

    parts = []
    for l in range(DEPTH):
        parts += [p.reshape(-1, LANE) for p in small[l]]
    parts += [dgfin.reshape(-1, LANE), loss8]
    tot = _allreduce_small(jnp.concatenate(parts, axis=0))
    off = 0
    per_layer = []
    for l in range(DEPTH):
        vals = []
        for rows in _SMALL_ROWS:
            vals.append(tot[off:off + rows])
            off += rows
        per_layer.append(vals)
    gfin_sum = tot[off:off + D // LANE].reshape(D)
    loss = tot[off + D // LANE, 0]

    def layer_stack(fn):
        return jnp.stack([fn(*per_layer[l]) for l in range(DEPTH)])

    fw_sum = layer_stack(lambda a, b, c_, d, e, f, g: _uninterleave(g.reshape(8, 2 * DFF)))
    cw_sum = layer_stack(lambda a, b, c_, d, e, f, g: d.reshape(CPAD, CC)[:CK])
    sm_sum = layer_stack(lambda a, b, c_, d, e, f, g: e.reshape(8, CC))
    go_sum = layer_stack(lambda a, b, c_, d, e, f, g: c_.reshape(8, WA))
    grads.update({
        "ln1_g": layer_stack(lambda a, b, c_, d, e, f, g: a.reshape(D)),
        "b_forget": layer_stack(lambda a, b, c_, d, e, f, g: b[0, :N_FG]),
        "g_out_fox": go_sum[:, 0],
        "g_out_dil": go_sum[:, 1],
        "conv_w": lax.dynamic_slice_in_dim(cw_sum, chip * cw_cols, cw_cols, axis=2),
        "conv_b": sm_sum[:, 0],
        "cnorm_g": sm_sum[:, 1],
        "cnorm_b": sm_sum[:, 2],
        "ln2_g": layer_stack(lambda a, b, c_, d, e, f, g: f.reshape(D)),
        "ffn_conv_w": lax.dynamic_slice_in_dim(fw_sum[:, :FK], chip * up_cols, up_cols, axis=2),
        "ffn_conv_b": fw_sum[:, FK],
        "g_final": gfin_sum,
    })

    for n in names:
        if n not in delta:
            delta[n], new_m[n], new_v[n] = _adamw(w[n], grads[n], m[n], v[n], "adamw_" + n)
    return (loss, dx.reshape(nb, S, D), *[grads[n] for n in names], *[delta[n] for n in names],
            *[new_m[n] for n in names], *[new_v[n] for n in names])
```

```python
import functools

import jax
import jax.numpy as jnp
from jax import lax
from jax.experimental import pallas as pl
from jax.experimental.pallas import tpu as pltpu

F32 = jnp.float32
BF16 = jnp.bfloat16

D = 1024
S = 2048
DEPTH = 2
HD = 64
WA = 384
NHP = 3
CC = 256
CK = 31
FK = 3
DFF = 2816
NIN = 2822
NP = 3072
SCALE = 0.125
EPS = 1e-6
NEG = -1e30
NCHIP = 4
NDEV = 8
LANE = 128

CB_FOX, CB_DIL = 0, 9
CB_GV, CB_GG = 9, 10
CB_FA = 22

ADAM_LR, ADAM_B1, ADAM_B2, ADAM_EPS, ADAM_WD, ADAM_STEP = 0.001, 0.9, 0.999, 1e-08, 0.01, 10

N_QKV = 3 * WA
N_FG = 2 * NHP

VMEM_LIMIT = 56 * 1024 * 1024


def _cparams(sem=None):
    return pltpu.CompilerParams(dimension_semantics=sem, vmem_limit_bytes=VMEM_LIMIT)


def _split3(x):
    hi = x.astype(BF16)
    r1 = x - hi.astype(F32)
    mid = r1.astype(BF16)
    lo = (r1 - mid.astype(F32)).astype(BF16)
    return hi, mid, lo


def _sum8(x):
    r, c = x.shape
    return jnp.sum(x.reshape(r // 8, 8, c), axis=0)


def _sigmoid(z):
    return 0.5 * jnp.tanh(0.5 * z) + 0.5


def _matmul(a, b, *, ta=False, tb=False, out_dtype=F32, add=None, tm, tn, tk, name):
    M = a.shape[1] if ta else a.shape[0]
    K = a.shape[0] if ta else a.shape[1]
    N = b.shape[0] if tb else b.shape[1]
    assert (b.shape[1] if tb else b.shape[0]) == K
    assert M % tm == 0 and N % tn == 0 and K % tk == 0, (M, N, K, tm, tn, tk)
    nk = K // tk
    dn = (((0 if ta else 1,), (1 if tb else 0,)), ((), ()))

    def body(*refs):
        if add is not None:
            a_ref, b_ref, add_ref, o_ref, acc = refs
        else:
            a_ref, b_ref, o_ref, acc = refs
        k = pl.program_id(2)
        prod = lax.dot_general(a_ref[...].astype(BF16), b_ref[...].astype(BF16), dn, preferred_element_type=F32)

        def finish(r):
            if add is not None:
                r = r + add_ref[...]
            o_ref[...] = r.astype(o_ref.dtype)

        if nk == 1:
            finish(prod)
        else:
            @pl.when(k == 0)
            def _():
                acc[...] = prod

            @pl.when(k > 0)
            def _():
                acc[...] += prod

            @pl.when(k == nk - 1)
            def _():
                finish(acc[...])

    a_spec = pl.BlockSpec((tk, tm), lambda i, j, k: (k, i)) if ta else pl.BlockSpec((tm, tk), lambda i, j, k: (i, k))
    b_spec = pl.BlockSpec((tn, tk), lambda i, j, k: (j, k)) if tb else pl.BlockSpec((tk, tn), lambda i, j, k: (k, j))
    o_spec = pl.BlockSpec((tm, tn), lambda i, j, k: (i, j))
    in_specs = [a_spec, b_spec]
    args = [a, b]
    if add is not None:
        in_specs.append(o_spec)
        args.append(add)
    return pl.pallas_call(
        body, name=name, grid=(M // tm, N // tn, nk),
        in_specs=in_specs, out_specs=o_spec,
        out_shape=jax.ShapeDtypeStruct((M, N), out_dtype),
        scratch_shapes=[pltpu.VMEM((tm, tn) if nk > 1 else (8, 128), F32)],
        compiler_params=_cparams(("parallel", "parallel", "arbitrary")),
    )(*args)


def _rms_rows(xv, g_ref):
    r = lax.rsqrt(jnp.mean(xv * xv, axis=1, keepdims=True) + EPS)
    return (xv * r * g_ref[...]).astype(BF16)


def _rms_matmul(x, g, b, *, tm, name):
    T, K = x.shape
    N = b.shape[1]

    def body(x_ref, g_ref, b_ref, h_ref, o_ref):
        h = _rms_rows(x_ref[...], g_ref)
        h_ref[...] = h
        o_ref[...] = jnp.dot(h, b_ref[...], preferred_element_type=F32)

    return pl.pallas_call(
        body, name=name, grid=(T // tm,),
        in_specs=[pl.BlockSpec((tm, K), lambda i: (i, 0)), pl.BlockSpec((1, K), lambda i: (0, 0)),
                  pl.BlockSpec((K, N), lambda i: (0, 0))],
        out_specs=[pl.BlockSpec((tm, K), lambda i: (i, 0)), pl.BlockSpec((tm, N), lambda i: (i, 0))],
        out_shape=[jax.ShapeDtypeStruct((T, K), BF16), jax.ShapeDtypeStruct((T, N), F32)],
        compiler_params=_cparams(("parallel",)),
    )(x, g.reshape(1, K), b)


def _rms_bwd(x, g, dh, dres, name):
    T = x.shape[0]
    tr = 512

    def body(x_ref, g_ref, dh_ref, dres_ref, dx_ref, dxb_ref, dg_ref):
        i = pl.program_id(0)
        xv = x_ref[...]
        dhv = dh_ref[...].astype(F32)
        r = lax.rsqrt(jnp.mean(xv * xv, axis=1, keepdims=True) + EPS)
        a = dhv * g_ref[...]
        dx = dres_ref[...] + r * a - xv * (r * r * r * jnp.mean(xv * a, axis=1, keepdims=True))
        dx_ref[...] = dx
        dxb_ref[...] = dx.astype(BF16)
        part = jnp.sum(dhv * xv * r, axis=0, keepdims=True)

        @pl.when(i == 0)
        def _():
            dg_ref[...] = part

        @pl.when(i > 0)
        def _():
            dg_ref[...] += part

    row = pl.BlockSpec((tr, D), lambda i: (i, 0))
    vec = pl.BlockSpec((1, D), lambda i: (0, 0))
    return pl.pallas_call(
        body, name=name, grid=(T // tr,),
        in_specs=[row, vec, row, row], out_specs=[row, row, vec],
        out_shape=[jax.ShapeDtypeStruct((T, D), F32), jax.ShapeDtypeStruct((T, D), BF16),
                   jax.ShapeDtypeStruct((1, D), F32)],
        compiler_params=_cparams(("arbitrary",)),
    )(x, g.reshape(1, D), dh, dres)


def _loss_head(x, g, target, name):
    T = x.shape[0]
    tr = 512

    def body(x_ref, g_ref, t_ref, loss_ref, dx_ref, dxb_ref, dg_ref):
        i = pl.program_id(0)
        xv = x_ref[...]
        gv = g_ref[...]
        r = lax.rsqrt(jnp.mean(xv * xv, axis=1, keepdims=True) + EPS)
        n = xv * r
        err = n * gv - t_ref[...]
        lpart = 0.5 * jnp.sum(jnp.mean(err * err, axis=1, keepdims=True), axis=0, keepdims=True)
        dy = err * (1.0 / D)
        a = dy * gv
        dx = r * a - xv * (r * r * r * jnp.mean(xv * a, axis=1, keepdims=True))
        dx_ref[...] = dx
        dxb_ref[...] = dx.astype(BF16)
        part = jnp.sum(dy * n, axis=0, keepdims=True)
        lfull = jnp.broadcast_to(lpart, (8, LANE))

        @pl.when(i == 0)
        def _():
            dg_ref[...] = part
            loss_ref[...] = lfull

        @pl.when(i > 0)
        def _():
            dg_ref[...] += part
            loss_ref[...] += lfull

    row = pl.BlockSpec((tr, D), lambda i: (i, 0))
    vec = pl.BlockSpec((1, D), lambda i: (0, 0))
    lsp = pl.BlockSpec((8, LANE), lambda i: (0, 0))
    return pl.pallas_call(
        body, name=name, grid=(T // tr,),
        in_specs=[row, vec, row], out_specs=[lsp, row, row, vec],
        out_shape=[jax.ShapeDtypeStruct((8, LANE), F32), jax.ShapeDtypeStruct((T, D), F32),
                   jax.ShapeDtypeStruct((T, D), BF16), jax.ShapeDtypeStruct((1, D), F32)],
        compiler_params=_cparams(("arbitrary",)),
    )(x, g.reshape(1, D), target)


CUM_BLK = 256


def _tri(n, upper):
    r = lax.broadcasted_iota(jnp.int32, (n, n), 0)
    c = lax.broadcasted_iota(jnp.int32, (n, n), 1)
    return jnp.where((c >= r) if upper else (c <= r), 1.0, 0.0).astype(BF16)


def _tri_apply(tri, x):
    hi, mid, lo = _split3(x)
    out = jnp.dot(tri, hi, preferred_element_type=F32)
    out = out + jnp.dot(tri, mid, preferred_element_type=F32)
    return out + jnp.dot(tri, lo, preferred_element_type=F32)


def _forget_fwd(P, bf_pad, nb, name):
    nblk = S // CUM_BLK

    def body(fa_ref, b_ref, c_ref):
        tri = _tri(CUM_BLK, upper=False)
        carry = jnp.zeros((1, LANE), F32)
        for i in range(nblk):
            z = fa_ref[pl.ds(i * CUM_BLK, CUM_BLK), :] + b_ref[...]
            lf = jnp.minimum(z, 0.0) - jnp.log(1.0 + jnp.exp(-jnp.abs(z)))
            cb = _tri_apply(tri, lf) + carry
            c_ref[pl.ds(i * CUM_BLK, CUM_BLK), :] = cb
            carry = cb[CUM_BLK - 1:CUM_BLK, :]

    return pl.pallas_call(
        body, name=name, grid=(nb,),
        in_specs=[pl.BlockSpec((S, LANE), lambda b: (b, CB_FA)), pl.BlockSpec((1, LANE), lambda b: (0, 0))],
        out_specs=pl.BlockSpec((S, LANE), lambda b: (b, 0)),
        out_shape=jax.ShapeDtypeStruct((nb * S, LANE), F32),
        compiler_params=_cparams(("parallel",)),
    )(P, bf_pad)


def _forget_bwd(P, bf_pad, dcb, nb, name):
    nblk = S // CUM_BLK

    def body(fa_ref, b_ref, dc_ref, dfa_ref, db_ref):
        b = pl.program_id(0)
        tri = _tri(CUM_BLK, upper=True)
        lane = lax.broadcasted_iota(jnp.int32, (CUM_BLK, LANE), 1)
        carry = jnp.zeros((1, LANE), F32)
        dbacc = jnp.zeros((1, LANE), F32)
        for i in reversed(range(nblk)):
            rows = pl.ds(i * CUM_BLK, CUM_BLK)
            dc = jnp.zeros((CUM_BLK, LANE), F32)
            dcv = dc_ref[rows, :]
            for h in range(2 * NHP):
                dc = jnp.where(lane == h, -dcv[:, HD * h:HD * h + 1], dc)
            dl = _tri_apply(tri, dc) + carry
            carry = dl[0:1, :]
            z = fa_ref[rows, :] + b_ref[...]
            dz = jnp.where(lane < 2 * NHP, dl * (1.0 - _sigmoid(z)), 0.0)
            dfa_ref[rows, :] = dz.astype(BF16)
            dbacc = dbacc + jnp.sum(dz, axis=0, keepdims=True)

        dbfull = jnp.broadcast_to(dbacc, (8, LANE))

        @pl.when(b == 0)
        def _():
            db_ref[...] = dbfull

        @pl.when(b > 0)
        def _():
            db_ref[...] += dbfull

    return pl.pallas_call(
        body, name=name, grid=(nb,),
        in_specs=[pl.BlockSpec((S, LANE), lambda b: (b, CB_FA)), pl.BlockSpec((1, LANE), lambda b: (0, 0)),
                  pl.BlockSpec((S, WA), lambda b: (b, 0))],
        out_specs=[pl.BlockSpec((S, LANE), lambda b: (b, 0)), pl.BlockSpec((8, LANE), lambda b: (0, 0))],
        out_shape=[jax.ShapeDtypeStruct((nb * S, LANE), BF16), jax.ShapeDtypeStruct((8, LANE), F32)],
        compiler_params=_cparams(("arbitrary",)),
    )(P, bf_pad, dcb)


FQ = 256
NT_DIMS = (((1,), (1,)), ((), ()))
AUGW = 6


def _qkv_blk(base, i):
    return pl.BlockSpec((S, LANE), lambda b, hp: (b, base + 3 * hp + i))


def _dqkv_blk(base):
    return pl.BlockSpec((S, 3 * LANE), lambda b, hp: (b, base // 3 + hp))


def _head_masks(shape):
    lane = lax.broadcasted_iota(jnp.int32, shape, 1)
    return lane < HD, lane >= HD


def _fox_bias_terms(c_ref, hp):
    lane = lax.broadcasted_iota(jnp.int32, (S, LANE), 1)
    cv = c_ref[...]
    return [_split3(jnp.sum(jnp.where(lane == 2 * hp + e, cv, 0.0), axis=1, keepdims=True)) for e in range(2)]


def _fox_ext(x, terms, side, heads, only):
    lane = lax.broadcasted_iota(jnp.int32, (S, LANE), 1)
    one = jnp.ones((S, 1), BF16)
    aug = jnp.zeros((S, LANE), BF16)
    for e in heads:
        hi, mid, lo = terms[e]
        cols = (hi, mid, lo, one, one, one) if side == "q" else (one, one, one, -hi, -mid, -lo)
        for i, col in enumerate(cols):
            aug = jnp.where(lane == AUGW * e + i, col, aug)
    if only is not None:
        x = jnp.where(_head_masks((S, LANE))[only], x, jnp.zeros_like(x))
    return jnp.concatenate([x, aug], axis=1)


def _halves(x, lane_mask):
    return jnp.where(lane_mask, x[0:FQ, 0:LANE], x[FQ:2 * FQ, 0:LANE])


def _fox_fwd(P, c, nb, name):
    def body(q_ref, k_ref, v_ref, c_ref, o_ref, lse_ref, qm0, qm1, kx, vx):
        hp = pl.program_id(1)
        terms = _fox_bias_terms(c_ref, hp)
        qv = (q_ref[...] * SCALE).astype(BF16)
        qm0[...] = _fox_ext(qv, terms, "q", (0,), 0)
        qm1[...] = _fox_ext(qv, terms, "q", (1,), 1)
        kx[...] = _fox_ext(k_ref[...].astype(BF16), terms, "k", (0, 1), None)
        lane = lax.broadcasted_iota(jnp.int32, (S, LANE), 1)
        vx[...] = jnp.concatenate([v_ref[...].astype(BF16), jnp.where(lane == 0, 1.0, 0.0).astype(BF16)], axis=1)
        tmask = _head_masks((FQ, LANE))[0]
        row = lax.broadcasted_iota(jnp.int32, (2 * FQ, FQ), 0) & (FQ - 1)
        col = lax.broadcasted_iota(jnp.int32, (2 * FQ, FQ), 1)
        for i in range(S // FQ):
            r0 = i * FQ
            qt = jnp.concatenate([qm0[pl.ds(r0, FQ), :], qm1[pl.ds(r0, FQ), :]], axis=0)
            sd = lax.dot_general(qt, kx[pl.ds(r0, FQ), :], NT_DIMS, preferred_element_type=F32)
            sd = jnp.where(col <= row, sd, NEG)
            m = jnp.max(sd, axis=1, keepdims=True)
            if i > 0:
                so = lax.dot_general(qt, kx[pl.ds(0, r0), :], NT_DIMS, preferred_element_type=F32)
                m = jnp.maximum(m, jnp.max(so, axis=1, keepdims=True))
            acc = jnp.dot(jnp.exp(sd - m).astype(BF16), vx[pl.ds(r0, FQ), :], preferred_element_type=F32)
            if i > 0:
                acc = acc + jnp.dot(jnp.exp(so - m).astype(BF16), vx[pl.ds(0, r0), :], preferred_element_type=F32)
            l = acc[:, LANE:LANE + 1]
            o_ref[pl.ds(r0, FQ), :] = _halves(acc / l, tmask)
            lse_ref[pl.ds(r0, FQ), :] = _halves(jnp.broadcast_to(m + jnp.log(l), (2 * FQ, LANE)), tmask)

    def colblk(off):
        return pl.BlockSpec((S, LANE), lambda b, hp: (b, off + hp))

    return pl.pallas_call(
        body, name=name, grid=(nb, NHP),
        in_specs=[_qkv_blk(CB_FOX, 0), _qkv_blk(CB_FOX, 1), _qkv_blk(CB_FOX, 2),
                  pl.BlockSpec((S, LANE), lambda b, hp: (b, 0))],
        out_specs=[colblk(0), colblk(0)],
        out_shape=[jax.ShapeDtypeStruct((nb * S, WA), F32), jax.ShapeDtypeStruct((nb * S, WA), F32)],
        scratch_shapes=[pltpu.VMEM((S, 2 * LANE), BF16)] * 4,
        compiler_params=_cparams(("parallel", "parallel")),
    )(P, P, P, c)


def _fox_bwd(P, c, o, lse, do, nb, name):
    def body(q_ref, k_ref, v_ref, c_ref, o_ref, lse_ref, do_ref, dp_ref, dc_ref,
             km0, km1, qx, vm0, vm1, dob, kt0, kt1, rows, dqt, rsum):
        hp = pl.program_id(1)
        terms = _fox_bias_terms(c_ref, hp)
        kv = k_ref[...].astype(BF16)
        km0[...] = _fox_ext(kv, terms, "k", (0,), 0)
        km1[...] = _fox_ext(kv, terms, "k", (1,), 1)
        qx[...] = _fox_ext((q_ref[...] * SCALE).astype(BF16), terms, "q", (0, 1), None)
        masks = _head_masks((S, LANE))
        vv = v_ref[...].astype(BF16)
        zero = jnp.zeros((S, LANE), BF16)
        vm0[...] = jnp.where(masks[0], vv, zero)
        vm1[...] = jnp.where(masks[1], vv, zero)
        dov = do_ref[...]
        dob[...] = dov.astype(BF16)
        ktf = k_ref[...].T
        prodt = (dov * o_ref[...]).T
        lset = lse_ref[...].T
        hrow = lax.broadcasted_iota(jnp.int32, (LANE, S), 0)
        kt0[...] = jnp.where(hrow < HD, ktf, 0.0).astype(BF16)
        kt1[...] = jnp.where(hrow >= HD, ktf, 0.0).astype(BF16)
        for e in range(2):
            rows[e:e + 1, :] = lset[HD * e:HD * e + 1, :]
            rows[2 + e:3 + e, :] = jnp.sum(prodt[HD * e:HD * (e + 1), :], axis=0, keepdims=True)
        dqt[...] = jnp.zeros_like(dqt)
        rsum[...] = jnp.zeros_like(rsum)
        tmask = _head_masks((FQ, LANE))[0]
        row = lax.broadcasted_iota(jnp.int32, (2 * FQ, FQ), 0) & (FQ - 1)
        col = lax.broadcasted_iota(jnp.int32, (2 * FQ, FQ), 1)
        for j in range(S // FQ):
            k0 = j * FQ
            rest = S - k0 - FQ
            spans = [(k0, FQ)] + ([(k0 + FQ, rest)] if rest > 0 else [])
            kte = jnp.concatenate([km0[pl.ds(k0, FQ), :], km1[pl.ds(k0, FQ), :]], axis=0)
            vte = jnp.concatenate([vm0[pl.ds(k0, FQ), :], vm1[pl.ds(k0, FQ), :]], axis=0)
            ktt = jnp.concatenate([kt0[:, pl.ds(k0, FQ)], kt1[:, pl.ds(k0, FQ)]], axis=1)
            dke = jnp.zeros((2 * FQ, 2 * LANE), F32)
            dve = jnp.zeros((2 * FQ, LANE), F32)
            cse = jnp.zeros((2 * FQ, 1), F32)
            for si, (q0, n) in enumerate(spans):
                qs = qx[pl.ds(q0, n), :]
                dos = dob[pl.ds(q0, n), :]
                st = lax.dot_general(kte, qs, NT_DIMS, preferred_element_type=F32)
                if si == 0:
                    st = jnp.where(col >= row, st, NEG)
                dpt = lax.dot_general(vte, dos, NT_DIMS, preferred_element_type=F32)
                pts, dsts = [], []
                for e in range(2):
                    pe = jnp.exp(st[FQ * e:FQ * (e + 1), :] - rows[e:e + 1, pl.ds(q0, n)])
                    de = pe * (dpt[FQ * e:FQ * (e + 1), :] - rows[2 + e:3 + e, pl.ds(q0, n)])
                    rsum[HD * e:HD * e + 8, pl.ds(q0, n)] += _sum8(de)
                    pts.append(pe)
                    dsts.append(de)
                pt = jnp.concatenate(pts, axis=0)
                dst = jnp.concatenate(dsts, axis=0)
                dsb = dst.astype(BF16)
                dve = dve + jnp.dot(pt.astype(BF16), dos, preferred_element_type=F32)
                dke = dke + jnp.dot(dsb, qs, preferred_element_type=F32)
                dqt[:, pl.ds(q0, n)] += jnp.dot(ktt, dsb, preferred_element_type=F32)
                cse = cse + jnp.sum(dst, axis=1, keepdims=True)
            dp_ref[pl.ds(k0, FQ), LANE:2 * LANE] = _halves(dke, tmask).astype(BF16)
            dp_ref[pl.ds(k0, FQ), 2 * LANE:3 * LANE] = _halves(dve, tmask).astype(BF16)
            dc_ref[pl.ds(k0, FQ), :] = _halves(jnp.broadcast_to(cse, (2 * FQ, LANE)), tmask)
        dp_ref[:, 0:LANE] = (dqt[...].T * SCALE).astype(BF16)
        tot = [jnp.sum(rsum[HD * e:HD * e + 8, :], axis=0, keepdims=True) for e in range(2)]
        dc_ref[...] = dc_ref[...] - jnp.where(hrow == 0, tot[0], jnp.where(hrow == HD, tot[1], 0.0)).T

    def colblk(off):
        return pl.BlockSpec((S, LANE), lambda b, hp: (b, off + hp))

    wide = pltpu.VMEM((S, 2 * LANE), BF16)
    half = pltpu.VMEM((S, LANE), BF16)
    return pl.pallas_call(
        body, name=name, grid=(nb, NHP),
        in_specs=[_qkv_blk(CB_FOX, 0), _qkv_blk(CB_FOX, 1), _qkv_blk(CB_FOX, 2),
                  pl.BlockSpec((S, LANE), lambda b, hp: (b, 0)), colblk(0), colblk(0), colblk(0)],
        out_specs=[_dqkv_blk(CB_FOX), colblk(0)],
        out_shape=[jax.ShapeDtypeStruct((nb * S, NP), BF16), jax.ShapeDtypeStruct((nb * S, WA), F32)],
        scratch_shapes=[wide, wide, wide, half, half, half, pltpu.VMEM((LANE, S), BF16), pltpu.VMEM((LANE, S), BF16),
                        pltpu.VMEM((8, S), F32), pltpu.VMEM((LANE, S), F32), pltpu.VMEM((LANE, S), F32)],
        compiler_params=_cparams(("parallel", "parallel")),
    )(P, P, P, c, o, lse, do)


DILS = (1, 4, 16)
DB = 128


def _regroup_load(ref, d, scale=None):
    if d == 1:
        v = ref[...]
    else:
        L = S // d
        v = jnp.concatenate([ref[pl.ds(r, L, stride=d), :] for r in range(d)], axis=0)
    return v if scale is None else v * scale


def _regroup_store(ref, d, val_ref, accumulate):
    L = S // d
    for r in range(d):
        src = val_ref[pl.ds(r * L, L), :]
        dst = (slice(None), slice(None)) if d == 1 else (pl.ds(r, L, stride=d), slice(None))
        if accumulate:
            ref[dst] = ref[dst] + src
        else:
            ref[dst] = src


def _dil_bands():
    qi = lax.broadcasted_iota(jnp.int32, (DB, 2 * DB), 0)
    ki = lax.broadcasted_iota(jnp.int32, (DB, 2 * DB), 1)
    band = (ki >= qi) & (ki <= qi + DB)
    return band, band & (ki >= DB)


def _dil_valid(bands, bk, d):
    band, own = bands
    has_prev = (bk % ((S // d) // DB)) > 0
    return own | (band & has_prev)


def _dil_keys(kd, vd, r0, bands, bk, d):
    if S // d == DB:
        valid = bands[1][:, DB:]
        kk, vv = kd[pl.ds(r0 + DB, DB), :], vd[pl.ds(r0 + DB, DB), :]
    else:
        valid = _dil_valid(bands, bk, d)
        kk, vv = kd[pl.ds(r0, 2 * DB), :], vd[pl.ds(r0, 2 * DB), :]
    return kk, vv, jnp.concatenate([valid, valid], axis=0)


def _stack_heads(x, masks):
    zero = jnp.zeros_like(x)
    return jnp.concatenate([jnp.where(masks[0], x, zero), jnp.where(masks[1], x, zero)], axis=0)


def _dil_fwd(P, nb, name):
    nblk = S // DB

    def body(q_ref, k_ref, v_ref, o_ref, lse_ref, qd, kd, vd, rnum, rm, rl, num_n, m_n, l_n):
        masks = _head_masks((DB, LANE))
        bands = _dil_bands()
        for bi, d in enumerate(DILS):
            qd[...] = _regroup_load(q_ref, d, SCALE).astype(BF16)
            kd[pl.ds(0, DB), :] = jnp.zeros((DB, LANE), BF16)
            vd[pl.ds(0, DB), :] = jnp.zeros((DB, LANE), BF16)
            kd[pl.ds(DB, S), :] = _regroup_load(k_ref, d).astype(BF16)
            vd[pl.ds(DB, S), :] = _regroup_load(v_ref, d).astype(BF16)

            def blk(bk, _, d=d):
                r0 = pl.multiple_of(bk * DB, DB)
                qt = qd[pl.ds(r0, DB), :]
                kk, vv, valid = _dil_keys(kd, vd, r0, bands, bk, d)
                qm = _stack_heads(qt, masks)
                s = lax.dot_general(qm, kk, NT_DIMS, preferred_element_type=F32)
                s = jnp.where(valid, s, NEG)
                m = jnp.max(s, axis=1, keepdims=True)
                p = jnp.exp(s - m)
                l = jnp.sum(p, axis=1, keepdims=True)
                num = jnp.dot(p.astype(BF16), vv, preferred_element_type=F32)
                rnum[pl.ds(r0, DB), :] = jnp.where(masks[0], num[0:DB], num[DB:2 * DB])
                rm[pl.ds(r0, DB), :] = jnp.where(masks[0], m[0:DB], m[DB:2 * DB])
                rl[pl.ds(r0, DB), :] = jnp.where(masks[0], l[0:DB], l[DB:2 * DB])
                return 0

            lax.fori_loop(0, nblk, blk, 0, unroll=nblk)
            _regroup_store(num_n.at[bi], d, rnum, False)
            _regroup_store(m_n.at[bi], d, rm, False)
            _regroup_store(l_n.at[bi], d, rl, False)

        m_all = jnp.maximum(jnp.maximum(m_n[0], m_n[1]), m_n[2])
        num = jnp.zeros((S, LANE), F32)
        den = jnp.zeros((S, LANE), F32)
        for bi in range(3):
            a = jnp.exp(m_n[bi] - m_all)
            num = num + a * num_n[bi]
            den = den + a * l_n[bi]
        o_ref[...] = num / den
        lse_ref[...] = m_all + jnp.log(den)

    def colblk(off):
        return pl.BlockSpec((S, LANE), lambda b, hp: (b, off + hp))

    return pl.pallas_call(
        body, name=name, grid=(nb, NHP),
        in_specs=[_qkv_blk(CB_DIL, 0), _qkv_blk(CB_DIL, 1), _qkv_blk(CB_DIL, 2)],
        out_specs=[colblk(0), colblk(0)],
        out_shape=[jax.ShapeDtypeStruct((nb * S, WA), F32), jax.ShapeDtypeStruct((nb * S, WA), F32)],
        scratch_shapes=[pltpu.VMEM((S, LANE), BF16), pltpu.VMEM((S + DB, LANE), BF16), pltpu.VMEM((S + DB, LANE), BF16),
                        pltpu.VMEM((S, LANE), F32), pltpu.VMEM((S, LANE), F32), pltpu.VMEM((S, LANE), F32),
                        pltpu.VMEM((3, S, LANE), F32), pltpu.VMEM((3, S, LANE), F32), pltpu.VMEM((3, S, LANE), F32)],
        compiler_params=_cparams(("parallel", "parallel")),
    )(P, P, P)


def _dil_bwd(P, o, lse, do, dP, nb, name):
    nblk = S // DB

    def body(q_ref, k_ref, v_ref, o_ref, lse_ref, do_ref, dp_in, dp_ref,
             qd, kd, vd, dod, lsed, dsd, dsum, dq_r, dk_r, dv_r, dq_n, dk_n, dv_n):
        del dp_in
        masks = _head_masks((DB, LANE))
        fmask = _head_masks((S, LANE))
        prod = do_ref[...] * o_ref[...]
        d0 = jnp.sum(jnp.where(fmask[0], prod, 0.0), axis=1, keepdims=True)
        d1 = jnp.sum(jnp.where(fmask[1], prod, 0.0), axis=1, keepdims=True)
        dsum[...] = jnp.where(fmask[0], d0, d1)
        tn = (((0,), (0,)), ((), ()))
        bands = _dil_bands()
        for bi, d in enumerate(DILS):
            qd[...] = _regroup_load(q_ref, d, SCALE).astype(BF16)
            kd[pl.ds(0, DB), :] = jnp.zeros((DB, LANE), BF16)
            vd[pl.ds(0, DB), :] = jnp.zeros((DB, LANE), BF16)
            kd[pl.ds(DB, S), :] = _regroup_load(k_ref, d).astype(BF16)
            vd[pl.ds(DB, S), :] = _regroup_load(v_ref, d).astype(BF16)
            dod[...] = _regroup_load(do_ref, d).astype(BF16)
            lsed[...] = _regroup_load(lse_ref, d)
            dsd[...] = _regroup_load(dsum, d)
            dq_r[...] = jnp.zeros_like(dq_r)
            dk_r[...] = jnp.zeros_like(dk_r)
            dv_r[...] = jnp.zeros_like(dv_r)

            def blk(bk, _, d=d):
                r0 = pl.multiple_of(bk * DB, DB)
                qt = qd[pl.ds(r0, DB), :]
                dot = dod[pl.ds(r0, DB), :]
                lt = lsed[pl.ds(r0, DB), :]
                dt = dsd[pl.ds(r0, DB), :]
                kk, vv, valid = _dil_keys(kd, vd, r0, bands, bk, d)
                kw = kk.shape[0]
                qm = _stack_heads(qt, masks)
                dom = _stack_heads(dot, masks)
                lcol = jnp.concatenate([lt[:, 0:1], lt[:, HD:HD + 1]], axis=0)
                dcol = jnp.concatenate([dt[:, 0:1], dt[:, HD:HD + 1]], axis=0)
                s = lax.dot_general(qm, kk, NT_DIMS, preferred_element_type=F32)
                s = jnp.where(valid, s, NEG)
                p = jnp.exp(s - lcol)
                dp = lax.dot_general(dom, vv, NT_DIMS, preferred_element_type=F32)
                ds = (p * (dp - dcol)).astype(BF16)
                dvt = lax.dot_general(p.astype(BF16), dom, tn, preferred_element_type=F32)
                dkt = lax.dot_general(ds, qm, tn, preferred_element_type=F32)
                dqt = jnp.dot(ds, kk, preferred_element_type=F32)
                dq_r[pl.ds(r0, DB), :] = jnp.where(masks[0], dqt[0:DB], dqt[DB:2 * DB])
                dk_r[pl.ds(r0 + 2 * DB - kw, kw), :] += dkt
                dv_r[pl.ds(r0 + 2 * DB - kw, kw), :] += dvt
                return 0

            lax.fori_loop(0, nblk, blk, 0, unroll=nblk)
            _regroup_store(dq_n, d, dq_r, bi > 0)
            _regroup_store(dk_n, d, dk_r.at[pl.ds(DB, S)], bi > 0)
            _regroup_store(dv_n, d, dv_r.at[pl.ds(DB, S)], bi > 0)

        dp_ref[:, 0:LANE] = (dq_n[...] * SCALE).astype(BF16)
        dp_ref[:, LANE:2 * LANE] = dk_n[...].astype(BF16)
        dp_ref[:, 2 * LANE:3 * LANE] = dv_n[...].astype(BF16)

    def colblk(off):
        return pl.BlockSpec((S, LANE), lambda b, hp: (b, off + hp))

    big = pltpu.VMEM((S, LANE), F32)
    bigp = pltpu.VMEM((S + DB, LANE), F32)
    return pl.pallas_call(
        body, name=name, grid=(nb, NHP),
        in_specs=[_qkv_blk(CB_DIL, 0), _qkv_blk(CB_DIL, 1), _qkv_blk(CB_DIL, 2), colblk(0), colblk(0), colblk(0),
                  pl.BlockSpec(memory_space=pl.ANY)],
        out_specs=_dqkv_blk(CB_DIL), out_shape=jax.ShapeDtypeStruct((nb * S, NP), BF16),
        input_output_aliases={6: 0},
        scratch_shapes=[pltpu.VMEM((S, LANE), BF16), pltpu.VMEM((S + DB, LANE), BF16), pltpu.VMEM((S + DB, LANE), BF16),
                        pltpu.VMEM((S, LANE), BF16), big, big, big, big, bigp, bigp, big, big, big],
        compiler_params=_cparams(("parallel", "parallel")),
    )(P, P, P, o, lse, do, dP)


RC = 256
NSHW = 4


def _window(src, start, buf):
    if start % 8 == 0:
        return src[pl.ds(start, RC), :]
    buf[...] = src[pl.ds(start, RC), :]
    return buf[...]
CPAD = 32


NSUB = 8
CROWS = S + CPAD


def _preshift(src, dst):
    for b in range(NSUB):
        dst[b] = src[pl.ds(b, CROWS), :]


def _shifted(dst, start):
    return dst[start % NSUB, pl.ds(start - start % NSUB, RC), :]


def _conv_chunk(gsh, r0, cw_ref, cb_ref):
    acc = jnp.zeros((RC, CC), F32) + cb_ref[...]
    for k in range(CK):
        acc = acc + cw_ref[k:k + 1, :] * _shifted(gsh, r0 + CPAD - (CK - 1) + k)
    return acc


def _cnorm(c0, cng_ref, cnb_ref):
    mu = jnp.mean(c0, axis=1, keepdims=True)
    xc = c0 - mu
    rstd = lax.rsqrt(jnp.mean(xc * xc, axis=1, keepdims=True) + EPS)
    n = xc * rstd
    return n, rstd, n * cng_ref[...] + cnb_ref[...]


NORM_ROWS = 512
YC_BLK = 2 * WA // CC


def _attn_norm_fwd(of, od, gof, god, name):
    T = of.shape[0]

    def body(of_ref, od_ref, gof_ref, god_ref, y_ref):
        for i, (src, g_ref) in enumerate(((of_ref, gof_ref), (od_ref, god_ref))):
            v = src[...]
            r = lax.rsqrt(jnp.mean(v * v, axis=1, keepdims=True) + EPS)
            y_ref[:, i * WA:(i + 1) * WA] = (v * r * g_ref[...]).astype(BF16)

    row = lambda w: pl.BlockSpec((NORM_ROWS, w), lambda i: (i, 0))
    par = pl.BlockSpec((1, WA), lambda i: (0, 0))
    return pl.pallas_call(
        body, name=name, grid=(T // NORM_ROWS,),
        in_specs=[row(WA), row(WA), par, par], out_specs=row(2 * WA),
        out_shape=jax.ShapeDtypeStruct((T, D), BF16),
        compiler_params=_cparams(("parallel",)),
    )(of, od, gof, god)


def _attn_norm_bwd(of, od, dy, gof, god, name):
    T = of.shape[0]

    def body(of_ref, od_ref, dy_ref, gof_ref, god_ref, dof_ref, dod_ref, dgo_ref):
        @pl.when(pl.program_id(0) == 0)
        def _():
            dgo_ref[...] = jnp.zeros_like(dgo_ref)

        for i, (src, g_ref, dst) in enumerate(((of_ref, gof_ref, dof_ref), (od_ref, god_ref, dod_ref))):
            v = src[...]
            dyv = dy_ref[:, i * WA:(i + 1) * WA].astype(F32)
            r = lax.rsqrt(jnp.mean(v * v, axis=1, keepdims=True) + EPS)
            a = dyv * g_ref[...]
            dst[...] = r * a - v * (r * r * r * jnp.mean(v * a, axis=1, keepdims=True))
            dgo_ref[i:i + 1, :] += jnp.sum(dyv * v * r, axis=0, keepdims=True)

    row = lambda w: pl.BlockSpec((NORM_ROWS, w), lambda i: (i, 0))
    par = pl.BlockSpec((1, WA), lambda i: (0, 0))
    return pl.pallas_call(
        body, name=name, grid=(T // NORM_ROWS,),
        in_specs=[row(WA), row(WA), row(2 * WA), par, par],
        out_specs=[row(WA), row(WA), pl.BlockSpec((8, WA), lambda i: (0, 0))],
        out_shape=[jax.ShapeDtypeStruct((T, WA), F32), jax.ShapeDtypeStruct((T, WA), F32),
                   jax.ShapeDtypeStruct((8, WA), F32)],
        compiler_params=_cparams(("arbitrary",)),
    )(of, od, dy, gof, god)


def _conv_specs():
    gblk = lambda off: pl.BlockSpec((S, CC), lambda b: (b, off))
    par = lambda r: pl.BlockSpec((r, CC), lambda b: (0, 0))
    return gblk, par


def _conv_fwd(P, y, cw, cb, cng, cnb, nb, name):
    def body(gv_ref, gg_ref, cw_ref, cb_ref, cng_ref, cnb_ref, y_in, y_ref, c0_ref, gpad, gsh):
        del y_in
        gpad[pl.ds(0, CPAD), :] = jnp.zeros((CPAD, CC), F32)
        gpad[pl.ds(CPAD, S), :] = gv_ref[...] * _sigmoid(gg_ref[...])
        gpad[pl.ds(CROWS, NSUB), :] = jnp.zeros((NSUB, CC), F32)
        _preshift(gpad, gsh)
        for ci in range(S // RC):
            r0 = ci * RC
            c0 = _conv_chunk(gsh, r0, cw_ref, cb_ref)
            c0_ref[pl.ds(r0, RC), :] = c0
            _, _, z = _cnorm(c0, cng_ref, cnb_ref)
            y_ref[pl.ds(r0, RC), :] = (z * _sigmoid(z)).astype(BF16)

    gblk, par = _conv_specs()
    return pl.pallas_call(
        body, name=name, grid=(nb,),
        in_specs=[gblk(CB_GV), gblk(CB_GG), par(CPAD), par(1), par(1), par(1), pl.BlockSpec(memory_space=pl.ANY)],
        out_specs=[gblk(YC_BLK), gblk(0)],
        out_shape=[jax.ShapeDtypeStruct((nb * S, D), BF16), jax.ShapeDtypeStruct((nb * S, CC), F32)],
        input_output_aliases={6: 0},
        scratch_shapes=[pltpu.VMEM((CROWS + NSUB, CC), F32), pltpu.VMEM((NSUB, CROWS, CC), F32)],
        compiler_params=_cparams(("parallel",)),
    )(P, P, cw, cb, cng, cnb, y)


def _conv_bwd(P, c0, dy, dfa, dP, cw, cng, cnb, nb, name):
    def body(gv_ref, gg_ref, c0_ref, dy_ref, dfa_ref, cw_ref, cng_ref, cnb_ref, dp_in, dg_ref, dcw_ref, dsm_ref,
             dpad, dsh):
        del dp_in
        dg_ref[:, 2 * CC:2 * CC + LANE] = dfa_ref[...]
        dg_ref[:, 2 * CC + LANE:3 * CC] = jnp.zeros((S, CC - LANE), BF16)
        @pl.when(pl.program_id(0) == 0)
        def _():
            dcw_ref[...] = jnp.zeros_like(dcw_ref)
            dsm_ref[...] = jnp.zeros_like(dsm_ref)

        dpad[pl.ds(S, CPAD + NSUB), :] = jnp.zeros((CPAD + NSUB, CC), F32)
        zero = jnp.zeros((8, CC), F32)
        dcb, dcng, dcnb = zero, zero, zero
        for ci in range(S // RC):
            r0 = ci * RC
            n, rstd, z = _cnorm(c0_ref[pl.ds(r0, RC), :], cng_ref, cnb_ref)
            sz = _sigmoid(z)
            dz = dy_ref[pl.ds(r0, RC), :].astype(F32) * (sz * (1.0 + z * (1.0 - sz)))
            dcng = dcng + _sum8(dz * n)
            dcnb = dcnb + _sum8(dz)
            dn = dz * cng_ref[...]
            dc0 = rstd * (dn - jnp.mean(dn, axis=1, keepdims=True) - n * jnp.mean(dn * n, axis=1, keepdims=True))
            dcb = dcb + _sum8(dc0)
            dpad[pl.ds(r0, RC), :] = dc0
        dsm_ref[0:1, :] += jnp.sum(dcb, axis=0, keepdims=True)
        dsm_ref[1:2, :] += jnp.sum(dcng, axis=0, keepdims=True)
        dsm_ref[2:3, :] += jnp.sum(dcnb, axis=0, keepdims=True)

        _preshift(dpad, dsh)
        dws = [zero] * CK
        for ci in range(S // RC):
            r0 = ci * RC
            sg = _sigmoid(gg_ref[pl.ds(r0, RC), :])
            gvc = gv_ref[pl.ds(r0, RC), :]
            glu = gvc * sg
            dgl = jnp.zeros((RC, CC), F32)
            for k in range(CK):
                win = _shifted(dsh, r0 + (CK - 1) - k)
                dws[k] = dws[k] + _sum8(win * glu)
                dgl = dgl + cw_ref[k:k + 1, :] * win
            dg_ref[pl.ds(r0, RC), 0:CC] = (dgl * sg).astype(BF16)
            dg_ref[pl.ds(r0, RC), CC:2 * CC] = (dgl * gvc * sg * (1.0 - sg)).astype(BF16)
        for k in range(CK):
            dcw_ref[k:k + 1, :] += jnp.sum(dws[k], axis=0, keepdims=True)

    gblk, par = _conv_specs()
    return pl.pallas_call(
        body, name=name, grid=(nb,),
        in_specs=[gblk(CB_GV), gblk(CB_GG), gblk(0), gblk(YC_BLK), pl.BlockSpec((S, LANE), lambda b: (b, 0)),
                  par(CPAD), par(1), par(1), pl.BlockSpec(memory_space=pl.ANY)],
        out_specs=[pl.BlockSpec((S, 3 * CC), lambda b: (b, CB_GV // 3)), par(CPAD), par(8)],
        out_shape=[jax.ShapeDtypeStruct((nb * S, NP), BF16), jax.ShapeDtypeStruct((CPAD, CC), F32),
                   jax.ShapeDtypeStruct((8, CC), F32)],
        input_output_aliases={8: 0},
        scratch_shapes=[pltpu.VMEM((CROWS + NSUB, CC), F32), pltpu.VMEM((NSUB, CROWS, CC), F32)],
        compiler_params=_cparams(("arbitrary",)),
    )(P, P, c0, dy, dfa, cw, cng, cnb, dP)


FC = 512
FPAD = 8
NFB = 2 * DFF // FC


def _ffn_u2_chunk(upad, r0, fw_ref, fb_ref):
    acc = jnp.zeros((RC, FC), F32) + fb_ref[...]
    for k in range(FK):
        acc = acc + fw_ref[k:k + 1, :] * upad[pl.ds(r0 + FPAD - (FK - 1) + k, RC), :]
    return acc


def _ffn_fwd(U, fw, fb, nb, name):
    def body(u_ref, fw_ref, fb_ref, h_ref, u2_ref, upad):
        upad[pl.ds(0, FPAD), :] = jnp.zeros((FPAD, FC), F32)
        upad[pl.ds(FPAD, S), :] = u_ref[...].astype(F32)
        for ci in range(S // RC):
            r0 = ci * RC
            u2 = _ffn_u2_chunk(upad, r0, fw_ref, fb_ref)
            u2_ref[pl.ds(r0, RC), :] = u2.astype(BF16)
            a2, b2 = u2[:, :FC // 2], u2[:, FC // 2:]
            h_ref[pl.ds(r0, RC), :] = (a2 * _sigmoid(a2) * b2).astype(BF16)

    return pl.pallas_call(
        body, name=name, grid=(nb, NFB),
        in_specs=[pl.BlockSpec((S, FC), lambda b, j: (b, j)), pl.BlockSpec((8, FC), lambda b, j: (0, j)),
                  pl.BlockSpec((1, FC), lambda b, j: (0, j))],
        out_specs=[pl.BlockSpec((S, FC // 2), lambda b, j: (b, j)), pl.BlockSpec((S, FC), lambda b, j: (b, j))],
        out_shape=[jax.ShapeDtypeStruct((nb * S, DFF), BF16), jax.ShapeDtypeStruct((nb * S, 2 * DFF), BF16)],
        scratch_shapes=[pltpu.VMEM((S + FPAD, FC), F32)],
        compiler_params=_cparams(("parallel", "parallel")),
    )(U, fw, fb)


def _ffn_bwd(U, U2, dhid, fw, nb, name):
    def body(u_ref, u2_ref, dh_ref, fw_ref, du_ref, dfw_ref, dpad, shw):
        @pl.when(pl.program_id(1) == 0)
        def _():
            dfw_ref[...] = jnp.zeros_like(dfw_ref)

        dpad[pl.ds(S, FPAD), :] = jnp.zeros((FPAD, FC), F32)
        zero = jnp.zeros((8, FC), F32)
        dbias = zero
        for ci in range(S // RC):
            r0 = ci * RC
            u2 = u2_ref[pl.ds(r0, RC), :].astype(F32)
            a2, b2 = u2[:, :FC // 2], u2[:, FC // 2:]
            sa = _sigmoid(a2)
            dh = dh_ref[pl.ds(r0, RC), :].astype(F32)
            du2 = jnp.concatenate([dh * b2 * (sa * (1.0 + a2 * (1.0 - sa))), dh * a2 * sa], axis=1)
            dpad[pl.ds(r0, RC), :] = du2
            dbias = dbias + _sum8(du2)
        dws = [zero] * FK
        for ci in range(S // RC):
            r0 = ci * RC
            uc = u_ref[pl.ds(r0, RC), :].astype(F32)
            du = jnp.zeros((RC, FC), F32)
            for k in range(FK):
                win = _window(dpad, r0 + (FK - 1) - k, shw.at[k % NSHW])
                dws[k] = dws[k] + _sum8(win * uc)
                du = du + fw_ref[k:k + 1, :] * win
            du_ref[pl.ds(r0, RC), :] = du.astype(BF16)
        for k in range(FK):
            dfw_ref[k:k + 1, :] += jnp.sum(dws[k], axis=0, keepdims=True)
        dfw_ref[FK:FK + 1, :] += jnp.sum(dbias, axis=0, keepdims=True)

    blk = pl.BlockSpec((S, FC), lambda j, b: (b, j))
    return pl.pallas_call(
        body, name=name, grid=(NFB, nb),
        in_specs=[blk, blk, pl.BlockSpec((S, FC // 2), lambda j, b: (b, j)), pl.BlockSpec((8, FC), lambda j, b: (0, j))],
        out_specs=[blk, pl.BlockSpec((8, FC), lambda j, b: (0, j))],
        out_shape=[jax.ShapeDtypeStruct((nb * S, 2 * DFF), BF16), jax.ShapeDtypeStruct((8, 2 * DFF), F32)],
        scratch_shapes=[pltpu.VMEM((S + FPAD, FC), F32), pltpu.VMEM((NSHW, RC, FC), F32)],
        compiler_params=_cparams(("parallel", "arbitrary")),
    )(U, U2, dhid, fw)


def _matmul_ffn(a, b, mode, *, out_dtype=F32, tm=1024, tk=2048, norm_g=None, name):
    HF = FC // 2
    if mode == "fwd":
        M, K = a.shape
        tm = 512

        def body(a_ref, g_ref, b_ref, h_ref, o_ref):
            av = _rms_rows(a_ref[...], g_ref)
            h_ref[...] = av
            for j in range(NFB):
                for half in range(2):
                    bv = b_ref[:, half * DFF + j * HF:half * DFF + (j + 1) * HF]
                    o_ref[:, j * FC + half * HF:j * FC + (half + 1) * HF] = jnp.dot(
                        av, bv, preferred_element_type=F32).astype(o_ref.dtype)

        return pl.pallas_call(
            body, name=name, grid=(M // tm,),
            in_specs=[pl.BlockSpec((tm, K), lambda i: (i, 0)), pl.BlockSpec((1, K), lambda i: (0, 0)),
                      pl.BlockSpec((K, 2 * DFF), lambda i: (0, 0))],
            out_specs=[pl.BlockSpec((tm, K), lambda i: (i, 0)), pl.BlockSpec((tm, 2 * DFF), lambda i: (i, 0))],
            out_shape=[jax.ShapeDtypeStruct((M, K), BF16), jax.ShapeDtypeStruct((M, 2 * DFF), out_dtype)],
            compiler_params=_cparams(("parallel",)),
        )(a, norm_g.reshape(1, K), b)
    if mode == "dx":
        M = a.shape[0]
        N = b.shape[0]
        tm = 512

        def body(a_ref, b_ref, o_ref):
            acc = None
            for j in range(NFB):
                for half in range(2):
                    av = a_ref[:, j * FC + half * HF:j * FC + (half + 1) * HF]
                    bv = b_ref[:, half * DFF + j * HF:half * DFF + (j + 1) * HF]
                    d = lax.dot_general(av, bv, NT_DIMS, preferred_element_type=F32)
                    acc = d if acc is None else acc + d
            o_ref[...] = acc.astype(o_ref.dtype)

        return pl.pallas_call(
            body, name=name, grid=(M // tm,),
            in_specs=[pl.BlockSpec((tm, 2 * DFF), lambda i: (i, 0)), pl.BlockSpec((N, 2 * DFF), lambda i: (0, 0))],
            out_specs=pl.BlockSpec((tm, N), lambda i: (i, 0)),
            out_shape=jax.ShapeDtypeStruct((M, N), out_dtype),
            compiler_params=_cparams(("parallel",)),
        )(a, b)
    assert mode == "dw"
    T, M = a.shape
    nk = T // tk

    def body(a_ref, g_ref, oa_ref, ob_ref, acc):
        k = pl.program_id(1)
        prod = lax.dot_general(a_ref[...], g_ref[...], (((0,), (0,)), ((), ())), preferred_element_type=F32)

        @pl.when(k == 0)
        def _():
            acc[...] = prod

        @pl.when(k > 0)
        def _():
            acc[...] += prod

        @pl.when(k == nk - 1)
        def _():
            oa_ref[...] = acc[:, :HF]
            ob_ref[...] = acc[:, HF:]

    half = pl.BlockSpec((M, HF), lambda j, k: (0, j))
    return pl.pallas_call(
        body, name=name, grid=(NFB, nk),
        in_specs=[pl.BlockSpec((tk, M), lambda j, k: (k, 0)), pl.BlockSpec((tk, FC), lambda j, k: (k, j))],
        out_specs=[half, half],
        out_shape=[jax.ShapeDtypeStruct((M, DFF), F32)] * 2,
        scratch_shapes=[pltpu.VMEM((M, FC), F32)],
        compiler_params=_cparams(("parallel", "arbitrary")),
    )(a, b)


def _adamw_body(w_ref, g_ref, m_ref, v_ref, d_ref, nm_ref, nv_ref):
    g = g_ref[...]
    m = ADAM_B1 * m_ref[...] + (1.0 - ADAM_B1) * g
    v = ADAM_B2 * v_ref[...] + (1.0 - ADAM_B2) * (g * g)
    m_hat = m / (1.0 - ADAM_B1 ** ADAM_STEP)
    v_hat = v / (1.0 - ADAM_B2 ** ADAM_STEP)
    d_ref[...] = -ADAM_LR * (m_hat / (jnp.sqrt(v_hat) + ADAM_EPS) + ADAM_WD * w_ref[...])
    nm_ref[...] = m
    nv_ref[...] = v


def _adamw(w, g, m, v, name):
    shape = w.shape
    R = 1
    for s in shape[:-1]:
        R *= s
    C = shape[-1]
    args = [a.reshape(R, C) for a in (w, g, m, v)]
    tr = R
    for cand in (512, 352, 256, 128, 64, 32, 16, 8):
        if R % cand == 0 and cand * C * 4 * 14 <= 24 * 1024 * 1024:
            tr = cand
            break
    blk = pl.BlockSpec((tr, C), lambda i: (i, 0))
    outs = pl.pallas_call(
        functools.partial(_adamw_body), name=name, grid=(R // tr,),
        in_specs=[blk] * 4, out_specs=[blk] * 3,
        out_shape=[jax.ShapeDtypeStruct((R, C), F32)] * 3,
        compiler_params=_cparams(("parallel",)),
    )(*args)
    return [o.reshape(shape) for o in outs]


REF_FOX, REF_DIL, REF_GATE = 0, N_QKV + N_FG, 2 * N_QKV + N_FG


def _pack_cols_value(w):
    parts = []
    for ref0 in (REF_FOX, REF_DIL):
        for hp in range(NHP):
            parts += [w[:, ref0 + i * WA + hp * LANE:ref0 + i * WA + (hp + 1) * LANE] for i in range(3)]
    parts += [w[:, REF_GATE:NIN], w[:, N_QKV:N_QKV + N_FG], jnp.zeros((w.shape[0], NP - NIN), w.dtype)]
    return jnp.concatenate(parts, axis=1)


def _unpack_cols_value(g):
    def group(cb):
        return [g[:, (cb + 3 * hp + i) * LANE:(cb + 3 * hp + i + 1) * LANE] for i in range(3) for hp in range(NHP)]
    fa0 = 2 * N_QKV + 2 * CC
    return jnp.concatenate(group(CB_FOX) + [g[:, fa0:fa0 + N_FG]] + group(CB_DIL) + [g[:, 2 * N_QKV:fa0]], axis=1)


def _adamw_layers(w, gs, m, v, name, packed=False):
    _, R, C = w.shape
    Cg = gs[0].shape[1]
    tr = 128 if R % 128 == 0 else 176
    assert R % tr == 0 and len(gs) == DEPTH == 2

    def body(w_ref, g0_ref, g1_ref, m_ref, v_ref, g_out, d_ref, nm_ref, nv_ref):
        g = jnp.where(pl.program_id(0) == 0, g0_ref[...], g1_ref[...])
        if packed:
            g = _unpack_cols_value(g)
        g_out[...] = g
        mn = ADAM_B1 * m_ref[...] + (1.0 - ADAM_B1) * g
        vn = ADAM_B2 * v_ref[...] + (1.0 - ADAM_B2) * (g * g)
        m_hat = mn / (1.0 - ADAM_B1 ** ADAM_STEP)
        v_hat = vn / (1.0 - ADAM_B2 ** ADAM_STEP)
        d_ref[...] = -ADAM_LR * (m_hat / (jnp.sqrt(v_hat) + ADAM_EPS) + ADAM_WD * w_ref[...])
        nm_ref[...] = mn
        nv_ref[...] = vn

    lay = pl.BlockSpec((None, tr, C), lambda l, i: (l, i, 0))
    gsp = pl.BlockSpec((tr, Cg), lambda l, i: (i, 0))
    return pl.pallas_call(
        body, name=name, grid=(DEPTH, R // tr),
        in_specs=[lay, gsp, gsp, lay, lay], out_specs=[lay] * 4,
        out_shape=[jax.ShapeDtypeStruct((DEPTH, R, C), F32)] * 4,
        compiler_params=_cparams(("parallel", "parallel")),
    )(w, gs[0], gs[1], m, v)


def _pack_w_in(w_in):
    _, R, _ = w_in.shape

    def body(w_ref, o_ref):
        o_ref[...] = _pack_cols_value(w_ref[...]).astype(BF16)

    return pl.pallas_call(
        body, name="pack_w_in", grid=(DEPTH,),
        in_specs=[pl.BlockSpec((None, R, NIN), lambda l: (l, 0, 0))],
        out_specs=pl.BlockSpec((None, R, NP), lambda l: (l, 0, 0)),
        out_shape=jax.ShapeDtypeStruct((DEPTH, R, NP), BF16),
        compiler_params=_cparams(("parallel",)),
    )(w_in)


def _rs_row_tile(H):
    th = 128 if H % 128 == 0 else 176
    assert H % th == 0
    return th


def _add_half(g, r1, place, name):
    _, R, C = g.shape
    H = R // 2
    th = _rs_row_tile(H)
    nh = H // th

    def body(s_ref, g_ref, r_ref, o_ref):
        o_ref[...] = (g_ref[...] + r_ref[...]).astype(BF16)

    grid_spec = pltpu.PrefetchScalarGridSpec(
        num_scalar_prefetch=1, grid=(NCHIP, nh),
        in_specs=[pl.BlockSpec((None, th, C), lambda p, i, s: (p, s[1] * nh + i, 0)),
                  pl.BlockSpec((None, th, C), lambda p, i, s: (p, i, 0))],
        out_specs=pl.BlockSpec((None, th, C), lambda p, i, s: (p, i, 0)))
    return pl.pallas_call(
        body, name=name, grid_spec=grid_spec, out_shape=jax.ShapeDtypeStruct((NCHIP, H, C), BF16),
        compiler_params=_cparams(("parallel", "parallel")),
    )(place, g, r1)


def _sum_slots(g, r1, r2, place, name):
    _, R, C = g.shape
    H = R // 2
    th = _rs_row_tile(H)
    nh = H // th

    def body(s_ref, g_ref, r1_ref, r2_ref, o_ref):
        acc = g_ref[...] + r1_ref[...]
        for j in range(NCHIP - 1):
            acc = acc + r2_ref[j].astype(F32)
        o_ref[...] = acc

    grid_spec = pltpu.PrefetchScalarGridSpec(
        num_scalar_prefetch=1, grid=(nh,),
        in_specs=[pl.BlockSpec((None, th, C), lambda i, s: (s[0], s[1] * nh + i, 0)),
                  pl.BlockSpec((None, th, C), lambda i, s: (s[0], i, 0)),
                  pl.BlockSpec((NCHIP - 1, th, C), lambda i, s: (0, i, 0))],
        out_specs=pl.BlockSpec((None, th, C), lambda i, s: (s[1], i, 0)))
    return pl.pallas_call(
        body, name=name, grid_spec=grid_spec, out_shape=jax.ShapeDtypeStruct((2, H, C), F32),
        compiler_params=_cparams(("parallel",)),
    )(place, g, r1, r2)


MESH = pl.DeviceIdType.MESH
HBM = pl.BlockSpec(memory_space=pltpu.HBM)


def _place():
    x, y, c = lax.axis_index("x"), lax.axis_index("y"), lax.axis_index("c")
    chips = [(1 - x, y), (x, 1 - y), (1 - x, 1 - y)]
    return x, y, c, chips


def _rcopy(src, dst, ssem, rsem, dev):
    return pltpu.make_async_remote_copy(src_ref=src, dst_ref=dst, send_sem=ssem, recv_sem=rsem,
                                        device_id=dev, device_id_type=MESH)


AG_CHUNK_BYTES = 1 << 20


def _stage_rows(R, C, dtype):
    rows = R
    while rows * C * jnp.dtype(dtype).itemsize > AG_CHUNK_BYTES and rows % 32 == 0:
        rows //= 2
    return rows


def _allgather(shards, split):
    n = len(shards)
    nout = n * DEPTH
    rows = [_stage_rows(s.shape[1], s.shape[2], s.dtype) for s in shards]

    def body(*refs):
        ins, outs = refs[:n], refs[n:n + nout]
        stages = refs[n + nout:2 * n + nout]
        ssem, rsem, fssem, frsem, isem, osem = refs[2 * n + nout:]
        x, y, c, chips = _place()
        me = 2 * x + y
        sib = (x, y, 1 - c)

        def window(t, l, chip, half):
            if not split[t]:
                return outs[t * DEPTH + l].at[chip]
            H = shards[t].shape[1] // 2
            return outs[t * DEPTH + l].at[chip, pl.ds(half * H, H)]

        sends = []
        for t in range(n):
            H = shards[t].shape[1] // 2
            for l in range(DEPTH):
                src = ins[t].at[l, pl.ds(c * H, H)] if split[t] else ins[t].at[l]
                for j, (cx, cy) in enumerate(chips):
                    k = (t * DEPTH + l) * 3 + j
                    cp = _rcopy(src, window(t, l, me, c), ssem.at[k], rsem.at[k], (cx, cy, c))
                    cp.start()
                    sends.append(cp)
        for t in range(n):
            nch = shards[t].shape[1] // rows[t]
            outc = []
            for l in range(DEPTH):
                for i in range(nch):
                    slot = len(outc) % 2
                    if len(outc) >= 2:
                        outc[-2].wait()
                    rs = pl.ds(i * rows[t], rows[t])
                    cin = pltpu.make_async_copy(ins[t].at[l, rs], stages[t].at[slot], isem.at[t])
                    cin.start()
                    cin.wait()
                    co = pltpu.make_async_copy(stages[t].at[slot], outs[t * DEPTH + l].at[me, rs], osem.at[2 * t + slot])
                    co.start()
                    outc.append(co)
            for co in outc[-2:]:
                co.wait()
        for t in range(n):
            for l in range(DEPTH):
                for j, (cx, cy) in enumerate(chips):
                    k = (t * DEPTH + l) * 3 + j
                    win = window(t, l, 2 * cx + cy, c)
                    _rcopy(win, win, ssem.at[k], rsem.at[k], (cx, cy, c)).wait_recv()
                    if split[t]:
                        cp = _rcopy(win, win, fssem.at[k], frsem.at[k], sib)
                        cp.start()
                        sends.append(cp)
        for t in range(n):
            if split[t]:
                for l in range(DEPTH):
                    for j, (cx, cy) in enumerate(chips):
                        k = (t * DEPTH + l) * 3 + j
                        win = window(t, l, 2 * cx + cy, 1 - c)
                        _rcopy(win, win, fssem.at[k], frsem.at[k], sib).wait_recv()
        for cp in sends:
            cp.wait_send()

    out_shape = [jax.ShapeDtypeStruct((NCHIP,) + s.shape[1:], s.dtype) for s in shards for _ in range(DEPTH)]
    outs = pl.pallas_call(
        body, name="allgather_weights", in_specs=[HBM] * n, out_specs=[HBM] * nout, out_shape=out_shape,
        scratch_shapes=[pltpu.VMEM((2, r, s.shape[2]), s.dtype) for r, s in zip(rows, shards)]
        + [pltpu.SemaphoreType.DMA((3 * nout,))] * 4 + [pltpu.SemaphoreType.DMA((n,)), pltpu.SemaphoreType.DMA((2 * n,))],
        compiler_params=pltpu.CompilerParams(vmem_limit_bytes=VMEM_LIMIT),
    )(*shards)
    return [outs[t * DEPTH:(t + 1) * DEPTH] for t in range(n)]


def _rs_pair_exchange(gs):
    n = len(gs)

    def body(*refs):
        ins, outs = refs[:n], refs[n:2 * n]
        ssem, rsem = refs[2 * n:]
        x, y, c, _ = _place()
        cps = []
        for t in range(n):
            H = gs[t].shape[1] // 2
            cp = _rcopy(ins[t].at[:, pl.ds((1 - c) * H, H)], outs[t], ssem.at[t], rsem.at[t], (x, y, 1 - c))
            cp.start()
            cps.append(cp)
        for cp in cps:
            cp.wait_recv()
        for cp in cps:
            cp.wait_send()

    out_shape = [jax.ShapeDtypeStruct((NCHIP, g.shape[1] // 2, g.shape[2]), F32) for g in gs]
    return pl.pallas_call(
        body, name="rs_pair_exchange", in_specs=[HBM] * n, out_specs=[HBM] * n, out_shape=out_shape,
        scratch_shapes=[pltpu.SemaphoreType.DMA((n,))] * 2,
    )(*gs)


def _rs_chip_scatter(hs):
    n = len(hs)

    def body(*refs):
        ins, outs = refs[:n], refs[n:2 * n]
        ssem, rsem = refs[2 * n:]
        x, y, c, chips = _place()
        sends = []
        for t in range(n):
            for j, (cx, cy) in enumerate(chips):
                cp = _rcopy(ins[t].at[2 * cx + cy], outs[t].at[j], ssem.at[3 * t + j], rsem.at[3 * t + j], (cx, cy, c))
                cp.start()
                sends.append(cp)
        for cp in sends:
            cp.wait_recv()
        for cp in sends:
            cp.wait_send()

    out_shape = [jax.ShapeDtypeStruct((NCHIP - 1,) + h.shape[1:], h.dtype) for h in hs]
    return pl.pallas_call(
        body, name="rs_chip_scatter", in_specs=[HBM] * n, out_specs=[HBM] * n, out_shape=out_shape,
        scratch_shapes=[pltpu.SemaphoreType.DMA((3 * n,))] * 2,
    )(*hs)


def _rs_pair_gather(fs):
    n = len(fs)

    def body(*refs):
        bufs = refs[n:2 * n]
        ssem, rsem = refs[2 * n:]
        x, y, c, _ = _place()
        sends = []
        for t in range(n):
            cp = _rcopy(bufs[t].at[c], bufs[t].at[c], ssem.at[t], rsem.at[t], (x, y, 1 - c))
            cp.start()
            sends.append(cp)
        for t in range(n):
            win = bufs[t].at[1 - c]
            _rcopy(win, win, ssem.at[t], rsem.at[t], (x, y, 1 - c)).wait_recv()
        for cp in sends:
            cp.wait_send()

    out_shape = [jax.ShapeDtypeStruct(f.shape, F32) for f in fs]
    return pl.pallas_call(
        body, name="rs_pair_gather", in_specs=[HBM] * n, out_specs=[HBM] * n, out_shape=out_shape,
        input_output_aliases={t: t for t in range(n)},
        scratch_shapes=[pltpu.SemaphoreType.DMA((n,))] * 2,
    )(*fs)


def _allreduce_small(buf):
    R = buf.shape[0]

    def body(in_ref, out_ref, slots, ssem, rsem):
        x, y, c, _ = _place()
        me = 4 * x + 2 * y + c
        slots[me] = in_ref[...]
        cps = []
        for k in range(1, NDEV):
            px = 1 - x if k & 4 else x
            py = 1 - y if k & 2 else y
            pc = 1 - c if k & 1 else c
            cp = _rcopy(in_ref, slots.at[me], ssem.at[k - 1], rsem.at[k - 1], (px, py, pc))
            cp.start()
            cps.append((cp, 4 * px + 2 * py + pc))
        for k, (cp, peer) in enumerate(cps):
            _rcopy(in_ref, slots.at[peer], ssem.at[k], rsem.at[k], (x, y, c)).wait_recv()
        for cp, _ in cps:
            cp.wait_send()
        acc = slots[0]
        for p in range(1, NDEV):
            acc = acc + slots[p]
        out_ref[...] = acc

    return pl.pallas_call(
        body, name="allreduce_small", out_shape=jax.ShapeDtypeStruct((R, LANE), F32),
        in_specs=[pl.BlockSpec(memory_space=pltpu.VMEM)], out_specs=pl.BlockSpec(memory_space=pltpu.VMEM),
        scratch_shapes=[pltpu.VMEM((NDEV, R, LANE), F32), pltpu.SemaphoreType.DMA((NDEV - 1,)),
                        pltpu.SemaphoreType.DMA((NDEV - 1,))],
        compiler_params=pltpu.CompilerParams(vmem_limit_bytes=VMEM_LIMIT),
    )(buf)


def _interleave(a):
    lead = a.shape[:-1]
    return a.reshape(*lead, 2, NFB, FC // 2).swapaxes(-3, -2).reshape(*lead, 2 * DFF)


def _uninterleave(a):
    lead = a.shape[:-1]
    return a.reshape(*lead, NFB, 2, FC // 2).swapaxes(-3, -2).reshape(*lead, 2 * DFF)


def _train_compute(xt, tgt, W, nb):
    saved = []
    xc = xt
    for l in range(DEPTH):
        t = f"_l{l}"
        h, P = _rms_matmul(xc, W["ln1"][l], W["in"][l], tm=512, name="proj_in" + t)
        c = _forget_fwd(P, W["bf"][l], nb, "forget_fwd" + t)
        of, lsef = _fox_fwd(P, c, nb, "fox_fwd" + t)
        od, lsed = _dil_fwd(P, nb, "dil_fwd" + t)
        convp = (W["cw"][l], W["cb"][l], W["cng"][l], W["cnb"][l])
        y = _attn_norm_fwd(of, od, W["gof"][l], W["god"][l], "attn_norm_fwd" + t)
        y, c0 = _conv_fwd(P, y, *convp, nb, "conv_fwd" + t)
        xm = _matmul(y, W["o"][l], add=xc, tm=1024, tn=1024, tk=D, name="proj_out" + t)
        h2, U = _matmul_ffn(xm, W["up"][l], "fwd", out_dtype=BF16, norm_g=W["ln2"][l], name="ffn_up" + t)
        hid, U2 = _ffn_fwd(U, W["fw"][l], W["fb"][l], nb, "ffn_act_fwd" + t)
        xo = _matmul(hid, W["down"][l], add=xm, tm=512, tn=D, tk=DFF, name="ffn_down" + t)
        saved.append((xc, h, P, c, of, lsef, od, lsed, convp, c0, y, xm, h2, U, U2, hid))
        xc = xo

    loss8, dx, dxb, dgfin = _loss_head(xc, W["gfin"], tgt, "loss_head")

    big = [None] * DEPTH
    small = [None] * DEPTH
    tk_dw = min(4096, xt.shape[0])
    for l in reversed(range(DEPTH)):
        t = f"_l{l}"
        xin, h, P, c, of, lsef, od, lsed, convp, c0, y, xm, h2, U, U2, hid = saved[l]
        dhid = _matmul(dxb, W["down"][l], tb=True, out_dtype=BF16, tm=1024, tn=DFF, tk=D, name="ffn_down_dx" + t)
        dWd = _matmul(hid, dxb, ta=True, tm=DFF // 2, tn=D, tk=2048, name="ffn_down_dw" + t)
        dU, dfw = _ffn_bwd(U, U2, dhid, W["fw"][l], nb, "ffn_act_bwd" + t)
        dh2 = _matmul_ffn(dU, W["up"][l], "dx", out_dtype=BF16, name="ffn_up_dx" + t)
        dWup = _matmul_ffn(h2, dU, "dw", tk=tk_dw, name="ffn_up_dw" + t)
        dxm, dxmb, dln2 = _rms_bwd(xm, W["ln2"][l], dh2, dx, "rms2_bwd" + t)
        dy = _matmul(dxmb, W["o"][l], tb=True, out_dtype=BF16, tm=1024, tn=D, tk=D, name="proj_out_dx" + t)
        dWo = _matmul(y, dxmb, ta=True, tm=D, tn=D, tk=tk_dw, name="proj_out_dw" + t)
        dof, dod, dgo = _attn_norm_bwd(of, od, dy, W["gof"][l], W["god"][l], "attn_norm_bwd" + t)
        dP, dcb = _fox_bwd(P, c, of, lsef, dof, nb, "fox_bwd" + t)
        dfa, dbf = _forget_bwd(P, W["bf"][l], dcb, nb, "forget_bwd" + t)
        dP = _dil_bwd(P, od, lsed, dod, dP, nb, "dil_bwd" + t)
        dP, dcw, dsm = _conv_bwd(P, c0, dy, dfa, dP, convp[0], convp[2], convp[3], nb, "conv_bwd" + t)
        dh = _matmul(dP, W["in"][l], tb=True, out_dtype=BF16, tm=1024, tn=D, tk=NP, name="proj_in_dx" + t)
        dWin = _matmul(h, dP, ta=True, tm=D, tn=1024, tk=tk_dw, name="proj_in_dw" + t)
        dx, dxb, dln1 = _rms_bwd(xin, W["ln1"][l], dh, dxm, "rms1_bwd" + t)
        big[l] = (dWin, dWo, dWup, dWd)
        small[l] = (dln1, dbf, dgo, dcw, dsm, dln2, dfw)
    return loss8, dx, big, small, dgfin


_SMALL_ROWS = (D // LANE, 8, 8 * WA // LANE, CPAD * CC // LANE, 8 * CC // LANE, D // LANE, 8 * 2 * DFF // LANE)


def kernel(x, ln1_g, w_in, b_forget, g_out_fox, g_out_dil, conv_w, conv_b, cnorm_g, cnorm_b, w_o, ln2_g, w_up, ffn_conv_w, ffn_conv_b, w_down, g_final, loss_target, m_ln1_g, m_w_in, m_b_forget, m_g_out_fox, m_g_out_dil, m_conv_w, m_conv_b, m_cnorm_g, m_cnorm_b, m_w_o, m_ln2_g, m_w_up, m_ffn_conv_w, m_ffn_conv_b, m_w_down, m_g_final, v_ln1_g, v_w_in, v_b_forget, v_g_out_fox, v_g_out_dil, v_conv_w, v_conv_b, v_cnorm_g, v_cnorm_b, v_w_o, v_ln2_g, v_w_up, v_ffn_conv_w, v_ffn_conv_b, v_w_down, v_g_final):
    names = ("ln1_g", "w_in", "b_forget", "g_out_fox", "g_out_dil", "conv_w", "conv_b", "cnorm_g", "cnorm_b",
             "w_o", "ln2_g", "w_up", "ffn_conv_w", "ffn_conv_b", "w_down", "g_final")
    w = dict(zip(names, (ln1_g, w_in, b_forget, g_out_fox, g_out_dil, conv_w, conv_b, cnorm_g, cnorm_b,
                         w_o, ln2_g, w_up, ffn_conv_w, ffn_conv_b, w_down, g_final)))
    m = dict(zip(names, (m_ln1_g, m_w_in, m_b_forget, m_g_out_fox, m_g_out_dil, m_conv_w, m_conv_b, m_cnorm_g,
                         m_cnorm_b, m_w_o, m_ln2_g, m_w_up, m_ffn_conv_w, m_ffn_conv_b, m_w_down, m_g_final)))
    v = dict(zip(names, (v_ln1_g, v_w_in, v_b_forget, v_g_out_fox, v_g_out_dil, v_conv_w, v_conv_b, v_cnorm_g,
                         v_cnorm_b, v_w_o, v_ln2_g, v_w_up, v_ffn_conv_w, v_ffn_conv_b, v_w_down, v_g_final)))
    nb = x.shape[0]
    T = nb * S
    xi, yi, ci = lax.axis_index("x"), lax.axis_index("y"), lax.axis_index("c")
    chip = 2 * xi + yi
    cw_cols = CC // NCHIP
    up_cols = 2 * DFF // NCHIP

    shards = [_pack_w_in(w_in), w_o.astype(BF16), w_up.astype(BF16), w_down.astype(BF16),
              jnp.pad(ffn_conv_w, ((0, 0), (0, 8 - FK), (0, 0))),
              jnp.pad(conv_w, ((0, 0), (0, CPAD - CK), (0, LANE - cw_cols)))]
    g_in, g_o, g_up, g_dn, g_fw, g_cw = _allgather(shards, (True, True, True, True, False, False))
    fb_full = _interleave(ffn_conv_b)
    W = {
        "in": [g.reshape(D, NP) for g in g_in],
        "o": [g.reshape(D, D) for g in g_o],
        "up": [g.transpose(1, 0, 2).reshape(D, 2 * DFF) for g in g_up],
        "down": [g.reshape(DFF, D) for g in g_dn],
        "ln1": [ln1_g[l] for l in range(DEPTH)],
        "ln2": [ln2_g[l] for l in range(DEPTH)],
        "bf": [jnp.pad(b_forget[l], (0, LANE - N_FG)).reshape(1, LANE) for l in range(DEPTH)],
        "gof": [g_out_fox[l].reshape(1, WA) for l in range(DEPTH)],
        "god": [g_out_dil[l].reshape(1, WA) for l in range(DEPTH)],
        "cw": [g[..., :cw_cols].transpose(1, 0, 2).reshape(CPAD, CC) for g in g_cw],
        "cb": [conv_b[l].reshape(1, CC) for l in range(DEPTH)],
        "cng": [cnorm_g[l].reshape(1, CC) for l in range(DEPTH)],
        "cnb": [cnorm_b[l].reshape(1, CC) for l in range(DEPTH)],
        "fw": [_interleave(g.transpose(1, 0, 2).reshape(8, 2 * DFF)) for g in g_fw],
        "fb": [fb_full[l].reshape(1, 2 * DFF) for l in range(DEPTH)],
        "gfin": g_final,
    }

    loss8, dx, big, small, dgfin = _train_compute(x.reshape(T, D), loss_target.reshape(T, D), W, nb)

    gs = []
    for l in range(DEPTH):
        dWin, dWo, dWup, dWd = big[l]
        gs += [dWin.reshape(NCHIP, D // NCHIP, NP), dWo.reshape(NCHIP, D // NCHIP, D),
               jnp.stack([half[:, i * up_cols:(i + 1) * up_cols] for half in dWup for i in range(2)]),
               dWd.reshape(NCHIP, DFF // NCHIP, D)]
    r1 = _rs_pair_exchange(gs)
    place = jnp.stack([chip, ci]).astype(jnp.int32)
    hs = [_add_half(g, r, place, f"rs_add_pair_{i}") for i, (g, r) in enumerate(zip(gs, r1))]
    r2 = _rs_chip_scatter(hs)
    fs = [_sum_slots(g, a, b, place, f"rs_add_chips_{i}") for i, (g, a, b) in enumerate(zip(gs, r1, r2))]
    red = _rs_pair_gather(fs)
    red = [r.reshape(r.shape[0] * r.shape[1], r.shape[2]) for r in red]
    grads, delta, new_m, new_v = {}, {}, {}, {}
    for i, n in enumerate(("w_in", "w_o", "w_up", "w_down")):
        grads[n], delta[n], new_m[n], new_v[n] = _adamw_layers(
            w[n], [red[4 * l + i] for l in range(DEPTH)], m[n], v[n], "adamw_" + n, packed=(n == "w_in"))

    parts = []
    for l in range(DEPTH):
        parts += [p.reshape(-1, LANE) for p in small[l]]
    parts += [dgfin.reshape(-1, LANE), loss8]
    tot = _allreduce_small(jnp.concatenate(parts, axis=0))
    off = 0
    per_layer = []
    for l in range(DEPTH):
        vals = []
        for rows in _SMALL_ROWS:
            vals.append(tot[off:off + rows])
            off += rows
        per_layer.append(vals)
    gfin_sum = tot[off:off + D // LANE].reshape(D)
    loss = tot[off + D // LANE, 0]

    def layer_stack(fn):
        return jnp.stack([fn(*per_layer[l]) for l in range(DEPTH)])

    fw_sum = layer_stack(lambda a, b, c_, d, e, f, g: _uninterleave(g.reshape(8, 2 * DFF)))
    cw_sum = layer_stack(lambda a, b, c_, d, e, f, g: d.reshape(CPAD, CC)[:CK])
    sm_sum = layer_stack(lambda a, b, c_, d, e, f, g: e.reshape(8, CC))
    go_sum = layer_stack(lambda a, b, c_, d, e, f, g: c_.reshape(8, WA))
    grads.update({
        "ln1_g": layer_stack(lambda a, b, c_, d, e, f, g: a.reshape(D)),
        "b_forget": layer_stack(lambda a, b, c_, d, e, f, g: b[0, :N_FG]),
        "g_out_fox": go_sum[:, 0],
        "g_out_dil": go_sum[:, 1],
        "conv_w": lax.dynamic_slice_in_dim(cw_sum, chip * cw_cols, cw_cols, axis=2),
        "conv_b": sm_sum[:, 0],
        "cnorm_g": sm_sum[:, 1],
        "cnorm_b": sm_sum[:, 2],
        "ln2_g": layer_stack(lambda a, b, c_, d, e, f, g: f.reshape(D)),
        "ffn_conv_w": lax.dynamic_slice_in_dim(fw_sum[:, :FK], chip * up_cols, up_cols, axis=2),
        "ffn_conv_b": fw_sum[:, FK],
        "g_final": gfin_sum,
    })

    for n in names:
        if n not in delta:
            delta[n], new_m[n], new_v[n] = _adamw(w[n], grads[n], m[n], v[n], "adamw_" + n)
    return (loss, dx.reshape(nb, S, D), *[grads[n] for n in names], *[delta[n] for n in names],
            *[new_m[n] for n in names], *[new_v[n] for n in names])
```

```python
import functools

import jax
import jax.numpy as jnp
from jax import lax
from jax.experimental import pallas as pl
from jax.experimental.pallas import tpu as pltpu

F32 = jnp.float32
BF16 = jnp.bfloat16

D = 1024
S = 2048
DEPTH = 2
HD = 64
WA = 384
NHP = 3
CC = 256
CK = 31
FK = 3
DFF = 2816
NIN = 2822
NP = 3072
SCALE = 0.125
EPS = 1e-6
NEG = -1e30
NCHIP = 4
NDEV = 8
LANE = 128

CB_FOX, CB_DIL = 0, 9
CB_GV, CB_GG = 9, 10
CB_FA = 22

ADAM_LR, ADAM_B1, ADAM_B2, ADAM_EPS, ADAM_WD, ADAM_STEP = 0.001, 0.9, 0.999, 1e-08, 0.01, 10

N_QKV = 3 * WA
N_FG = 2 * NHP

VMEM_LIMIT = 56 * 1024 * 1024


def _cparams(sem=None):
    return pltpu.CompilerParams(dimension_semantics=sem, vmem_limit_bytes=VMEM_LIMIT)


def _split3(x):
    hi = x.astype(BF16)
    r1 = x - hi.astype(F32)
    mid = r1.astype(BF16)
    lo = (r1 - mid.astype(F32)).astype(BF16)
    return hi, mid, lo


def _sum8(x):
    r, c = x.shape
    return jnp.sum(x.reshape(r // 8, 8, c), axis=0)


def _sigmoid(z):
    return 0.5 * jnp.tanh(0.5 * z) + 0.5


def _matmul(a, b, *, ta=False, tb=False, out_dtype=F32, add=None, tm, tn, tk, name):
    M = a.shape[1] if ta else a.shape[0]
    K = a.shape[0] if ta else a.shape[1]
    N = b.shape[0] if tb else b.shape[1]
    assert (b.shape[1] if tb else b.shape[0]) == K
    assert M % tm == 0 and N % tn == 0 and K % tk == 0, (M, N, K, tm, tn, tk)
    nk = K // tk
    dn = (((0 if ta else 1,), (1 if tb else 0,)), ((), ()))

    def body(*refs):
        if add is not None:
            a_ref, b_ref, add_ref, o_ref, acc = refs
        else:
            a_ref, b_ref, o_ref, acc = refs
        k = pl.program_id(2)
        prod = lax.dot_general(a_ref[...].astype(BF16), b_ref[...].astype(BF16), dn, preferred_element_type=F32)

        def finish(r):
            if add is not None:
                r = r + add_ref[...]
            o_ref[...] = r.astype(o_ref.dtype)

        if nk == 1:
            finish(prod)
        else:
            @pl.when(k == 0)
            def _():
                acc[...] = prod

            @pl.when(k > 0)
            def _():
                acc[...] += prod

            @pl.when(k == nk - 1)
            def _():
                finish(acc[...])

    a_spec = pl.BlockSpec((tk, tm), lambda i, j, k: (k, i)) if ta else pl.BlockSpec((tm, tk), lambda i, j, k: (i, k))
    b_spec = pl.BlockSpec((tn, tk), lambda i, j, k: (j, k)) if tb else pl.BlockSpec((tk, tn), lambda i, j, k: (k, j))
    o_spec = pl.BlockSpec((tm, tn), lambda i, j, k: (i, j))
    in_specs = [a_spec, b_spec]
    args = [a, b]
    if add is not None:
        in_specs.append(o_spec)
        args.append(add)
    return pl.pallas_call(
        body, name=name, grid=(M // tm, N // tn, nk),
        in_specs=in_specs, out_specs=o_spec,
        out_shape=jax.ShapeDtypeStruct((M, N), out_dtype),
        scratch_shapes=[pltpu.VMEM((tm, tn) if nk > 1 else (8, 128), F32)],
        compiler_params=_cparams(("parallel", "parallel", "arbitrary")),
    )(*args)


def _rms_rows(xv, g_ref):
    r = lax.rsqrt(jnp.mean(xv * xv, axis=1, keepdims=True) + EPS)
    return (xv * r * g_ref[...]).astype(BF16)


def _rms_matmul(x, g, b, *, tm, name):
    T, K = x.shape
    N = b.shape[1]

    def body(x_ref, g_ref, b_ref, h_ref, o_ref):
        h = _rms_rows(x_ref[...], g_ref)
        h_ref[...] = h
        o_ref[...] = jnp.dot(h, b_ref[...], preferred_element_type=F32)

    return pl.pallas_call(
        body, name=name, grid=(T // tm,),
        in_specs=[pl.BlockSpec((tm, K), lambda i: (i, 0)), pl.BlockSpec((1, K), lambda i: (0, 0)),
                  pl.BlockSpec((K, N), lambda i: (0, 0))],
        out_specs=[pl.BlockSpec((tm, K), lambda i: (i, 0)), pl.BlockSpec((tm, N), lambda i: (i, 0))],
        out_shape=[jax.ShapeDtypeStruct((T, K), BF16), jax.ShapeDtypeStruct((T, N), F32)],
        compiler_params=_cparams(("parallel",)),
    )(x, g.reshape(1, K), b)


def _rms_bwd(x, g, dh, dres, name):
    T = x.shape[0]
    tr = 512

    def body(x_ref, g_ref, dh_ref, dres_ref, dx_ref, dxb_ref, dg_ref):
        i = pl.program_id(0)
        xv = x_ref[...]
        dhv = dh_ref[...].astype(F32)
        r = lax.rsqrt(jnp.mean(xv * xv, axis=1, keepdims=True) + EPS)
        a = dhv * g_ref[...]
        dx = dres_ref[...] + r * a - xv * (r * r * r * jnp.mean(xv * a, axis=1, keepdims=True))
        dx_ref[...] = dx
        dxb_ref[...] = dx.astype(BF16)
        part = jnp.sum(dhv * xv * r, axis=0, keepdims=True)

        @pl.when(i == 0)
        def _():
            dg_ref[...] = part

        @pl.when(i > 0)
        def _():
            dg_ref[...] += part

    row = pl.BlockSpec((tr, D), lambda i: (i, 0))
    vec = pl.BlockSpec((1, D), lambda i: (0, 0))
    return pl.pallas_call(
        body, name=name, grid=(T // tr,),
        in_specs=[row, vec, row, row], out_specs=[row, row, vec],
        out_shape=[jax.ShapeDtypeStruct((T, D), F32), jax.ShapeDtypeStruct((T, D), BF16),
                   jax.ShapeDtypeStruct((1, D), F32)],
        compiler_params=_cparams(("arbitrary",)),
    )(x, g.reshape(1, D), dh, dres)


def _loss_head(x, g, target, name):
    T = x.shape[0]
    tr = 512

    def body(x_ref, g_ref, t_ref, loss_ref, dx_ref, dxb_ref, dg_ref):
        i = pl.program_id(0)
        xv = x_ref[...]
        gv = g_ref[...]
        r = lax.rsqrt(jnp.mean(xv * xv, axis=1, keepdims=True) + EPS)
        n = xv * r
        err = n * gv - t_ref[...]
        lpart = 0.5 * jnp.sum(jnp.mean(err * err, axis=1, keepdims=True), axis=0, keepdims=True)
        dy = err * (1.0 / D)
        a = dy * gv
        dx = r * a - xv * (r * r * r * jnp.mean(xv * a, axis=1, keepdims=True))
        dx_ref[...] = dx
        dxb_ref[...] = dx.astype(BF16)
        part = jnp.sum(dy * n, axis=0, keepdims=True)
        lfull = jnp.broadcast_to(lpart, (8, LANE))

        @pl.when(i == 0)
        def _():
            dg_ref[...] = part
            loss_ref[...] = lfull

        @pl.when(i > 0)
        def _():
            dg_ref[...] += part
            loss_ref[...] += lfull

    row = pl.BlockSpec((tr, D), lambda i: (i, 0))
    vec = pl.BlockSpec((1, D), lambda i: (0, 0))
    lsp = pl.BlockSpec((8, LANE), lambda i: (0, 0))
    return pl.pallas_call(
        body, name=name, grid=(T // tr,),
        in_specs=[row, vec, row], out_specs=[lsp, row, row, vec],
        out_shape=[jax.ShapeDtypeStruct((8, LANE), F32), jax.ShapeDtypeStruct((T, D), F32),
                   jax.ShapeDtypeStruct((T, D), BF16), jax.ShapeDtypeStruct((1, D), F32)],
        compiler_params=_cparams(("arbitrary",)),
    )(x, g.reshape(1, D), target)


CUM_BLK = 256


def _tri(n, upper):
    r = lax.broadcasted_iota(jnp.int32, (n, n), 0)
    c = lax.broadcasted_iota(jnp.int32, (n, n), 1)
    return jnp.where((c >= r) if upper else (c <= r), 1.0, 0.0).astype(BF16)


def _tri_apply(tri, x):
    hi, mid, lo = _split3(x)
    out = jnp.dot(tri, hi, preferred_element_type=F32)
    out = out + jnp.dot(tri, mid, preferred_element_type=F32)
    return out + jnp.dot(tri, lo, preferred_element_type=F32)


def _forget_fwd(P, bf_pad, nb, name):
    nblk = S // CUM_BLK

    def body(fa_ref, b_ref, c_ref):
        tri = _tri(CUM_BLK, upper=False)
        carry = jnp.zeros((1, LANE), F32)
        for i in range(nblk):
            z = fa_ref[pl.ds(i * CUM_BLK, CUM_BLK), :] + b_ref[...]
            lf = jnp.minimum(z, 0.0) - jnp.log(1.0 + jnp.exp(-jnp.abs(z)))
            cb = _tri_apply(tri, lf) + carry
            c_ref[pl.ds(i * CUM_BLK, CUM_BLK), :] = cb
            carry = cb[CUM_BLK - 1:CUM_BLK, :]

    return pl.pallas_call(
        body, name=name, grid=(nb,),
        in_specs=[pl.BlockSpec((S, LANE), lambda b: (b, CB_FA)), pl.BlockSpec((1, LANE), lambda b: (0, 0))],
        out_specs=pl.BlockSpec((S, LANE), lambda b: (b, 0)),
        out_shape=jax.ShapeDtypeStruct((nb * S, LANE), F32),
        compiler_params=_cparams(("parallel",)),
    )(P, bf_pad)


def _forget_bwd(P, bf_pad, dcb, nb, name):
    nblk = S // CUM_BLK

    def body(fa_ref, b_ref, dc_ref, dfa_ref, db_ref):
        b = pl.program_id(0)
        tri = _tri(CUM_BLK, upper=True)
        lane = lax.broadcasted_iota(jnp.int32, (CUM_BLK, LANE), 1)
        carry = jnp.zeros((1, LANE), F32)
        dbacc = jnp.zeros((1, LANE), F32)
        for i in reversed(range(nblk)):
            rows = pl.ds(i * CUM_BLK, CUM_BLK)
            dc = jnp.zeros((CUM_BLK, LANE), F32)
            dcv = dc_ref[rows, :]
            for h in range(2 * NHP):
                dc = jnp.where(lane == h, -dcv[:, HD * h:HD * h + 1], dc)
            dl = _tri_apply(tri, dc) + carry
            carry = dl[0:1, :]
            z = fa_ref[rows, :] + b_ref[...]
            dz = jnp.where(lane < 2 * NHP, dl * (1.0 - _sigmoid(z)), 0.0)
            dfa_ref[rows, :] = dz.astype(BF16)
            dbacc = dbacc + jnp.sum(dz, axis=0, keepdims=True)

        dbfull = jnp.broadcast_to(dbacc, (8, LANE))

        @pl.when(b == 0)
        def _():
            db_ref[...] = dbfull

        @pl.when(b > 0)
        def _():
            db_ref[...] += dbfull

    return pl.pallas_call(
        body, name=name, grid=(nb,),
        in_specs=[pl.BlockSpec((S, LANE), lambda b: (b, CB_FA)), pl.BlockSpec((1, LANE), lambda b: (0, 0)),
                  pl.BlockSpec((S, WA), lambda b: (b, 0))],
        out_specs=[pl.BlockSpec((S, LANE), lambda b: (b, 0)), pl.BlockSpec((8, LANE), lambda b: (0, 0))],
        out_shape=[jax.ShapeDtypeStruct((nb * S, LANE), BF16), jax.ShapeDtypeStruct((8, LANE), F32)],
        compiler_params=_cparams(("arbitrary",)),
    )(P, bf_pad, dcb)


FQ = 256
NT_DIMS = (((1,), (1,)), ((), ()))
AUGW = 6


def _qkv_blk(base, i):
    return pl.BlockSpec((S, LANE), lambda b, hp: (b, base + 3 * hp + i))


def _dqkv_blk(base):
    return pl.BlockSpec((S, 3 * LANE), lambda b, hp: (b, base // 3 + hp))


def _head_masks(shape):
    lane = lax.broadcasted_iota(jnp.int32, shape, 1)
    return lane < HD, lane >= HD


def _fox_bias_terms(c_ref, hp):
    lane = lax.broadcasted_iota(jnp.int32, (S, LANE), 1)
    cv = c_ref[...]
    return [_split3(jnp.sum(jnp.where(lane == 2 * hp + e, cv, 0.0), axis=1, keepdims=True)) for e in range(2)]


def _fox_ext(x, terms, side, heads, only):
    lane = lax.broadcasted_iota(jnp.int32, (S, LANE), 1)
    one = jnp.ones((S, 1), BF16)
    aug = jnp.zeros((S, LANE), BF16)
    for e in heads:
        hi, mid, lo = terms[e]
        cols = (hi, mid, lo, one, one, one) if side == "q" else (one, one, one, -hi, -mid, -lo)
        for i, col in enumerate(cols):
            aug = jnp.where(lane == AUGW * e + i, col, aug)
    if only is not None:
        x = jnp.where(_head_masks((S, LANE))[only], x, jnp.zeros_like(x))
    return jnp.concatenate([x, aug], axis=1)


def _halves(x, lane_mask):
    return jnp.where(lane_mask, x[0:FQ, 0:LANE], x[FQ:2 * FQ, 0:LANE])


def _fox_fwd(P, c, nb, name):
    def body(q_ref, k_ref, v_ref, c_ref, o_ref, lse_ref, qm0, qm1, kx, vx):
        hp = pl.program_id(1)
        terms = _fox_bias_terms(c_ref, hp)
        qv = (q_ref[...] * SCALE).astype(BF16)
        qm0[...] = _fox_ext(qv, terms, "q", (0,), 0)
        qm1[...] = _fox_ext(qv, terms, "q", (1,), 1)
        kx[...] = _fox_ext(k_ref[...].astype(BF16), terms, "k", (0, 1), None)
        lane = lax.broadcasted_iota(jnp.int32, (S, LANE), 1)
        vx[...] = jnp.concatenate([v_ref[...].astype(BF16), jnp.where(lane == 0, 1.0, 0.0).astype(BF16)], axis=1)
        tmask = _head_masks((FQ, LANE))[0]
        row = lax.broadcasted_iota(jnp.int32, (2 * FQ, FQ), 0) & (FQ - 1)
        col = lax.broadcasted_iota(jnp.int32, (2 * FQ, FQ), 1)
        for i in range(S // FQ):
            r0 = i * FQ
            qt = jnp.concatenate([qm0[pl.ds(r0, FQ), :], qm1[pl.ds(r0, FQ), :]], axis=0)
            sd = lax.dot_general(qt, kx[pl.ds(r0, FQ), :], NT_DIMS, preferred_element_type=F32)
            sd = jnp.where(col <= row, sd, NEG)
            m = jnp.max(sd, axis=1, keepdims=True)
            if i > 0:
                so = lax.dot_general(qt, kx[pl.ds(0, r0), :], NT_DIMS, preferred_element_type=F32)
                m = jnp.maximum(m, jnp.max(so, axis=1, keepdims=True))
            acc = jnp.dot(jnp.exp(sd - m).astype(BF16), vx[pl.ds(r0, FQ), :], preferred_element_type=F32)
            if i > 0:
                acc = acc + jnp.dot(jnp.exp(so - m).astype(BF16), vx[pl.ds(0, r0), :], preferred_element_type=F32)
            l = acc[:, LANE:LANE + 1]
            o_ref[pl.ds(r0, FQ), :] = _halves(acc / l, tmask)
            lse_ref[pl.ds(r0, FQ), :] = _halves(jnp.broadcast_to(m + jnp.log(l), (2 * FQ, LANE)), tmask)

    def colblk(off):
        return pl.BlockSpec((S, LANE), lambda b, hp: (b, off + hp))

    return pl.pallas_call(
        body, name=name, grid=(nb, NHP),
        in_specs=[_qkv_blk(CB_FOX, 0), _qkv_blk(CB_FOX, 1), _qkv_blk(CB_FOX, 2),
                  pl.BlockSpec((S, LANE), lambda b, hp: (b, 0))],
        out_specs=[colblk(0), colblk(0)],
        out_shape=[jax.ShapeDtypeStruct((nb * S, WA), F32), jax.ShapeDtypeStruct((nb * S, WA), F32)],
        scratch_shapes=[pltpu.VMEM((S, 2 * LANE), BF16)] * 4,
        compiler_params=_cparams(("parallel", "parallel")),
    )(P, P, P, c)


def _fox_bwd(P, c, o, lse, do, nb, name):
    def body(q_ref, k_ref, v_ref, c_ref, o_ref, lse_ref, do_ref, dp_ref, dc_ref,
             km0, km1, qx, vm0, vm1, dob, kt0, kt1, rows, dqt, rsum):
        hp = pl.program_id(1)
        terms = _fox_bias_terms(c_ref, hp)
        kv = k_ref[...].astype(BF16)
        km0[...] = _fox_ext(kv, terms, "k", (0,), 0)
        km1[...] = _fox_ext(kv, terms, "k", (1,), 1)
        qx[...] = _fox_ext((q_ref[...] * SCALE).astype(BF16), terms, "q", (0, 1), None)
        masks = _head_masks((S, LANE))
        vv = v_ref[...].astype(BF16)
        zero = jnp.zeros((S, LANE), BF16)
        vm0[...] = jnp.where(masks[0], vv, zero)
        vm1[...] = jnp.where(masks[1], vv, zero)
        dov = do_ref[...]
        dob[...] = dov.astype(BF16)
        ktf = k_ref[...].T
        prodt = (dov * o_ref[...]).T
        lset = lse_ref[...].T
        hrow = lax.broadcasted_iota(jnp.int32, (LANE, S), 0)
        kt0[...] = jnp.where(hrow < HD, ktf, 0.0).astype(BF16)
        kt1[...] = jnp.where(hrow >= HD, ktf, 0.0).astype(BF16)
        for e in range(2):
            rows[e:e + 1, :] = lset[HD * e:HD * e + 1, :]
            rows[2 + e:3 + e, :] = jnp.sum(prodt[HD * e:HD * (e + 1), :], axis=0, keepdims=True)
        dqt[...] = jnp.zeros_like(dqt)
        rsum[...] = jnp.zeros_like(rsum)
        tmask = _head_masks((FQ, LANE))[0]
        row = lax.broadcasted_iota(jnp.int32, (2 * FQ, FQ), 0) & (FQ - 1)
        col = lax.broadcasted_iota(jnp.int32, (2 * FQ, FQ), 1)
        for j in range(S // FQ):
            k0 = j * FQ
            rest = S - k0 - FQ
            spans = [(k0, FQ)] + ([(k0 + FQ, rest)] if rest > 0 else [])
            kte = jnp.concatenate([km0[pl.ds(k0, FQ), :], km1[pl.ds(k0, FQ), :]], axis=0)
            vte = jnp.concatenate([vm0[pl.ds(k0, FQ), :], vm1[pl.ds(k0, FQ), :]], axis=0)
            ktt = jnp.concatenate([kt0[:, pl.ds(k0, FQ)], kt1[:, pl.ds(k0, FQ)]], axis=1)
            dke = jnp.zeros((2 * FQ, 2 * LANE), F32)
            dve = jnp.zeros((2 * FQ, LANE), F32)
            cse = jnp.zeros((2 * FQ, 1), F32)
            for si, (q0, n) in enumerate(spans):
                qs = qx[pl.ds(q0, n), :]
                dos = dob[pl.ds(q0, n), :]
                st = lax.dot_general(kte, qs, NT_DIMS, preferred_element_type=F32)
                if si == 0:
                    st = jnp.where(col >= row, st, NEG)
                dpt = lax.dot_general(vte, dos, NT_DIMS, preferred_element_type=F32)
                pts, dsts = [], []
                for e in range(2):
                    pe = jnp.exp(st[FQ * e:FQ * (e + 1), :] - rows[e:e + 1, pl.ds(q0, n)])
                    de = pe * (dpt[FQ * e:FQ * (e + 1), :] - rows[2 + e:3 + e, pl.ds(q0, n)])
                    rsum[HD * e:HD * e + 8, pl.ds(q0, n)] += _sum8(de)
                    pts.append(pe)
                    dsts.append(de)
                pt = jnp.concatenate(pts, axis=0)
                dst = jnp.concatenate(dsts, axis=0)
                dsb = dst.astype(BF16)
                dve = dve + jnp.dot(pt.astype(BF16), dos, preferred_element_type=F32)
                dke = dke + jnp.dot(dsb, qs, preferred_element_type=F32)
                dqt[:, pl.ds(q0, n)] += jnp.dot(ktt, dsb, preferred_element_type=F32)
                cse = cse + jnp.sum(dst, axis=1, keepdims=True)
            dp_ref[pl.ds(k0, FQ), LANE:2 * LANE] = _halves(dke, tmask).astype(BF16)
            dp_ref[pl.ds(k0, FQ), 2 * LANE:3 * LANE] = _halves(dve, tmask).astype(BF16)
            dc_ref[pl.ds(k0, FQ), :] = _halves(jnp.broadcast_to(cse, (2 * FQ, LANE)), tmask)
        dp_ref[:, 0:LANE] = (dqt[...].T * SCALE).astype(BF16)
        tot = [jnp.sum(rsum[HD * e:HD * e + 8, :], axis=0, keepdims=True) for e in range(2)]
        dc_ref[...] = dc_ref[...] - jnp.where(hrow == 0, tot[0], jnp.where(hrow == HD, tot[1], 0.0)).T

    def colblk(off):
        return pl.BlockSpec((S, LANE), lambda b, hp: (b, off + hp))

    wide = pltpu.VMEM((S, 2 * LANE), BF16)
    half = pltpu.VMEM((S, LANE), BF16)
    return pl.pallas_call(
        body, name=name, grid=(nb, NHP),
        in_specs=[_qkv_blk(CB_FOX, 0), _qkv_blk(CB_FOX, 1), _qkv_blk(CB_FOX, 2),
                  pl.BlockSpec((S, LANE), lambda b, hp: (b, 0)), colblk(0), colblk(0), colblk(0)],
        out_specs=[_dqkv_blk(CB_FOX), colblk(0)],
        out_shape=[jax.ShapeDtypeStruct((nb * S, NP), BF16), jax.ShapeDtypeStruct((nb * S, WA), F32)],
        scratch_shapes=[wide, wide, wide, half, half, half, pltpu.VMEM((LANE, S), BF16), pltpu.VMEM((LANE, S), BF16),
                        pltpu.VMEM((8, S), F32), pltpu.VMEM((LANE, S), F32), pltpu.VMEM((LANE, S), F32)],
        compiler_params=_cparams(("parallel", "parallel")),
    )(P, P, P, c, o, lse, do)


DILS = (1, 4, 16)
DB = 128


def _regroup_load(ref, d, scale=None):
    if d == 1:
        v = ref[...]
    else:
        L = S // d
        v = jnp.concatenate([ref[pl.ds(r, L, stride=d), :] for r in range(d)], axis=0)
    return v if scale is None else v * scale


def _regroup_store(ref, d, val_ref, accumulate):
    L = S // d
    for r in range(d):
        src = val_ref[pl.ds(r * L, L), :]
        dst = (slice(None), slice(None)) if d == 1 else (pl.ds(r, L, stride=d), slice(None))
        if accumulate:
            ref[dst] = ref[dst] + src
        else:
            ref[dst] = src


def _dil_bands():
    qi = lax.broadcasted_iota(jnp.int32, (DB, 2 * DB), 0)
    ki = lax.broadcasted_iota(jnp.int32, (DB, 2 * DB), 1)
    band = (ki >= qi) & (ki <= qi + DB)
    return band, band & (ki >= DB)


def _dil_valid(bands, bk, d):
    band, own = bands
    has_prev = (bk % ((S // d) // DB)) > 0
    return own | (band & has_prev)


def _dil_keys(kd, vd, r0, bands, bk, d):
    if S // d == DB:
        valid = bands[1][:, DB:]
        kk, vv = kd[pl.ds(r0 + DB, DB), :], vd[pl.ds(r0 + DB, DB), :]
    else:
        valid = _dil_valid(bands, bk, d)
        kk, vv = kd[pl.ds(r0, 2 * DB), :], vd[pl.ds(r0, 2 * DB), :]
    return kk, vv, jnp.concatenate([valid, valid], axis=0)


def _stack_heads(x, masks):
    zero = jnp.zeros_like(x)
    return jnp.concatenate([jnp.where(masks[0], x, zero), jnp.where(masks[1], x, zero)], axis=0)


def _dil_fwd(P, nb, name):
    nblk = S // DB

    def body(q_ref, k_ref, v_ref, o_ref, lse_ref, qd, kd, vd, rnum, rm, rl, num_n, m_n, l_n):
        masks = _head_masks((DB, LANE))
        bands = _dil_bands()
        for bi, d in enumerate(DILS):
            qd[...] = _regroup_load(q_ref, d, SCALE).astype(BF16)
            kd[pl.ds(0, DB), :] = jnp.zeros((DB, LANE), BF16)
            vd[pl.ds(0, DB), :] = jnp.zeros((DB, LANE), BF16)
            kd[pl.ds(DB, S), :] = _regroup_load(k_ref, d).astype(BF16)
            vd[pl.ds(DB, S), :] = _regroup_load(v_ref, d).astype(BF16)

            def blk(bk, _, d=d):
                r0 = pl.multiple_of(bk * DB, DB)
                qt = qd[pl.ds(r0, DB), :]
                kk, vv, valid = _dil_keys(kd, vd, r0, bands, bk, d)
                qm = _stack_heads(qt, masks)
                s = lax.dot_general(qm, kk, NT_DIMS, preferred_element_type=F32)
                s = jnp.where(valid, s, NEG)
                m = jnp.max(s, axis=1, keepdims=True)
                p = jnp.exp(s - m)
                l = jnp.sum(p, axis=1, keepdims=True)
                num = jnp.dot(p.astype(BF16), vv, preferred_element_type=F32)
                rnum[pl.ds(r0, DB), :] = jnp.where(masks[0], num[0:DB], num[DB:2 * DB])
                rm[pl.ds(r0, DB), :] = jnp.where(masks[0], m[0:DB], m[DB:2 * DB])
                rl[pl.ds(r0, DB), :] = jnp.where(masks[0], l[0:DB], l[DB:2 * DB])
                return 0

            lax.fori_loop(0, nblk, blk, 0, unroll=nblk)
            _regroup_store(num_n.at[bi], d, rnum, False)
            _regroup_store(m_n.at[bi], d, rm, False)
            _regroup_store(l_n.at[bi], d, rl, False)

        m_all = jnp.maximum(jnp.maximum(m_n[0], m_n[1]), m_n[2])
        num = jnp.zeros((S, LANE), F32)
        den = jnp.zeros((S, LANE), F32)
        for bi in range(3):
            a = jnp.exp(m_n[bi] - m_all)
            num = num + a * num_n[bi]
            den = den + a * l_n[bi]
        o_ref[...] = num / den
        lse_ref[...] = m_all + jnp.log(den)

    def colblk(off):
        return pl.BlockSpec((S, LANE), lambda b, hp: (b, off + hp))

    return pl.pallas_call(
        body, name=name, grid=(nb, NHP),
        in_specs=[_qkv_blk(CB_DIL, 0), _qkv_blk(CB_DIL, 1), _qkv_blk(CB_DIL, 2)],
        out_specs=[colblk(0), colblk(0)],
        out_shape=[jax.ShapeDtypeStruct((nb * S, WA), F32), jax.ShapeDtypeStruct((nb * S, WA), F32)],
        scratch_shapes=[pltpu.VMEM((S, LANE), BF16), pltpu.VMEM((S + DB, LANE), BF16), pltpu.VMEM((S + DB, LANE), BF16),
                        pltpu.VMEM((S, LANE), F32), pltpu.VMEM((S, LANE), F32), pltpu.VMEM((S, LANE), F32),
                        pltpu.VMEM((3, S, LANE), F32), pltpu.VMEM((3, S, LANE), F32), pltpu.VMEM((3, S, LANE), F32)],
        compiler_params=_cparams(("parallel", "parallel")),
    )(P, P, P)


def _dil_bwd(P, o, lse, do, dP, nb, name):
    nblk = S // DB

    def body(q_ref, k_ref, v_ref, o_ref, lse_ref, do_ref, dp_in, dp_ref,
             qd, kd, vd, dod, lsed, dsd, dsum, dq_r, dk_r, dv_r, dq_n, dk_n, dv_n):
        del dp_in
        masks = _head_masks((DB, LANE))
        fmask = _head_masks((S, LANE))
        prod = do_ref[...] * o_ref[...]
        d0 = jnp.sum(jnp.where(fmask[0], prod, 0.0), axis=1, keepdims=True)
        d1 = jnp.sum(jnp.where(fmask[1], prod, 0.0), axis=1, keepdims=True)
        dsum[...] = jnp.where(fmask[0], d0, d1)
        tn = (((0,), (0,)), ((), ()))
        bands = _dil_bands()
        for bi, d in enumerate(DILS):
            qd[...] = _regroup_load(q_ref, d, SCALE).astype(BF16)
            kd[pl.ds(0, DB), :] = jnp.zeros((DB, LANE), BF16)
            vd[pl.ds(0, DB), :] = jnp.zeros((DB, LANE), BF16)
            kd[pl.ds(DB, S), :] = _regroup_load(k_ref, d).astype(BF16)
            vd[pl.ds(DB, S), :] = _regroup_load(v_ref, d).astype(BF16)
            dod[...] = _regroup_load(do_ref, d).astype(BF16)
            lsed[...] = _regroup_load(lse_ref, d)
            dsd[...] = _regroup_load(dsum, d)
            dq_r[...] = jnp.zeros_like(dq_r)
            dk_r[...] = jnp.zeros_like(dk_r)
            dv_r[...] = jnp.zeros_like(dv_r)

            def blk(bk, _, d=d):
                r0 = pl.multiple_of(bk * DB, DB)
                qt = qd[pl.ds(r0, DB), :]
                dot = dod[pl.ds(r0, DB), :]
                lt = lsed[pl.ds(r0, DB), :]
                dt = dsd[pl.ds(r0, DB), :]
                kk, vv, valid = _dil_keys(kd, vd, r0, bands, bk, d)
                kw = kk.shape[0]
                qm = _stack_heads(qt, masks)
                dom = _stack_heads(dot, masks)
                lcol = jnp.concatenate([lt[:, 0:1], lt[:, HD:HD + 1]], axis=0)
                dcol = jnp.concatenate([dt[:, 0:1], dt[:, HD:HD + 1]], axis=0)
                s = lax.dot_general(qm, kk, NT_DIMS, preferred_element_type=F32)
                s = jnp.where(valid, s, NEG)
                p = jnp.exp(s - lcol)
                dp = lax.dot_general(dom, vv, NT_DIMS, preferred_element_type=F32)
                ds = (p * (dp - dcol)).astype(BF16)
                dvt = lax.dot_general(p.astype(BF16), dom, tn, preferred_element_type=F32)
                dkt = lax.dot_general(ds, qm, tn, preferred_element_type=F32)
                dqt = jnp.dot(ds, kk, preferred_element_type=F32)
                dq_r[pl.ds(r0, DB), :] = jnp.where(masks[0], dqt[0:DB], dqt[DB:2 * DB])
                dk_r[pl.ds(r0 + 2 * DB - kw, kw), :] += dkt
                dv_r[pl.ds(r0 + 2 * DB - kw, kw), :] += dvt
                return 0

            lax.fori_loop(0, nblk, blk, 0, unroll=nblk)
            _regroup_store(dq_n, d, dq_r, bi > 0)
            _regroup_store(dk_n, d, dk_r.at[pl.ds(DB, S)], bi > 0)
            _regroup_store(dv_n, d, dv_r.at[pl.ds(DB, S)], bi > 0)

        dp_ref[:, 0:LANE] = (dq_n[...] * SCALE).astype(BF16)
        dp_ref[:, LANE:2 * LANE] = dk_n[...].astype(BF16)
        dp_ref[:, 2 * LANE:3 * LANE] = dv_n[...].astype(BF16)

    def colblk(off):
        return pl.BlockSpec((S, LANE), lambda b, hp: (b, off + hp))

    big = pltpu.VMEM((S, LANE), F32)
    bigp = pltpu.VMEM((S + DB, LANE), F32)
    return pl.pallas_call(
        body, name=name, grid=(nb, NHP),
        in_specs=[_qkv_blk(CB_DIL, 0), _qkv_blk(CB_DIL, 1), _qkv_blk(CB_DIL, 2), colblk(0), colblk(0), colblk(0),
                  pl.BlockSpec(memory_space=pl.ANY)],
        out_specs=_dqkv_blk(CB_DIL), out_shape=jax.ShapeDtypeStruct((nb * S, NP), BF16),
        input_output_aliases={6: 0},
        scratch_shapes=[pltpu.VMEM((S, LANE), BF16), pltpu.VMEM((S + DB, LANE), BF16), pltpu.VMEM((S + DB, LANE), BF16),
                        pltpu.VMEM((S, LANE), BF16), big, big, big, big, bigp, bigp, big, big, big],
        compiler_params=_cparams(("parallel", "parallel")),
    )(P, P, P, o, lse, do, dP)


RC = 256
NSHW = 4


def _window(src, start, buf):
    rows = buf.shape[0]
    if start % 8 == 0:
        return src[pl.ds(start, rows), :]
    buf[...] = src[pl.ds(start, rows), :]
    return buf[...]
CPAD = 32


NSUB = 8
CROWS = S + CPAD


def _preshift(src, dst):
    for b in range(NSUB):
        dst[b] = src[pl.ds(b, CROWS), :]


def _shifted(dst, start):
    return dst[start % NSUB, pl.ds(start - start % NSUB, RC), :]


def _conv_chunk(gsh, r0, cw_ref, cb_ref):
    acc = jnp.zeros((RC, CC), F32) + cb_ref[...]
    for k in range(CK):
        acc = acc + cw_ref[k:k + 1, :] * _shifted(gsh, r0 + CPAD - (CK - 1) + k)
    return acc


def _cnorm(c0, cng_ref, cnb_ref):
    mu = jnp.mean(c0, axis=1, keepdims=True)
    xc = c0 - mu
    rstd = lax.rsqrt(jnp.mean(xc * xc, axis=1, keepdims=True) + EPS)
    n = xc * rstd
    return n, rstd, n * cng_ref[...] + cnb_ref[...]


NORM_ROWS = 512
YC_BLK = 2 * WA // CC


def _attn_norm_fwd(of, od, gof, god, name):
    T = of.shape[0]

    def body(of_ref, od_ref, gof_ref, god_ref, y_ref):
        for i, (src, g_ref) in enumerate(((of_ref, gof_ref), (od_ref, god_ref))):
            v = src[...]
            r = lax.rsqrt(jnp.mean(v * v, axis=1, keepdims=True) + EPS)
            y_ref[:, i * WA:(i + 1) * WA] = (v * r * g_ref[...]).astype(BF16)

    row = lambda w: pl.BlockSpec((NORM_ROWS, w), lambda i: (i, 0))
    par = pl.BlockSpec((1, WA), lambda i: (0, 0))
    return pl.pallas_call(
        body, name=name, grid=(T // NORM_ROWS,),
        in_specs=[row(WA), row(WA), par, par], out_specs=row(2 * WA),
        out_shape=jax.ShapeDtypeStruct((T, D), BF16),
        compiler_params=_cparams(("parallel",)),
    )(of, od, gof, god)


def _attn_norm_bwd(of, od, dy, gof, god, name):
    T = of.shape[0]

    def body(of_ref, od_ref, dy_ref, gof_ref, god_ref, dof_ref, dod_ref, dgo_ref):
        @pl.when(pl.program_id(0) == 0)
        def _():
            dgo_ref[...] = jnp.zeros_like(dgo_ref)

        for i, (src, g_ref, dst) in enumerate(((of_ref, gof_ref, dof_ref), (od_ref, god_ref, dod_ref))):
            v = src[...]
            dyv = dy_ref[:, i * WA:(i + 1) * WA].astype(F32)
            r = lax.rsqrt(jnp.mean(v * v, axis=1, keepdims=True) + EPS)
            a = dyv * g_ref[...]
            dst[...] = r * a - v * (r * r * r * jnp.mean(v * a, axis=1, keepdims=True))
            dgo_ref[i:i + 1, :] += jnp.sum(dyv * v * r, axis=0, keepdims=True)

    row = lambda w: pl.BlockSpec((NORM_ROWS, w), lambda i: (i, 0))
    par = pl.BlockSpec((1, WA), lambda i: (0, 0))
    return pl.pallas_call(
        body, name=name, grid=(T // NORM_ROWS,),
        in_specs=[row(WA), row(WA), row(2 * WA), par, par],
        out_specs=[row(WA), row(WA), pl.BlockSpec((8, WA), lambda i: (0, 0))],
        out_shape=[jax.ShapeDtypeStruct((T, WA), F32), jax.ShapeDtypeStruct((T, WA), F32),
                   jax.ShapeDtypeStruct((8, WA), F32)],
        compiler_params=_cparams(("arbitrary",)),
    )(of, od, dy, gof, god)


def _conv_specs():
    gblk = lambda off: pl.BlockSpec((S, CC), lambda b: (b, off))
    par = lambda r: pl.BlockSpec((r, CC), lambda b: (0, 0))
    return gblk, par


def _conv_fwd(P, y, cw, cb, cng, cnb, nb, name):
    def body(gv_ref, gg_ref, cw_ref, cb_ref, cng_ref, cnb_ref, y_in, y_ref, c0_ref, gpad, gsh):
        del y_in
        gpad[pl.ds(0, CPAD), :] = jnp.zeros((CPAD, CC), F32)
        gpad[pl.ds(CPAD, S), :] = gv_ref[...] * _sigmoid(gg_ref[...])
        gpad[pl.ds(CROWS, NSUB), :] = jnp.zeros((NSUB, CC), F32)
        _preshift(gpad, gsh)
        for ci in range(S // RC):
            r0 = ci * RC
            c0 = _conv_chunk(gsh, r0, cw_ref, cb_ref)
            c0_ref[pl.ds(r0, RC), :] = c0
            _, _, z = _cnorm(c0, cng_ref, cnb_ref)
            y_ref[pl.ds(r0, RC), :] = (z * _sigmoid(z)).astype(BF16)

    gblk, par = _conv_specs()
    return pl.pallas_call(
        body, name=name, grid=(nb,),
        in_specs=[gblk(CB_GV), gblk(CB_GG), par(CPAD), par(1), par(1), par(1), pl.BlockSpec(memory_space=pl.ANY)],
        out_specs=[gblk(YC_BLK), gblk(0)],
        out_shape=[jax.ShapeDtypeStruct((nb * S, D), BF16), jax.ShapeDtypeStruct((nb * S, CC), F32)],
        input_output_aliases={6: 0},
        scratch_shapes=[pltpu.VMEM((CROWS + NSUB, CC), F32), pltpu.VMEM((NSUB, CROWS, CC), F32)],
        compiler_params=_cparams(("parallel",)),
    )(P, P, cw, cb, cng, cnb, y)


def _conv_bwd(P, c0, dy, dfa, dP, cw, cng, cnb, nb, name):
    def body(gv_ref, gg_ref, c0_ref, dy_ref, dfa_ref, cw_ref, cng_ref, cnb_ref, dp_in, dg_ref, dcw_ref, dsm_ref,
             dpad, dsh):
        del dp_in
        dg_ref[:, 2 * CC:2 * CC + LANE] = dfa_ref[...]
        dg_ref[:, 2 * CC + LANE:3 * CC] = jnp.zeros((S, CC - LANE), BF16)
        @pl.when(pl.program_id(0) == 0)
        def _():
            dcw_ref[...] = jnp.zeros_like(dcw_ref)
            dsm_ref[...] = jnp.zeros_like(dsm_ref)

        dpad[pl.ds(S, CPAD + NSUB), :] = jnp.zeros((CPAD + NSUB, CC), F32)
        zero = jnp.zeros((8, CC), F32)
        dcb, dcng, dcnb = zero, zero, zero
        for ci in range(S // RC):
            r0 = ci * RC
            n, rstd, z = _cnorm(c0_ref[pl.ds(r0, RC), :], cng_ref, cnb_ref)
            sz = _sigmoid(z)
            dz = dy_ref[pl.ds(r0, RC), :].astype(F32) * (sz * (1.0 + z * (1.0 - sz)))
            dcng = dcng + _sum8(dz * n)
            dcnb = dcnb + _sum8(dz)
            dn = dz * cng_ref[...]
            dc0 = rstd * (dn - jnp.mean(dn, axis=1, keepdims=True) - n * jnp.mean(dn * n, axis=1, keepdims=True))
            dcb = dcb + _sum8(dc0)
            dpad[pl.ds(r0, RC), :] = dc0
        dsm_ref[0:1, :] += jnp.sum(dcb, axis=0, keepdims=True)
        dsm_ref[1:2, :] += jnp.sum(dcng, axis=0, keepdims=True)
        dsm_ref[2:3, :] += jnp.sum(dcnb, axis=0, keepdims=True)

        _preshift(dpad, dsh)
        dws = [zero] * CK
        for ci in range(S // RC):
            r0 = ci * RC
            sg = _sigmoid(gg_ref[pl.ds(r0, RC), :])
            gvc = gv_ref[pl.ds(r0, RC), :]
            glu = gvc * sg
            dgl = jnp.zeros((RC, CC), F32)
            for k in range(CK):
                win = _shifted(dsh, r0 + (CK - 1) - k)
                dws[k] = dws[k] + _sum8(win * glu)
                dgl = dgl + cw_ref[k:k + 1, :] * win
            dg_ref[pl.ds(r0, RC), 0:CC] = (dgl * sg).astype(BF16)
            dg_ref[pl.ds(r0, RC), CC:2 * CC] = (dgl * gvc * sg * (1.0 - sg)).astype(BF16)
        for k in range(CK):
            dcw_ref[k:k + 1, :] += jnp.sum(dws[k], axis=0, keepdims=True)

    gblk, par = _conv_specs()
    return pl.pallas_call(
        body, name=name, grid=(nb,),
        in_specs=[gblk(CB_GV), gblk(CB_GG), gblk(0), gblk(YC_BLK), pl.BlockSpec((S, LANE), lambda b: (b, 0)),
                  par(CPAD), par(1), par(1), pl.BlockSpec(memory_space=pl.ANY)],
        out_specs=[pl.BlockSpec((S, 3 * CC), lambda b: (b, CB_GV // 3)), par(CPAD), par(8)],
        out_shape=[jax.ShapeDtypeStruct((nb * S, NP), BF16), jax.ShapeDtypeStruct((CPAD, CC), F32),
                   jax.ShapeDtypeStruct((8, CC), F32)],
        input_output_aliases={8: 0},
        scratch_shapes=[pltpu.VMEM((CROWS + NSUB, CC), F32), pltpu.VMEM((NSUB, CROWS, CC), F32)],
        compiler_params=_cparams(("arbitrary",)),
    )(P, P, c0, dy, dfa, cw, cng, cnb, dP)


FC = 512
FPAD = 8
FRC = 32
NFB = 2 * DFF // FC


def _ffn_u2_chunk(upad, r0, fw_ref, fb_ref):
    acc = jnp.zeros((FRC, FC), F32) + fb_ref[...]
    for k in range(FK):
        acc = acc + fw_ref[k:k + 1, :] * upad[pl.ds(r0 + FPAD - (FK - 1) + k, FRC), :]
    return acc


def _ffn_fwd(U, fw, fb, nb, name):
    def body(u_ref, fw_ref, fb_ref, h_ref, u2_ref, upad):
        upad[pl.ds(0, FPAD), :] = jnp.zeros((FPAD, FC), F32)
        upad[pl.ds(FPAD, S), :] = u_ref[...].astype(F32)
        for ci in range(S // FRC):
            r0 = ci * FRC
            u2 = _ffn_u2_chunk(upad, r0, fw_ref, fb_ref)
            u2_ref[pl.ds(r0, FRC), :] = u2.astype(BF16)
            a2, b2 = u2[:, :FC // 2], u2[:, FC // 2:]
            h_ref[pl.ds(r0, FRC), :] = (a2 * _sigmoid(a2) * b2).astype(BF16)

    return pl.pallas_call(
        body, name=name, grid=(nb, NFB),
        in_specs=[pl.BlockSpec((S, FC), lambda b, j: (b, j)), pl.BlockSpec((8, FC), lambda b, j: (0, j)),
                  pl.BlockSpec((1, FC), lambda b, j: (0, j))],
        out_specs=[pl.BlockSpec((S, FC // 2), lambda b, j: (b, j)), pl.BlockSpec((S, FC), lambda b, j: (b, j))],
        out_shape=[jax.ShapeDtypeStruct((nb * S, DFF), BF16), jax.ShapeDtypeStruct((nb * S, 2 * DFF), BF16)],
        scratch_shapes=[pltpu.VMEM((S + FPAD, FC), F32)],
        compiler_params=_cparams(("parallel", "parallel")),
    )(U, fw, fb)


def _ffn_bwd(U, U2, dhid, fw, nb, name):
    def body(u_ref, u2_ref, dh_ref, fw_ref, du_ref, dfw_ref, dpad, shw):
        @pl.when(pl.program_id(1) == 0)
        def _():
            dfw_ref[...] = jnp.zeros_like(dfw_ref)

        dpad[pl.ds(S, FPAD), :] = jnp.zeros((FPAD, FC), F32)
        zero = jnp.zeros((8, FC), F32)
        dbias = zero
        for ci in range(S // FRC):
            r0 = ci * FRC
            u2 = u2_ref[pl.ds(r0, FRC), :].astype(F32)
            a2, b2 = u2[:, :FC // 2], u2[:, FC // 2:]
            sa = _sigmoid(a2)
            dh = dh_ref[pl.ds(r0, FRC), :].astype(F32)
            du2 = jnp.concatenate([dh * b2 * (sa * (1.0 + a2 * (1.0 - sa))), dh * a2 * sa], axis=1)
            dpad[pl.ds(r0, FRC), :] = du2
            dbias = dbias + _sum8(du2)
        dws = [zero] * FK
        for ci in range(S // FRC):
            r0 = ci * FRC
            uc = u_ref[pl.ds(r0, FRC), :].astype(F32)
            du = jnp.zeros((FRC, FC), F32)
            for k in range(FK):
                win = _window(dpad, r0 + (FK - 1) - k, shw.at[k % NSHW])
                dws[k] = dws[k] + _sum8(win * uc)
                du = du + fw_ref[k:k + 1, :] * win
            du_ref[pl.ds(r0, FRC), :] = du.astype(BF16)
        for k in range(FK):
            dfw_ref[k:k + 1, :] += jnp.sum(dws[k], axis=0, keepdims=True)
        dfw_ref[FK:FK + 1, :] += jnp.sum(dbias, axis=0, keepdims=True)

    blk = pl.BlockSpec((S, FC), lambda j, b: (b, j))
    return pl.pallas_call(
        body, name=name, grid=(NFB, nb),
        in_specs=[blk, blk, pl.BlockSpec((S, FC // 2), lambda j, b: (b, j)), pl.BlockSpec((8, FC), lambda j, b: (0, j))],
        out_specs=[blk, pl.BlockSpec((8, FC), lambda j, b: (0, j))],
        out_shape=[jax.ShapeDtypeStruct((nb * S, 2 * DFF), BF16), jax.ShapeDtypeStruct((8, 2 * DFF), F32)],
        scratch_shapes=[pltpu.VMEM((S + FPAD, FC), F32), pltpu.VMEM((NSHW, FRC, FC), F32)],
        compiler_params=_cparams(("parallel", "arbitrary")),
    )(U, U2, dhid, fw)


def _matmul_ffn(a, b, mode, *, out_dtype=F32, tm=1024, tk=2048, norm_g=None, name):
    HF = FC // 2
    if mode == "fwd":
        M, K = a.shape
        tm = 512

        def body(a_ref, g_ref, b_ref, h_ref, o_ref):
            av = _rms_rows(a_ref[...], g_ref)
            h_ref[...] = av
            for j in range(NFB):
                for half in range(2):
                    bv = b_ref[:, half * DFF + j * HF:half * DFF + (j + 1) * HF]
                    o_ref[:, j * FC + half * HF:j * FC + (half + 1) * HF] = jnp.dot(
                        av, bv, preferred_element_type=F32).astype(o_ref.dtype)

        return pl.pallas_call(
            body, name=name, grid=(M // tm,),
            in_specs=[pl.BlockSpec((tm, K), lambda i: (i, 0)), pl.BlockSpec((1, K), lambda i: (0, 0)),
                      pl.BlockSpec((K, 2 * DFF), lambda i: (0, 0))],
            out_specs=[pl.BlockSpec((tm, K), lambda i: (i, 0)), pl.BlockSpec((tm, 2 * DFF), lambda i: (i, 0))],
            out_shape=[jax.ShapeDtypeStruct((M, K), BF16), jax.ShapeDtypeStruct((M, 2 * DFF), out_dtype)],
            compiler_params=_cparams(("parallel",)),
        )(a, norm_g.reshape(1, K), b)
    if mode == "dx":
        M = a.shape[0]
        N = b.shape[0]
        tm = 512

        def body(a_ref, b_ref, o_ref):
            acc = None
            for j in range(NFB):
                for half in range(2):
                    av = a_ref[:, j * FC + half * HF:j * FC + (half + 1) * HF]
                    bv = b_ref[:, half * DFF + j * HF:half * DFF + (j + 1) * HF]
                    d = lax.dot_general(av, bv, NT_DIMS, preferred_element_type=F32)
                    acc = d if acc is None else acc + d
            o_ref[...] = acc.astype(o_ref.dtype)

        return pl.pallas_call(
            body, name=name, grid=(M // tm,),
            in_specs=[pl.BlockSpec((tm, 2 * DFF), lambda i: (i, 0)), pl.BlockSpec((N, 2 * DFF), lambda i: (0, 0))],
            out_specs=pl.BlockSpec((tm, N), lambda i: (i, 0)),
            out_shape=jax.ShapeDtypeStruct((M, N), out_dtype),
            compiler_params=_cparams(("parallel",)),
        )(a, b)
    assert mode == "dw"
    T, M = a.shape
    nk = T // tk

    def body(a_ref, g_ref, oa_ref, ob_ref, acc):
        k = pl.program_id(1)
        prod = lax.dot_general(a_ref[...], g_ref[...], (((0,), (0,)), ((), ())), preferred_element_type=F32)

        @pl.when(k == 0)
        def _():
            acc[...] = prod

        @pl.when(k > 0)
        def _():
            acc[...] += prod

        @pl.when(k == nk - 1)
        def _():
            oa_ref[...] = acc[:, :HF]
            ob_ref[...] = acc[:, HF:]

    half = pl.BlockSpec((M, HF), lambda j, k: (0, j))
    return pl.pallas_call(
        body, name=name, grid=(NFB, nk),
        in_specs=[pl.BlockSpec((tk, M), lambda j, k: (k, 0)), pl.BlockSpec((tk, FC), lambda j, k: (k, j))],
        out_specs=[half, half],
        out_shape=[jax.ShapeDtypeStruct((M, DFF), F32)] * 2,
        scratch_shapes=[pltpu.VMEM((M, FC), F32)],
        compiler_params=_cparams(("parallel", "arbitrary")),
    )(a, b)


def _adamw_body(w_ref, g_ref, m_ref, v_ref, d_ref, nm_ref, nv_ref):
    g = g_ref[...]
    m = ADAM_B1 * m_ref[...] + (1.0 - ADAM_B1) * g
    v = ADAM_B2 * v_ref[...] + (1.0 - ADAM_B2) * (g * g)
    m_hat = m / (1.0 - ADAM_B1 ** ADAM_STEP)
    v_hat = v / (1.0 - ADAM_B2 ** ADAM_STEP)
    d_ref[...] = -ADAM_LR * (m_hat / (jnp.sqrt(v_hat) + ADAM_EPS) + ADAM_WD * w_ref[...])
    nm_ref[...] = m
    nv_ref[...] = v


def _adamw(w, g, m, v, name):
    shape = w.shape
    R = 1
    for s in shape[:-1]:
        R *= s
    C = shape[-1]
    args = [a.reshape(R, C) for a in (w, g, m, v)]
    tr = R
    for cand in (512, 352, 256, 128, 64, 32, 16, 8):
        if R % cand == 0 and cand * C * 4 * 14 <= 24 * 1024 * 1024:
            tr = cand
            break
    blk = pl.BlockSpec((tr, C), lambda i: (i, 0))
    outs = pl.pallas_call(
        functools.partial(_adamw_body), name=name, grid=(R // tr,),
        in_specs=[blk] * 4, out_specs=[blk] * 3,
        out_shape=[jax.ShapeDtypeStruct((R, C), F32)] * 3,
        compiler_params=_cparams(("parallel",)),
    )(*args)
    return [o.reshape(shape) for o in outs]


REF_FOX, REF_DIL, REF_GATE = 0, N_QKV + N_FG, 2 * N_QKV + N_FG


def _pack_cols_value(w):
    parts = []
    for ref0 in (REF_FOX, REF_DIL):
        for hp in range(NHP):
            parts += [w[:, ref0 + i * WA + hp * LANE:ref0 + i * WA + (hp + 1) * LANE] for i in range(3)]
    parts += [w[:, REF_GATE:NIN], w[:, N_QKV:N_QKV + N_FG], jnp.zeros((w.shape[0], NP - NIN), w.dtype)]
    return jnp.concatenate(parts, axis=1)


def _unpack_cols_value(g):
    def group(cb):
        return [g[:, (cb + 3 * hp + i) * LANE:(cb + 3 * hp + i + 1) * LANE] for i in range(3) for hp in range(NHP)]
    fa0 = 2 * N_QKV + 2 * CC
    return jnp.concatenate(group(CB_FOX) + [g[:, fa0:fa0 + N_FG]] + group(CB_DIL) + [g[:, 2 * N_QKV:fa0]], axis=1)


def _adamw_layers(w, gs, m, v, name, packed=False):
    _, R, C = w.shape
    Cg = gs[0].shape[1]
    tr = 128 if R % 128 == 0 else 176
    assert R % tr == 0 and len(gs) == DEPTH == 2

    def body(w_ref, g0_ref, g1_ref, m_ref, v_ref, g_out, d_ref, nm_ref, nv_ref):
        g = jnp.where(pl.program_id(0) == 0, g0_ref[...], g1_ref[...])
        if packed:
            g = _unpack_cols_value(g)
        g_out[...] = g
        mn = ADAM_B1 * m_ref[...] + (1.0 - ADAM_B1) * g
        vn = ADAM_B2 * v_ref[...] + (1.0 - ADAM_B2) * (g * g)
        m_hat = mn / (1.0 - ADAM_B1 ** ADAM_STEP)
        v_hat = vn / (1.0 - ADAM_B2 ** ADAM_STEP)
        d_ref[...] = -ADAM_LR * (m_hat / (jnp.sqrt(v_hat) + ADAM_EPS) + ADAM_WD * w_ref[...])
        nm_ref[...] = mn
        nv_ref[...] = vn

    lay = pl.BlockSpec((None, tr, C), lambda l, i: (l, i, 0))
    gsp = pl.BlockSpec((tr, Cg), lambda l, i: (i, 0))
    return pl.pallas_call(
        body, name=name, grid=(DEPTH, R // tr),
        in_specs=[lay, gsp, gsp, lay, lay], out_specs=[lay] * 4,
        out_shape=[jax.ShapeDtypeStruct((DEPTH, R, C), F32)] * 4,
        compiler_params=_cparams(("parallel", "parallel")),
    )(w, gs[0], gs[1], m, v)


def _pack_w_in(w_in):
    _, R, _ = w_in.shape

    def body(w_ref, o_ref):
        o_ref[...] = _pack_cols_value(w_ref[...]).astype(BF16)

    return pl.pallas_call(
        body, name="pack_w_in", grid=(DEPTH,),
        in_specs=[pl.BlockSpec((None, R, NIN), lambda l: (l, 0, 0))],
        out_specs=pl.BlockSpec((None, R, NP), lambda l: (l, 0, 0)),
        out_shape=jax.ShapeDtypeStruct((DEPTH, R, NP), BF16),
        compiler_params=_cparams(("parallel",)),
    )(w_in)


def _rs_row_tile(H):
    th = 128 if H % 128 == 0 else 176
    assert H % th == 0
    return th


def _add_half(g, r1, place, name):
    _, R, C = g.shape
    H = R // 2
    th = _rs_row_tile(H)
    nh = H // th

    def body(s_ref, g_ref, r_ref, o_ref):
        o_ref[...] = (g_ref[...] + r_ref[...]).astype(BF16)

    grid_spec = pltpu.PrefetchScalarGridSpec(
        num_scalar_prefetch=1, grid=(NCHIP, nh),
        in_specs=[pl.BlockSpec((None, th, C), lambda p, i, s: (p, s[1] * nh + i, 0)),
                  pl.BlockSpec((None, th, C), lambda p, i, s: (p, i, 0))],
        out_specs=pl.BlockSpec((None, th, C), lambda p, i, s: (p, i, 0)))
    return pl.pallas_call(
        body, name=name, grid_spec=grid_spec, out_shape=jax.ShapeDtypeStruct((NCHIP, H, C), BF16),
        compiler_params=_cparams(("parallel", "parallel")),
    )(place, g, r1)


def _sum_slots(g, r1, r2, place, name):
    _, R, C = g.shape
    H = R // 2
    th = _rs_row_tile(H)
    nh = H // th

    def body(s_ref, g_ref, r1_ref, r2_ref, o_ref):
        acc = g_ref[...] + r1_ref[...]
        for j in range(NCHIP - 1):
            acc = acc + r2_ref[j].astype(F32)
        o_ref[...] = acc

    grid_spec = pltpu.PrefetchScalarGridSpec(
        num_scalar_prefetch=1, grid=(nh,),
        in_specs=[pl.BlockSpec((None, th, C), lambda i, s: (s[0], s[1] * nh + i, 0)),
                  pl.BlockSpec((None, th, C), lambda i, s: (s[0], i, 0)),
                  pl.BlockSpec((NCHIP - 1, th, C), lambda i, s: (0, i, 0))],
        out_specs=pl.BlockSpec((None, th, C), lambda i, s: (s[1], i, 0)))
    return pl.pallas_call(
        body, name=name, grid_spec=grid_spec, out_shape=jax.ShapeDtypeStruct((2, H, C), F32),
        compiler_params=_cparams(("parallel",)),
    )(place, g, r1, r2)


MESH = pl.DeviceIdType.MESH
HBM = pl.BlockSpec(memory_space=pltpu.HBM)


def _place():
    x, y, c = lax.axis_index("x"), lax.axis_index("y"), lax.axis_index("c")
    chips = [(1 - x, y), (x, 1 - y), (1 - x, 1 - y)]
    return x, y, c, chips


def _rcopy(src, dst, ssem, rsem, dev):
    return pltpu.make_async_remote_copy(src_ref=src, dst_ref=dst, send_sem=ssem, recv_sem=rsem,
                                        device_id=dev, device_id_type=MESH)


AG_CHUNK_BYTES = 1 << 20


def _stage_rows(R, C, dtype):
    rows = R
    while rows * C * jnp.dtype(dtype).itemsize > AG_CHUNK_BYTES and rows % 32 == 0:
        rows //= 2
    return rows


def _allgather(shards, split):
    n = len(shards)
    nout = n * DEPTH
    rows = [_stage_rows(s.shape[1], s.shape[2], s.dtype) for s in shards]

    def body(*refs):
        ins, outs = refs[:n], refs[n:n + nout]
        stages = refs[n + nout:2 * n + nout]
        ssem, rsem, fssem, frsem, isem, osem = refs[2 * n + nout:]
        x, y, c, chips = _place()
        me = 2 * x + y
        sib = (x, y, 1 - c)

        def window(t, l, chip, half):
            if not split[t]:
                return outs[t * DEPTH + l].at[chip]
            H = shards[t].shape[1] // 2
            return outs[t * DEPTH + l].at[chip, pl.ds(half * H, H)]

        sends = []
        for t in range(n):
            H = shards[t].shape[1] // 2
            for l in range(DEPTH):
                src = ins[t].at[l, pl.ds(c * H, H)] if split[t] else ins[t].at[l]
                for j, (cx, cy) in enumerate(chips):
                    k = (t * DEPTH + l) * 3 + j
                    cp = _rcopy(src, window(t, l, me, c), ssem.at[k], rsem.at[k], (cx, cy, c))
                    cp.start()
                    sends.append(cp)
        for t in range(n):
            nch = shards[t].shape[1] // rows[t]
            outc = []
            for l in range(DEPTH):
                for i in range(nch):
                    slot = len(outc) % 2
                    if len(outc) >= 2:
                        outc[-2].wait()
                    rs = pl.ds(i * rows[t], rows[t])
                    cin = pltpu.make_async_copy(ins[t].at[l, rs], stages[t].at[slot], isem.at[t])
                    cin.start()
                    cin.wait()
                    co = pltpu.make_async_copy(stages[t].at[slot], outs[t * DEPTH + l].at[me, rs], osem.at[2 * t + slot])
                    co.start()
                    outc.append(co)
            for co in outc[-2:]:
                co.wait()
        for t in range(n):
            for l in range(DEPTH):
                for j, (cx, cy) in enumerate(chips):
                    k = (t * DEPTH + l) * 3 + j
                    win = window(t, l, 2 * cx + cy, c)
                    _rcopy(win, win, ssem.at[k], rsem.at[k], (cx, cy, c)).wait_recv()
                    if split[t]:
                        cp = _rcopy(win, win, fssem.at[k], frsem.at[k], sib)
                        cp.start()
                        sends.append(cp)
        for t in range(n):
            if split[t]:
                for l in range(DEPTH):
                    for j, (cx, cy) in enumerate(chips):
                        k = (t * DEPTH + l) * 3 + j
                        win = window(t, l, 2 * cx + cy, 1 - c)
                        _rcopy(win, win, fssem.at[k], frsem.at[k], sib).wait_recv()
        for cp in sends:
            cp.wait_send()

    out_shape = [jax.ShapeDtypeStruct((NCHIP,) + s.shape[1:], s.dtype) for s in shards for _ in range(DEPTH)]
    outs = pl.pallas_call(
        body, name="allgather_weights", in_specs=[HBM] * n, out_specs=[HBM] * nout, out_shape=out_shape,
        scratch_shapes=[pltpu.VMEM((2, r, s.shape[2]), s.dtype) for r, s in zip(rows, shards)]
        + [pltpu.SemaphoreType.DMA((3 * nout,))] * 4 + [pltpu.SemaphoreType.DMA((n,)), pltpu.SemaphoreType.DMA((2 * n,))],
        compiler_params=pltpu.CompilerParams(vmem_limit_bytes=VMEM_LIMIT),
    )(*shards)
    return [outs[t * DEPTH:(t + 1) * DEPTH] for t in range(n)]


def _rs_pair_exchange(gs):
    n = len(gs)

    def body(*refs):
        ins, outs = refs[:n], refs[n:2 * n]
        ssem, rsem = refs[2 * n:]
        x, y, c, _ = _place()
        cps = []
        for t in range(n):
            H = gs[t].shape[1] // 2
            cp = _rcopy(ins[t].at[:, pl.ds((1 - c) * H, H)], outs[t], ssem.at[t], rsem.at[t], (x, y, 1 - c))
            cp.start()
            cps.append(cp)
        for cp in cps:
            cp.wait_recv()
        for cp in cps:
            cp.wait_send()

    out_shape = [jax.ShapeDtypeStruct((NCHIP, g.shape[1] // 2, g.shape[2]), F32) for g in gs]
    return pl.pallas_call(
        body, name="rs_pair_exchange", in_specs=[HBM] * n, out_specs=[HBM] * n, out_shape=out_shape,
        scratch_shapes=[pltpu.SemaphoreType.DMA((n,))] * 2,
    )(*gs)


def _rs_chip_scatter(hs):
    n = len(hs)

    def body(*refs):
        ins, outs = refs[:n], refs[n:2 * n]
        ssem, rsem = refs[2 * n:]
        x, y, c, chips = _place()
        sends = []
        for t in range(n):
            for j, (cx, cy) in enumerate(chips):
                cp = _rcopy(ins[t].at[2 * cx + cy], outs[t].at[j], ssem.at[3 * t + j], rsem.at[3 * t + j], (cx, cy, c))
                cp.start()
                sends.append(cp)
        for cp in sends:
            cp.wait_recv()
        for cp in sends:
            cp.wait_send()

    out_shape = [jax.ShapeDtypeStruct((NCHIP - 1,) + h.shape[1:], h.dtype) for h in hs]
    return pl.pallas_call(
        body, name="rs_chip_scatter", in_specs=[HBM] * n, out_specs=[HBM] * n, out_shape=out_shape,
        scratch_shapes=[pltpu.SemaphoreType.DMA((3 * n,))] * 2,
    )(*hs)


def _rs_pair_gather(fs):
    n = len(fs)

    def body(*refs):
        bufs = refs[n:2 * n]
        ssem, rsem = refs[2 * n:]
        x, y, c, _ = _place()
        sends = []
        for t in range(n):
            cp = _rcopy(bufs[t].at[c], bufs[t].at[c], ssem.at[t], rsem.at[t], (x, y, 1 - c))
            cp.start()
            sends.append(cp)
        for t in range(n):
            win = bufs[t].at[1 - c]
            _rcopy(win, win, ssem.at[t], rsem.at[t], (x, y, 1 - c)).wait_recv()
        for cp in sends:
            cp.wait_send()

    out_shape = [jax.ShapeDtypeStruct(f.shape, F32) for f in fs]
    return pl.pallas_call(
        body, name="rs_pair_gather", in_specs=[HBM] * n, out_specs=[HBM] * n, out_shape=out_shape,
        input_output_aliases={t: t for t in range(n)},
        scratch_shapes=[pltpu.SemaphoreType.DMA((n,))] * 2,
    )(*fs)


def _allreduce_small(buf):
    R = buf.shape[0]

    def body(in_ref, out_ref, slots, ssem, rsem):
        x, y, c, _ = _place()
        me = 4 * x + 2 * y + c
        slots[me] = in_ref[...]
        cps = []
        for k in range(1, NDEV):
            px = 1 - x if k & 4 else x
            py = 1 - y if k & 2 else y
            pc = 1 - c if k & 1 else c
            cp = _rcopy(in_ref, slots.at[me], ssem.at[k - 1], rsem.at[k - 1], (px, py, pc))
            cp.start()
            cps.append((cp, 4 * px + 2 * py + pc))
        for k, (cp, peer) in enumerate(cps):
            _rcopy(in_ref, slots.at[peer], ssem.at[k], rsem.at[k], (x, y, c)).wait_recv()
        for cp, _ in cps:
            cp.wait_send()
        acc = slots[0]
        for p in range(1, NDEV):
            acc = acc + slots[p]
        out_ref[...] = acc

    return pl.pallas_call(
        body, name="allreduce_small", out_shape=jax.ShapeDtypeStruct((R, LANE), F32),
        in_specs=[pl.BlockSpec(memory_space=pltpu.VMEM)], out_specs=pl.BlockSpec(memory_space=pltpu.VMEM),
        scratch_shapes=[pltpu.VMEM((NDEV, R, LANE), F32), pltpu.SemaphoreType.DMA((NDEV - 1,)),
                        pltpu.SemaphoreType.DMA((NDEV - 1,))],
        compiler_params=pltpu.CompilerParams(vmem_limit_bytes=VMEM_LIMIT),
    )(buf)


def _interleave(a):
    lead = a.shape[:-1]
    return a.reshape(*lead, 2, NFB, FC // 2).swapaxes(-3, -2).reshape(*lead, 2 * DFF)


def _uninterleave(a):
    lead = a.shape[:-1]
    return a.reshape(*lead, NFB, 2, FC // 2).swapaxes(-3, -2).reshape(*lead, 2 * DFF)


def _train_compute(xt, tgt, W, nb):
    saved = []
    xc = xt
    for l in range(DEPTH):
        t = f"_l{l}"
        h, P = _rms_matmul(xc, W["ln1"][l], W["in"][l], tm=512, name="proj_in" + t)
        c = _forget_fwd(P, W["bf"][l], nb, "forget_fwd" + t)
        of, lsef = _fox_fwd(P, c, nb, "fox_fwd" + t)
        od, lsed = _dil_fwd(P, nb, "dil_fwd" + t)
        convp = (W["cw"][l], W["cb"][l], W["cng"][l], W["cnb"][l])
        y = _attn_norm_fwd(of, od, W["gof"][l], W["god"][l], "attn_norm_fwd" + t)
        y, c0 = _conv_fwd(P, y, *convp, nb, "conv_fwd" + t)
        xm = _matmul(y, W["o"][l], add=xc, tm=1024, tn=1024, tk=D, name="proj_out" + t)
        h2, U = _matmul_ffn(xm, W["up"][l], "fwd", out_dtype=BF16, norm_g=W["ln2"][l], name="ffn_up" + t)
        hid, U2 = _ffn_fwd(U, W["fw"][l], W["fb"][l], nb, "ffn_act_fwd" + t)
        xo = _matmul(hid, W["down"][l], add=xm, tm=512, tn=D, tk=DFF, name="ffn_down" + t)
        saved.append((xc, h, P, c, of, lsef, od, lsed, convp, c0, y, xm, h2, U, U2, hid))
        xc = xo

    loss8, dx, dxb, dgfin = _loss_head(xc, W["gfin"], tgt, "loss_head")

    big = [None] * DEPTH
    small = [None] * DEPTH
    tk_dw = min(4096, xt.shape[0])
    for l in reversed(range(DEPTH)):
        t = f"_l{l}"
        xin, h, P, c, of, lsef, od, lsed, convp, c0, y, xm, h2, U, U2, hid = saved[l]
        dhid = _matmul(dxb, W["down"][l], tb=True, out_dtype=BF16, tm=1024, tn=DFF, tk=D, name="ffn_down_dx" + t)
        dWd = _matmul(hid, dxb, ta=True, tm=DFF // 2, tn=D, tk=2048, name="ffn_down_dw" + t)
        dU, dfw = _ffn_bwd(U, U2, dhid, W["fw"][l], nb, "ffn_act_bwd" + t)
        dh2 = _matmul_ffn(dU, W["up"][l], "dx", out_dtype=BF16, name="ffn_up_dx" + t)
        dWup = _matmul_ffn(h2, dU, "dw", tk=tk_dw, name="ffn_up_dw" + t)
        dxm, dxmb, dln2 = _rms_bwd(xm, W["ln2"][l], dh2, dx, "rms2_bwd" + t)
        dy = _matmul(dxmb, W["o"][l], tb=True, out_dtype=BF16, tm=1024, tn=D, tk=D, name="proj_out_dx" + t)
        dWo = _matmul(y, dxmb, ta=True, tm=D, tn=D, tk=tk_dw, name="proj_out_dw" + t)
        dof, dod, dgo = _attn_norm_bwd(of, od, dy, W["gof"][l], W["god"][l], "attn_norm_bwd" + t)
        dP, dcb = _fox_bwd(P, c, of, lsef, dof, nb, "fox_bwd" + t)
        dfa, dbf = _forget_bwd(P, W["bf"][l], dcb, nb, "forget_bwd" + t)
        dP = _dil_bwd(P, od, lsed, dod, dP, nb, "dil_bwd" + t)
        dP, dcw, dsm = _conv_bwd(P, c0, dy, dfa, dP, convp[0], convp[2], convp[3], nb, "conv_bwd" + t)
        dh = _matmul(dP, W["in"][l], tb=True, out_dtype=BF16, tm=1024, tn=D, tk=NP, name="proj_in_dx" + t)
        dWin = _matmul(h, dP, ta=True, tm=D, tn=1024, tk=tk_dw, name="proj_in_dw" + t)
        dx, dxb, dln1 = _rms_bwd(xin, W["ln1"][l], dh, dxm, "rms1_bwd" + t)
        big[l] = (dWin, dWo, dWup, dWd)
        small[l] = (dln1, dbf, dgo, dcw, dsm, dln2, dfw)
    return loss8, dx, big, small, dgfin


_SMALL_ROWS = (D // LANE, 8, 8 * WA // LANE, CPAD * CC // LANE, 8 * CC // LANE, D // LANE, 8 * 2 * DFF // LANE)


def kernel(x, ln1_g, w_in, b_forget, g_out_fox, g_out_dil, conv_w, conv_b, cnorm_g, cnorm_b, w_o, ln2_g, w_up, ffn_conv_w, ffn_conv_b, w_down, g_final, loss_target, m_ln1_g, m_w_in, m_b_forget, m_g_out_fox, m_g_out_dil, m_conv_w, m_conv_b, m_cnorm_g, m_cnorm_b, m_w_o, m_ln2_g, m_w_up, m_ffn_conv_w, m_ffn_conv_b, m_w_down, m_g_final, v_ln1_g, v_w_in, v_b_forget, v_g_out_fox, v_g_out_dil, v_conv_w, v_conv_b, v_cnorm_g, v_cnorm_b, v_w_o, v_ln2_g, v_w_up, v_ffn_conv_w, v_ffn_conv_b, v_w_down, v_g_final):
    names = ("ln1_g", "w_in", "b_forget", "g_out_fox", "g_out_dil", "conv_w", "conv_b", "cnorm_g", "cnorm_b",
             "w_o", "ln2_g", "w_up", "ffn_conv_w", "ffn_conv_b", "w_down", "g_final")
    w = dict(zip(names, (ln1_g, w_in, b_forget, g_out_fox, g_out_dil, conv_w, conv_b, cnorm_g, cnorm_b,
                         w_o, ln2_g, w_up, ffn_conv_w, ffn_conv_b, w_down, g_final)))
    m = dict(zip(names, (m_ln1_g, m_w_in, m_b_forget, m_g_out_fox, m_g_out_dil, m_conv_w, m_conv_b, m_cnorm_g,
                         m_cnorm_b, m_w_o, m_ln2_g, m_w_up, m_ffn_conv_w, m_ffn_conv_b, m_w_down, m_g_final)))
    v = dict(zip(names, (v_ln1_g, v_w_in, v_b_forget, v_g_out_fox, v_g_out_dil, v_conv_w, v_conv_b, v_cnorm_g,
                         v_cnorm_b, v_w_o, v_ln2_g, v_w_up, v_ffn_conv_w, v_ffn_conv_b, v_w_down, v_g_final)))
    nb = x.shape[0]
    T = nb * S
    xi, yi, ci = lax.axis_index("x"), lax.axis_index("y"), lax.axis_index("c")
    chip = 2 * xi + yi
    cw_cols = CC // NCHIP
    up_cols = 2 * DFF // NCHIP

    shards = [_pack_w_in(w_in), w_o.astype(BF16), w_up.astype(BF16), w_down.astype(BF16),
              jnp.pad(ffn_conv_w, ((0, 0), (0, 8 - FK), (0, 0))),
              jnp.pad(conv_w, ((0, 0), (0, CPAD - CK), (0, LANE - cw_cols)))]
    g_in, g_o, g_up, g_dn, g_fw, g_cw = _allgather(shards, (True, True, True, True, False, False))
    fb_full = _interleave(ffn_conv_b)
    W = {
        "in": [g.reshape(D, NP) for g in g_in],
        "o": [g.reshape(D, D) for g in g_o],
        "up": [g.transpose(1, 0, 2).reshape(D, 2 * DFF) for g in g_up],
        "down": [g.reshape(DFF, D) for g in g_dn],
        "ln1": [ln1_g[l] for l in range(DEPTH)],
        "ln2": [ln2_g[l] for l in range(DEPTH)],
        "bf": [jnp.pad(b_forget[l], (0, LANE - N_FG)).reshape(1, LANE) for l in range(DEPTH)],
        "gof": [g_out_fox[l].reshape(1, WA) for l in range(DEPTH)],
        "god": [g_out_dil[l].reshape(1, WA) for l in range(DEPTH)],
        "cw": [g[..., :cw_cols].transpose(1, 0, 2).reshape(CPAD, CC) for g in g_cw],
        "cb": [conv_b[l].reshape(1, CC) for l in range(DEPTH)],
        "cng": [cnorm_g[l].reshape(1, CC) for l in range(DEPTH)],
        "cnb": [cnorm_b[l].reshape(1, CC) for l in range(DEPTH)],
        "fw": [_interleave(g.transpose(1, 0, 2).reshape(8, 2 * DFF)) for g in g_fw],
        "fb": [fb_full[l].reshape(1, 2 * DFF) for l in range(DEPTH)],
        "gfin": g_final,
    }

    loss8, dx, big, small, dgfin = _train_compute(x.reshape(T, D), loss_target.reshape(T, D), W, nb)

    gs = []
    for l in range(DEPTH):
        dWin, dWo, dWup, dWd = big[l]
        gs += [dWin.reshape(NCHIP, D // NCHIP, NP), dWo.reshape(NCHIP, D // NCHIP, D),
               jnp.stack([half[:, i * up_cols:(i + 1) * up_cols] for half in dWup for i in range(2)]),
               dWd.reshape(NCHIP, DFF // NCHIP, D)]
    r1 = _rs_pair_exchange(gs)
    place = jnp.stack([chip, ci]).astype(jnp.int32)
    hs = [_add_half(g, r, place, f"rs_add_pair_{i}") for i, (g, r) in enumerate(zip(gs, r1))]
    r2 = _rs_chip_scatter(hs)
    fs = [_sum_slots(g, a, b, place, f"rs_add_chips_{i}") for i, (g, a, b) in enumerate(zip(gs, r1, r2))]
    red = _rs_pair_gather(fs)
    red = [r.reshape(r.shape[0] * r.shape[1], r.shape[2]) for r in red]
    grads, delta, new_m, new_v = {}, {}, {}, {}
    for i, n in enumerate(("w_in", "w_o", "w_up", "w_down")):
        grads[n], delta[n], new_m[n], new_v[n] = _adamw_layers(
            w[n], [red[4 * l + i] for l in range(DEPTH)], m[n], v[n], "adamw_" + n, packed=(n == "w_in"))

    parts = []
    for l in range(DEPTH):
        parts += [p.reshape(-1, LANE) for p in small[l]]
    parts += [dgfin.reshape(-1, LANE), loss8]
    tot = _allreduce_small(jnp.concatenate(parts, axis=0))
    off = 0
    per_layer = []
    for l in range(DEPTH):
        vals = []
        for rows in _SMALL_ROWS:
            vals.append(tot[off:off + rows])
            off += rows
        per_layer.append(vals)
    gfin_sum = tot[off:off + D // LANE].reshape(D)
    loss = tot[off + D // LANE, 0]

    def layer_stack(fn):
        return jnp.stack([fn(*per_layer[l]) for l in range(DEPTH)])

    fw_sum = layer_stack(lambda a, b, c_, d, e, f, g: _uninterleave(g.reshape(8, 2 * DFF)))
    cw_sum = layer_stack(lambda a, b, c_, d, e, f, g: d.reshape(CPAD, CC)[:CK])
    sm_sum = layer_stack(lambda a, b, c_, d, e, f, g: e.reshape(8, CC))
    go_sum = layer_stack(lambda a, b, c_, d, e, f, g: c_.reshape(8, WA))
    grads.update({
        "ln1_g": layer_stack(lambda a, b, c_, d, e, f, g: a.reshape(D)),
        "b_forget": layer_stack(lambda a, b, c_, d, e, f, g: b[0, :N_FG]),
        "g_out_fox": go_sum[:, 0],
        "g_out_dil": go_sum[:, 1],
        "conv_w": lax.dynamic_slice_in_dim(cw_sum, chip * cw_cols, cw_cols, axis=2),
        "conv_b": sm_sum[:, 0],
        "cnorm_g": sm_sum[:, 1],
        "cnorm_b": sm_sum[:, 2],
        "ln2_g": layer_stack(lambda a, b, c_, d, e, f, g: f.reshape(D)),
        "ffn_conv_w": lax.dynamic_slice_in_dim(fw_sum[:, :FK], chip * up_cols, up_cols, axis=2),
        "ffn_conv_b": fw_sum[:, FK],
        "g_final": gfin_sum,
    })

    for n in names:
        if n not in delta:
            delta[n], new_m[n], new_v[n] = _adamw(w[n], grads[n], m[n], v[n], "adamw_" + n)
    return (loss, dx.reshape(nb, S, D), *[grads[n] for n in names], *[delta[n] for n in names],
            *[new_m[n] for n in names], *[new_v[n] for n in names])
```

```python
import functools

import jax
import jax.numpy as jnp
from jax import lax
from jax.experimental import pallas as pl
from jax.experimental.pallas import tpu as pltpu

F32 = jnp.float32
BF16 = jnp.bfloat16

D = 1024
S = 2048
DEPTH = 2
HD = 64
WA = 384
NHP = 3
CC = 256
CK = 31
FK = 3
DFF = 2816
NIN = 2822
NP = 3072
SCALE = 0.125
EPS = 1e-6
NEG = -1e30
NCHIP = 4
NDEV = 8
LANE = 128

CB_FOX, CB_DIL = 0, 9
CB_GV, CB_GG = 9, 10
CB_FA = 22

ADAM_LR, ADAM_B1, ADAM_B2, ADAM_EPS, ADAM_WD, ADAM_STEP = 0.001, 0.9, 0.999, 1e-08, 0.01, 10

N_QKV = 3 * WA
N_FG = 2 * NHP

VMEM_LIMIT = 56 * 1024 * 1024


def _cparams(sem=None):
    return pltpu.CompilerParams(dimension_semantics=sem, vmem_limit_bytes=VMEM_LIMIT)


def _split3(x):
    hi = x.astype(BF16)
    r1 = x - hi.astype(F32)
    mid = r1.astype(BF16)
    lo = (r1 - mid.astype(F32)).astype(BF16)
    return hi, mid, lo


def _sum8(x):
    r, c = x.shape
    return jnp.sum(x.reshape(r // 8, 8, c), axis=0)


def _sigmoid(z):
    return 0.5 * jnp.tanh(0.5 * z) + 0.5


def _matmul(a, b, *, ta=False, tb=False, out_dtype=F32, add=None, tm, tn, tk, name):
    M = a.shape[1] if ta else a.shape[0]
    K = a.shape[0] if ta else a.shape[1]
    N = b.shape[0] if tb else b.shape[1]
    assert (b.shape[1] if tb else b.shape[0]) == K
    assert M % tm == 0 and N % tn == 0 and K % tk == 0, (M, N, K, tm, tn, tk)
    nk = K // tk
    dn = (((0 if ta else 1,), (1 if tb else 0,)), ((), ()))

    def body(*refs):
        if add is not None:
            a_ref, b_ref, add_ref, o_ref, acc = refs
        else:
            a_ref, b_ref, o_ref, acc = refs
        k = pl.program_id(2)
        prod = lax.dot_general(a_ref[...].astype(BF16), b_ref[...].astype(BF16), dn, preferred_element_type=F32)

        def finish(r):
            if add is not None:
                r = r + add_ref[...]
            o_ref[...] = r.astype(o_ref.dtype)

        if nk == 1:
            finish(prod)
        else:
            @pl.when(k == 0)
            def _():
                acc[...] = prod

            @pl.when(k > 0)
            def _():
                acc[...] += prod

            @pl.when(k == nk - 1)
            def _():
                finish(acc[...])

    a_spec = pl.BlockSpec((tk, tm), lambda i, j, k: (k, i)) if ta else pl.BlockSpec((tm, tk), lambda i, j, k: (i, k))
    b_spec = pl.BlockSpec((tn, tk), lambda i, j, k: (j, k)) if tb else pl.BlockSpec((tk, tn), lambda i, j, k: (k, j))
    o_spec = pl.BlockSpec((tm, tn), lambda i, j, k: (i, j))
    in_specs = [a_spec, b_spec]
    args = [a, b]
    if add is not None:
        in_specs.append(o_spec)
        args.append(add)
    return pl.pallas_call(
        body, name=name, grid=(M // tm, N // tn, nk),
        in_specs=in_specs, out_specs=o_spec,
        out_shape=jax.ShapeDtypeStruct((M, N), out_dtype),
        scratch_shapes=[pltpu.VMEM((tm, tn) if nk > 1 else (8, 128), F32)],
        compiler_params=_cparams(("parallel", "parallel", "arbitrary")),
    )(*args)


def _rms_rows(xv, g_ref):
    r = lax.rsqrt(jnp.mean(xv * xv, axis=1, keepdims=True) + EPS)
    return (xv * r * g_ref[...]).astype(BF16)


def _rms_matmul(x, g, b, *, tm, name):
    T, K = x.shape
    N = b.shape[1]

    def body(x_ref, g_ref, b_ref, h_ref, o_ref):
        h = _rms_rows(x_ref[...], g_ref)
        h_ref[...] = h
        o_ref[...] = jnp.dot(h, b_ref[...], preferred_element_type=F32)

    return pl.pallas_call(
        body, name=name, grid=(T // tm,),
        in_specs=[pl.BlockSpec((tm, K), lambda i: (i, 0)), pl.BlockSpec((1, K), lambda i: (0, 0)),
                  pl.BlockSpec((K, N), lambda i: (0, 0))],
        out_specs=[pl.BlockSpec((tm, K), lambda i: (i, 0)), pl.BlockSpec((tm, N), lambda i: (i, 0))],
        out_shape=[jax.ShapeDtypeStruct((T, K), BF16), jax.ShapeDtypeStruct((T, N), F32)],
        compiler_params=_cparams(("parallel",)),
    )(x, g.reshape(1, K), b)


def _rms_bwd(x, g, dh, dres, name):
    T = x.shape[0]
    tr = 512

    def body(x_ref, g_ref, dh_ref, dres_ref, dx_ref, dxb_ref, dg_ref):
        i = pl.program_id(0)
        xv = x_ref[...]
        dhv = dh_ref[...].astype(F32)
        r = lax.rsqrt(jnp.mean(xv * xv, axis=1, keepdims=True) + EPS)
        a = dhv * g_ref[...]
        dx = dres_ref[...] + r * a - xv * (r * r * r * jnp.mean(xv * a, axis=1, keepdims=True))
        dx_ref[...] = dx
        dxb_ref[...] = dx.astype(BF16)
        part = jnp.sum(dhv * xv * r, axis=0, keepdims=True)

        @pl.when(i == 0)
        def _():
            dg_ref[...] = part

        @pl.when(i > 0)
        def _():
            dg_ref[...] += part

    row = pl.BlockSpec((tr, D), lambda i: (i, 0))
    vec = pl.BlockSpec((1, D), lambda i: (0, 0))
    return pl.pallas_call(
        body, name=name, grid=(T // tr,),
        in_specs=[row, vec, row, row], out_specs=[row, row, vec],
        out_shape=[jax.ShapeDtypeStruct((T, D), F32), jax.ShapeDtypeStruct((T, D), BF16),
                   jax.ShapeDtypeStruct((1, D), F32)],
        compiler_params=_cparams(("arbitrary",)),
    )(x, g.reshape(1, D), dh, dres)


def _loss_head(x, g, target, name):
    T = x.shape[0]
    tr = 512

    def body(x_ref, g_ref, t_ref, loss_ref, dx_ref, dxb_ref, dg_ref):
        i = pl.program_id(0)
        xv = x_ref[...]
        gv = g_ref[...]
        r = lax.rsqrt(jnp.mean(xv * xv, axis=1, keepdims=True) + EPS)
        n = xv * r
        err = n * gv - t_ref[...]
        lpart = 0.5 * jnp.sum(jnp.mean(err * err, axis=1, keepdims=True), axis=0, keepdims=True)
        dy = err * (1.0 / D)
        a = dy * gv
        dx = r * a - xv * (r * r * r * jnp.mean(xv * a, axis=1, keepdims=True))
        dx_ref[...] = dx
        dxb_ref[...] = dx.astype(BF16)
        part = jnp.sum(dy * n, axis=0, keepdims=True)
        lfull = jnp.broadcast_to(lpart, (8, LANE))

        @pl.when(i == 0)
        def _():
            dg_ref[...] = part
            loss_ref[...] = lfull

        @pl.when(i > 0)
        def _():
            dg_ref[...] += part
            loss_ref[...] += lfull

    row = pl.BlockSpec((tr, D), lambda i: (i, 0))
    vec = pl.BlockSpec((1, D), lambda i: (0, 0))
    lsp = pl.BlockSpec((8, LANE), lambda i: (0, 0))
    return pl.pallas_call(
        body, name=name, grid=(T // tr,),
        in_specs=[row, vec, row], out_specs=[lsp, row, row, vec],
        out_shape=[jax.ShapeDtypeStruct((8, LANE), F32), jax.ShapeDtypeStruct((T, D), F32),
                   jax.ShapeDtypeStruct((T, D), BF16), jax.ShapeDtypeStruct((1, D), F32)],
        compiler_params=_cparams(("arbitrary",)),
    )(x, g.reshape(1, D), target)


CUM_BLK = 256


def _tri(n, upper):
    r = lax.broadcasted_iota(jnp.int32, (n, n), 0)
    c = lax.broadcasted_iota(jnp.int32, (n, n), 1)
    return jnp.where((c >= r) if upper else (c <= r), 1.0, 0.0).astype(BF16)


def _tri_apply(tri, x):
    hi, mid, lo = _split3(x)
    out = jnp.dot(tri, hi, preferred_element_type=F32)
    out = out + jnp.dot(tri, mid, preferred_element_type=F32)
    return out + jnp.dot(tri, lo, preferred_element_type=F32)


def _forget_fwd(P, bf_pad, nb, name):
    nblk = S // CUM_BLK

    def body(fa_ref, b_ref, c_ref):
        tri = _tri(CUM_BLK, upper=False)
        carry = jnp.zeros((1, LANE), F32)
        for i in range(nblk):
            z = fa_ref[pl.ds(i * CUM_BLK, CUM_BLK), :] + b_ref[...]
            lf = jnp.minimum(z, 0.0) - jnp.log(1.0 + jnp.exp(-jnp.abs(z)))
            cb = _tri_apply(tri, lf) + carry
            c_ref[pl.ds(i * CUM_BLK, CUM_BLK), :] = cb
            carry = cb[CUM_BLK - 1:CUM_BLK, :]

    return pl.pallas_call(
        body, name=name, grid=(nb,),
        in_specs=[pl.BlockSpec((S, LANE), lambda b: (b, CB_FA)), pl.BlockSpec((1, LANE), lambda b: (0, 0))],
        out_specs=pl.BlockSpec((S, LANE), lambda b: (b, 0)),
        out_shape=jax.ShapeDtypeStruct((nb * S, LANE), F32),
        compiler_params=_cparams(("parallel",)),
    )(P, bf_pad)


def _forget_bwd(P, bf_pad, dcb, nb, name):
    nblk = S // CUM_BLK

    def body(fa_ref, b_ref, dc_ref, dfa_ref, db_ref):
        b = pl.program_id(0)
        tri = _tri(CUM_BLK, upper=True)
        lane = lax.broadcasted_iota(jnp.int32, (CUM_BLK, LANE), 1)
        carry = jnp.zeros((1, LANE), F32)
        dbacc = jnp.zeros((1, LANE), F32)
        for i in reversed(range(nblk)):
            rows = pl.ds(i * CUM_BLK, CUM_BLK)
            dc = jnp.zeros((CUM_BLK, LANE), F32)
            dcv = dc_ref[rows, :]
            for h in range(2 * NHP):
                dc = jnp.where(lane == h, -dcv[:, HD * h:HD * h + 1], dc)
            dl = _tri_apply(tri, dc) + carry
            carry = dl[0:1, :]
            z = fa_ref[rows, :] + b_ref[...]
            dz = jnp.where(lane < 2 * NHP, dl * (1.0 - _sigmoid(z)), 0.0)
            dfa_ref[rows, :] = dz.astype(BF16)
            dbacc = dbacc + jnp.sum(dz, axis=0, keepdims=True)

        dbfull = jnp.broadcast_to(dbacc, (8, LANE))

        @pl.when(b == 0)
        def _():
            db_ref[...] = dbfull

        @pl.when(b > 0)
        def _():
            db_ref[...] += dbfull

    return pl.pallas_call(
        body, name=name, grid=(nb,),
        in_specs=[pl.BlockSpec((S, LANE), lambda b: (b, CB_FA)), pl.BlockSpec((1, LANE), lambda b: (0, 0)),
                  pl.BlockSpec((S, WA), lambda b: (b, 0))],
        out_specs=[pl.BlockSpec((S, LANE), lambda b: (b, 0)), pl.BlockSpec((8, LANE), lambda b: (0, 0))],
        out_shape=[jax.ShapeDtypeStruct((nb * S, LANE), BF16), jax.ShapeDtypeStruct((8, LANE), F32)],
        compiler_params=_cparams(("arbitrary",)),
    )(P, bf_pad, dcb)


FQ = 256
NT_DIMS = (((1,), (1,)), ((), ()))
AUGW = 6


def _qkv_blk(base, i):
    return pl.BlockSpec((S, LANE), lambda b, hp: (b, base + 3 * hp + i))


def _dqkv_blk(base):
    return pl.BlockSpec((S, 3 * LANE), lambda b, hp: (b, base // 3 + hp))


def _head_masks(shape):
    lane = lax.broadcasted_iota(jnp.int32, shape, 1)
    return lane < HD, lane >= HD


def _fox_bias_terms(c_ref, hp):
    lane = lax.broadcasted_iota(jnp.int32, (S, LANE), 1)
    cv = c_ref[...]
    return [_split3(jnp.sum(jnp.where(lane == 2 * hp + e, cv, 0.0), axis=1, keepdims=True)) for e in range(2)]


def _fox_ext(x, terms, side, heads, only):
    lane = lax.broadcasted_iota(jnp.int32, (S, LANE), 1)
    one = jnp.ones((S, 1), BF16)
    aug = jnp.zeros((S, LANE), BF16)
    for e in heads:
        hi, mid, lo = terms[e]
        cols = (hi, mid, lo, one, one, one) if side == "q" else (one, one, one, -hi, -mid, -lo)
        for i, col in enumerate(cols):
            aug = jnp.where(lane == AUGW * e + i, col, aug)
    if only is not None:
        x = jnp.where(_head_masks((S, LANE))[only], x, jnp.zeros_like(x))
    return jnp.concatenate([x, aug], axis=1)


def _halves(x, lane_mask):
    return jnp.where(lane_mask, x[0:FQ, 0:LANE], x[FQ:2 * FQ, 0:LANE])


def _fox_fwd(P, c, nb, name):
    def body(q_ref, k_ref, v_ref, c_ref, o_ref, lse_ref, qm0, qm1, kx, vx):
        hp = pl.program_id(1)
        terms = _fox_bias_terms(c_ref, hp)
        qv = (q_ref[...] * SCALE).astype(BF16)
        qm0[...] = _fox_ext(qv, terms, "q", (0,), 0)
        qm1[...] = _fox_ext(qv, terms, "q", (1,), 1)
        kx[...] = _fox_ext(k_ref[...].astype(BF16), terms, "k", (0, 1), None)
        lane = lax.broadcasted_iota(jnp.int32, (S, LANE), 1)
        vx[...] = jnp.concatenate([v_ref[...].astype(BF16), jnp.where(lane == 0, 1.0, 0.0).astype(BF16)], axis=1)
        tmask = _head_masks((FQ, LANE))[0]
        row = lax.broadcasted_iota(jnp.int32, (2 * FQ, FQ), 0) & (FQ - 1)
        col = lax.broadcasted_iota(jnp.int32, (2 * FQ, FQ), 1)
        for i in range(S // FQ):
            r0 = i * FQ
            qt = jnp.concatenate([qm0[pl.ds(r0, FQ), :], qm1[pl.ds(r0, FQ), :]], axis=0)
            sd = lax.dot_general(qt, kx[pl.ds(r0, FQ), :], NT_DIMS, preferred_element_type=F32)
            sd = jnp.where(col <= row, sd, NEG)
            m = jnp.max(sd, axis=1, keepdims=True)
            if i > 0:
                so = lax.dot_general(qt, kx[pl.ds(0, r0), :], NT_DIMS, preferred_element_type=F32)
                m = jnp.maximum(m, jnp.max(so, axis=1, keepdims=True))
            acc = jnp.dot(jnp.exp(sd - m).astype(BF16), vx[pl.ds(r0, FQ), :], preferred_element_type=F32)
            if i > 0:
                acc = acc + jnp.dot(jnp.exp(so - m).astype(BF16), vx[pl.ds(0, r0), :], preferred_element_type=F32)
            l = acc[:, LANE:LANE + 1]
            o_ref[pl.ds(r0, FQ), :] = _halves(acc / l, tmask)
            lse_ref[pl.ds(r0, FQ), :] = _halves(jnp.broadcast_to(m + jnp.log(l), (2 * FQ, LANE)), tmask)

    def colblk(off):
        return pl.BlockSpec((S, LANE), lambda b, hp: (b, off + hp))

    return pl.pallas_call(
        body, name=name, grid=(nb, NHP),
        in_specs=[_qkv_blk(CB_FOX, 0), _qkv_blk(CB_FOX, 1), _qkv_blk(CB_FOX, 2),
                  pl.BlockSpec((S, LANE), lambda b, hp: (b, 0))],
        out_specs=[colblk(0), colblk(0)],
        out_shape=[jax.ShapeDtypeStruct((nb * S, WA), F32), jax.ShapeDtypeStruct((nb * S, WA), F32)],
        scratch_shapes=[pltpu.VMEM((S, 2 * LANE), BF16)] * 4,
        compiler_params=_cparams(("parallel", "parallel")),
    )(P, P, P, c)


def _fox_bwd(P, c, o, lse, do, nb, name):
    def body(q_ref, k_ref, v_ref, c_ref, o_ref, lse_ref, do_ref, dp_ref, dc_ref,
             km0, km1, qx, vm0, vm1, dob, kt0, kt1, rows, dqt, rsum):
        hp = pl.program_id(1)
        terms = _fox_bias_terms(c_ref, hp)
        kv = k_ref[...].astype(BF16)
        km0[...] = _fox_ext(kv, terms, "k", (0,), 0)
        km1[...] = _fox_ext(kv, terms, "k", (1,), 1)
        qx[...] = _fox_ext((q_ref[...] * SCALE).astype(BF16), terms, "q", (0, 1), None)
        masks = _head_masks((S, LANE))
        vv = v_ref[...].astype(BF16)
        zero = jnp.zeros((S, LANE), BF16)
        vm0[...] = jnp.where(masks[0], vv, zero)
        vm1[...] = jnp.where(masks[1], vv, zero)
        dov = do_ref[...]
        dob[...] = dov.astype(BF16)
        ktf = k_ref[...].T
        prodt = (dov * o_ref[...]).T
        lset = lse_ref[...].T
        hrow = lax.broadcasted_iota(jnp.int32, (LANE, S), 0)
        kt0[...] = jnp.where(hrow < HD, ktf, 0.0).astype(BF16)
        kt1[...] = jnp.where(hrow >= HD, ktf, 0.0).astype(BF16)
        for e in range(2):
            rows[e:e + 1, :] = lset[HD * e:HD * e + 1, :]
            rows[2 + e:3 + e, :] = jnp.sum(prodt[HD * e:HD * (e + 1), :], axis=0, keepdims=True)
        dqt[...] = jnp.zeros_like(dqt)
        rsum[...] = jnp.zeros_like(rsum)
        tmask = _head_masks((FQ, LANE))[0]
        row = lax.broadcasted_iota(jnp.int32, (2 * FQ, FQ), 0) & (FQ - 1)
        col = lax.broadcasted_iota(jnp.int32, (2 * FQ, FQ), 1)
        for j in range(S // FQ):
            k0 = j * FQ
            rest = S - k0 - FQ
            spans = [(k0, FQ)] + ([(k0 + FQ, rest)] if rest > 0 else [])
            kte = jnp.concatenate([km0[pl.ds(k0, FQ), :], km1[pl.ds(k0, FQ), :]], axis=0)
            vte = jnp.concatenate([vm0[pl.ds(k0, FQ), :], vm1[pl.ds(k0, FQ), :]], axis=0)
            ktt = jnp.concatenate([kt0[:, pl.ds(k0, FQ)], kt1[:, pl.ds(k0, FQ)]], axis=1)
            dke = jnp.zeros((2 * FQ, 2 * LANE), F32)
            dve = jnp.zeros((2 * FQ, LANE), F32)
            cse = jnp.zeros((2 * FQ, 1), F32)
            for si, (q0, n) in enumerate(spans):
                qs = qx[pl.ds(q0, n), :]
                dos = dob[pl.ds(q0, n), :]
                st = lax.dot_general(kte, qs, NT_DIMS, preferred_element_type=F32)
                if si == 0:
                    st = jnp.where(col >= row, st, NEG)
                dpt = lax.dot_general(vte, dos, NT_DIMS, preferred_element_type=F32)
                pts, dsts = [], []
                for e in range(2):
                    pe = jnp.exp(st[FQ * e:FQ * (e + 1), :] - rows[e:e + 1, pl.ds(q0, n)])
                    de = pe * (dpt[FQ * e:FQ * (e + 1), :] - rows[2 + e:3 + e, pl.ds(q0, n)])
                    rsum[HD * e:HD * e + 8, pl.ds(q0, n)] += _sum8(de)
                    pts.append(pe)
                    dsts.append(de)
                pt = jnp.concatenate(pts, axis=0)
                dst = jnp.concatenate(dsts, axis=0)
                dsb = dst.astype(BF16)
                dve = dve + jnp.dot(pt.astype(BF16), dos, preferred_element_type=F32)
                dke = dke + jnp.dot(dsb, qs, preferred_element_type=F32)
                dqt[:, pl.ds(q0, n)] += jnp.dot(ktt, dsb, preferred_element_type=F32)
                cse = cse + jnp.sum(dst, axis=1, keepdims=True)
            dp_ref[pl.ds(k0, FQ), LANE:2 * LANE] = _halves(dke, tmask).astype(BF16)
            dp_ref[pl.ds(k0, FQ), 2 * LANE:3 * LANE] = _halves(dve, tmask).astype(BF16)
            dc_ref[pl.ds(k0, FQ), :] = _halves(jnp.broadcast_to(cse, (2 * FQ, LANE)), tmask)
        dp_ref[:, 0:LANE] = (dqt[...].T * SCALE).astype(BF16)
        tot = [jnp.sum(rsum[HD * e:HD * e + 8, :], axis=0, keepdims=True) for e in range(2)]
        dc_ref[...] = dc_ref[...] - jnp.where(hrow == 0, tot[0], jnp.where(hrow == HD, tot[1], 0.0)).T

    def colblk(off):
        return pl.BlockSpec((S, LANE), lambda b, hp: (b, off + hp))

    wide = pltpu.VMEM((S, 2 * LANE), BF16)
    half = pltpu.VMEM((S, LANE), BF16)
    return pl.pallas_call(
        body, name=name, grid=(nb, NHP),
        in_specs=[_qkv_blk(CB_FOX, 0), _qkv_blk(CB_FOX, 1), _qkv_blk(CB_FOX, 2),
                  pl.BlockSpec((S, LANE), lambda b, hp: (b, 0)), colblk(0), colblk(0), colblk(0)],
        out_specs=[_dqkv_blk(CB_FOX), colblk(0)],
        out_shape=[jax.ShapeDtypeStruct((nb * S, NP), BF16), jax.ShapeDtypeStruct((nb * S, WA), F32)],
        scratch_shapes=[wide, wide, wide, half, half, half, pltpu.VMEM((LANE, S), BF16), pltpu.VMEM((LANE, S), BF16),
                        pltpu.VMEM((8, S), F32), pltpu.VMEM((LANE, S), F32), pltpu.VMEM((LANE, S), F32)],
        compiler_params=_cparams(("parallel", "parallel")),
    )(P, P, P, c, o, lse, do)


DILS = (1, 4, 16)
DB = 128


def _regroup_load(ref, d, scale=None):
    if d == 1:
        v = ref[...]
    else:
        L = S // d
        v = jnp.concatenate([ref[pl.ds(r, L, stride=d), :] for r in range(d)], axis=0)
    return v if scale is None else v * scale


def _regroup_store(ref, d, val_ref, accumulate):
    L = S // d
    for r in range(d):
        src = val_ref[pl.ds(r * L, L), :]
        dst = (slice(None), slice(None)) if d == 1 else (pl.ds(r, L, stride=d), slice(None))
        if accumulate:
            ref[dst] = ref[dst] + src
        else:
            ref[dst] = src


def _dil_bands():
    qi = lax.broadcasted_iota(jnp.int32, (DB, 2 * DB), 0)
    ki = lax.broadcasted_iota(jnp.int32, (DB, 2 * DB), 1)
    band = (ki >= qi) & (ki <= qi + DB)
    return band, band & (ki >= DB)


def _dil_valid(bands, bk, d):
    band, own = bands
    has_prev = (bk % ((S // d) // DB)) > 0
    return own | (band & has_prev)


def _dil_keys(kd, vd, r0, bands, bk, d):
    if S // d == DB:
        valid = bands[1][:, DB:]
        kk, vv = kd[pl.ds(r0 + DB, DB), :], vd[pl.ds(r0 + DB, DB), :]
    else:
        valid = _dil_valid(bands, bk, d)
        kk, vv = kd[pl.ds(r0, 2 * DB), :], vd[pl.ds(r0, 2 * DB), :]
    return kk, vv, jnp.concatenate([valid, valid], axis=0)


def _stack_heads(x, masks):
    zero = jnp.zeros_like(x)
    return jnp.concatenate([jnp.where(masks[0], x, zero), jnp.where(masks[1], x, zero)], axis=0)


def _dil_fwd(P, nb, name):
    nblk = S // DB

    def body(q_ref, k_ref, v_ref, o_ref, lse_ref, qd, kd, vd, rnum, rm, rl, num_n, m_n, l_n):
        masks = _head_masks((DB, LANE))
        bands = _dil_bands()
        for bi, d in enumerate(DILS):
            qd[...] = _regroup_load(q_ref, d, SCALE).astype(BF16)
            kd[pl.ds(0, DB), :] = jnp.zeros((DB, LANE), BF16)
            vd[pl.ds(0, DB), :] = jnp.zeros((DB, LANE), BF16)
            kd[pl.ds(DB, S), :] = _regroup_load(k_ref, d).astype(BF16)
            vd[pl.ds(DB, S), :] = _regroup_load(v_ref, d).astype(BF16)

            def blk(bk, _, d=d):
                r0 = pl.multiple_of(bk * DB, DB)
                qt = qd[pl.ds(r0, DB), :]
                kk, vv, valid = _dil_keys(kd, vd, r0, bands, bk, d)
                qm = _stack_heads(qt, masks)
                s = lax.dot_general(qm, kk, NT_DIMS, preferred_element_type=F32)
                s = jnp.where(valid, s, NEG)
                m = jnp.max(s, axis=1, keepdims=True)
                p = jnp.exp(s - m)
                l = jnp.sum(p, axis=1, keepdims=True)
                num = jnp.dot(p.astype(BF16), vv, preferred_element_type=F32)
                rnum[pl.ds(r0, DB), :] = jnp.where(masks[0], num[0:DB], num[DB:2 * DB])
                rm[pl.ds(r0, DB), :] = jnp.where(masks[0], m[0:DB], m[DB:2 * DB])
                rl[pl.ds(r0, DB), :] = jnp.where(masks[0], l[0:DB], l[DB:2 * DB])
                return 0

            lax.fori_loop(0, nblk, blk, 0, unroll=nblk)
            _regroup_store(num_n.at[bi], d, rnum, False)
            _regroup_store(m_n.at[bi], d, rm, False)
            _regroup_store(l_n.at[bi], d, rl, False)

        m_all = jnp.maximum(jnp.maximum(m_n[0], m_n[1]), m_n[2])
        num = jnp.zeros((S, LANE), F32)
        den = jnp.zeros((S, LANE), F32)
        for bi in range(3):
            a = jnp.exp(m_n[bi] - m_all)
            num = num + a * num_n[bi]
            den = den + a * l_n[bi]
        o_ref[...] = num / den
        lse_ref[...] = m_all + jnp.log(den)

    def colblk(off):
        return pl.BlockSpec((S, LANE), lambda b, hp: (b, off + hp))

    return pl.pallas_call(
        body, name=name, grid=(nb, NHP),
        in_specs=[_qkv_blk(CB_DIL, 0), _qkv_blk(CB_DIL, 1), _qkv_blk(CB_DIL, 2)],
        out_specs=[colblk(0), colblk(0)],
        out_shape=[jax.ShapeDtypeStruct((nb * S, WA), F32), jax.ShapeDtypeStruct((nb * S, WA), F32)],
        scratch_shapes=[pltpu.VMEM((S, LANE), BF16), pltpu.VMEM((S + DB, LANE), BF16), pltpu.VMEM((S + DB, LANE), BF16),
                        pltpu.VMEM((S, LANE), F32), pltpu.VMEM((S, LANE), F32), pltpu.VMEM((S, LANE), F32),
                        pltpu.VMEM((3, S, LANE), F32), pltpu.VMEM((3, S, LANE), F32), pltpu.VMEM((3, S, LANE), F32)],
        compiler_params=_cparams(("parallel", "parallel")),
    )(P, P, P)


def _dil_bwd(P, o, lse, do, dP, nb, name):
    nblk = S // DB

    def body(q_ref, k_ref, v_ref, o_ref, lse_ref, do_ref, dp_in, dp_ref,
             qd, kd, vd, dod, lsed, dsd, dsum, dq_r, dk_r, dv_r, dq_n, dk_n, dv_n):
        del dp_in
        masks = _head_masks((DB, LANE))
        fmask = _head_masks((S, LANE))
        prod = do_ref[...] * o_ref[...]
        d0 = jnp.sum(jnp.where(fmask[0], prod, 0.0), axis=1, keepdims=True)
        d1 = jnp.sum(jnp.where(fmask[1], prod, 0.0), axis=1, keepdims=True)
        dsum[...] = jnp.where(fmask[0], d0, d1)
        tn = (((0,), (0,)), ((), ()))
        bands = _dil_bands()
        for bi, d in enumerate(DILS):
            qd[...] = _regroup_load(q_ref, d, SCALE).astype(BF16)
            kd[pl.ds(0, DB), :] = jnp.zeros((DB, LANE), BF16)
            vd[pl.ds(0, DB), :] = jnp.zeros((DB, LANE), BF16)
            kd[pl.ds(DB, S), :] = _regroup_load(k_ref, d).astype(BF16)
            vd[pl.ds(DB, S), :] = _regroup_load(v_ref, d).astype(BF16)
            dod[...] = _regroup_load(do_ref, d).astype(BF16)
            lsed[...] = _regroup_load(lse_ref, d)
            dsd[...] = _regroup_load(dsum, d)
            dq_r[...] = jnp.zeros_like(dq_r)
            dk_r[...] = jnp.zeros_like(dk_r)
            dv_r[...] = jnp.zeros_like(dv_r)

            def blk(bk, _, d=d):
                r0 = pl.multiple_of(bk * DB, DB)
                qt = qd[pl.ds(r0, DB), :]
                dot = dod[pl.ds(r0, DB), :]
                lt = lsed[pl.ds(r0, DB), :]
                dt = dsd[pl.ds(r0, DB), :]
                kk, vv, valid = _dil_keys(kd, vd, r0, bands, bk, d)
                kw = kk.shape[0]
                qm = _stack_heads(qt, masks)
                dom = _stack_heads(dot, masks)
                lcol = jnp.concatenate([lt[:, 0:1], lt[:, HD:HD + 1]], axis=0)
                dcol = jnp.concatenate([dt[:, 0:1], dt[:, HD:HD + 1]], axis=0)
                s = lax.dot_general(qm, kk, NT_DIMS, preferred_element_type=F32)
                s = jnp.where(valid, s, NEG)
                p = jnp.exp(s - lcol)
                dp = lax.dot_general(dom, vv, NT_DIMS, preferred_element_type=F32)
                ds = (p * (dp - dcol)).astype(BF16)
                dvt = lax.dot_general(p.astype(BF16), dom, tn, preferred_element_type=F32)
                dkt = lax.dot_general(ds, qm, tn, preferred_element_type=F32)
                dqt = jnp.dot(ds, kk, preferred_element_type=F32)
                dq_r[pl.ds(r0, DB), :] = jnp.where(masks[0], dqt[0:DB], dqt[DB:2 * DB])
                dk_r[pl.ds(r0 + 2 * DB - kw, kw), :] += dkt
                dv_r[pl.ds(r0 + 2 * DB - kw, kw), :] += dvt
                return 0

            lax.fori_loop(0, nblk, blk, 0, unroll=nblk)
            _regroup_store(dq_n, d, dq_r, bi > 0)
            _regroup_store(dk_n, d, dk_r.at[pl.ds(DB, S)], bi > 0)
            _regroup_store(dv_n, d, dv_r.at[pl.ds(DB, S)], bi > 0)

        dp_ref[:, 0:LANE] = (dq_n[...] * SCALE).astype(BF16)
        dp_ref[:, LANE:2 * LANE] = dk_n[...].astype(BF16)
        dp_ref[:, 2 * LANE:3 * LANE] = dv_n[...].astype(BF16)

    def colblk(off):
        return pl.BlockSpec((S, LANE), lambda b, hp: (b, off + hp))

    big = pltpu.VMEM((S, LANE), F32)
    bigp = pltpu.VMEM((S + DB, LANE), F32)
    return pl.pallas_call(
        body, name=name, grid=(nb, NHP),
        in_specs=[_qkv_blk(CB_DIL, 0), _qkv_blk(CB_DIL, 1), _qkv_blk(CB_DIL, 2), colblk(0), colblk(0), colblk(0),
                  pl.BlockSpec(memory_space=pl.ANY)],
        out_specs=_dqkv_blk(CB_DIL), out_shape=jax.ShapeDtypeStruct((nb * S, NP), BF16),
        input_output_aliases={6: 0},
        scratch_shapes=[pltpu.VMEM((S, LANE), BF16), pltpu.VMEM((S + DB, LANE), BF16), pltpu.VMEM((S + DB, LANE), BF16),
                        pltpu.VMEM((S, LANE), BF16), big, big, big, big, bigp, bigp, big, big, big],
        compiler_params=_cparams(("parallel", "parallel")),
    )(P, P, P, o, lse, do, dP)


RC = 256
NSHW = 4


def _window(src, start, buf):
    rows = buf.shape[0]
    if start % 8 == 0:
        return src[pl.ds(start, rows), :]
    buf[...] = src[pl.ds(start, rows), :]
    return buf[...]
CPAD = 32


NSUB = 8
CROWS = S + CPAD


def _preshift(src, dst):
    for b in range(NSUB):
        dst[b] = src[pl.ds(b, CROWS), :]


def _shifted(dst, start):
    return dst[start % NSUB, pl.ds(start - start % NSUB, RC), :]


def _conv_chunk(gsh, r0, cw_ref, cb_ref):
    acc = jnp.zeros((RC, CC), F32) + cb_ref[...]
    for k in range(CK):
        acc = acc + cw_ref[k:k + 1, :] * _shifted(gsh, r0 + CPAD - (CK - 1) + k)
    return acc


def _cnorm(c0, cng_ref, cnb_ref):
    mu = jnp.mean(c0, axis=1, keepdims=True)
    xc = c0 - mu
    rstd = lax.rsqrt(jnp.mean(xc * xc, axis=1, keepdims=True) + EPS)
    n = xc * rstd
    return n, rstd, n * cng_ref[...] + cnb_ref[...]


NORM_ROWS = 512
YC_BLK = 2 * WA // CC


def _attn_norm_fwd(of, od, gof, god, name):
    T = of.shape[0]

    def body(of_ref, od_ref, gof_ref, god_ref, y_ref):
        for i, (src, g_ref) in enumerate(((of_ref, gof_ref), (od_ref, god_ref))):
            v = src[...]
            r = lax.rsqrt(jnp.mean(v * v, axis=1, keepdims=True) + EPS)
            y_ref[:, i * WA:(i + 1) * WA] = (v * r * g_ref[...]).astype(BF16)

    row = lambda w: pl.BlockSpec((NORM_ROWS, w), lambda i: (i, 0))
    par = pl.BlockSpec((1, WA), lambda i: (0, 0))
    return pl.pallas_call(
        body, name=name, grid=(T // NORM_ROWS,),
        in_specs=[row(WA), row(WA), par, par], out_specs=row(2 * WA),
        out_shape=jax.ShapeDtypeStruct((T, D), BF16),
        compiler_params=_cparams(("parallel",)),
    )(of, od, gof, god)


def _attn_norm_bwd(of, od, dy, gof, god, name):
    T = of.shape[0]

    def body(of_ref, od_ref, dy_ref, gof_ref, god_ref, dof_ref, dod_ref, dgo_ref):
        @pl.when(pl.program_id(0) == 0)
        def _():
            dgo_ref[...] = jnp.zeros_like(dgo_ref)

        for i, (src, g_ref, dst) in enumerate(((of_ref, gof_ref, dof_ref), (od_ref, god_ref, dod_ref))):
            v = src[...]
            dyv = dy_ref[:, i * WA:(i + 1) * WA].astype(F32)
            r = lax.rsqrt(jnp.mean(v * v, axis=1, keepdims=True) + EPS)
            a = dyv * g_ref[...]
            dst[...] = r * a - v * (r * r * r * jnp.mean(v * a, axis=1, keepdims=True))
            dgo_ref[i:i + 1, :] += jnp.sum(dyv * v * r, axis=0, keepdims=True)

    row = lambda w: pl.BlockSpec((NORM_ROWS, w), lambda i: (i, 0))
    par = pl.BlockSpec((1, WA), lambda i: (0, 0))
    return pl.pallas_call(
        body, name=name, grid=(T // NORM_ROWS,),
        in_specs=[row(WA), row(WA), row(2 * WA), par, par],
        out_specs=[row(WA), row(WA), pl.BlockSpec((8, WA), lambda i: (0, 0))],
        out_shape=[jax.ShapeDtypeStruct((T, WA), F32), jax.ShapeDtypeStruct((T, WA), F32),
                   jax.ShapeDtypeStruct((8, WA), F32)],
        compiler_params=_cparams(("arbitrary",)),
    )(of, od, dy, gof, god)


def _conv_specs():
    gblk = lambda off: pl.BlockSpec((S, CC), lambda b: (b, off))
    par = lambda r: pl.BlockSpec((r, CC), lambda b: (0, 0))
    return gblk, par


def _conv_fwd(P, y, cw, cb, cng, cnb, nb, name):
    def body(gv_ref, gg_ref, cw_ref, cb_ref, cng_ref, cnb_ref, y_in, y_ref, c0_ref, gpad, gsh):
        del y_in
        gpad[pl.ds(0, CPAD), :] = jnp.zeros((CPAD, CC), F32)
        gpad[pl.ds(CPAD, S), :] = gv_ref[...] * _sigmoid(gg_ref[...])
        gpad[pl.ds(CROWS, NSUB), :] = jnp.zeros((NSUB, CC), F32)
        _preshift(gpad, gsh)
        for ci in range(S // RC):
            r0 = ci * RC
            c0 = _conv_chunk(gsh, r0, cw_ref, cb_ref)
            c0_ref[pl.ds(r0, RC), :] = c0
            _, _, z = _cnorm(c0, cng_ref, cnb_ref)
            y_ref[pl.ds(r0, RC), :] = (z * _sigmoid(z)).astype(BF16)

    gblk, par = _conv_specs()
    return pl.pallas_call(
        body, name=name, grid=(nb,),
        in_specs=[gblk(CB_GV), gblk(CB_GG), par(CPAD), par(1), par(1), par(1), pl.BlockSpec(memory_space=pl.ANY)],
        out_specs=[gblk(YC_BLK), gblk(0)],
        out_shape=[jax.ShapeDtypeStruct((nb * S, D), BF16), jax.ShapeDtypeStruct((nb * S, CC), F32)],
        input_output_aliases={6: 0},
        scratch_shapes=[pltpu.VMEM((CROWS + NSUB, CC), F32), pltpu.VMEM((NSUB, CROWS, CC), F32)],
        compiler_params=_cparams(("parallel",)),
    )(P, P, cw, cb, cng, cnb, y)


def _conv_bwd(P, c0, dy, dfa, dP, cw, cng, cnb, nb, name):
    def body(gv_ref, gg_ref, c0_ref, dy_ref, dfa_ref, cw_ref, cng_ref, cnb_ref, dp_in, dg_ref, dcw_ref, dsm_ref,
             dpad, dsh):
        del dp_in
        dg_ref[:, 2 * CC:2 * CC + LANE] = dfa_ref[...]
        dg_ref[:, 2 * CC + LANE:3 * CC] = jnp.zeros((S, CC - LANE), BF16)
        @pl.when(pl.program_id(0) == 0)
        def _():
            dcw_ref[...] = jnp.zeros_like(dcw_ref)
            dsm_ref[...] = jnp.zeros_like(dsm_ref)

        dpad[pl.ds(S, CPAD + NSUB), :] = jnp.zeros((CPAD + NSUB, CC), F32)
        zero = jnp.zeros((8, CC), F32)
        dcb, dcng, dcnb = zero, zero, zero
        for ci in range(S // RC):
            r0 = ci * RC
            n, rstd, z = _cnorm(c0_ref[pl.ds(r0, RC), :], cng_ref, cnb_ref)
            sz = _sigmoid(z)
            dz = dy_ref[pl.ds(r0, RC), :].astype(F32) * (sz * (1.0 + z * (1.0 - sz)))
            dcng = dcng + _sum8(dz * n)
            dcnb = dcnb + _sum8(dz)
            dn = dz * cng_ref[...]
            dc0 = rstd * (dn - jnp.mean(dn, axis=1, keepdims=True) - n * jnp.mean(dn * n, axis=1, keepdims=True))
            dcb = dcb + _sum8(dc0)
            dpad[pl.ds(r0, RC), :] = dc0
        dsm_ref[0:1, :] += jnp.sum(dcb, axis=0, keepdims=True)
        dsm_ref[1:2, :] += jnp.sum(dcng, axis=0, keepdims=True)
        dsm_ref[2:3, :] += jnp.sum(dcnb, axis=0, keepdims=True)

        _preshift(dpad, dsh)
        dws = [zero] * CK
        for ci in range(S // RC):
            r0 = ci * RC
            sg = _sigmoid(gg_ref[pl.ds(r0, RC), :])
            gvc = gv_ref[pl.ds(r0, RC), :]
            glu = gvc * sg
            dgl = jnp.zeros((RC, CC), F32)
            for k in range(CK):
                win = _shifted(dsh, r0 + (CK - 1) - k)
                dws[k] = dws[k] + _sum8(win * glu)
                dgl = dgl + cw_ref[k:k + 1, :] * win
            dg_ref[pl.ds(r0, RC), 0:CC] = (dgl * sg).astype(BF16)
            dg_ref[pl.ds(r0, RC), CC:2 * CC] = (dgl * gvc * sg * (1.0 - sg)).astype(BF16)
        for k in range(CK):
            dcw_ref[k:k + 1, :] += jnp.sum(dws[k], axis=0, keepdims=True)

    gblk, par = _conv_specs()
    return pl.pallas_call(
        body, name=name, grid=(nb,),
        in_specs=[gblk(CB_GV), gblk(CB_GG), gblk(0), gblk(YC_BLK), pl.BlockSpec((S, LANE), lambda b: (b, 0)),
                  par(CPAD), par(1), par(1), pl.BlockSpec(memory_space=pl.ANY)],
        out_specs=[pl.BlockSpec((S, 3 * CC), lambda b: (b, CB_GV // 3)), par(CPAD), par(8)],
        out_shape=[jax.ShapeDtypeStruct((nb * S, NP), BF16), jax.ShapeDtypeStruct((CPAD, CC), F32),
                   jax.ShapeDtypeStruct((8, CC), F32)],
        input_output_aliases={8: 0},
        scratch_shapes=[pltpu.VMEM((CROWS + NSUB, CC), F32), pltpu.VMEM((NSUB, CROWS, CC), F32)],
        compiler_params=_cparams(("arbitrary",)),
    )(P, P, c0, dy, dfa, cw, cng, cnb, dP)


FC = 512
FPAD = 8
FRC = 32
NFB = 2 * DFF // FC


def _ffn_u2_chunk(upad, r0, fw_ref, fb_ref):
    acc = jnp.zeros((FRC, FC), F32) + fb_ref[...]
    for k in range(FK):
        acc = acc + fw_ref[k:k + 1, :] * upad[pl.ds(r0 + FPAD - (FK - 1) + k, FRC), :]
    return acc


def _ffn_fwd(U, fw, fb, nb, name):
    def body(u_ref, fw_ref, fb_ref, h_ref, u2_ref, upad):
        upad[pl.ds(0, FPAD), :] = jnp.zeros((FPAD, FC), F32)
        upad[pl.ds(FPAD, S), :] = u_ref[...].astype(F32)
        for ci in range(S // FRC):
            r0 = ci * FRC
            u2 = _ffn_u2_chunk(upad, r0, fw_ref, fb_ref)
            u2_ref[pl.ds(r0, FRC), :] = u2.astype(BF16)
            a2, b2 = u2[:, :FC // 2], u2[:, FC // 2:]
            h_ref[pl.ds(r0, FRC), :] = (a2 * _sigmoid(a2) * b2).astype(BF16)

    return pl.pallas_call(
        body, name=name, grid=(nb, NFB),
        in_specs=[pl.BlockSpec((S, FC), lambda b, j: (b, j)), pl.BlockSpec((8, FC), lambda b, j: (0, j)),
                  pl.BlockSpec((1, FC), lambda b, j: (0, j))],
        out_specs=[pl.BlockSpec((S, FC // 2), lambda b, j: (b, j)), pl.BlockSpec((S, FC), lambda b, j: (b, j))],
        out_shape=[jax.ShapeDtypeStruct((nb * S, DFF), BF16), jax.ShapeDtypeStruct((nb * S, 2 * DFF), BF16)],
        scratch_shapes=[pltpu.VMEM((S + FPAD, FC), F32)],
        compiler_params=_cparams(("parallel", "parallel")),
    )(U, fw, fb)


def _ffn_bwd(U, U2, dhid, fw, nb, name):
    def body(u_ref, u2_ref, dh_ref, fw_ref, du_ref, dfw_ref, dpad, shw):
        @pl.when(pl.program_id(1) == 0)
        def _():
            dfw_ref[...] = jnp.zeros_like(dfw_ref)

        dpad[pl.ds(S, FPAD), :] = jnp.zeros((FPAD, FC), F32)
        zero = jnp.zeros((8, FC), F32)
        dbias = zero
        for ci in range(S // FRC):
            r0 = ci * FRC
            u2 = u2_ref[pl.ds(r0, FRC), :].astype(F32)
            a2, b2 = u2[:, :FC // 2], u2[:, FC // 2:]
            sa = _sigmoid(a2)
            dh = dh_ref[pl.ds(r0, FRC), :].astype(F32)
            du2 = jnp.concatenate([dh * b2 * (sa * (1.0 + a2 * (1.0 - sa))), dh * a2 * sa], axis=1)
            dpad[pl.ds(r0, FRC), :] = du2
            dbias = dbias + _sum8(du2)
        dws = [zero] * FK
        for ci in range(S // FRC):
            r0 = ci * FRC
            uc = u_ref[pl.ds(r0, FRC), :].astype(F32)
            du = jnp.zeros((FRC, FC), F32)
            for k in range(FK):
                win = _window(dpad, r0 + (FK - 1) - k, shw.at[k % NSHW])
                dws[k] = dws[k] + _sum8(win * uc)
                du = du + fw_ref[k:k + 1, :] * win
            du_ref[pl.ds(r0, FRC), :] = du.astype(BF16)
        for k in range(FK):
            dfw_ref[k:k + 1, :] += jnp.sum(dws[k], axis=0, keepdims=True)
        dfw_ref[FK:FK + 1, :] += jnp.sum(dbias, axis=0, keepdims=True)

    blk = pl.BlockSpec((S, FC), lambda j, b: (b, j))
    return pl.pallas_call(
        body, name=name, grid=(NFB, nb),
        in_specs=[blk, blk, pl.BlockSpec((S, FC // 2), lambda j, b: (b, j)), pl.BlockSpec((8, FC), lambda j, b: (0, j))],
        out_specs=[blk, pl.BlockSpec((8, FC), lambda j, b: (0, j))],
        out_shape=[jax.ShapeDtypeStruct((nb * S, 2 * DFF), BF16), jax.ShapeDtypeStruct((8, 2 * DFF), F32)],
        scratch_shapes=[pltpu.VMEM((S + FPAD, FC), F32), pltpu.VMEM((NSHW, FRC, FC), F32)],
        compiler_params=_cparams(("parallel", "arbitrary")),
    )(U, U2, dhid, fw)


def _matmul_ffn(a, b, mode, *, out_dtype=F32, tm=1024, tk=2048, norm_g=None, name):
    HF = FC // 2
    if mode == "fwd":
        M, K = a.shape
        tm = 512

        def body(a_ref, g_ref, b_ref, h_ref, o_ref):
            av = _rms_rows(a_ref[...], g_ref)
            h_ref[...] = av
            for j in range(NFB):
                for half in range(2):
                    bv = b_ref[:, half * DFF + j * HF:half * DFF + (j + 1) * HF]
                    o_ref[:, j * FC + half * HF:j * FC + (half + 1) * HF] = jnp.dot(
                        av, bv, preferred_element_type=F32).astype(o_ref.dtype)

        return pl.pallas_call(
            body, name=name, grid=(M // tm,),
            in_specs=[pl.BlockSpec((tm, K), lambda i: (i, 0)), pl.BlockSpec((1, K), lambda i: (0, 0)),
                      pl.BlockSpec((K, 2 * DFF), lambda i: (0, 0))],
            out_specs=[pl.BlockSpec((tm, K), lambda i: (i, 0)), pl.BlockSpec((tm, 2 * DFF), lambda i: (i, 0))],
            out_shape=[jax.ShapeDtypeStruct((M, K), BF16), jax.ShapeDtypeStruct((M, 2 * DFF), out_dtype)],
            compiler_params=_cparams(("parallel",)),
        )(a, norm_g.reshape(1, K), b)
    if mode == "dx":
        M = a.shape[0]
        N = b.shape[0]
        tm = 512

        def body(a_ref, b_ref, o_ref):
            acc = None
            for j in range(NFB):
                for half in range(2):
                    av = a_ref[:, j * FC + half * HF:j * FC + (half + 1) * HF]
                    bv = b_ref[:, half * DFF + j * HF:half * DFF + (j + 1) * HF]
                    d = lax.dot_general(av, bv, NT_DIMS, preferred_element_type=F32)
                    acc = d if acc is None else acc + d
            o_ref[...] = acc.astype(o_ref.dtype)

        return pl.pallas_call(
            body, name=name, grid=(M // tm,),
            in_specs=[pl.BlockSpec((tm, 2 * DFF), lambda i: (i, 0)), pl.BlockSpec((N, 2 * DFF), lambda i: (0, 0))],
            out_specs=pl.BlockSpec((tm, N), lambda i: (i, 0)),
            out_shape=jax.ShapeDtypeStruct((M, N), out_dtype),
            compiler_params=_cparams(("parallel",)),
        )(a, b)
    assert mode == "dw"
    T, M = a.shape
    nk = T // tk

    def body(a_ref, g_ref, oa_ref, ob_ref, acc):
        k = pl.program_id(1)
        prod = lax.dot_general(a_ref[...], g_ref[...], (((0,), (0,)), ((), ())), preferred_element_type=F32)

        @pl.when(k == 0)
        def _():
            acc[...] = prod

        @pl.when(k > 0)
        def _():
            acc[...] += prod

        @pl.when(k == nk - 1)
        def _():
            oa_ref[...] = acc[:, :HF]
            ob_ref[...] = acc[:, HF:]

    half = pl.BlockSpec((M, HF), lambda j, k: (0, j))
    return pl.pallas_call(
        body, name=name, grid=(NFB, nk),
        in_specs=[pl.BlockSpec((tk, M), lambda j, k: (k, 0)), pl.BlockSpec((tk, FC), lambda j, k: (k, j))],
        out_specs=[half, half],
        out_shape=[jax.ShapeDtypeStruct((M, DFF), F32)] * 2,
        scratch_shapes=[pltpu.VMEM((M, FC), F32)],
        compiler_params=_cparams(("parallel", "arbitrary")),
    )(a, b)


def _adamw_body(w_ref, g_ref, m_ref, v_ref, d_ref, nm_ref, nv_ref):
    g = g_ref[...]
    m = ADAM_B1 * m_ref[...] + (1.0 - ADAM_B1) * g
    v = ADAM_B2 * v_ref[...] + (1.0 - ADAM_B2) * (g * g)
    m_hat = m / (1.0 - ADAM_B1 ** ADAM_STEP)
    v_hat = v / (1.0 - ADAM_B2 ** ADAM_STEP)
    d_ref[...] = -ADAM_LR * (m_hat / (jnp.sqrt(v_hat) + ADAM_EPS) + ADAM_WD * w_ref[...])
    nm_ref[...] = m
    nv_ref[...] = v


def _adamw(w, g, m, v, name):
    shape = w.shape
    R = 1
    for s in shape[:-1]:
        R *= s
    C = shape[-1]
    args = [a.reshape(R, C) for a in (w, g, m, v)]
    tr = R
    for cand in (512, 352, 256, 128, 64, 32, 16, 8):
        if R % cand == 0 and cand * C * 4 * 14 <= 24 * 1024 * 1024:
            tr = cand
            break
    blk = pl.BlockSpec((tr, C), lambda i: (i, 0))
    outs = pl.pallas_call(
        functools.partial(_adamw_body), name=name, grid=(R // tr,),
        in_specs=[blk] * 4, out_specs=[blk] * 3,
        out_shape=[jax.ShapeDtypeStruct((R, C), F32)] * 3,
        compiler_params=_cparams(("parallel",)),
    )(*args)
    return [o.reshape(shape) for o in outs]


REF_FOX, REF_DIL, REF_GATE = 0, N_QKV + N_FG, 2 * N_QKV + N_FG


def _pack_cols_value(w):
    parts = []
    for ref0 in (REF_FOX, REF_DIL):
        for hp in range(NHP):
            parts += [w[:, ref0 + i * WA + hp * LANE:ref0 + i * WA + (hp + 1) * LANE] for i in range(3)]
    parts += [w[:, REF_GATE:NIN], w[:, N_QKV:N_QKV + N_FG], jnp.zeros((w.shape[0], NP - NIN), w.dtype)]
    return jnp.concatenate(parts, axis=1)


def _unpack_cols_value(g):
    def group(cb):
        return [g[:, (cb + 3 * hp + i) * LANE:(cb + 3 * hp + i + 1) * LANE] for i in range(3) for hp in range(NHP)]
    fa0 = 2 * N_QKV + 2 * CC
    return jnp.concatenate(group(CB_FOX) + [g[:, fa0:fa0 + N_FG]] + group(CB_DIL) + [g[:, 2 * N_QKV:fa0]], axis=1)


def _adamw_layers(w, gs, m, v, name, packed=False):
    _, R, C = w.shape
    Cg = gs[0].shape[1]
    tr = 128 if R % 128 == 0 else 176
    assert R % tr == 0 and len(gs) == DEPTH == 2

    def body(w_ref, g0_ref, g1_ref, m_ref, v_ref, g_out, d_ref, nm_ref, nv_ref):
        g = jnp.where(pl.program_id(0) == 0, g0_ref[...], g1_ref[...])
        if packed:
            g = _unpack_cols_value(g)
        g_out[...] = g
        mn = ADAM_B1 * m_ref[...] + (1.0 - ADAM_B1) * g
        vn = ADAM_B2 * v_ref[...] + (1.0 - ADAM_B2) * (g * g)
        m_hat = mn / (1.0 - ADAM_B1 ** ADAM_STEP)
        v_hat = vn / (1.0 - ADAM_B2 ** ADAM_STEP)
        d_ref[...] = -ADAM_LR * (m_hat / (jnp.sqrt(v_hat) + ADAM_EPS) + ADAM_WD * w_ref[...])
        nm_ref[...] = mn
        nv_ref[...] = vn

    lay = pl.BlockSpec((None, tr, C), lambda l, i: (l, i, 0))
    gsp = pl.BlockSpec((tr, Cg), lambda l, i: (i, 0))
    return pl.pallas_call(
        body, name=name, grid=(DEPTH, R // tr),
        in_specs=[lay, gsp, gsp, lay, lay], out_specs=[lay] * 4,
        out_shape=[jax.ShapeDtypeStruct((DEPTH, R, C), F32)] * 4,
        compiler_params=_cparams(("parallel", "parallel")),
    )(w, gs[0], gs[1], m, v)


def _pack_w_in(w_in):
    _, R, _ = w_in.shape

    def body(w_ref, o_ref):
        o_ref[...] = _pack_cols_value(w_ref[...]).astype(BF16)

    return pl.pallas_call(
        body, name="pack_w_in", grid=(DEPTH,),
        in_specs=[pl.BlockSpec((None, R, NIN), lambda l: (l, 0, 0))],
        out_specs=pl.BlockSpec((None, R, NP), lambda l: (l, 0, 0)),
        out_shape=jax.ShapeDtypeStruct((DEPTH, R, NP), BF16),
        compiler_params=_cparams(("parallel",)),
    )(w_in)


def _rs_row_tile(H):
    th = H if H <= 512 else 128
    assert H % th == 0 and th % 16 == 0
    return th


def _add_half(g, r1, place, name):
    _, R, C = g.shape
    H = R // 2
    th = _rs_row_tile(H)
    nh = H // th

    def body(s_ref, g_ref, r_ref, o_ref):
        o_ref[...] = (g_ref[...] + r_ref[...]).astype(BF16)

    grid_spec = pltpu.PrefetchScalarGridSpec(
        num_scalar_prefetch=1, grid=(NCHIP, nh),
        in_specs=[pl.BlockSpec((None, th, C), lambda p, i, s: (p, s[1] * nh + i, 0)),
                  pl.BlockSpec((None, th, C), lambda p, i, s: (p, i, 0))],
        out_specs=pl.BlockSpec((None, th, C), lambda p, i, s: (p, i, 0)))
    return pl.pallas_call(
        body, name=name, grid_spec=grid_spec, out_shape=jax.ShapeDtypeStruct((NCHIP, H, C), BF16),
        compiler_params=_cparams(("parallel", "parallel")),
    )(place, g, r1)


def _sum_slots(g, r1, r2, place, name):
    _, R, C = g.shape
    H = R // 2
    th = _rs_row_tile(H)
    nh = H // th

    def body(s_ref, g_ref, r1_ref, r2_ref, o_ref):
        acc = g_ref[...] + r1_ref[...]
        for j in range(NCHIP - 1):
            acc = acc + r2_ref[j].astype(F32)
        o_ref[...] = acc

    grid_spec = pltpu.PrefetchScalarGridSpec(
        num_scalar_prefetch=1, grid=(nh,),
        in_specs=[pl.BlockSpec((None, th, C), lambda i, s: (s[0], s[1] * nh + i, 0)),
                  pl.BlockSpec((None, th, C), lambda i, s: (s[0], i, 0)),
                  pl.BlockSpec((NCHIP - 1, th, C), lambda i, s: (0, i, 0))],
        out_specs=pl.BlockSpec((None, th, C), lambda i, s: (s[1], i, 0)))
    return pl.pallas_call(
        body, name=name, grid_spec=grid_spec, out_shape=jax.ShapeDtypeStruct((2, H, C), F32),
        compiler_params=_cparams(("parallel",)),
    )(place, g, r1, r2)


MESH = pl.DeviceIdType.MESH
HBM = pl.BlockSpec(memory_space=pltpu.HBM)


def _place():
    x, y, c = lax.axis_index("x"), lax.axis_index("y"), lax.axis_index("c")
    chips = [(1 - x, y), (x, 1 - y), (1 - x, 1 - y)]
    return x, y, c, chips


def _rcopy(src, dst, ssem, rsem, dev):
    return pltpu.make_async_remote_copy(src_ref=src, dst_ref=dst, send_sem=ssem, recv_sem=rsem,
                                        device_id=dev, device_id_type=MESH)


AG_CHUNK_BYTES = 1 << 20


def _stage_rows(R, C, dtype):
    rows = R
    while rows * C * jnp.dtype(dtype).itemsize > AG_CHUNK_BYTES and rows % 32 == 0:
        rows //= 2
    return rows


def _allgather(shards, split):
    n = len(shards)
    nout = n * DEPTH
    rows = [_stage_rows(s.shape[1], s.shape[2], s.dtype) for s in shards]

    def body(*refs):
        ins, outs = refs[:n], refs[n:n + nout]
        stages = refs[n + nout:2 * n + nout]
        ssem, rsem, fssem, frsem, isem, osem = refs[2 * n + nout:]
        x, y, c, chips = _place()
        me = 2 * x + y
        sib = (x, y, 1 - c)

        def window(t, l, chip, half):
            if not split[t]:
                return outs[t * DEPTH + l].at[chip]
            H = shards[t].shape[1] // 2
            return outs[t * DEPTH + l].at[chip, pl.ds(half * H, H)]

        sends = []
        for t in range(n):
            H = shards[t].shape[1] // 2
            for l in range(DEPTH):
                src = ins[t].at[l, pl.ds(c * H, H)] if split[t] else ins[t].at[l]
                for j, (cx, cy) in enumerate(chips):
                    k = (t * DEPTH + l) * 3 + j
                    cp = _rcopy(src, window(t, l, me, c), ssem.at[k], rsem.at[k], (cx, cy, c))
                    cp.start()
                    sends.append(cp)
        for t in range(n):
            nch = shards[t].shape[1] // rows[t]
            outc = []
            for l in range(DEPTH):
                for i in range(nch):
                    slot = len(outc) % 2
                    if len(outc) >= 2:
                        outc[-2].wait()
                    rs = pl.ds(i * rows[t], rows[t])
                    cin = pltpu.make_async_copy(ins[t].at[l, rs], stages[t].at[slot], isem.at[t])
                    cin.start()
                    cin.wait()
                    co = pltpu.make_async_copy(stages[t].at[slot], outs[t * DEPTH + l].at[me, rs], osem.at[2 * t + slot])
                    co.start()
                    outc.append(co)
            for co in outc[-2:]:
                co.wait()
        for t in range(n):
            for l in range(DEPTH):
                for j, (cx, cy) in enumerate(chips):
                    k = (t * DEPTH + l) * 3 + j
                    win = window(t, l, 2 * cx + cy, c)
                    _rcopy(win, win, ssem.at[k], rsem.at[k], (cx, cy, c)).wait_recv()
                    if split[t]:
                        cp = _rcopy(win, win, fssem.at[k], frsem.at[k], sib)
                        cp.start()
                        sends.append(cp)
        for t in range(n):
            if split[t]:
                for l in range(DEPTH):
                    for j, (cx, cy) in enumerate(chips):
                        k = (t * DEPTH + l) * 3 + j
                        win = window(t, l, 2 * cx + cy, 1 - c)
                        _rcopy(win, win, fssem.at[k], frsem.at[k], sib).wait_recv()
        for cp in sends:
            cp.wait_send()

    out_shape = [jax.ShapeDtypeStruct((NCHIP,) + s.shape[1:], s.dtype) for s in shards for _ in range(DEPTH)]
    outs = pl.pallas_call(
        body, name="allgather_weights", in_specs=[HBM] * n, out_specs=[HBM] * nout, out_shape=out_shape,
        scratch_shapes=[pltpu.VMEM((2, r, s.shape[2]), s.dtype) for r, s in zip(rows, shards)]
        + [pltpu.SemaphoreType.DMA((3 * nout,))] * 4 + [pltpu.SemaphoreType.DMA((n,)), pltpu.SemaphoreType.DMA((2 * n,))],
        compiler_params=pltpu.CompilerParams(vmem_limit_bytes=VMEM_LIMIT),
    )(*shards)
    return [outs[t * DEPTH:(t + 1) * DEPTH] for t in range(n)]


def _rs_pair_exchange(gs):
    n = len(gs)

    def body(*refs):
        ins, outs = refs[:n], refs[n:2 * n]
        ssem, rsem = refs[2 * n:]
        x, y, c, _ = _place()
        cps = []
        for t in range(n):
            H = gs[t].shape[1] // 2
            cp = _rcopy(ins[t].at[:, pl.ds((1 - c) * H, H)], outs[t], ssem.at[t], rsem.at[t], (x, y, 1 - c))
            cp.start()
            cps.append(cp)
        for cp in cps:
            cp.wait_recv()
        for cp in cps:
            cp.wait_send()

    out_shape = [jax.ShapeDtypeStruct((NCHIP, g.shape[1] // 2, g.shape[2]), F32) for g in gs]
    return pl.pallas_call(
        body, name="rs_pair_exchange", in_specs=[HBM] * n, out_specs=[HBM] * n, out_shape=out_shape,
        scratch_shapes=[pltpu.SemaphoreType.DMA((n,))] * 2,
    )(*gs)


def _rs_chip_scatter(hs):
    n = len(hs)

    def body(*refs):
        ins, outs = refs[:n], refs[n:2 * n]
        ssem, rsem = refs[2 * n:]
        x, y, c, chips = _place()
        sends = []
        for t in range(n):
            for j, (cx, cy) in enumerate(chips):
                cp = _rcopy(ins[t].at[2 * cx + cy], outs[t].at[j], ssem.at[3 * t + j], rsem.at[3 * t + j], (cx, cy, c))
                cp.start()
                sends.append(cp)
        for cp in sends:
            cp.wait_recv()
        for cp in sends:
            cp.wait_send()

    out_shape = [jax.ShapeDtypeStruct((NCHIP - 1,) + h.shape[1:], h.dtype) for h in hs]
    return pl.pallas_call(
        body, name="rs_chip_scatter", in_specs=[HBM] * n, out_specs=[HBM] * n, out_shape=out_shape,
        scratch_shapes=[pltpu.SemaphoreType.DMA((3 * n,))] * 2,
    )(*hs)


def _rs_pair_gather(fs):
    n = len(fs)

    def body(*refs):
        bufs = refs[n:2 * n]
        ssem, rsem = refs[2 * n:]
        x, y, c, _ = _place()
        sends = []
        for t in range(n):
            cp = _rcopy(bufs[t].at[c], bufs[t].at[c], ssem.at[t], rsem.at[t], (x, y, 1 - c))
            cp.start()
            sends.append(cp)
        for t in range(n):
            win = bufs[t].at[1 - c]
            _rcopy(win, win, ssem.at[t], rsem.at[t], (x, y, 1 - c)).wait_recv()
        for cp in sends:
            cp.wait_send()

    out_shape = [jax.ShapeDtypeStruct(f.shape, F32) for f in fs]
    return pl.pallas_call(
        body, name="rs_pair_gather", in_specs=[HBM] * n, out_specs=[HBM] * n, out_shape=out_shape,
        input_output_aliases={t: t for t in range(n)},
        scratch_shapes=[pltpu.SemaphoreType.DMA((n,))] * 2,
    )(*fs)


def _allreduce_small(buf):
    R = buf.shape[0]

    def body(in_ref, out_ref, slots, ssem, rsem):
        x, y, c, _ = _place()
        me = 4 * x + 2 * y + c
        slots[me] = in_ref[...]
        cps = []
        for k in range(1, NDEV):
            px = 1 - x if k & 4 else x
            py = 1 - y if k & 2 else y
            pc = 1 - c if k & 1 else c
            cp = _rcopy(in_ref, slots.at[me], ssem.at[k - 1], rsem.at[k - 1], (px, py, pc))
            cp.start()
            cps.append((cp, 4 * px + 2 * py + pc))
        for k, (cp, peer) in enumerate(cps):
            _rcopy(in_ref, slots.at[peer], ssem.at[k], rsem.at[k], (x, y, c)).wait_recv()
        for cp, _ in cps:
            cp.wait_send()
        acc = slots[0]
        for p in range(1, NDEV):
            acc = acc + slots[p]
        out_ref[...] = acc

    return pl.pallas_call(
        body, name="allreduce_small", out_shape=jax.ShapeDtypeStruct((R, LANE), F32),
        in_specs=[pl.BlockSpec(memory_space=pltpu.VMEM)], out_specs=pl.BlockSpec(memory_space=pltpu.VMEM),
        scratch_shapes=[pltpu.VMEM((NDEV, R, LANE), F32), pltpu.SemaphoreType.DMA((NDEV - 1,)),
                        pltpu.SemaphoreType.DMA((NDEV - 1,))],
        compiler_params=pltpu.CompilerParams(vmem_limit_bytes=VMEM_LIMIT),
    )(buf)


def _interleave(a):
    lead = a.shape[:-1]
    return a.reshape(*lead, 2, NFB, FC // 2).swapaxes(-3, -2).reshape(*lead, 2 * DFF)


def _uninterleave(a):
    lead = a.shape[:-1]
    return a.reshape(*lead, NFB, 2, FC // 2).swapaxes(-3, -2).reshape(*lead, 2 * DFF)


def _train_compute(xt, tgt, W, nb):
    saved = []
    xc = xt
    for l in range(DEPTH):
        t = f"_l{l}"
        h, P = _rms_matmul(xc, W["ln1"][l], W["in"][l], tm=512, name="proj_in" + t)
        c = _forget_fwd(P, W["bf"][l], nb, "forget_fwd" + t)
        of, lsef = _fox_fwd(P, c, nb, "fox_fwd" + t)
        od, lsed = _dil_fwd(P, nb, "dil_fwd" + t)
        convp = (W["cw"][l], W["cb"][l], W["cng"][l], W["cnb"][l])
        y = _attn_norm_fwd(of, od, W["gof"][l], W["god"][l], "attn_norm_fwd" + t)
        y, c0 = _conv_fwd(P, y, *convp, nb, "conv_fwd" + t)
        xm = _matmul(y, W["o"][l], add=xc, tm=1024, tn=1024, tk=D, name="proj_out" + t)
        h2, U = _matmul_ffn(xm, W["up"][l], "fwd", out_dtype=BF16, norm_g=W["ln2"][l], name="ffn_up" + t)
        hid, U2 = _ffn_fwd(U, W["fw"][l], W["fb"][l], nb, "ffn_act_fwd" + t)
        xo = _matmul(hid, W["down"][l], add=xm, tm=512, tn=D, tk=DFF, name="ffn_down" + t)
        saved.append((xc, h, P, c, of, lsef, od, lsed, convp, c0, y, xm, h2, U, U2, hid))
        xc = xo

    loss8, dx, dxb, dgfin = _loss_head(xc, W["gfin"], tgt, "loss_head")

    big = [None] * DEPTH
    small = [None] * DEPTH
    tk_dw = min(4096, xt.shape[0])
    for l in reversed(range(DEPTH)):
        t = f"_l{l}"
        xin, h, P, c, of, lsef, od, lsed, convp, c0, y, xm, h2, U, U2, hid = saved[l]
        dhid = _matmul(dxb, W["down"][l], tb=True, out_dtype=BF16, tm=1024, tn=DFF, tk=D, name="ffn_down_dx" + t)
        dWd = _matmul(hid, dxb, ta=True, tm=DFF // 2, tn=D, tk=2048, name="ffn_down_dw" + t)
        dU, dfw = _ffn_bwd(U, U2, dhid, W["fw"][l], nb, "ffn_act_bwd" + t)
        dh2 = _matmul_ffn(dU, W["up"][l], "dx", out_dtype=BF16, name="ffn_up_dx" + t)
        dWup = _matmul_ffn(h2, dU, "dw", tk=tk_dw, name="ffn_up_dw" + t)
        dxm, dxmb, dln2 = _rms_bwd(xm, W["ln2"][l], dh2, dx, "rms2_bwd" + t)
        dy = _matmul(dxmb, W["o"][l], tb=True, out_dtype=BF16, tm=1024, tn=D, tk=D, name="proj_out_dx" + t)
        dWo = _matmul(y, dxmb, ta=True, tm=D, tn=D, tk=tk_dw, name="proj_out_dw" + t)
        dof, dod, dgo = _attn_norm_bwd(of, od, dy, W["gof"][l], W["god"][l], "attn_norm_bwd" + t)
        dP, dcb = _fox_bwd(P, c, of, lsef, dof, nb, "fox_bwd" + t)
        dfa, dbf = _forget_bwd(P, W["bf"][l], dcb, nb, "forget_bwd" + t)
        dP = _dil_bwd(P, od, lsed, dod, dP, nb, "dil_bwd" + t)
        dP, dcw, dsm = _conv_bwd(P, c0, dy, dfa, dP, convp[0], convp[2], convp[3], nb, "conv_bwd" + t)
        dh = _matmul(dP, W["in"][l], tb=True, out_dtype=BF16, tm=1024, tn=D, tk=NP, name="proj_in_dx" + t)
        dWin = _matmul(h, dP, ta=True, tm=D, tn=1024, tk=tk_dw, name="proj_in_dw" + t)
        dx, dxb, dln1 = _rms_bwd(xin, W["ln1"][l], dh, dxm, "rms1_bwd" + t)
        big[l] = (dWin, dWo, dWup, dWd)
        small[l] = (dln1, dbf, dgo, dcw, dsm, dln2, dfw)
    return loss8, dx, big, small, dgfin


_SMALL_ROWS = (D // LANE, 8, 8 * WA // LANE, CPAD * CC // LANE, 8 * CC // LANE, D // LANE, 8 * 2 * DFF // LANE)


def kernel(x, ln1_g, w_in, b_forget, g_out_fox, g_out_dil, conv_w, conv_b, cnorm_g, cnorm_b, w_o, ln2_g, w_up, ffn_conv_w, ffn_conv_b, w_down, g_final, loss_target, m_ln1_g, m_w_in, m_b_forget, m_g_out_fox, m_g_out_dil, m_conv_w, m_conv_b, m_cnorm_g, m_cnorm_b, m_w_o, m_ln2_g, m_w_up, m_ffn_conv_w, m_ffn_conv_b, m_w_down, m_g_final, v_ln1_g, v_w_in, v_b_forget, v_g_out_fox, v_g_out_dil, v_conv_w, v_conv_b, v_cnorm_g, v_cnorm_b, v_w_o, v_ln2_g, v_w_up, v_ffn_conv_w, v_ffn_conv_b, v_w_down, v_g_final):
    names = ("ln1_g", "w_in", "b_forget", "g_out_fox", "g_out_dil", "conv_w", "conv_b", "cnorm_g", "cnorm_b",
             "w_o", "ln2_g", "w_up", "ffn_conv_w", "ffn_conv_b", "w_down", "g_final")
    w = dict(zip(names, (ln1_g, w_in, b_forget, g_out_fox, g_out_dil, conv_w, conv_b, cnorm_g, cnorm_b,
                         w_o, ln2_g, w_up, ffn_conv_w, ffn_conv_b, w_down, g_final)))
    m = dict(zip(names, (m_ln1_g, m_w_in, m_b_forget, m_g_out_fox, m_g_out_dil, m_conv_w, m_conv_b, m_cnorm_g,
                         m_cnorm_b, m_w_o, m_ln2_g, m_w_up, m_ffn_conv_w, m_ffn_conv_b, m_w_down, m_g_final)))
    v = dict(zip(names, (v_ln1_g, v_w_in, v_b_forget, v_g_out_fox, v_g_out_dil, v_conv_w, v_conv_b, v_cnorm_g,
                         v_cnorm_b, v_w_o, v_ln2_g, v_w_up, v_ffn_conv_w, v_ffn_conv_b, v_w_down, v_g_final)))
    nb = x.shape[0]
    T = nb * S
    xi, yi, ci = lax.axis_index("x"), lax.axis_index("y"), lax.axis_index("c")
    chip = 2 * xi + yi
    cw_cols = CC // NCHIP
    up_cols = 2 * DFF // NCHIP

    shards = [_pack_w_in(w_in), w_o.astype(BF16), w_up.astype(BF16), w_down.astype(BF16),
              jnp.pad(ffn_conv_w, ((0, 0), (0, 8 - FK), (0, 0))),
              jnp.pad(conv_w, ((0, 0), (0, CPAD - CK), (0, LANE - cw_cols)))]
    g_in, g_o, g_up, g_dn, g_fw, g_cw = _allgather(shards, (True, True, True, True, False, False))
    fb_full = _interleave(ffn_conv_b)
    W = {
        "in": [g.reshape(D, NP) for g in g_in],
        "o": [g.reshape(D, D) for g in g_o],
        "up": [g.transpose(1, 0, 2).reshape(D, 2 * DFF) for g in g_up],
        "down": [g.reshape(DFF, D) for g in g_dn],
        "ln1": [ln1_g[l] for l in range(DEPTH)],
        "ln2": [ln2_g[l] for l in range(DEPTH)],
        "bf": [jnp.pad(b_forget[l], (0, LANE - N_FG)).reshape(1, LANE) for l in range(DEPTH)],
        "gof": [g_out_fox[l].reshape(1, WA) for l in range(DEPTH)],
        "god": [g_out_dil[l].reshape(1, WA) for l in range(DEPTH)],
        "cw": [g[..., :cw_cols].transpose(1, 0, 2).reshape(CPAD, CC) for g in g_cw],
        "cb": [conv_b[l].reshape(1, CC) for l in range(DEPTH)],
        "cng": [cnorm_g[l].reshape(1, CC) for l in range(DEPTH)],
        "cnb": [cnorm_b[l].reshape(1, CC) for l in range(DEPTH)],
        "fw": [_interleave(g.transpose(1, 0, 2).reshape(8, 2 * DFF)) for g in g_fw],
        "fb": [fb_full[l].reshape(1, 2 * DFF) for l in range(DEPTH)],
        "gfin": g_final,
    }

    loss8, dx, big, small, dgfin = _train_compute(x.reshape(T, D), loss_target.reshape(T, D), W, nb)

    gs = []
    for l in range(DEPTH):
        dWin, dWo, dWup, dWd = big[l]
        gs += [dWin.reshape(NCHIP, D // NCHIP, NP), dWo.reshape(NCHIP, D // NCHIP, D),
               jnp.stack([half[:, i * up_cols:(i + 1) * up_cols] for half in dWup for i in range(2)]),
               dWd.reshape(NCHIP, DFF // NCHIP, D)]
    r1 = _rs_pair_exchange(gs)
    place = jnp.stack([chip, ci]).astype(jnp.int32)
    hs = [_add_half(g, r, place, f"rs_add_pair_{i}") for i, (g, r) in enumerate(zip(gs, r1))]
    r2 = _rs_chip_scatter(hs)
    fs = [_sum_slots(g, a, b, place, f"rs_add_chips_{i}") for i, (g, a, b) in enumerate(zip(gs, r1, r2))]
    red = _rs_pair_gather(fs)
    red = [r.reshape(r.shape[0] * r.shape[1], r.shape[2]) for r in red]
    grads, delta, new_m, new_v = {}, {}, {}, {}
    for i, n in enumerate(("w_in", "w_o", "w_up", "w_down")):
        grads[n], delta[n], new_m[n], new_v[n] = _adamw_layers(
            w[n], [red[4 * l + i] for l in range(DEPTH)], m[n], v[n], "adamw_" + n, packed=(n == "w_in"))

    parts = []
    for l in range(DEPTH):
        parts += [p.reshape(-1, LANE) for p in small[l]]
    parts += [dgfin.reshape(-1, LANE), loss8]
    tot = _allreduce_small(jnp.concatenate(parts, axis=0))
    off = 0
    per_layer = []
    for l in range(DEPTH):
        vals = []
        for rows in _SMALL_ROWS:
            vals.append(tot[off:off + rows])
            off += rows
        per_layer.append(vals)
    gfin_sum = tot[off:off + D // LANE].reshape(D)
    loss = tot[off + D // LANE, 0]

    def layer_stack(fn):
        return jnp.stack([fn(*per_layer[l]) for l in range(DEPTH)])

    fw_sum = layer_stack(lambda a, b, c_, d, e, f, g: _uninterleave(g.reshape(8, 2 * DFF)))
    cw_sum = layer_stack(lambda a, b, c_, d, e, f, g: d.reshape(CPAD, CC)[:CK])
    sm_sum = layer_stack(lambda a, b, c_, d, e, f, g: e.reshape(8, CC))
    go_sum = layer_stack(lambda a, b, c_, d, e, f, g: c_.reshape(8, WA))
    grads.update({
        "ln1_g": layer_stack(lambda a, b, c_, d, e, f, g: a.reshape(D)),
        "b_forget": layer_stack(lambda a, b, c_, d, e, f, g: b[0, :N_FG]),
        "g_out_fox": go_sum[:, 0],
        "g_out_dil": go_sum[:, 1],
        "conv_w": lax.dynamic_slice_in_dim(cw_sum, chip * cw_cols, cw_cols, axis=2),
        "conv_b": sm_sum[:, 0],
        "cnorm_g": sm_sum[:, 1],
        "cnorm_b": sm_sum[:, 2],
        "ln2_g": layer_stack(lambda a, b, c_, d, e, f, g: f.reshape(D)),
        "ffn_conv_w": lax.dynamic_slice_in_dim(fw_sum[:, :FK], chip * up_cols, up_cols, axis=2),
        "ffn_conv_b": fw_sum[:, FK],
        "g_final": gfin_sum,
    })

    for n in names:
        if n not in delta:
            delta[n], new_m[n], new_v[n] = _adamw(w[n], grads[n], m[n], v[n], "adamw_" + n)
    return (loss, dx.reshape(nb, S, D), *[grads[n] for n in names], *[delta[n] for n in names],
            *[new_m[n] for n in names], *[new_v[n] for n in names])
```
